```python
import math
import jax, jax.numpy as jnp
from jax import lax
import numpy as np

D_MODEL = 2048
BATCH = 8
SEQ = 2048
DEPTH = 1

SSM_WIDTH = D_MODEL // 2
SSM_GROUP = 16
SSM_GROUPS = SSM_WIDTH // SSM_GROUP
SSM_STATE = 64
RET_WIDTH = D_MODEL - SSM_WIDTH
RET_HEADS = 8
RET_HEAD_DIM = RET_WIDTH // RET_HEADS
RET_CHUNK = 128
ROPE_BASE = 10000.0
N_GROUPS = 4
EXPERTS_PER_GROUP = 8
N_EXPERTS = N_GROUPS * EXPERTS_PER_GROUP
TOP_K = 2
D_EXPERT = D_MODEL // 2
MOE_BLOCK = 256
N_MOD = 6
NORM_EPS = 1e-6
GN_EPS = 1e-5
DT_MIN = 1e-3
DT_MAX = 1e-1

kernel_name = "hymba_s5_retnet_hiermoe_adaln"


def rms_norm(x, gain):
    xf = x.astype(jnp.float32)
    y = xf * lax.rsqrt(jnp.mean(xf * xf, axis=-1, keepdims=True) + NORM_EPS)
    return (y * gain.astype(jnp.float32)).astype(x.dtype)


def cmul(ar, ai, br, bi):
    return ar * br - ai * bi, ar * bi + ai * br


def s5_mixer(u, a_re, a_im, b_re, b_im, c_re, c_im, d_skip, log_dt, w_glu, beta):
    bn, L, _ = u.shape
    f32 = jnp.float32
    uf = u.astype(f32).reshape(bn, L, SSM_GROUPS, SSM_GROUP)
    a_re = a_re.astype(f32)
    a_im = a_im.astype(f32)
    dt = jnp.exp(log_dt.astype(f32))[:, None]
    mag = jnp.exp(dt * a_re)
    ab_re = mag * jnp.cos(dt * a_im)
    ab_im = mag * jnp.sin(dt * a_im)
    inv_abs2 = 1.0 / (a_re * a_re + a_im * a_im)
    n_re = ab_re - 1.0
    f_re = (n_re * a_re + ab_im * a_im) * inv_abs2
    f_im = (ab_im * a_re - n_re * a_im) * inv_abs2
    bb_re, bb_im = cmul(f_re[..., None], f_im[..., None],
                        b_re.astype(f32), b_im.astype(f32))
    bu_re = jnp.einsum('blgh,gph->blgp', uf, bb_re)
    bu_im = jnp.einsum('blgh,gph->blgp', uf, bb_im)
    a_seq_re = jnp.broadcast_to(ab_re, (1, L, SSM_GROUPS, SSM_STATE))
    a_seq_im = jnp.broadcast_to(ab_im, (1, L, SSM_GROUPS, SSM_STATE))

    def combine(e1, e2):
        ar1, ai1, br1, bi1 = e1
        ar2, ai2, br2, bi2 = e2
        ar, ai = cmul(ar2, ai2, ar1, ai1)
        xr, xi = cmul(ar2, ai2, br1, bi1)
        return ar, ai, xr + br2, xi + bi2

    _, _, s_re, s_im = lax.associative_scan(
        combine, (a_seq_re, a_seq_im, bu_re, bu_im), axis=1)
    y = (jnp.einsum('blgp,ghp->blgh', s_re, c_re.astype(f32))
         - jnp.einsum('blgp,ghp->blgh', s_im, c_im.astype(f32))
         + d_skip.astype(f32) * uf)
    y = y.reshape(bn, L, SSM_WIDTH)
    z = jax.nn.gelu(y)
    out = z * jax.nn.sigmoid(z @ w_glu.astype(f32))
    return rms_norm(out, beta)


def rope(x, L):
    half = RET_HEAD_DIM // 2
    inv_freq = 1.0 / (ROPE_BASE ** (jnp.arange(half, dtype=jnp.float32) * 2.0 / RET_HEAD_DIM))
    ang = jnp.arange(L, dtype=jnp.float32)[:, None] * inv_freq[None, :]
    cos = jnp.cos(ang)[None, :, None, :]
    sin = jnp.sin(ang)[None, :, None, :]
    x1, x2 = x[..., :half], x[..., half:]
    return jnp.concatenate([x1 * cos - x2 * sin, x2 * cos + x1 * sin], axis=-1)


def retention(q, k, v, g, beta):
    bn, L, _ = q.shape
    f32 = jnp.float32
    H, Dh, C = RET_HEADS, RET_HEAD_DIM, RET_CHUNK
    nc = L // C
    q = rope(q.astype(f32).reshape(bn, L, H, Dh), L)
    k = rope(k.astype(f32).reshape(bn, L, H, Dh), L) * (Dh ** -0.5)
    v = v.astype(f32).reshape(bn, L, H, Dh)
    gamma = 1.0 - jnp.exp2(-5.0 - jnp.arange(H, dtype=f32))
    log_g = jnp.log(gamma)
    idx = jnp.arange(C, dtype=f32)
    rel = idx[:, None] - idx[None, :]
    decay = jnp.where(rel >= 0, jnp.exp(log_g[:, None, None] * jnp.maximum(rel, 0.0)), 0.0)
    qc = q.reshape(bn, nc, C, H, Dh)
    kc = k.reshape(bn, nc, C, H, Dh)
    vc = v.reshape(bn, nc, C, H, Dh)
    s = jnp.einsum('bnchd,bnmhd->bnhcm', qc, kc) * decay[None, None]
    inner = jnp.einsum('bnhcm,bnmhe->bnche', s, vc)
    zeta = jnp.exp(log_g[:, None] * (C - 1.0 - idx)[None, :])
    kv = jnp.einsum('bnmhd,hm,bnmhe->bnhde', kc, zeta, vc)
    g_chunk = jnp.exp(log_g * C)[None, :, None, None]

    def step(r, kv_n):
        return r * g_chunk + kv_n, r

    _, r_prev = lax.scan(step, jnp.zeros((bn, H, Dh, Dh), f32), jnp.moveaxis(kv, 1, 0))
    r_prev = jnp.moveaxis(r_prev, 0, 1)
    xi = jnp.exp(log_g[None, :] * (idx + 1.0)[:, None])
    cross = jnp.einsum('bnchd,bnhde->bnche', qc, r_prev) * xi[None, None, :, :, None]
    y = (inner + cross).reshape(bn, L, H, Dh)
    mu = jnp.mean(y, axis=-1, keepdims=True)
    var = jnp.mean(jnp.square(y - mu), axis=-1, keepdims=True)
    y = ((y - mu) * lax.rsqrt(var + GN_EPS)).reshape(bn, L, RET_WIDTH) * beta.astype(f32)
    return jax.nn.silu(g.astype(f32)) * y


def hier_moe(h, w_rg, b_rg, w_re, b_re, w_gate, w_up, w_down):
    bn, L, D = h.shape
    T = bn * L
    f32 = jnp.float32
    hf = h.reshape(T, D)
    gl = (hf @ w_rg + b_rg).astype(f32)
    gp = jax.nn.softmax(gl, axis=-1)
    g_sel = jnp.argmax(gl, axis=-1)
    g_w = jnp.take_along_axis(gp, g_sel[:, None], axis=1)[:, 0]
    el = (hf @ w_re + b_re).astype(f32).reshape(T, N_GROUPS, EXPERTS_PER_GROUP)
    el_sel = jnp.take_along_axis(el, g_sel[:, None, None], axis=1)[:, 0]
    top_l, top_j = lax.top_k(el_sel, TOP_K)
    top_w = jax.nn.softmax(top_l, axis=-1) * g_w[:, None]
    eid = g_sel[:, None] * EXPERTS_PER_GROUP + top_j
    A = T * TOP_K
    e_flat = eid.reshape(A).astype(jnp.int32)
    w_flat = top_w.reshape(A)
    tok = jnp.arange(A, dtype=jnp.int32) // TOP_K
    order = jnp.argsort(e_flat)
    s_e = e_flat[order]
    s_tok = tok[order]
    s_w = w_flat[order]
    counts = jnp.bincount(e_flat, length=N_EXPERTS)
    padded = ((counts + MOE_BLOCK - 1) // MOE_BLOCK) * MOE_BLOCK
    off = jnp.cumsum(counts) - counts
    pend = jnp.cumsum(padded)
    poff = pend - padded
    dest = poff[s_e] + jnp.arange(A, dtype=jnp.int32) - off[s_e]
    nb = -(-A // MOE_BLOCK) + N_EXPERTS
    P = nb * MOE_BLOCK
    slot_tok = jnp.full((P,), T, jnp.int32).at[dest].set(s_tok)
    slot_w = jnp.zeros((P,), f32).at[dest].set(s_w)
    block_e = jnp.minimum(
        jnp.searchsorted(pend, jnp.arange(nb, dtype=jnp.int32) * MOE_BLOCK, side='right'),
        N_EXPERTS - 1)
    x_pad = jnp.concatenate([hf, jnp.zeros((1, D), hf.dtype)], axis=0)
    xs = x_pad[slot_tok].reshape(nb, MOE_BLOCK, D)

    def expert_block(args):
        xb, e = args
        return (jax.nn.silu(xb @ w_gate[e]) * (xb @ w_up[e])) @ w_down[e]

    ys = lax.map(expert_block, (xs, block_e)).reshape(P, D)
    ys = ys * slot_w[:, None].astype(ys.dtype)
    out = jax.ops.segment_sum(ys, slot_tok, num_segments=T + 1)[:T]
    return out.reshape(bn, L, D)


def setup_inputs(seed: int = 0) -> dict:
    key = jax.random.key(seed)
    ks = jax.random.split(key, 32)
    f32 = jnp.float32
    D, G, P, Hc = D_MODEL, SSM_GROUPS, SSM_STATE, SSM_GROUP
    nrm = lambda k, shape, s: jax.random.normal(k, shape, f32) * s
    gain = lambda k, shape: 1.0 + 0.02 * jax.random.normal(k, shape, f32)
    return {
        "x": jax.random.normal(ks[0], (BATCH, SEQ, D), f32),
        "c": jax.random.normal(ks[1], (BATCH, D), f32),
        "w_ada": nrm(ks[2], (DEPTH, D, N_MOD * D), D ** -0.5),
        "b_ada": nrm(ks[3], (DEPTH, N_MOD * D), 0.02),
        "g_norm1": gain(ks[4], (DEPTH, D)),
        "w_in": nrm(ks[5], (DEPTH, D, SSM_WIDTH + 4 * RET_WIDTH), D ** -0.5),
        "ssm_a_re": -0.5 + 0.01 * jax.random.normal(ks[6], (DEPTH, G, P), f32),
        "ssm_a_im": math.pi * jnp.arange(P, dtype=f32)[None, None, :]
                    + 0.01 * jax.random.normal(ks[7], (DEPTH, G, P), f32),
        "ssm_b_re": nrm(ks[8], (DEPTH, G, P, Hc), (2 * Hc) ** -0.5),
        "ssm_b_im": nrm(ks[9], (DEPTH, G, P, Hc), (2 * Hc) ** -0.5),
        "ssm_c_re": nrm(ks[10], (DEPTH, G, Hc, P), (2 * P) ** -0.5),
        "ssm_c_im": nrm(ks[11], (DEPTH, G, Hc, P), (2 * P) ** -0.5),
        "ssm_d": nrm(ks[12], (DEPTH, G, Hc), 1.0),
        "ssm_log_dt": jax.random.uniform(ks[13], (DEPTH, G), f32,
                                         math.log(DT_MIN), math.log(DT_MAX)),
        "w_glu": nrm(ks[14], (DEPTH, SSM_WIDTH, SSM_WIDTH), SSM_WIDTH ** -0.5),
        "beta_ssm": gain(ks[15], (DEPTH, SSM_WIDTH)),
        "beta_ret": gain(ks[16], (DEPTH, RET_WIDTH)),
        "w_out": nrm(ks[17], (DEPTH, D, D), D ** -0.5),
        "g_norm2": gain(ks[18], (DEPTH, D)),
        "w_router_group": nrm(ks[19], (DEPTH, D, N_GROUPS), D ** -0.5),
        "b_router_group": nrm(ks[20], (DEPTH, N_GROUPS), 0.01),
        "w_router_expert": nrm(ks[21], (DEPTH, D, N_EXPERTS), D ** -0.5),
        "b_router_expert": nrm(ks[22], (DEPTH, N_EXPERTS), 0.01),
        "w_gate": nrm(ks[23], (DEPTH, N_EXPERTS, D, D_EXPERT), D ** -0.5),
        "w_up": nrm(ks[24], (DEPTH, N_EXPERTS, D, D_EXPERT), D ** -0.5),
        "w_down": nrm(ks[25], (DEPTH, N_EXPERTS, D_EXPERT, D), D_EXPERT ** -0.5),
        "g_final": gain(ks[26], (D,)),
    }


def reference(x, c, w_ada, b_ada, g_norm1, w_in, ssm_a_re, ssm_a_im, ssm_b_re, ssm_b_im,
              ssm_c_re, ssm_c_im, ssm_d, ssm_log_dt, w_glu, beta_ssm, beta_ret, w_out,
              g_norm2, w_router_group, b_router_group, w_router_expert, b_router_expert,
              w_gate, w_up, w_down, g_final):
    splits = [SSM_WIDTH, SSM_WIDTH + RET_WIDTH, SSM_WIDTH + 2 * RET_WIDTH,
              SSM_WIDTH + 3 * RET_WIDTH]
    for l in range(DEPTH):
        mod = jax.nn.silu(c) @ w_ada[l] + b_ada[l]
        sh1, sc1, gt1, sh2, sc2, gt2 = jnp.split(mod, N_MOD, axis=-1)
        h = rms_norm(x, g_norm1[l]) * (1.0 + sc1[:, None, :]) + sh1[:, None, :]
        proj = h @ w_in[l]
        u, q, k, v, g = jnp.split(proj, splits, axis=-1)
        y_ssm = s5_mixer(u, ssm_a_re[l], ssm_a_im[l], ssm_b_re[l], ssm_b_im[l],
                         ssm_c_re[l], ssm_c_im[l], ssm_d[l], ssm_log_dt[l],
                         w_glu[l], beta_ssm[l])
        y_ret = retention(q, k, v, g, beta_ret[l])
        mixed = jnp.concatenate([y_ssm, y_ret], axis=-1).astype(x.dtype) @ w_out[l]
        x = x + gt1[:, None, :] * mixed
        h2 = rms_norm(x, g_norm2[l]) * (1.0 + sc2[:, None, :]) + sh2[:, None, :]
        y_moe = hier_moe(h2, w_router_group[l], b_router_group[l], w_router_expert[l],
                         b_router_expert[l], w_gate[l], w_up[l], w_down[l])
        x = x + gt2[:, None, :] * y_moe.astype(x.dtype)
    return rms_norm(x, g_final)
```

```python
import functools
import math

import jax
import jax.numpy as jnp
from jax import lax
from jax.experimental import pallas as pl
from jax.experimental.pallas import tpu as pltpu

F32 = jnp.float32
BF16 = jnp.bfloat16
I32 = jnp.int32

SSM_GROUP = 16
SSM_STATE = 64
RET_HEAD_DIM = 128
ROPE_BASE = 10000.0
N_GROUPS = 4
EXPERTS_PER_GROUP = 8
N_EXPERTS = N_GROUPS * EXPERTS_PER_GROUP
N_MOD = 6
NORM_EPS = 1e-6
GN_EPS = 1e-5

LANES = 128
SUBLANES = 8
VMEM_LIMIT = 56 * 1024 * 1024

ROUTE_LANE0 = N_GROUPS
MOE_BLOCK = 256
RET_CHUNK = 256
S5_TT = 128


def _cparams(sem):
    return pltpu.CompilerParams(dimension_semantics=sem, vmem_limit_bytes=VMEM_LIMIT)


def _silu(x):
    return x * jax.nn.sigmoid(x)


def _ada_kernel(c_ref, w_ref, b_ref, o_ref):
    s = _silu(c_ref[...])
    o_ref[...] = jnp.dot(s.astype(BF16), w_ref[...].astype(BF16),
                         preferred_element_type=F32) + b_ref[...]


def _ada(c, w, b, tn=1024):
    bn, d = c.shape
    n = w.shape[1]
    return pl.pallas_call(
        _ada_kernel,
        grid=(n // tn,),
        in_specs=[pl.BlockSpec((bn, d), lambda j: (0, 0)),
                  pl.BlockSpec((d, tn), lambda j: (0, j)),
                  pl.BlockSpec((1, tn), lambda j: (0, j))],
        out_specs=pl.BlockSpec((bn, tn), lambda j: (0, j)),
        out_shape=jax.ShapeDtypeStruct((bn, n), F32),
        compiler_params=_cparams(("arbitrary",)),
        name="ada_mod",
    )(c, w, b.reshape(1, n))


def _inproj_kernel(x_ref, mod_ref, g_ref, w_ref, u_ref, r_ref, h_scr):
    j = pl.program_id(1)

    @pl.when(j == 0)
    def _():
        x = x_ref[...]
        y = x * lax.rsqrt(jnp.mean(x * x, axis=-1, keepdims=True) + NORM_EPS) * g_ref[...]
        m = mod_ref[0]
        h_scr[...] = (y * (1.0 + m[1:2, :]) + m[0:1, :]).astype(BF16)

    acc = jnp.dot(h_scr[...], w_ref[...], preferred_element_type=F32).astype(BF16)

    @pl.when(j == 0)
    def _():
        u_ref[...] = acc

    @pl.when(j > 0)
    def _():
        r_ref[...] = acc


def _inproj(x2d, mod3, g1, w_in_bf, bn, seq, sw, tm=1024):
    t, d = x2d.shape
    n = w_in_bf.shape[1]
    tn = sw
    tm = min(tm, seq)
    tpb = seq // tm
    return pl.pallas_call(
        _inproj_kernel,
        grid=(t // tm, n // tn),
        in_specs=[pl.BlockSpec((tm, d), lambda i, j: (i, 0)),
                  pl.BlockSpec((1, N_MOD, d), lambda i, j: (i // tpb, 0, 0)),
                  pl.BlockSpec((1, d), lambda i, j: (0, 0)),
                  pl.BlockSpec((d, tn), lambda i, j: (0, j))],
        out_specs=[pl.BlockSpec((tm, tn), lambda i, j: (i % tpb, i // tpb)),
                   pl.BlockSpec((tm, tn), lambda i, j: (i, jnp.maximum(j - 1, 0)))],
        out_shape=[jax.ShapeDtypeStruct((seq, bn * tn), BF16),
                   jax.ShapeDtypeStruct((t, n - tn), BF16)],
        scratch_shapes=[pltpu.VMEM((tm, d), BF16)],
        compiler_params=_cparams(("arbitrary", "arbitrary")),
        name="norm_inproj",
    )(x2d, mod3, g1.reshape(1, d), w_in_bf)


def _s5_disc_kernel(are_ref, aim_ref, ldt_ref, bre_ref, bim_ref,
                    abre_ref, abim_ref, bbre_ref, bbim_ref):
    a_re = are_ref[...]
    a_im = aim_ref[...]
    dt = jnp.exp(ldt_ref[...])
    mag = jnp.exp(dt * a_re)
    ab_re = mag * jnp.cos(dt * a_im)
    ab_im = mag * jnp.sin(dt * a_im)
    inv_abs2 = 1.0 / (a_re * a_re + a_im * a_im)
    n_re = ab_re - 1.0
    f_re = (n_re * a_re + ab_im * a_im) * inv_abs2
    f_im = (ab_im * a_re - n_re * a_im) * inv_abs2
    abre_ref[...] = ab_re
    abim_ref[...] = ab_im
    b_re = bre_ref[...]
    b_im = bim_ref[...]
    fr = f_re[:, None, :]
    fi = f_im[:, None, :]
    bbre_ref[...] = fr * b_re - fi * b_im
    bbim_ref[...] = fr * b_im + fi * b_re


def _s5_disc(a_re, a_im, log_dt, bt_re, bt_im):
    g, p = a_re.shape
    h = bt_re.shape[1]
    return pl.pallas_call(
        _s5_disc_kernel,
        out_shape=[jax.ShapeDtypeStruct((g, p), F32), jax.ShapeDtypeStruct((g, p), F32),
                   jax.ShapeDtypeStruct((g, h, p), F32), jax.ShapeDtypeStruct((g, h, p), F32)],
        name="s5_discretise",
    )(a_re, a_im, log_dt.reshape(g, 1), bt_re, bt_im)


def _s5_pack(ab_re, ab_im, bbt_re, bbt_im, c_re, c_im):
    g, h, p = bbt_re.shape
    npair = g // 2
    ppb = LANES // (2 * h)
    eye2 = jnp.eye(2, dtype=F32)
    qsel = jax.nn.one_hot(jnp.arange(npair) % ppb, ppb, dtype=F32)

    def in_mat(bbt):
        b4 = bbt.reshape(npair, 2, h, p)
        return jnp.einsum('ab,xahp->xahbp', eye2, b4).reshape(npair, 2 * h, 2 * p)

    bsmall = jnp.concatenate([in_mat(bbt_re), in_mat(bbt_im)], axis=-1)
    bz = jnp.einsum('xq,xrc->xqrc', qsel, bsmall).reshape(npair, LANES, 4 * p)

    def out_mat(c):
        c4 = jnp.swapaxes(c, 1, 2).reshape(npair, 2, p, h)
        return jnp.einsum('ab,xaph->xapbh', eye2, c4).reshape(npair, 2 * p, 2 * h)

    csmall = jnp.concatenate([out_mat(c_re), -out_mat(c_im)], axis=1)
    cz = jnp.einsum('xq,xrc->xrqc', qsel, csmall).reshape(npair, 4 * p, LANES)
    a_re = ab_re.reshape(npair // ppb, ppb, 2 * p)
    a_im = ab_im.reshape(npair // ppb, ppb, 2 * p)
    return bz.astype(BF16), cz.astype(BF16), a_re, a_im


def _s5_kernel(u_ref, bz_ref, cz_ref, are_ref, aim_ref, d_ref, o_ref, state, buf, *, bn, tt, ppb):
    t = pl.program_id(0)
    kb = pl.program_id(1)

    @pl.when(t == 0)
    def _():
        state[kb] = jnp.zeros(state.shape[1:], F32)

    u = u_ref[...]
    for q in range(ppb):
        bu = jnp.dot(u, bz_ref[q], preferred_element_type=F32)
        buf[2 * q] = bu[:, :LANES]
        buf[2 * q + 1] = bu[:, LANES:]

    a_re = [jnp.broadcast_to(are_ref[0, q:q + 1, :], (bn, LANES)) for q in range(ppb)]
    a_im = [jnp.broadcast_to(aim_ref[0, q:q + 1, :], (bn, LANES)) for q in range(ppb)]
    s0 = tuple(state[kb, j] for j in range(2 * ppb))

    def step(i, s):
        rows = pl.ds(pl.multiple_of(i * bn, bn), bn)
        new = []
        for q in range(ppb):
            s_re, s_im = s[2 * q], s[2 * q + 1]
            n_re = a_re[q] * s_re - a_im[q] * s_im + buf[2 * q, rows, :]
            n_im = a_re[q] * s_im + a_im[q] * s_re + buf[2 * q + 1, rows, :]
            buf[2 * q, rows, :] = n_re
            buf[2 * q + 1, rows, :] = n_im
            new += [n_re, n_im]
        return tuple(new)

    s1 = lax.fori_loop(0, tt, step, s0, unroll=4)
    for j in range(2 * ppb):
        state[kb, j] = s1[j]

    acc = d_ref[...] * u.astype(F32)
    for q in range(ppb):
        lhs = jnp.concatenate([buf[2 * q], buf[2 * q + 1]], axis=1).astype(BF16)
        acc = acc + jnp.dot(lhs, cz_ref[q], preferred_element_type=F32)
    o_ref[...] = acc


def _s5_scan(u_tb, bz, cz, a_re, a_im, d_skip, bn, seq, tt=S5_TT):
    rows, width = u_tb.shape
    nkb = width // LANES
    ppb = bz.shape[0] // nkb
    rt = tt * bn
    kern = functools.partial(_s5_kernel, bn=bn, tt=tt, ppb=ppb)
    return pl.pallas_call(
        kern,
        grid=(seq // tt, nkb),
        in_specs=[pl.BlockSpec((rt, LANES), lambda t, k: (t, k)),
                  pl.BlockSpec((ppb,) + bz.shape[1:], lambda t, k: (k, 0, 0)),
                  pl.BlockSpec((ppb,) + cz.shape[1:], lambda t, k: (k, 0, 0)),
                  pl.BlockSpec((1,) + a_re.shape[1:], lambda t, k: (k, 0, 0)),
                  pl.BlockSpec((1,) + a_im.shape[1:], lambda t, k: (k, 0, 0)),
                  pl.BlockSpec((1, LANES), lambda t, k: (0, k))],
        out_specs=pl.BlockSpec((rt, LANES), lambda t, k: (t, k)),
        out_shape=jax.ShapeDtypeStruct((rows, width), F32),
        scratch_shapes=[pltpu.VMEM((nkb, 2 * ppb, bn, LANES), F32),
                        pltpu.VMEM((2 * ppb, rt, LANES), F32)],
        compiler_params=_cparams(("arbitrary", "arbitrary")),
        name="s5_scan",
    )(u_tb, bz, cz, a_re, a_im, d_skip.reshape(1, width))


def _ret_kernel(q_ref, k_ref, v_ref, g_ref, cos_ref, sin_ref, lg_ref, beta_ref, o_ref, *, chunk):
    seq, dh = q_ref.shape[1], q_ref.shape[2]
    lg = lg_ref[0][:, :1]
    ri = lax.broadcasted_iota(I32, (chunk, chunk), 0)
    ci = lax.broadcasted_iota(I32, (chunk, chunk), 1)
    rel = (ri - ci).astype(F32)
    decay = jnp.where(rel >= 0, jnp.exp(lg * jnp.maximum(rel, 0.0)), 0.0)
    idx = lax.broadcasted_iota(I32, (chunk, 1), 0).astype(F32)
    zeta = jnp.exp(lg * (chunk - 1.0 - idx))
    xi = jnp.exp(lg * (idx + 1.0))
    g_chunk = jnp.exp(lg * float(chunk))
    beta = beta_ref[...]
    half = dh // 2

    def rope(xv, cs, sn):
        return xv * cs + pltpu.roll(xv, half, axis=1) * sn

    def body(n, r):
        sl = pl.ds(pl.multiple_of(n * chunk, chunk), chunk)
        cs = cos_ref[sl, :]
        sn = sin_ref[sl, :]
        q = rope(q_ref[0, sl, :].astype(F32), cs, sn)
        k = rope(k_ref[0, sl, :].astype(F32), cs, sn) * (dh ** -0.5)
        v = v_ref[0, sl, :]
        qb = q.astype(BF16)
        s = lax.dot_general(qb, k.astype(BF16), (((1,), (1,)), ((), ())),
                            preferred_element_type=F32) * decay
        y = jnp.dot(s.astype(BF16), v, preferred_element_type=F32)
        y = y + jnp.dot(qb, r.astype(BF16), preferred_element_type=F32) * xi
        kz = (k * zeta).T.astype(BF16)
        r_new = r * g_chunk + jnp.dot(kz, v, preferred_element_type=F32)
        mu = jnp.mean(y, axis=-1, keepdims=True)
        yc = y - mu
        var = jnp.mean(yc * yc, axis=-1, keepdims=True)
        yn = yc * lax.rsqrt(var + GN_EPS) * beta
        o_ref[0, sl, :] = (_silu(g_ref[0, sl, :].astype(F32)) * yn).astype(BF16)
        return r_new

    lax.fori_loop(0, seq // chunk, body, jnp.zeros((dh, dh), F32))


def _retention(rest3, cosf, sinf, log_g, beta_ret, chunk=RET_CHUNK):
    bn, seq, _ = rest3.shape
    dh = RET_HEAD_DIM
    heads = beta_ret.shape[0] // dh
    blk = lambda off: pl.BlockSpec((1, seq, dh), lambda b, h, off=off: (b, 0, off + h))
    kern = functools.partial(_ret_kernel, chunk=chunk)
    return pl.pallas_call(
        kern,
        grid=(bn, heads),
        in_specs=[blk(0), blk(heads), blk(2 * heads), blk(3 * heads),
                  pl.BlockSpec((seq, dh), lambda b, h: (0, 0)),
                  pl.BlockSpec((seq, dh), lambda b, h: (0, 0)),
                  pl.BlockSpec((1, 1, LANES), lambda b, h: (h, 0, 0)),
                  pl.BlockSpec((1, dh), lambda b, h: (0, h))],
        out_specs=pl.BlockSpec((1, seq, dh), lambda b, h: (b, 0, h)),
        out_shape=jax.ShapeDtypeStruct((bn, seq, heads * dh), BF16),
        compiler_params=_cparams(("arbitrary", "arbitrary")),
        name="retention",
    )(rest3, rest3, rest3, rest3, cosf, sinf, log_g, beta_ret.reshape(1, heads * dh))


def _gelu_tanh(x):
    return 0.5 * x * (1.0 + jnp.tanh(math.sqrt(2.0 / math.pi) * (x + 0.044715 * (x * x * x))))


def _split_bf16(x):
    hi = x.astype(BF16)
    lo = (x - hi.astype(F32)).astype(BF16)
    return hi, lo


def _mix_kernel(ys_ref, yr_ref, x_ref, mod_ref, wglu_ref, bssm_ref, wout_ref, g2_ref,
                wrh_ref, wrl_ref, br_ref, x1_ref, h2_ref, lg_ref):
    sw = ys_ref.shape[1]
    z = _gelu_tanh(ys_ref[...])
    gate = jax.nn.sigmoid(jnp.dot(z.astype(BF16), wglu_ref[...], preferred_element_type=F32))
    o = z * gate
    y_ssm = o * lax.rsqrt(jnp.mean(o * o, axis=-1, keepdims=True) + NORM_EPS) * bssm_ref[...]
    mixed = (jnp.dot(y_ssm.astype(BF16), wout_ref[:sw, :], preferred_element_type=F32)
             + jnp.dot(yr_ref[...], wout_ref[sw:, :], preferred_element_type=F32))
    m = mod_ref[0]
    x1 = x_ref[...] + m[2:3, :] * mixed
    x1_ref[...] = x1
    y = x1 * lax.rsqrt(jnp.mean(x1 * x1, axis=-1, keepdims=True) + NORM_EPS) * g2_ref[...]
    h2 = y * (1.0 + m[4:5, :]) + m[3:4, :]
    h2_ref[...] = h2
    hi, lo = _split_bf16(h2)
    wrh = wrh_ref[...]
    lg_ref[...] = (jnp.dot(hi, wrh, preferred_element_type=F32)
                   + jnp.dot(lo, wrh, preferred_element_type=F32)
                   + jnp.dot(hi, wrl_ref[...], preferred_element_type=F32)
                   + br_ref[...])


def _mix(ypre_tb, yret, x2d, mod3, wglu_bf, beta_ssm, wout_bf, g2, wr_hi, wr_lo, br, bn, seq, tm=256):
    t, d = x2d.shape
    sw = wglu_bf.shape[0]
    tpb = seq // tm
    const = lambda shape: pl.BlockSpec(shape, lambda i: (0,) * len(shape),
                                       pipeline_mode=pl.Buffered(1))
    return pl.pallas_call(
        _mix_kernel,
        grid=(t // tm,),
        in_specs=[pl.BlockSpec((tm, sw), lambda i: (i % tpb, i // tpb)),
                  pl.BlockSpec((tm, d - sw), lambda i: (i, 0)),
                  pl.BlockSpec((tm, d), lambda i: (i, 0)),
                  pl.BlockSpec((1, N_MOD, d), lambda i: (i // tpb, 0, 0)),
                  const((sw, sw)), const((1, sw)), const((d, d)), const((1, d)),
                  const((d, LANES)), const((d, LANES)), const((1, LANES))],
        out_specs=[pl.BlockSpec((tm, d), lambda i: (i, 0)),
                   pl.BlockSpec((tm, d), lambda i: (i, 0)),
                   pl.BlockSpec((tm, LANES), lambda i: (i, 0))],
        out_shape=[jax.ShapeDtypeStruct((t, d), F32),
                   jax.ShapeDtypeStruct((t, d), F32),
                   jax.ShapeDtypeStruct((t, LANES), F32)],
        compiler_params=_cparams(("arbitrary",)),
        name="mix_outproj_router",
    )(ypre_tb, yret, x2d, mod3, wglu_bf, beta_ssm.reshape(1, sw), wout_bf, g2.reshape(1, d),
      wr_hi, wr_lo, br)


def _route_kernel(lg_ref, meta_ref, w_ref, cnt_ref, carry):
    i = pl.program_id(0)
    tm = lg_ref.shape[0]

    @pl.when(i == 0)
    def _():
        carry[...] = jnp.zeros(carry.shape, F32)

    lg = lg_ref[...]
    lane = lax.broadcasted_iota(I32, (tm, LANES), 1).astype(F32)
    neg = jnp.float32(-jnp.inf)

    def first_max(vals):
        m = jnp.max(vals, axis=-1, keepdims=True)
        idx = jnp.min(jnp.where(vals == m, lane, float(LANES)), axis=-1, keepdims=True)
        return m, idx

    gl = jnp.where(lane < N_GROUPS, lg, neg)
    gmax, gsel = first_max(gl)
    g_w = 1.0 / jnp.sum(jnp.exp(gl - gmax), axis=-1, keepdims=True)
    lo = ROUTE_LANE0 + gsel * EXPERTS_PER_GROUP
    el = jnp.where((lane >= lo) & (lane < lo + EXPERTS_PER_GROUP), lg, neg)
    m1, i1 = first_max(el)
    m2, i2 = first_max(jnp.where(lane == i1, neg, el))
    e21 = jnp.exp(m2 - m1)
    w1 = g_w / (1.0 + e21)
    w2 = g_w * e21 / (1.0 + e21)

    sel1 = lane == i1
    sel2 = lane == i2
    onehot = jnp.where(sel1 | sel2, 1.0, 0.0).astype(BF16)
    ri = lax.broadcasted_iota(I32, (tm, tm), 0)
    ci = lax.broadcasted_iota(I32, (tm, tm), 1)
    tri = jnp.where(ci < ri, 1.0, 0.0).astype(BF16)
    base = carry[0:1, :]
    excl = jnp.dot(tri, onehot, preferred_element_type=F32) + base
    r1 = jnp.sum(jnp.where(sel1, excl, 0.0), axis=-1, keepdims=True)
    r2 = jnp.sum(jnp.where(sel2, excl, 0.0), axis=-1, keepdims=True)
    total = base + jnp.sum(jnp.where(sel1 | sel2, 1.0, 0.0), axis=0, keepdims=True)
    carry[...] = jnp.broadcast_to(total, carry.shape)
    cnt_ref[...] = jnp.broadcast_to(total, cnt_ref.shape)

    meta_ref[...] = (jnp.where(lane == 0, i1, 0.0) + jnp.where(lane == 1, i2, 0.0)
                     + jnp.where(lane == 2, r1, 0.0) + jnp.where(lane == 3, r2, 0.0))
    w_ref[...] = jnp.where(lane == 0, w1, 0.0) + jnp.where(lane == 1, w2, 0.0)


def _route(logits, tm=512):
    t = logits.shape[0]
    return pl.pallas_call(
        _route_kernel,
        grid=(t // tm,),
        in_specs=[pl.BlockSpec((tm, LANES), lambda i: (i, 0))],
        out_specs=[pl.BlockSpec((tm, LANES), lambda i: (i, 0)),
                   pl.BlockSpec((tm, LANES), lambda i: (i, 0)),
                   pl.BlockSpec((SUBLANES, LANES), lambda i: (0, 0))],
        out_shape=[jax.ShapeDtypeStruct((t, LANES), F32),
                   jax.ShapeDtypeStruct((t, LANES), F32),
                   jax.ShapeDtypeStruct((SUBLANES, LANES), F32)],
        scratch_shapes=[pltpu.VMEM((SUBLANES, LANES), F32)],
        compiler_params=_cparams(("arbitrary",)),
        name="route_topk",
    )(logits)


def _lane_cumsum(x):
    lane = lax.broadcasted_iota(I32, x.shape, 1)
    sh = 1
    while sh < LANES:
        x = x + jnp.where(lane >= sh, pltpu.roll(x, sh, axis=1), 0)
        sh *= 2
    return x


def _dest_kernel(meta_ref, cnt_ref, dest_ref, be_ref, *, nb):
    tm = meta_ref.shape[0]
    shift = MOE_BLOCK.bit_length() - 1
    cnt = cnt_ref[...].astype(I32)
    padded = ((cnt + (MOE_BLOCK - 1)) >> shift) << shift
    pend_i = _lane_cumsum(padded)
    pend = pend_i.astype(F32)[0:1, :]
    poff = (pend_i - padded).astype(F32)[0:1, :]
    meta = meta_ref[...]
    lane = lax.broadcasted_iota(I32, (tm, LANES), 1).astype(F32)
    pick = lambda col: jnp.sum(jnp.where(lane == col, meta, 0.0), axis=-1, keepdims=True)
    i1, i2, r1, r2 = pick(0), pick(1), pick(2), pick(3)
    d1 = jnp.sum(jnp.where(lane == i1, poff, 0.0), axis=-1, keepdims=True) + r1
    d2 = jnp.sum(jnp.where(lane == i2, poff, 0.0), axis=-1, keepdims=True) + r2
    dest_ref[...] = (jnp.where(lane == 0, d1, 0.0) + jnp.where(lane == 1, d2, 0.0)).astype(I32)

    nrow = be_ref.shape[0]
    blk = lax.broadcasted_iota(I32, (nrow, LANES), 0).astype(F32)
    lane_b = lax.broadcasted_iota(I32, (nrow, LANES), 1).astype(F32)
    is_e = (lane_b >= ROUTE_LANE0) & (lane_b < ROUTE_LANE0 + N_EXPERTS)
    below = jnp.sum(jnp.where(is_e & (pend <= blk * MOE_BLOCK), 1.0, 0.0), axis=-1, keepdims=True)
    block_e = jnp.minimum(below, N_EXPERTS - 1.0)
    used = jnp.sum(jnp.where(lane_b == ROUTE_LANE0 + N_EXPERTS - 1, pend, 0.0),
                   axis=-1, keepdims=True) * (1.0 / MOE_BLOCK)
    be = jnp.where(blk[:, :1] == nb, used, block_e)
    be_ref[...] = jnp.broadcast_to(be, (nrow, LANES)).astype(I32)


def _dest(meta, cnt, nb, tm=512):
    t = meta.shape[0]
    nrow = -(-(nb + 1) // SUBLANES) * SUBLANES
    kern = functools.partial(_dest_kernel, nb=nb)
    return pl.pallas_call(
        kern,
        grid=(t // tm,),
        in_specs=[pl.BlockSpec((tm, LANES), lambda i: (i, 0)),
                  pl.BlockSpec((SUBLANES, LANES), lambda i: (0, 0))],
        out_specs=[pl.BlockSpec((tm, LANES), lambda i: (i, 0)),
                   pl.BlockSpec((nrow, LANES), lambda i: (0, 0))],
        out_shape=[jax.ShapeDtypeStruct((t, LANES), I32),
                   jax.ShapeDtypeStruct((nrow, LANES), I32)],
        compiler_params=_cparams(("arbitrary",)),
        name="route_dest",
    )(meta, cnt)


def _invert_kernel(dest_ref, slot_ref):
    n_slot = slot_ref.shape[0]
    n_asg = dest_ref.shape[0]

    def fill(s, c):
        slot_ref[s] = jnp.int32(-1)
        return c

    lax.fori_loop(0, n_slot, fill, 0, unroll=8)

    def put(a, c):
        slot_ref[dest_ref[a]] = a
        return c

    lax.fori_loop(0, n_asg, put, 0, unroll=8)


def _invert(dest_flat, n_slot):
    return pl.pallas_call(
        _invert_kernel,
        in_specs=[pl.BlockSpec(memory_space=pltpu.SMEM)],
        out_specs=pl.BlockSpec(memory_space=pltpu.SMEM),
        out_shape=jax.ShapeDtypeStruct((n_slot,), I32),
        name="route_invert",
    )(dest_flat)


def _moe_up_kernel(slot_ref, be_ref, h2_hbm, wg_ref, wu_ref, o_ref, wbuf, xbuf, sem, *, nb):
    b = pl.program_id(0)
    used = be_ref[nb]
    blk = MOE_BLOCK
    fe = wg_ref.shape[2]

    def gather(block, slot):
        def issue(r, c):
            a = slot_ref[block * blk + r]
            tok = lax.shift_right_logical(jnp.maximum(a, 0), 1)
            pltpu.make_async_copy(h2_hbm.at[pl.ds(tok, 1)], xbuf.at[slot, pl.ds(r, 1)],
                                  sem.at[slot]).start()
            return c
        lax.fori_loop(0, blk, issue, 0, unroll=8)

    @pl.when((b == 0) & (used > 0))
    def _():
        gather(0, 0)

    @pl.when(b + 1 < used)
    def _():
        gather(b + 1, (b + 1) % 2)

    new_expert = (b == 0) | (be_ref[b] != be_ref[jnp.maximum(b - 1, 0)])

    @pl.when(new_expert & (b < used))
    def _():
        wbuf[:, :fe] = wg_ref[0].astype(BF16)
        wbuf[:, fe:] = wu_ref[0].astype(BF16)

    @pl.when(b < used)
    def _():
        slot = b % 2
        pltpu.make_async_copy(xbuf.at[slot], xbuf.at[slot], sem.at[slot]).wait()
        xb = xbuf[slot].astype(BF16)
        gu = jnp.dot(xb, wbuf[...], preferred_element_type=F32)
        o_ref[...] = (_silu(gu[:, :fe]) * gu[:, fe:]).astype(BF16)

    @pl.when(b >= used)
    def _():
        o_ref[...] = jnp.zeros(o_ref.shape, BF16)


def _moe_up(slot_a, block_e, h2, w_gate, w_up, nb):
    t, d = h2.shape
    fe = w_gate.shape[2]
    kern = functools.partial(_moe_up_kernel, nb=nb)
    grid_spec = pltpu.PrefetchScalarGridSpec(
        num_scalar_prefetch=2,
        grid=(nb,),
        in_specs=[pl.BlockSpec(memory_space=pl.ANY),
                  pl.BlockSpec((1, d, fe), lambda b, s, e: (e[b], 0, 0)),
                  pl.BlockSpec((1, d, fe), lambda b, s, e: (e[b], 0, 0))],
        out_specs=pl.BlockSpec((MOE_BLOCK, fe), lambda b, s, e: (b, 0)),
        scratch_shapes=[pltpu.VMEM((d, 2 * fe), BF16),
                        pltpu.VMEM((2, MOE_BLOCK, d), F32),
                        pltpu.SemaphoreType.DMA((2,))],
    )
    return pl.pallas_call(
        kern,
        grid_spec=grid_spec,
        out_shape=jax.ShapeDtypeStruct((nb * MOE_BLOCK, fe), BF16),
        compiler_params=_cparams(("arbitrary",)),
        name="moe_up",
    )(slot_a, block_e, h2, w_gate, w_up)


def _moe_down_kernel(slot_ref, be_ref, hm_ref, wd_ref, ys_hbm, wbuf, obuf, sem, *, nb, t):
    b = pl.program_id(0)
    used = be_ref[nb]
    blk = MOE_BLOCK
    d = wd_ref.shape[2]

    def wait_slot(slot):
        pltpu.make_async_copy(obuf.at[slot], obuf.at[slot], sem.at[slot]).wait()

    @pl.when(b == 0)
    def _():
        obuf[1] = jnp.zeros(obuf.shape[1:], F32)
        for half in range(2):
            pltpu.make_async_copy(obuf.at[1], ys_hbm.at[pl.ds(t, blk), pl.ds(half * d, d)],
                                  sem.at[2]).start()
        for half in range(2):
            pltpu.make_async_copy(obuf.at[1], ys_hbm.at[pl.ds(t, blk), pl.ds(half * d, d)],
                                  sem.at[2]).wait()

    @pl.when((b >= 2) & (b - 2 < used))
    def _():
        wait_slot(b % 2)

    new_expert = (b == 0) | (be_ref[b] != be_ref[jnp.maximum(b - 1, 0)])

    @pl.when(new_expert & (b < used))
    def _():
        wbuf[...] = wd_ref[0].astype(BF16)

    @pl.when(b < used)
    def _():
        slot = b % 2
        obuf[slot] = jnp.dot(hm_ref[...], wbuf[...], preferred_element_type=F32)

        def issue(r, c):
            a = slot_ref[b * blk + r]
            a = jnp.where(a >= 0, a, 2 * t + slot * blk + r)
            row = lax.shift_right_logical(a, 1)
            col = (a & 1) * d
            pltpu.make_async_copy(obuf.at[slot, pl.ds(r, 1)],
                                  ys_hbm.at[pl.ds(row, 1), pl.ds(pl.multiple_of(col, d), d)],
                                  sem.at[slot]).start()
            return c
        lax.fori_loop(0, blk, issue, 0, unroll=8)

    @pl.when(b == nb - 1)
    def _():
        @pl.when((b >= 1) & (b - 1 < used))
        def _():
            wait_slot((b - 1) % 2)

        @pl.when(b < used)
        def _():
            wait_slot(b % 2)


def _moe_down(slot_a, block_e, hmid, w_down, nb, t):
    fe, d = w_down.shape[1], w_down.shape[2]
    kern = functools.partial(_moe_down_kernel, nb=nb, t=t)
    grid_spec = pltpu.PrefetchScalarGridSpec(
        num_scalar_prefetch=2,
        grid=(nb,),
        in_specs=[pl.BlockSpec((MOE_BLOCK, fe), lambda b, s, e: (b, 0)),
                  pl.BlockSpec((1, fe, d), lambda b, s, e: (e[b], 0, 0))],
        out_specs=pl.BlockSpec(memory_space=pl.ANY),
        scratch_shapes=[pltpu.VMEM((fe, d), BF16),
                        pltpu.VMEM((2, MOE_BLOCK, d), F32),
                        pltpu.SemaphoreType.DMA((3,))],
    )
    return pl.pallas_call(
        kern,
        grid_spec=grid_spec,
        out_shape=jax.ShapeDtypeStruct((t + MOE_BLOCK, 2 * d), F32),
        compiler_params=_cparams(("arbitrary",)),
        name="moe_down",
    )(slot_a, block_e, hmid, w_down)


def _final_kernel(x1_ref, ys_ref, w_ref, mod_ref, gf_ref, o_ref):
    d = x1_ref.shape[1]
    w = w_ref[...]
    y = w[:, 0:1] * ys_ref[:, :d] + w[:, 1:2] * ys_ref[:, d:]
    x2 = x1_ref[...] + mod_ref[0][5:6, :] * y
    o_ref[...] = x2 * lax.rsqrt(jnp.mean(x2 * x2, axis=-1, keepdims=True) + NORM_EPS) * gf_ref[...]


def _final(x1, ys, w, mod3, g_final, seq, tm=512):
    t, d = x1.shape
    tpb = seq // tm
    return pl.pallas_call(
        _final_kernel,
        grid=(t // tm,),
        in_specs=[pl.BlockSpec((tm, d), lambda i: (i, 0)),
                  pl.BlockSpec((tm, 2 * d), lambda i: (i, 0)),
                  pl.BlockSpec((tm, LANES), lambda i: (i, 0)),
                  pl.BlockSpec((1, N_MOD, d), lambda i: (i // tpb, 0, 0)),
                  pl.BlockSpec((1, d), lambda i: (0, 0))],
        out_specs=pl.BlockSpec((tm, d), lambda i: (i, 0)),
        out_shape=jax.ShapeDtypeStruct((t, d), F32),
        compiler_params=_cparams(("arbitrary",)),
        name="combine_final_norm",
    )(x1, ys, w, mod3, g_final.reshape(1, d))


def _rope_tables(seq, dh):
    half = dh // 2
    inv_freq = 1.0 / (ROPE_BASE ** (jnp.arange(half, dtype=F32) * 2.0 / dh))
    ang = jnp.arange(seq, dtype=F32)[:, None] * inv_freq[None, :]
    cos, sin = jnp.cos(ang), jnp.sin(ang)
    return jnp.concatenate([cos, cos], axis=1), jnp.concatenate([-sin, sin], axis=1)


def _layer(x, mod, g_norm1, w_in, ssm_a_re, ssm_a_im, ssm_b_re, ssm_b_im, ssm_c_re, ssm_c_im,
           ssm_d, ssm_log_dt, w_glu, beta_ssm, beta_ret, w_out, g_norm2, w_rg, b_rg, w_re, b_re,
           w_gate, w_up, w_down):
    bn, seq, d = x.shape
    t = bn * seq
    x2d = x.reshape(t, d)
    mod3 = mod.reshape(bn, N_MOD, d)

    sw = w_glu.shape[0]
    u_wide, rest = _inproj(x2d, mod3, g_norm1, w_in.astype(BF16), bn, seq, sw)
    u_tb = u_wide.reshape(seq * bn, sw)

    ab_re, ab_im, bbt_re, bbt_im = _s5_disc(ssm_a_re, ssm_a_im, ssm_log_dt,
                                            jnp.swapaxes(ssm_b_re, 1, 2), jnp.swapaxes(ssm_b_im, 1, 2))
    bz, cz, a_re, a_im = _s5_pack(ab_re, ab_im, bbt_re, bbt_im, ssm_c_re, ssm_c_im)
    ypre_tb = _s5_scan(u_tb, bz, cz, a_re, a_im, ssm_d.reshape(-1), bn, seq)
    ypre_wide = ypre_tb.reshape(seq, bn * sw)

    heads = (d - sw) // RET_HEAD_DIM
    cosf, sinf = _rope_tables(seq, RET_HEAD_DIM)
    gamma = 1.0 - jnp.exp2(-5.0 - jnp.arange(heads, dtype=F32))
    log_g = jnp.broadcast_to(jnp.log(gamma)[:, None, None], (heads, 1, LANES))
    yret = _retention(rest.reshape(bn, seq, rest.shape[1]), cosf, sinf, log_g, beta_ret)

    n_route = N_GROUPS + N_EXPERTS
    wr = jnp.concatenate([w_rg, w_re, jnp.zeros((d, LANES - n_route), F32)], axis=1)
    br = jnp.concatenate([b_rg, b_re, jnp.zeros((LANES - n_route,), F32)]).reshape(1, LANES)
    wr_hi = wr.astype(BF16)
    wr_lo = (wr - wr_hi.astype(F32)).astype(BF16)
    x1, h2, logits = _mix(ypre_wide, yret.reshape(t, d - sw), x2d, mod3, w_glu.astype(BF16),
                          beta_ssm, w_out.astype(BF16), g_norm2, wr_hi, wr_lo, br, bn, seq)

    nb = -(-(t * 2) // MOE_BLOCK) + N_EXPERTS
    meta, w_tok, cnt = _route(logits)
    dest, be = _dest(meta, cnt, nb)
    slot_a = _invert(dest[:, :2].reshape(-1), nb * MOE_BLOCK)
    block_e = be[:nb + 1, 0]

    hmid = _moe_up(slot_a, block_e, h2, w_gate, w_up, nb)
    ys = _moe_down(slot_a, block_e, hmid, w_down, nb, t)
    return x1, ys, w_tok, mod3


def kernel(x, c, w_ada, b_ada, g_norm1, w_in, ssm_a_re, ssm_a_im, ssm_b_re, ssm_b_im, ssm_c_re,
           ssm_c_im, ssm_d, ssm_log_dt, w_glu, beta_ssm, beta_ret, w_out, g_norm2, w_router_group,
           b_router_group, w_router_expert, b_router_expert, w_gate, w_up, w_down, g_final):
    depth = w_ada.shape[0]
    bn, seq, d = x.shape
    assert depth == 1, "the final norm is fused into the single layer's last kernel"
    l = 0
    mod = _ada(c, w_ada[l], b_ada[l])
    x1, ys, w_tok, mod3 = _layer(
        x, mod, g_norm1[l], w_in[l], ssm_a_re[l], ssm_a_im[l], ssm_b_re[l], ssm_b_im[l],
        ssm_c_re[l], ssm_c_im[l], ssm_d[l], ssm_log_dt[l], w_glu[l], beta_ssm[l], beta_ret[l],
        w_out[l], g_norm2[l], w_router_group[l], b_router_group[l], w_router_expert[l],
        b_router_expert[l], w_gate[l], w_up[l], w_down[l])
    out = _final(x1, ys, w_tok, mod3, g_final, seq)
    return out.reshape(bn, seq, d)
```

```python
import functools
import math

import jax
import jax.numpy as jnp
from jax import lax
from jax.experimental import pallas as pl
from jax.experimental.pallas import tpu as pltpu

F32 = jnp.float32
BF16 = jnp.bfloat16
I32 = jnp.int32

SSM_GROUP = 16
SSM_STATE = 64
RET_HEAD_DIM = 128
ROPE_BASE = 10000.0
N_GROUPS = 4
EXPERTS_PER_GROUP = 8
N_EXPERTS = N_GROUPS * EXPERTS_PER_GROUP
N_MOD = 6
NORM_EPS = 1e-6
GN_EPS = 1e-5

LANES = 128
SUBLANES = 8
VMEM_LIMIT = 56 * 1024 * 1024

ROUTE_LANE0 = N_GROUPS
MOE_BLOCK = 256
MOE_NCHUNK_COLS = 256
RET_CHUNK = 256
S5_TT = 128


def _cparams(sem):
    return pltpu.CompilerParams(dimension_semantics=sem, vmem_limit_bytes=VMEM_LIMIT)


def _silu(x):
    return x * jax.nn.sigmoid(x)


def _ada_kernel(c_ref, w_ref, b_ref, o_ref):
    s = _silu(c_ref[...])
    o_ref[...] = jnp.dot(s.astype(BF16), w_ref[...].astype(BF16),
                         preferred_element_type=F32) + b_ref[...]


def _ada(c, w, b, tn=1024):
    bn, d = c.shape
    n = w.shape[1]
    return pl.pallas_call(
        _ada_kernel,
        grid=(n // tn,),
        in_specs=[pl.BlockSpec((bn, d), lambda j: (0, 0)),
                  pl.BlockSpec((d, tn), lambda j: (0, j)),
                  pl.BlockSpec((1, tn), lambda j: (0, j))],
        out_specs=pl.BlockSpec((bn, tn), lambda j: (0, j)),
        out_shape=jax.ShapeDtypeStruct((bn, n), F32),
        compiler_params=_cparams(("arbitrary",)),
        name="ada_mod",
    )(c, w, b.reshape(1, n))


def _inproj_kernel(x_ref, mod_ref, g_ref, w_ref, u_ref, r_ref, h_scr):
    j = pl.program_id(1)

    @pl.when(j == 0)
    def _():
        x = x_ref[...]
        y = x * lax.rsqrt(jnp.mean(x * x, axis=-1, keepdims=True) + NORM_EPS) * g_ref[...]
        m = mod_ref[0]
        h_scr[...] = (y * (1.0 + m[1:2, :]) + m[0:1, :]).astype(BF16)

    acc = jnp.dot(h_scr[...], w_ref[...], preferred_element_type=F32).astype(BF16)

    @pl.when(j == 0)
    def _():
        u_ref[...] = acc

    @pl.when(j > 0)
    def _():
        r_ref[...] = acc


def _inproj(x2d, mod3, g1, w_in_bf, bn, seq, sw, tm=1024):
    t, d = x2d.shape
    n = w_in_bf.shape[1]
    tn = sw
    tm = min(tm, seq)
    tpb = seq // tm
    return pl.pallas_call(
        _inproj_kernel,
        grid=(t // tm, n // tn),
        in_specs=[pl.BlockSpec((tm, d), lambda i, j: (i, 0)),
                  pl.BlockSpec((1, N_MOD, d), lambda i, j: (i // tpb, 0, 0)),
                  pl.BlockSpec((1, d), lambda i, j: (0, 0)),
                  pl.BlockSpec((d, tn), lambda i, j: (0, j))],
        out_specs=[pl.BlockSpec((tm, tn), lambda i, j: (i % tpb, i // tpb)),
                   pl.BlockSpec((tm, tn), lambda i, j: (i, jnp.maximum(j - 1, 0)))],
        out_shape=[jax.ShapeDtypeStruct((seq, bn * tn), BF16),
                   jax.ShapeDtypeStruct((t, n - tn), BF16)],
        scratch_shapes=[pltpu.VMEM((tm, d), BF16)],
        compiler_params=_cparams(("arbitrary", "arbitrary")),
        name="norm_inproj",
    )(x2d, mod3, g1.reshape(1, d), w_in_bf)


def _s5_disc_kernel(are_ref, aim_ref, ldt_ref, bre_ref, bim_ref,
                    abre_ref, abim_ref, bbre_ref, bbim_ref):
    a_re = are_ref[...]
    a_im = aim_ref[...]
    dt = jnp.exp(ldt_ref[...])
    mag = jnp.exp(dt * a_re)
    ab_re = mag * jnp.cos(dt * a_im)
    ab_im = mag * jnp.sin(dt * a_im)
    inv_abs2 = 1.0 / (a_re * a_re + a_im * a_im)
    n_re = ab_re - 1.0
    f_re = (n_re * a_re + ab_im * a_im) * inv_abs2
    f_im = (ab_im * a_re - n_re * a_im) * inv_abs2
    abre_ref[...] = ab_re
    abim_ref[...] = ab_im
    b_re = bre_ref[...]
    b_im = bim_ref[...]
    fr = f_re[:, None, :]
    fi = f_im[:, None, :]
    bbre_ref[...] = fr * b_re - fi * b_im
    bbim_ref[...] = fr * b_im + fi * b_re


def _s5_disc(a_re, a_im, log_dt, bt_re, bt_im):
    g, p = a_re.shape
    h = bt_re.shape[1]
    return pl.pallas_call(
        _s5_disc_kernel,
        out_shape=[jax.ShapeDtypeStruct((g, p), F32), jax.ShapeDtypeStruct((g, p), F32),
                   jax.ShapeDtypeStruct((g, h, p), F32), jax.ShapeDtypeStruct((g, h, p), F32)],
        name="s5_discretise",
    )(a_re, a_im, log_dt.reshape(g, 1), bt_re, bt_im)


def _s5_pack(ab_re, ab_im, bbt_re, bbt_im, c_re, c_im):
    g, h, p = bbt_re.shape
    npair = g // 2
    ppb = LANES // (2 * h)
    eye2 = jnp.eye(2, dtype=F32)
    qsel = jax.nn.one_hot(jnp.arange(npair) % ppb, ppb, dtype=F32)

    def in_mat(bbt):
        b4 = bbt.reshape(npair, 2, h, p)
        return jnp.einsum('ab,xahp->xahbp', eye2, b4).reshape(npair, 2 * h, 2 * p)

    bsmall = jnp.concatenate([in_mat(bbt_re), in_mat(bbt_im)], axis=-1)
    bz = jnp.einsum('xq,xrc->xqrc', qsel, bsmall).reshape(npair, LANES, 4 * p)

    def out_mat(c):
        c4 = jnp.swapaxes(c, 1, 2).reshape(npair, 2, p, h)
        return jnp.einsum('ab,xaph->xapbh', eye2, c4).reshape(npair, 2 * p, 2 * h)

    csmall = jnp.concatenate([out_mat(c_re), -out_mat(c_im)], axis=1)
    cz = jnp.einsum('xq,xrc->xrqc', qsel, csmall).reshape(npair, 4 * p, LANES)
    a_re = ab_re.reshape(npair // ppb, ppb, 2 * p)
    a_im = ab_im.reshape(npair // ppb, ppb, 2 * p)
    return bz.astype(BF16), cz.astype(BF16), a_re, a_im


def _s5_kernel(u_ref, bz_ref, cz_ref, are_ref, aim_ref, d_ref, o_ref, state, buf, *, bn, tt, ppb):
    t = pl.program_id(0)
    kb = pl.program_id(1)

    @pl.when(t == 0)
    def _():
        state[kb] = jnp.zeros(state.shape[1:], F32)

    u = u_ref[...]
    for q in range(ppb):
        bu = jnp.dot(u, bz_ref[q], preferred_element_type=F32)
        buf[2 * q] = bu[:, :LANES]
        buf[2 * q + 1] = bu[:, LANES:]

    a_re = [jnp.broadcast_to(are_ref[0, q:q + 1, :], (bn, LANES)) for q in range(ppb)]
    a_im = [jnp.broadcast_to(aim_ref[0, q:q + 1, :], (bn, LANES)) for q in range(ppb)]
    s0 = tuple(state[kb, j] for j in range(2 * ppb))

    def step(i, s):
        rows = pl.ds(pl.multiple_of(i * bn, bn), bn)
        new = []
        for q in range(ppb):
            s_re, s_im = s[2 * q], s[2 * q + 1]
            n_re = a_re[q] * s_re - a_im[q] * s_im + buf[2 * q, rows, :]
            n_im = a_re[q] * s_im + a_im[q] * s_re + buf[2 * q + 1, rows, :]
            buf[2 * q, rows, :] = n_re
            buf[2 * q + 1, rows, :] = n_im
            new += [n_re, n_im]
        return tuple(new)

    s1 = lax.fori_loop(0, tt, step, s0, unroll=4)
    for j in range(2 * ppb):
        state[kb, j] = s1[j]

    acc = d_ref[...] * u.astype(F32)
    for q in range(ppb):
        lhs = jnp.concatenate([buf[2 * q], buf[2 * q + 1]], axis=1).astype(BF16)
        acc = acc + jnp.dot(lhs, cz_ref[q], preferred_element_type=F32)
    o_ref[...] = acc


def _s5_scan(u_tb, bz, cz, a_re, a_im, d_skip, bn, seq, tt=S5_TT):
    rows, width = u_tb.shape
    nkb = width // LANES
    ppb = bz.shape[0] // nkb
    rt = tt * bn
    kern = functools.partial(_s5_kernel, bn=bn, tt=tt, ppb=ppb)
    return pl.pallas_call(
        kern,
        grid=(seq // tt, nkb),
        in_specs=[pl.BlockSpec((rt, LANES), lambda t, k: (t, k)),
                  pl.BlockSpec((ppb,) + bz.shape[1:], lambda t, k: (k, 0, 0)),
                  pl.BlockSpec((ppb,) + cz.shape[1:], lambda t, k: (k, 0, 0)),
                  pl.BlockSpec((1,) + a_re.shape[1:], lambda t, k: (k, 0, 0)),
                  pl.BlockSpec((1,) + a_im.shape[1:], lambda t, k: (k, 0, 0)),
                  pl.BlockSpec((1, LANES), lambda t, k: (0, k))],
        out_specs=pl.BlockSpec((rt, LANES), lambda t, k: (t, k)),
        out_shape=jax.ShapeDtypeStruct((rows, width), F32),
        scratch_shapes=[pltpu.VMEM((nkb, 2 * ppb, bn, LANES), F32),
                        pltpu.VMEM((2 * ppb, rt, LANES), F32)],
        compiler_params=_cparams(("arbitrary", "arbitrary")),
        name="s5_scan",
    )(u_tb, bz, cz, a_re, a_im, d_skip.reshape(1, width))


def _ret_kernel(q_ref, k_ref, v_ref, g_ref, cq_ref, sq_ref, ck_ref, sk_ref, lg_ref, beta_ref, o_ref,
                q_scr, k_scr, kzt_scr, y_scr, *, chunk):
    seq, dh = q_ref.shape[1], q_ref.shape[2]
    lg = lg_ref[0][:, :1]
    ri = lax.broadcasted_iota(I32, (chunk, chunk), 0)
    ci = lax.broadcasted_iota(I32, (chunk, chunk), 1)
    rel = (ri - ci).astype(F32)
    decay = jnp.where(rel >= 0, jnp.exp(lg * jnp.maximum(rel, 0.0)), 0.0)
    idx = lax.broadcasted_iota(I32, (chunk, 1), 0).astype(F32)
    xi = jnp.exp(lg * (idx + 1.0))
    g_chunk = jnp.exp(lg * float(chunk))
    beta = beta_ref[...]
    half = dh // 2
    pr = lax.broadcasted_iota(I32, (dh, dh), 0)
    pc = lax.broadcasted_iota(I32, (dh, dh), 1)
    swap = jnp.where(((pr + half) & (dh - 1)) == pc, 1.0, 0.0).astype(BF16)

    def rope(x_ref, cos_ref, sin_ref, sl):
        xb = x_ref[0, sl, :]
        rolled = jnp.dot(xb, swap, preferred_element_type=F32).astype(BF16)
        return xb * cos_ref[sl, :] + rolled * sin_ref[sl, :]

    zeta = jnp.exp(lg * (chunk - 1.0 - idx))
    for n in range(seq // chunk):
        sl = slice(n * chunk, (n + 1) * chunk)
        q_scr[sl, :] = rope(q_ref, cq_ref, sq_ref, sl)
        kb = rope(k_ref, ck_ref, sk_ref, sl)
        k_scr[sl, :] = kb
        kzt_scr[:, sl] = (kb.astype(F32) * zeta).T.astype(BF16)

    decay = decay.astype(BF16)
    r = jnp.zeros((dh, dh), F32)
    for n in range(seq // chunk):
        sl = slice(n * chunk, (n + 1) * chunk)
        qb = q_scr[sl, :]
        v = v_ref[0, sl, :]
        s = lax.dot_general(qb, k_scr[sl, :], (((1,), (1,)), ((), ())),
                            preferred_element_type=F32).astype(BF16) * decay
        y = jnp.dot(s, v, preferred_element_type=F32)
        y_scr[sl, :] = y + jnp.dot(qb, r.astype(BF16), preferred_element_type=F32) * xi
        r = r * g_chunk + jnp.dot(kzt_scr[:, sl], v, preferred_element_type=F32)

    y = y_scr[...]
    mu = jnp.mean(y, axis=-1, keepdims=True)
    yc = y - mu
    var = jnp.mean(yc * yc, axis=-1, keepdims=True)
    yn = yc * lax.rsqrt(var + GN_EPS) * beta
    o_ref[0] = (_silu(g_ref[0].astype(F32)) * yn).astype(BF16)


def _retention(rest3, cosf, sinf, log_g, beta_ret, chunk=RET_CHUNK):
    bn, seq, _ = rest3.shape
    dh = RET_HEAD_DIM
    heads = beta_ret.shape[0] // dh
    blk = lambda off: pl.BlockSpec((1, seq, dh), lambda b, h, off=off: (b, 0, off + h))
    k_scale = dh ** -0.5
    kern = functools.partial(_ret_kernel, chunk=chunk)
    return pl.pallas_call(
        kern,
        grid=(bn, heads),
        in_specs=[blk(0), blk(heads), blk(2 * heads), blk(3 * heads),
                  pl.BlockSpec((seq, dh), lambda b, h: (0, 0)),
                  pl.BlockSpec((seq, dh), lambda b, h: (0, 0)),
                  pl.BlockSpec((seq, dh), lambda b, h: (0, 0)),
                  pl.BlockSpec((seq, dh), lambda b, h: (0, 0)),
                  pl.BlockSpec((1, 1, LANES), lambda b, h: (h, 0, 0)),
                  pl.BlockSpec((1, dh), lambda b, h: (0, h))],
        out_specs=pl.BlockSpec((1, seq, dh), lambda b, h: (b, 0, h)),
        out_shape=jax.ShapeDtypeStruct((bn, seq, heads * dh), BF16),
        scratch_shapes=[pltpu.VMEM((seq, dh), BF16), pltpu.VMEM((seq, dh), BF16),
                        pltpu.VMEM((dh, seq), BF16), pltpu.VMEM((seq, dh), F32)],
        compiler_params=_cparams(("arbitrary", "arbitrary")),
        name="retention",
    )(rest3, rest3, rest3, rest3, cosf.astype(BF16), sinf.astype(BF16),
      (cosf * k_scale).astype(BF16), (sinf * k_scale).astype(BF16),
      log_g, beta_ret.reshape(1, heads * dh))


def _gelu_tanh(x):
    return 0.5 * x * (1.0 + jnp.tanh(math.sqrt(2.0 / math.pi) * (x + 0.044715 * (x * x * x))))


def _split_bf16(x):
    hi = x.astype(BF16)
    lo = (x - hi.astype(F32)).astype(BF16)
    return hi, lo


def _mix_kernel(ys_ref, yr_ref, x_ref, mod_ref, wglu_ref, bssm_ref, wout_ref, g2_ref,
                wrh_ref, wrl_ref, br_ref, x1_ref, h2_ref, lg_ref):
    sw = ys_ref.shape[1]
    z = _gelu_tanh(ys_ref[...])
    gate = jax.nn.sigmoid(jnp.dot(z.astype(BF16), wglu_ref[...], preferred_element_type=F32))
    o = z * gate
    y_ssm = o * lax.rsqrt(jnp.mean(o * o, axis=-1, keepdims=True) + NORM_EPS) * bssm_ref[...]
    mixed = (jnp.dot(y_ssm.astype(BF16), wout_ref[:sw, :], preferred_element_type=F32)
             + jnp.dot(yr_ref[...], wout_ref[sw:, :], preferred_element_type=F32))
    m = mod_ref[0]
    x1 = x_ref[...] + m[2:3, :] * mixed
    x1_ref[...] = x1
    y = x1 * lax.rsqrt(jnp.mean(x1 * x1, axis=-1, keepdims=True) + NORM_EPS) * g2_ref[...]
    h2 = y * (1.0 + m[4:5, :]) + m[3:4, :]
    h2_ref[...] = h2
    hi, lo = _split_bf16(h2)
    wrh = wrh_ref[...]
    lg_ref[...] = (jnp.dot(hi, wrh, preferred_element_type=F32)
                   + jnp.dot(lo, wrh, preferred_element_type=F32)
                   + jnp.dot(hi, wrl_ref[...], preferred_element_type=F32)
                   + br_ref[...])


def _mix(ypre_tb, yret, x2d, mod3, wglu_bf, beta_ssm, wout_bf, g2, wr_hi, wr_lo, br, bn, seq, tm=256):
    t, d = x2d.shape
    sw = wglu_bf.shape[0]
    tpb = seq // tm
    const = lambda shape: pl.BlockSpec(shape, lambda i: (0,) * len(shape),
                                       pipeline_mode=pl.Buffered(1))
    return pl.pallas_call(
        _mix_kernel,
        grid=(t // tm,),
        in_specs=[pl.BlockSpec((tm, sw), lambda i: (i % tpb, i // tpb)),
                  pl.BlockSpec((tm, d - sw), lambda i: (i, 0)),
                  pl.BlockSpec((tm, d), lambda i: (i, 0)),
                  pl.BlockSpec((1, N_MOD, d), lambda i: (i // tpb, 0, 0)),
                  const((sw, sw)), const((1, sw)), const((d, d)), const((1, d)),
                  const((d, LANES)), const((d, LANES)), const((1, LANES))],
        out_specs=[pl.BlockSpec((tm, d), lambda i: (i, 0)),
                   pl.BlockSpec((tm, d), lambda i: (i, 0)),
                   pl.BlockSpec((tm, LANES), lambda i: (i, 0))],
        out_shape=[jax.ShapeDtypeStruct((t, d), F32),
                   jax.ShapeDtypeStruct((t, d), F32),
                   jax.ShapeDtypeStruct((t, LANES), F32)],
        compiler_params=_cparams(("arbitrary",)),
        name="mix_outproj_router",
    )(ypre_tb, yret, x2d, mod3, wglu_bf, beta_ssm.reshape(1, sw), wout_bf, g2.reshape(1, d),
      wr_hi, wr_lo, br)


def _route_kernel(lg_ref, meta_ref, w_ref, cnt_ref, carry):
    i = pl.program_id(0)
    tm = lg_ref.shape[0]

    @pl.when(i == 0)
    def _():
        carry[...] = jnp.zeros(carry.shape, F32)

    lg = lg_ref[...]
    lane = lax.broadcasted_iota(I32, (tm, LANES), 1).astype(F32)
    neg = jnp.float32(-jnp.inf)

    def first_max(vals):
        m = jnp.max(vals, axis=-1, keepdims=True)
        idx = jnp.min(jnp.where(vals == m, lane, float(LANES)), axis=-1, keepdims=True)
        return m, idx

    gl = jnp.where(lane < N_GROUPS, lg, neg)
    gmax, gsel = first_max(gl)
    g_w = 1.0 / jnp.sum(jnp.exp(gl - gmax), axis=-1, keepdims=True)
    lo = ROUTE_LANE0 + gsel * EXPERTS_PER_GROUP
    el = jnp.where((lane >= lo) & (lane < lo + EXPERTS_PER_GROUP), lg, neg)
    m1, i1 = first_max(el)
    m2, i2 = first_max(jnp.where(lane == i1, neg, el))
    e21 = jnp.exp(m2 - m1)
    w1 = g_w / (1.0 + e21)
    w2 = g_w * e21 / (1.0 + e21)

    sel1 = lane == i1
    sel2 = lane == i2
    onehot = jnp.where(sel1 | sel2, 1.0, 0.0).astype(BF16)
    ri = lax.broadcasted_iota(I32, (tm, tm), 0)
    ci = lax.broadcasted_iota(I32, (tm, tm), 1)
    tri = jnp.where(ci < ri, 1.0, 0.0).astype(BF16)
    base = carry[0:1, :]
    excl = jnp.dot(tri, onehot, preferred_element_type=F32) + base
    r1 = jnp.sum(jnp.where(sel1, excl, 0.0), axis=-1, keepdims=True)
    r2 = jnp.sum(jnp.where(sel2, excl, 0.0), axis=-1, keepdims=True)
    total = base + jnp.sum(jnp.where(sel1 | sel2, 1.0, 0.0), axis=0, keepdims=True)
    carry[...] = jnp.broadcast_to(total, carry.shape)
    cnt_ref[...] = jnp.broadcast_to(total, cnt_ref.shape)

    meta_ref[...] = (jnp.where(lane == 0, i1, 0.0) + jnp.where(lane == 1, i2, 0.0)
                     + jnp.where(lane == 2, r1, 0.0) + jnp.where(lane == 3, r2, 0.0))
    w_ref[...] = jnp.where(lane == 0, w1, 0.0) + jnp.where(lane == 1, w2, 0.0)


def _route(logits, tm=512):
    t = logits.shape[0]
    return pl.pallas_call(
        _route_kernel,
        grid=(t // tm,),
        in_specs=[pl.BlockSpec((tm, LANES), lambda i: (i, 0))],
        out_specs=[pl.BlockSpec((tm, LANES), lambda i: (i, 0)),
                   pl.BlockSpec((tm, LANES), lambda i: (i, 0)),
                   pl.BlockSpec((SUBLANES, LANES), lambda i: (0, 0))],
        out_shape=[jax.ShapeDtypeStruct((t, LANES), F32),
                   jax.ShapeDtypeStruct((t, LANES), F32),
                   jax.ShapeDtypeStruct((SUBLANES, LANES), F32)],
        scratch_shapes=[pltpu.VMEM((SUBLANES, LANES), F32)],
        compiler_params=_cparams(("arbitrary",)),
        name="route_topk",
    )(logits)


def _lane_cumsum(x):
    lane = lax.broadcasted_iota(I32, x.shape, 1)
    sh = 1
    while sh < LANES:
        x = x + jnp.where(lane >= sh, pltpu.roll(x, sh, axis=1), 0)
        sh *= 2
    return x


def _dest_kernel(meta_ref, cnt_ref, dest_ref, be_ref, *, nb):
    tm = meta_ref.shape[0]
    shift = MOE_BLOCK.bit_length() - 1
    cnt = cnt_ref[...].astype(I32)
    padded = ((cnt + (MOE_BLOCK - 1)) >> shift) << shift
    pend_i = _lane_cumsum(padded)
    pend = pend_i.astype(F32)[0:1, :]
    poff = (pend_i - padded).astype(F32)[0:1, :]
    meta = meta_ref[...]
    lane = lax.broadcasted_iota(I32, (tm, LANES), 1).astype(F32)
    pick = lambda col: jnp.sum(jnp.where(lane == col, meta, 0.0), axis=-1, keepdims=True)
    i1, i2, r1, r2 = pick(0), pick(1), pick(2), pick(3)
    d1 = jnp.sum(jnp.where(lane == i1, poff, 0.0), axis=-1, keepdims=True) + r1
    d2 = jnp.sum(jnp.where(lane == i2, poff, 0.0), axis=-1, keepdims=True) + r2
    dest_ref[...] = (jnp.where(lane == 0, d1, 0.0) + jnp.where(lane == 1, d2, 0.0)).astype(I32)

    nrow = be_ref.shape[0]
    blk = lax.broadcasted_iota(I32, (nrow, LANES), 0).astype(F32)
    lane_b = lax.broadcasted_iota(I32, (nrow, LANES), 1).astype(F32)
    is_e = (lane_b >= ROUTE_LANE0) & (lane_b < ROUTE_LANE0 + N_EXPERTS)
    below = jnp.sum(jnp.where(is_e & (pend <= blk * MOE_BLOCK), 1.0, 0.0), axis=-1, keepdims=True)
    block_e = jnp.minimum(below, N_EXPERTS - 1.0)
    used = jnp.sum(jnp.where(lane_b == ROUTE_LANE0 + N_EXPERTS - 1, pend, 0.0),
                   axis=-1, keepdims=True) * (1.0 / MOE_BLOCK)
    be = jnp.where(blk[:, :1] == nb, used, block_e)
    be_ref[...] = jnp.broadcast_to(be, (nrow, LANES)).astype(I32)


def _dest(meta, cnt, nb, tm=512):
    t = meta.shape[0]
    nrow = -(-(nb + 1) // SUBLANES) * SUBLANES
    kern = functools.partial(_dest_kernel, nb=nb)
    return pl.pallas_call(
        kern,
        grid=(t // tm,),
        in_specs=[pl.BlockSpec((tm, LANES), lambda i: (i, 0)),
                  pl.BlockSpec((SUBLANES, LANES), lambda i: (0, 0))],
        out_specs=[pl.BlockSpec((tm, LANES), lambda i: (i, 0)),
                   pl.BlockSpec((nrow, LANES), lambda i: (0, 0))],
        out_shape=[jax.ShapeDtypeStruct((t, LANES), I32),
                   jax.ShapeDtypeStruct((nrow, LANES), I32)],
        compiler_params=_cparams(("arbitrary",)),
        name="route_dest",
    )(meta, cnt)


def _invert_kernel(dest_ref, row_ref, *, t):
    n_slot = row_ref.shape[0]
    n_asg = dest_ref.shape[0]
    spare = 2 * MOE_BLOCK

    def fill(s, c):
        row_ref[s] = 2 * t + (s & (spare - 1))
        return c

    lax.fori_loop(0, n_slot, fill, 0, unroll=8)

    def put(a, c):
        row_ref[dest_ref[a]] = (a & 1) * t + lax.shift_right_logical(a, 1)
        return c

    lax.fori_loop(0, n_asg, put, 0, unroll=8)


def _invert(dest_flat, n_slot, t):
    kern = functools.partial(_invert_kernel, t=t)
    return pl.pallas_call(
        kern,
        in_specs=[pl.BlockSpec(memory_space=pltpu.SMEM)],
        out_specs=pl.BlockSpec(memory_space=pltpu.SMEM),
        out_shape=jax.ShapeDtypeStruct((n_slot,), I32),
        name="route_invert",
    )(dest_flat)


def _moe_up_kernel(row_ref, be_ref, h2_hbm, wg_ref, wu_ref, o_ref, wbuf, xbuf, xb_scr, sem, *, nb):
    b = pl.program_id(0)
    used = be_ref[nb]
    blk = MOE_BLOCK
    fe = wg_ref.shape[2]
    t = h2_hbm.shape[0]

    def gather_rows(block, rows):
        slot = block % 2
        base = block * blk
        for r in rows:
            row = row_ref[base + r]
            tok = jnp.where(row >= t, row - t, row)
            tok = jnp.where(tok >= t, 0, tok)
            pltpu.make_async_copy(h2_hbm.at[pl.ds(tok, 1)],
                                  xbuf.at[slot, pl.ds(r, 1)], sem.at[slot]).start()

    @pl.when((b == 0) & (used > 0))
    def _():
        gather_rows(0, range(blk))

    new_expert = (b == 0) | (be_ref[b] != be_ref[jnp.maximum(b - 1, 0)])

    @pl.when(new_expert & (b < used))
    def _():
        wbuf[:, :fe] = wg_ref[0].astype(BF16)
        wbuf[:, fe:] = wu_ref[0].astype(BF16)

    nchunk = fe // MOE_NCHUNK_COLS
    rows_per = blk // nchunk

    def step(prefetch):
        slot = b % 2
        pltpu.make_async_copy(xbuf.at[slot], xbuf.at[slot], sem.at[slot]).wait()
        xb_scr[...] = xbuf[slot].astype(BF16)
        for c in range(nchunk):
            if prefetch:
                gather_rows(b + 1, range(c * rows_per, (c + 1) * rows_per))
            cols = slice(c * MOE_NCHUNK_COLS, (c + 1) * MOE_NCHUNK_COLS)
            ucols = slice(fe + c * MOE_NCHUNK_COLS, fe + (c + 1) * MOE_NCHUNK_COLS)
            g = jnp.dot(xb_scr[...], wbuf[:, cols], preferred_element_type=F32)
            u = jnp.dot(xb_scr[...], wbuf[:, ucols], preferred_element_type=F32)
            o_ref[:, cols] = (_silu(g) * u).astype(BF16)

    @pl.when(b + 1 < used)
    def _():
        step(True)

    @pl.when((b < used) & (b + 1 >= used))
    def _():
        step(False)

    @pl.when(b >= used)
    def _():
        o_ref[...] = jnp.zeros(o_ref.shape, BF16)


def _moe_up(slot_row, block_e, h2, w_gate, w_up, nb):
    t, d = h2.shape
    fe = w_gate.shape[2]
    kern = functools.partial(_moe_up_kernel, nb=nb)
    grid_spec = pltpu.PrefetchScalarGridSpec(
        num_scalar_prefetch=2,
        grid=(nb,),
        in_specs=[pl.BlockSpec(memory_space=pl.ANY),
                  pl.BlockSpec((1, d, fe), lambda b, s, e: (e[b], 0, 0)),
                  pl.BlockSpec((1, d, fe), lambda b, s, e: (e[b], 0, 0))],
        out_specs=pl.BlockSpec((MOE_BLOCK, fe), lambda b, s, e: (b, 0)),
        scratch_shapes=[pltpu.VMEM((d, 2 * fe), BF16),
                        pltpu.VMEM((2, MOE_BLOCK, d), F32),
                        pltpu.VMEM((MOE_BLOCK, d), BF16),
                        pltpu.SemaphoreType.DMA((2,))],
    )
    return pl.pallas_call(
        kern,
        grid_spec=grid_spec,
        out_shape=jax.ShapeDtypeStruct((nb * MOE_BLOCK, fe), BF16),
        compiler_params=_cparams(("arbitrary",)),
        name="moe_up",
    )(slot_row, block_e, h2, w_gate, w_up)


def _moe_down_kernel(row_ref, be_ref, hm_ref, wd_ref, ys_hbm, wbuf, obuf, sem, *, nb, t):
    b = pl.program_id(0)
    used = be_ref[nb]
    blk = MOE_BLOCK

    def wait_slot(slot):
        pltpu.make_async_copy(obuf.at[slot], obuf.at[slot], sem.at[slot]).wait()

    @pl.when(b == 0)
    def _():
        obuf[...] = jnp.zeros(obuf.shape, F32)
        for half in range(2):
            pltpu.make_async_copy(obuf.at[half], ys_hbm.at[pl.ds(2 * t + half * blk, blk)],
                                  sem.at[2]).start()
        for half in range(2):
            pltpu.make_async_copy(obuf.at[half], ys_hbm.at[pl.ds(2 * t + half * blk, blk)],
                                  sem.at[2]).wait()

    @pl.when((b >= 2) & (b - 2 < used))
    def _():
        wait_slot(b % 2)

    bc = jnp.minimum(b, nb - 1)
    new_expert = (b == 0) | (be_ref[bc] != be_ref[jnp.maximum(bc - 1, 0)])

    @pl.when(new_expert & (b < used))
    def _():
        wbuf[...] = wd_ref[0].astype(BF16)

    d = wbuf.shape[1]
    nchunk = d // MOE_NCHUNK_COLS
    rows_per = blk // nchunk

    def step(scatter, matmul):
        slot = (b - 1) % 2
        base = (b - 1) * blk
        for c in range(nchunk):
            if scatter:
                for r in range(c * rows_per, (c + 1) * rows_per):
                    pltpu.make_async_copy(obuf.at[slot, pl.ds(r, 1)],
                                          ys_hbm.at[pl.ds(row_ref[base + r], 1)],
                                          sem.at[slot]).start()
            if matmul:
                cols = slice(c * MOE_NCHUNK_COLS, (c + 1) * MOE_NCHUNK_COLS)
                obuf[b % 2, :, cols] = jnp.dot(hm_ref[...], wbuf[:, cols],
                                               preferred_element_type=F32)

    @pl.when((b >= 1) & (b < used))
    def _():
        step(True, True)

    @pl.when((b == 0) & (b < used))
    def _():
        step(False, True)

    @pl.when((b >= 1) & (b == used))
    def _():
        step(True, False)

    @pl.when((b == nb) & (used == nb))
    def _():
        wait_slot((nb - 1) % 2)


def _moe_down(slot_row, block_e, hmid, w_down, nb, t):
    fe, d = w_down.shape[1], w_down.shape[2]
    kern = functools.partial(_moe_down_kernel, nb=nb, t=t)
    grid_spec = pltpu.PrefetchScalarGridSpec(
        num_scalar_prefetch=2,
        grid=(nb + 1,),
        in_specs=[pl.BlockSpec((MOE_BLOCK, fe), lambda b, s, e: (jnp.minimum(b, nb - 1), 0)),
                  pl.BlockSpec((1, fe, d), lambda b, s, e: (e[jnp.minimum(b, nb - 1)], 0, 0))],
        out_specs=pl.BlockSpec(memory_space=pl.ANY),
        scratch_shapes=[pltpu.VMEM((fe, d), BF16),
                        pltpu.VMEM((2, MOE_BLOCK, d), F32),
                        pltpu.SemaphoreType.DMA((3,))],
    )
    return pl.pallas_call(
        kern,
        grid_spec=grid_spec,
        out_shape=jax.ShapeDtypeStruct((2 * t + 2 * MOE_BLOCK, d), F32),
        compiler_params=_cparams(("arbitrary",)),
        name="moe_down",
    )(slot_row, block_e, hmid, w_down)


def _final_kernel(x1_ref, y0_ref, y1_ref, w_ref, mod_ref, gf_ref, o_ref):
    w = w_ref[...]
    y = w[:, 0:1] * y0_ref[...] + w[:, 1:2] * y1_ref[...]
    x2 = x1_ref[...] + mod_ref[0][5:6, :] * y
    o_ref[...] = x2 * lax.rsqrt(jnp.mean(x2 * x2, axis=-1, keepdims=True) + NORM_EPS) * gf_ref[...]


def _final(x1, ys, w, mod3, g_final, seq, tm=512):
    t, d = x1.shape
    tpb = seq // tm
    return pl.pallas_call(
        _final_kernel,
        grid=(t // tm,),
        in_specs=[pl.BlockSpec((tm, d), lambda i: (i, 0)),
                  pl.BlockSpec((tm, d), lambda i: (i, 0)),
                  pl.BlockSpec((tm, d), lambda i: (i + t // tm, 0)),
                  pl.BlockSpec((tm, LANES), lambda i: (i, 0)),
                  pl.BlockSpec((1, N_MOD, d), lambda i: (i // tpb, 0, 0)),
                  pl.BlockSpec((1, d), lambda i: (0, 0))],
        out_specs=pl.BlockSpec((tm, d), lambda i: (i, 0)),
        out_shape=jax.ShapeDtypeStruct((t, d), F32),
        compiler_params=_cparams(("arbitrary",)),
        name="combine_final_norm",
    )(x1, ys, ys, w, mod3, g_final.reshape(1, d))


def _rope_tables(seq, dh):
    half = dh // 2
    inv_freq = 1.0 / (ROPE_BASE ** (jnp.arange(half, dtype=F32) * 2.0 / dh))
    ang = jnp.arange(seq, dtype=F32)[:, None] * inv_freq[None, :]
    cos, sin = jnp.cos(ang), jnp.sin(ang)
    return jnp.concatenate([cos, cos], axis=1), jnp.concatenate([-sin, sin], axis=1)


def _layer(x, mod, g_norm1, w_in, ssm_a_re, ssm_a_im, ssm_b_re, ssm_b_im, ssm_c_re, ssm_c_im,
           ssm_d, ssm_log_dt, w_glu, beta_ssm, beta_ret, w_out, g_norm2, w_rg, b_rg, w_re, b_re,
           w_gate, w_up, w_down):
    bn, seq, d = x.shape
    t = bn * seq
    x2d = x.reshape(t, d)
    mod3 = mod.reshape(bn, N_MOD, d)

    sw = w_glu.shape[0]
    u_wide, rest = _inproj(x2d, mod3, g_norm1, w_in.astype(BF16), bn, seq, sw)
    u_tb = u_wide.reshape(seq * bn, sw)

    ab_re, ab_im, bbt_re, bbt_im = _s5_disc(ssm_a_re, ssm_a_im, ssm_log_dt,
                                            jnp.swapaxes(ssm_b_re, 1, 2), jnp.swapaxes(ssm_b_im, 1, 2))
    bz, cz, a_re, a_im = _s5_pack(ab_re, ab_im, bbt_re, bbt_im, ssm_c_re, ssm_c_im)
    ypre_tb = _s5_scan(u_tb, bz, cz, a_re, a_im, ssm_d.reshape(-1), bn, seq)
    ypre_wide = ypre_tb.reshape(seq, bn * sw)

    heads = (d - sw) // RET_HEAD_DIM
    cosf, sinf = _rope_tables(seq, RET_HEAD_DIM)
    gamma = 1.0 - jnp.exp2(-5.0 - jnp.arange(heads, dtype=F32))
    log_g = jnp.broadcast_to(jnp.log(gamma)[:, None, None], (heads, 1, LANES))
    yret = _retention(rest.reshape(bn, seq, rest.shape[1]), cosf, sinf, log_g, beta_ret)

    n_route = N_GROUPS + N_EXPERTS
    wr = jnp.concatenate([w_rg, w_re, jnp.zeros((d, LANES - n_route), F32)], axis=1)
    br = jnp.concatenate([b_rg, b_re, jnp.zeros((LANES - n_route,), F32)]).reshape(1, LANES)
    wr_hi = wr.astype(BF16)
    wr_lo = (wr - wr_hi.astype(F32)).astype(BF16)
    x1, h2, logits = _mix(ypre_wide, yret.reshape(t, d - sw), x2d, mod3, w_glu.astype(BF16),
                          beta_ssm, w_out.astype(BF16), g_norm2, wr_hi, wr_lo, br, bn, seq)

    nb = -(-(t * 2) // MOE_BLOCK) + N_EXPERTS
    meta, w_tok, cnt = _route(logits)
    dest, be = _dest(meta, cnt, nb)
    slot_row = _invert(dest[:, :2].reshape(-1), nb * MOE_BLOCK, t)
    block_e = be[:nb + 1, 0]

    hmid = _moe_up(slot_row, block_e, h2, w_gate, w_up, nb)
    ys = _moe_down(slot_row, block_e, hmid, w_down, nb, t)
    return x1, ys, w_tok, mod3


def kernel(x, c, w_ada, b_ada, g_norm1, w_in, ssm_a_re, ssm_a_im, ssm_b_re, ssm_b_im, ssm_c_re,
           ssm_c_im, ssm_d, ssm_log_dt, w_glu, beta_ssm, beta_ret, w_out, g_norm2, w_router_group,
           b_router_group, w_router_expert, b_router_expert, w_gate, w_up, w_down, g_final):
    depth = w_ada.shape[0]
    bn, seq, d = x.shape
    assert depth == 1, "the final norm is fused into the single layer's last kernel"
    l = 0
    mod = _ada(c, w_ada[l], b_ada[l])
    x1, ys, w_tok, mod3 = _layer(
        x, mod, g_norm1[l], w_in[l], ssm_a_re[l], ssm_a_im[l], ssm_b_re[l], ssm_b_im[l],
        ssm_c_re[l], ssm_c_im[l], ssm_d[l], ssm_log_dt[l], w_glu[l], beta_ssm[l], beta_ret[l],
        w_out[l], g_norm2[l], w_router_group[l], b_router_group[l], w_router_expert[l],
        b_router_expert[l], w_gate[l], w_up[l], w_down[l])
    out = _final(x1, ys, w_tok, mod3, g_final, seq)
    return out.reshape(bn, seq, d)
```

```python
import functools
import math

import jax
import jax.numpy as jnp
from jax import lax
from jax.experimental import pallas as pl
from jax.experimental.pallas import tpu as pltpu

F32 = jnp.float32
BF16 = jnp.bfloat16
I32 = jnp.int32

SSM_GROUP = 16
SSM_STATE = 64
RET_HEAD_DIM = 128
ROPE_BASE = 10000.0
N_GROUPS = 4
EXPERTS_PER_GROUP = 8
N_EXPERTS = N_GROUPS * EXPERTS_PER_GROUP
N_MOD = 6
NORM_EPS = 1e-6
GN_EPS = 1e-5

LANES = 128
SUBLANES = 8
VMEM_LIMIT = 56 * 1024 * 1024

ROUTE_LANE0 = N_GROUPS
MOE_BLOCK = 256
MOE_NCHUNK_COLS = 256
RET_CHUNK = 256
S5_TT = 128


def _cparams(sem):
    return pltpu.CompilerParams(dimension_semantics=sem, vmem_limit_bytes=VMEM_LIMIT)


def _silu(x):
    return x * jax.nn.sigmoid(x)


def _ada_kernel(c_ref, w_ref, b_ref, o_ref):
    s = _silu(c_ref[...])
    o_ref[...] = jnp.dot(s.astype(BF16), w_ref[...].astype(BF16),
                         preferred_element_type=F32) + b_ref[...]


def _ada(c, w, b, tn=1024):
    bn, d = c.shape
    n = w.shape[1]
    return pl.pallas_call(
        _ada_kernel,
        grid=(n // tn,),
        in_specs=[pl.BlockSpec((bn, d), lambda j: (0, 0)),
                  pl.BlockSpec((d, tn), lambda j: (0, j)),
                  pl.BlockSpec((1, tn), lambda j: (0, j))],
        out_specs=pl.BlockSpec((bn, tn), lambda j: (0, j)),
        out_shape=jax.ShapeDtypeStruct((bn, n), F32),
        compiler_params=_cparams(("arbitrary",)),
        name="ada_mod",
    )(c, w, b.reshape(1, n))


def _inproj_kernel(x_ref, mod_ref, g_ref, w_ref, u_ref, r_ref, h_scr):
    j = pl.program_id(1)

    @pl.when(j == 0)
    def _():
        x = x_ref[...]
        y = x * lax.rsqrt(jnp.mean(x * x, axis=-1, keepdims=True) + NORM_EPS) * g_ref[...]
        m = mod_ref[0]
        h_scr[...] = (y * (1.0 + m[1:2, :]) + m[0:1, :]).astype(BF16)

    acc = jnp.dot(h_scr[...], w_ref[...], preferred_element_type=F32).astype(BF16)

    @pl.when(j == 0)
    def _():
        u_ref[...] = acc

    @pl.when(j > 0)
    def _():
        r_ref[...] = acc


def _inproj(x2d, mod3, g1, w_in_bf, bn, seq, sw, tm=1024):
    t, d = x2d.shape
    n = w_in_bf.shape[1]
    tn = sw
    tm = min(tm, seq)
    tpb = seq // tm
    return pl.pallas_call(
        _inproj_kernel,
        grid=(t // tm, n // tn),
        in_specs=[pl.BlockSpec((tm, d), lambda i, j: (i, 0)),
                  pl.BlockSpec((1, N_MOD, d), lambda i, j: (i // tpb, 0, 0)),
                  pl.BlockSpec((1, d), lambda i, j: (0, 0)),
                  pl.BlockSpec((d, tn), lambda i, j: (0, j))],
        out_specs=[pl.BlockSpec((tm, tn), lambda i, j: (i % tpb, i // tpb)),
                   pl.BlockSpec((tm, tn), lambda i, j: (i, jnp.maximum(j - 1, 0)))],
        out_shape=[jax.ShapeDtypeStruct((seq, bn * tn), BF16),
                   jax.ShapeDtypeStruct((t, n - tn), BF16)],
        scratch_shapes=[pltpu.VMEM((tm, d), BF16)],
        compiler_params=_cparams(("arbitrary", "arbitrary")),
        name="norm_inproj",
    )(x2d, mod3, g1.reshape(1, d), w_in_bf)


def _s5_disc_kernel(are_ref, aim_ref, ldt_ref, bre_ref, bim_ref,
                    abre_ref, abim_ref, bbre_ref, bbim_ref):
    a_re = are_ref[...]
    a_im = aim_ref[...]
    dt = jnp.exp(ldt_ref[...])
    mag = jnp.exp(dt * a_re)
    ab_re = mag * jnp.cos(dt * a_im)
    ab_im = mag * jnp.sin(dt * a_im)
    inv_abs2 = 1.0 / (a_re * a_re + a_im * a_im)
    n_re = ab_re - 1.0
    f_re = (n_re * a_re + ab_im * a_im) * inv_abs2
    f_im = (ab_im * a_re - n_re * a_im) * inv_abs2
    abre_ref[...] = ab_re
    abim_ref[...] = ab_im
    b_re = bre_ref[...]
    b_im = bim_ref[...]
    fr = f_re[:, None, :]
    fi = f_im[:, None, :]
    bbre_ref[...] = fr * b_re - fi * b_im
    bbim_ref[...] = fr * b_im + fi * b_re


def _s5_disc(a_re, a_im, log_dt, bt_re, bt_im):
    g, p = a_re.shape
    h = bt_re.shape[1]
    return pl.pallas_call(
        _s5_disc_kernel,
        out_shape=[jax.ShapeDtypeStruct((g, p), F32), jax.ShapeDtypeStruct((g, p), F32),
                   jax.ShapeDtypeStruct((g, h, p), F32), jax.ShapeDtypeStruct((g, h, p), F32)],
        name="s5_discretise",
    )(a_re, a_im, log_dt.reshape(g, 1), bt_re, bt_im)


def _s5_pack(ab_re, ab_im, bbt_re, bbt_im, c_re, c_im):
    g, h, p = bbt_re.shape
    npair = g // 2
    ppb = LANES // (2 * h)
    eye2 = jnp.eye(2, dtype=F32)
    qsel = jax.nn.one_hot(jnp.arange(npair) % ppb, ppb, dtype=F32)

    def in_mat(bbt):
        b4 = bbt.reshape(npair, 2, h, p)
        return jnp.einsum('ab,xahp->xahbp', eye2, b4).reshape(npair, 2 * h, 2 * p)

    bsmall = jnp.concatenate([in_mat(bbt_re), in_mat(bbt_im)], axis=-1)
    bz = jnp.einsum('xq,xrc->xqrc', qsel, bsmall).reshape(npair, LANES, 4 * p)

    def out_mat(c):
        c4 = jnp.swapaxes(c, 1, 2).reshape(npair, 2, p, h)
        return jnp.einsum('ab,xaph->xapbh', eye2, c4).reshape(npair, 2 * p, 2 * h)

    csmall = jnp.concatenate([out_mat(c_re), -out_mat(c_im)], axis=1)
    cz = jnp.einsum('xq,xrc->xrqc', qsel, csmall).reshape(npair, 4 * p, LANES)
    a_re = ab_re.reshape(npair // ppb, ppb, 2 * p)
    a_im = ab_im.reshape(npair // ppb, ppb, 2 * p)
    return bz.astype(BF16), cz.astype(BF16), a_re, a_im


def _s5_kernel(u_ref, bz_ref, cz_ref, are_ref, aim_ref, d_ref, o_ref, state, buf, *, bn, tt, ppb):
    t = pl.program_id(0)
    kb = pl.program_id(1)

    @pl.when(t == 0)
    def _():
        state[kb] = jnp.zeros(state.shape[1:], F32)

    u = u_ref[...]
    for q in range(ppb):
        bu = jnp.dot(u, bz_ref[q], preferred_element_type=F32)
        buf[2 * q] = bu[:, :LANES]
        buf[2 * q + 1] = bu[:, LANES:]

    a_re = [jnp.broadcast_to(are_ref[0, q:q + 1, :], (bn, LANES)) for q in range(ppb)]
    a_im = [jnp.broadcast_to(aim_ref[0, q:q + 1, :], (bn, LANES)) for q in range(ppb)]
    s0 = tuple(state[kb, j] for j in range(2 * ppb))

    def step(i, s):
        rows = pl.ds(pl.multiple_of(i * bn, bn), bn)
        new = []
        for q in range(ppb):
            s_re, s_im = s[2 * q], s[2 * q + 1]
            n_re = a_re[q] * s_re - a_im[q] * s_im + buf[2 * q, rows, :]
            n_im = a_re[q] * s_im + a_im[q] * s_re + buf[2 * q + 1, rows, :]
            buf[2 * q, rows, :] = n_re
            buf[2 * q + 1, rows, :] = n_im
            new += [n_re, n_im]
        return tuple(new)

    s1 = lax.fori_loop(0, tt, step, s0, unroll=4)
    for j in range(2 * ppb):
        state[kb, j] = s1[j]

    acc = d_ref[...] * u.astype(F32)
    for q in range(ppb):
        lhs = jnp.concatenate([buf[2 * q], buf[2 * q + 1]], axis=1).astype(BF16)
        acc = acc + jnp.dot(lhs, cz_ref[q], preferred_element_type=F32)
    o_ref[...] = acc


def _s5_scan(u_tb, bz, cz, a_re, a_im, d_skip, bn, seq, tt=S5_TT):
    rows, width = u_tb.shape
    nkb = width // LANES
    ppb = bz.shape[0] // nkb
    rt = tt * bn
    kern = functools.partial(_s5_kernel, bn=bn, tt=tt, ppb=ppb)
    return pl.pallas_call(
        kern,
        grid=(seq // tt, nkb),
        in_specs=[pl.BlockSpec((rt, LANES), lambda t, k: (t, k)),
                  pl.BlockSpec((ppb,) + bz.shape[1:], lambda t, k: (k, 0, 0)),
                  pl.BlockSpec((ppb,) + cz.shape[1:], lambda t, k: (k, 0, 0)),
                  pl.BlockSpec((1,) + a_re.shape[1:], lambda t, k: (k, 0, 0)),
                  pl.BlockSpec((1,) + a_im.shape[1:], lambda t, k: (k, 0, 0)),
                  pl.BlockSpec((1, LANES), lambda t, k: (0, k))],
        out_specs=pl.BlockSpec((rt, LANES), lambda t, k: (t, k)),
        out_shape=jax.ShapeDtypeStruct((rows, width), F32),
        scratch_shapes=[pltpu.VMEM((nkb, 2 * ppb, bn, LANES), F32),
                        pltpu.VMEM((2 * ppb, rt, LANES), F32)],
        compiler_params=_cparams(("arbitrary", "arbitrary")),
        name="s5_scan",
    )(u_tb, bz, cz, a_re, a_im, d_skip.reshape(1, width))


def _ret_kernel(q_ref, k_ref, v_ref, g_ref, cq_ref, sq_ref, ck_ref, sk_ref, lg_ref, beta_ref, o_ref,
                q_scr, k_scr, kzt_scr, y_scr, *, chunk):
    seq, dh = q_ref.shape[1], q_ref.shape[2]
    lg = lg_ref[0][:, :1]
    ri = lax.broadcasted_iota(I32, (chunk, chunk), 0)
    ci = lax.broadcasted_iota(I32, (chunk, chunk), 1)
    rel = (ri - ci).astype(F32)
    decay = jnp.where(rel >= 0, jnp.exp(lg * jnp.maximum(rel, 0.0)), 0.0)
    idx = lax.broadcasted_iota(I32, (chunk, 1), 0).astype(F32)
    xi = jnp.exp(lg * (idx + 1.0))
    g_chunk = jnp.exp(lg * float(chunk))
    beta = beta_ref[...]
    half = dh // 2
    pr = lax.broadcasted_iota(I32, (dh, dh), 0)
    pc = lax.broadcasted_iota(I32, (dh, dh), 1)
    swap = jnp.where(((pr + half) & (dh - 1)) == pc, 1.0, 0.0).astype(BF16)

    def rope(x_ref, cos_ref, sin_ref, sl):
        xb = x_ref[0, sl, :]
        rolled = jnp.dot(xb, swap, preferred_element_type=F32).astype(BF16)
        return xb * cos_ref[sl, :] + rolled * sin_ref[sl, :]

    zeta = jnp.exp(lg * (chunk - 1.0 - idx))
    for n in range(seq // chunk):
        sl = slice(n * chunk, (n + 1) * chunk)
        q_scr[sl, :] = rope(q_ref, cq_ref, sq_ref, sl)
        kb = rope(k_ref, ck_ref, sk_ref, sl)
        k_scr[sl, :] = kb
        kzt_scr[:, sl] = (kb.astype(F32) * zeta).T.astype(BF16)

    decay = decay.astype(BF16)
    r = jnp.zeros((dh, dh), F32)
    for n in range(seq // chunk):
        sl = slice(n * chunk, (n + 1) * chunk)
        qb = q_scr[sl, :]
        v = v_ref[0, sl, :]
        s = lax.dot_general(qb, k_scr[sl, :], (((1,), (1,)), ((), ())),
                            preferred_element_type=F32).astype(BF16) * decay
        y = jnp.dot(s, v, preferred_element_type=F32)
        y_scr[sl, :] = y + jnp.dot(qb, r.astype(BF16), preferred_element_type=F32) * xi
        r = r * g_chunk + jnp.dot(kzt_scr[:, sl], v, preferred_element_type=F32)

    y = y_scr[...]
    mu = jnp.mean(y, axis=-1, keepdims=True)
    yc = y - mu
    var = jnp.mean(yc * yc, axis=-1, keepdims=True)
    yn = yc * lax.rsqrt(var + GN_EPS) * beta
    o_ref[0] = (_silu(g_ref[0].astype(F32)) * yn).astype(BF16)


def _retention(rest3, cosf, sinf, log_g, beta_ret, chunk=RET_CHUNK):
    bn, seq, _ = rest3.shape
    dh = RET_HEAD_DIM
    heads = beta_ret.shape[0] // dh
    blk = lambda off: pl.BlockSpec((1, seq, dh), lambda b, h, off=off: (b, 0, off + h))
    k_scale = dh ** -0.5
    kern = functools.partial(_ret_kernel, chunk=chunk)
    return pl.pallas_call(
        kern,
        grid=(bn, heads),
        in_specs=[blk(0), blk(heads), blk(2 * heads), blk(3 * heads),
                  pl.BlockSpec((seq, dh), lambda b, h: (0, 0)),
                  pl.BlockSpec((seq, dh), lambda b, h: (0, 0)),
                  pl.BlockSpec((seq, dh), lambda b, h: (0, 0)),
                  pl.BlockSpec((seq, dh), lambda b, h: (0, 0)),
                  pl.BlockSpec((1, 1, LANES), lambda b, h: (h, 0, 0)),
                  pl.BlockSpec((1, dh), lambda b, h: (0, h))],
        out_specs=pl.BlockSpec((1, seq, dh), lambda b, h: (b, 0, h)),
        out_shape=jax.ShapeDtypeStruct((bn, seq, heads * dh), BF16),
        scratch_shapes=[pltpu.VMEM((seq, dh), BF16), pltpu.VMEM((seq, dh), BF16),
                        pltpu.VMEM((dh, seq), BF16), pltpu.VMEM((seq, dh), F32)],
        compiler_params=_cparams(("arbitrary", "arbitrary")),
        name="retention",
    )(rest3, rest3, rest3, rest3, cosf.astype(BF16), sinf.astype(BF16),
      (cosf * k_scale).astype(BF16), (sinf * k_scale).astype(BF16),
      log_g, beta_ret.reshape(1, heads * dh))


def _gelu_tanh(x):
    return 0.5 * x * (1.0 + jnp.tanh(math.sqrt(2.0 / math.pi) * (x + 0.044715 * (x * x * x))))


def _split_bf16(x):
    hi = x.astype(BF16)
    lo = (x - hi.astype(F32)).astype(BF16)
    return hi, lo


def _to_row_major(ref, val, col0=0):
    rows = val.shape[0]
    lines = ref.shape[0] // rows
    for j in range(val.shape[1] // LANES):
        ref[pl.ds(col0 + j, rows, stride=lines), :] = val[:, j * LANES:(j + 1) * LANES]


def _from_row_major(ref, rows, j):
    return ref[pl.ds(j, rows, stride=ref.shape[0] // rows), :]


def _mix_kernel(ys_ref, yr_ref, x_ref, mod_ref, wglu_ref, bssm_ref, wout_ref, g2_ref,
                wrh_ref, wrl_ref, br_ref, x1_ref, h2_ref, lg_ref):
    sw = ys_ref.shape[1]
    z = _gelu_tanh(ys_ref[...])
    gate = jax.nn.sigmoid(jnp.dot(z.astype(BF16), wglu_ref[...], preferred_element_type=F32))
    o = z * gate
    y_ssm = o * lax.rsqrt(jnp.mean(o * o, axis=-1, keepdims=True) + NORM_EPS) * bssm_ref[...]
    mixed = (jnp.dot(y_ssm.astype(BF16), wout_ref[:sw, :], preferred_element_type=F32)
             + jnp.dot(yr_ref[...], wout_ref[sw:, :], preferred_element_type=F32))
    m = mod_ref[0]
    x1 = x_ref[...] + m[2:3, :] * mixed
    x1_ref[...] = x1
    y = x1 * lax.rsqrt(jnp.mean(x1 * x1, axis=-1, keepdims=True) + NORM_EPS) * g2_ref[...]
    h2 = y * (1.0 + m[4:5, :]) + m[3:4, :]
    _to_row_major(h2_ref, h2)
    hi, lo = _split_bf16(h2)
    wrh = wrh_ref[...]
    lg_ref[...] = (jnp.dot(hi, wrh, preferred_element_type=F32)
                   + jnp.dot(lo, wrh, preferred_element_type=F32)
                   + jnp.dot(hi, wrl_ref[...], preferred_element_type=F32)
                   + br_ref[...])


def _mix(ypre_tb, yret, x2d, mod3, wglu_bf, beta_ssm, wout_bf, g2, wr_hi, wr_lo, br, bn, seq, tm=256):
    t, d = x2d.shape
    sw = wglu_bf.shape[0]
    tpb = seq // tm
    const = lambda shape: pl.BlockSpec(shape, lambda i: (0,) * len(shape),
                                       pipeline_mode=pl.Buffered(1))
    return pl.pallas_call(
        _mix_kernel,
        grid=(t // tm,),
        in_specs=[pl.BlockSpec((tm, sw), lambda i: (i % tpb, i // tpb)),
                  pl.BlockSpec((tm, d - sw), lambda i: (i, 0)),
                  pl.BlockSpec((tm, d), lambda i: (i, 0)),
                  pl.BlockSpec((1, N_MOD, d), lambda i: (i // tpb, 0, 0)),
                  const((sw, sw)), const((1, sw)), const((d, d)), const((1, d)),
                  const((d, LANES)), const((d, LANES)), const((1, LANES))],
        out_specs=[pl.BlockSpec((tm, d), lambda i: (i, 0)),
                   pl.BlockSpec((tm * (d // LANES), LANES), lambda i: (i, 0)),
                   pl.BlockSpec((tm, LANES), lambda i: (i, 0))],
        out_shape=[jax.ShapeDtypeStruct((t, d), F32),
                   jax.ShapeDtypeStruct((t * (d // LANES), LANES), F32),
                   jax.ShapeDtypeStruct((t, LANES), F32)],
        compiler_params=_cparams(("arbitrary",)),
        name="mix_outproj_router",
    )(ypre_tb, yret, x2d, mod3, wglu_bf, beta_ssm.reshape(1, sw), wout_bf, g2.reshape(1, d),
      wr_hi, wr_lo, br)


def _route_kernel(lg_ref, meta_ref, w_ref, cnt_ref, carry):
    i = pl.program_id(0)
    tm = lg_ref.shape[0]

    @pl.when(i == 0)
    def _():
        carry[...] = jnp.zeros(carry.shape, F32)

    lg = lg_ref[...]
    lane = lax.broadcasted_iota(I32, (tm, LANES), 1).astype(F32)
    neg = jnp.float32(-jnp.inf)

    def first_max(vals):
        m = jnp.max(vals, axis=-1, keepdims=True)
        idx = jnp.min(jnp.where(vals == m, lane, float(LANES)), axis=-1, keepdims=True)
        return m, idx

    gl = jnp.where(lane < N_GROUPS, lg, neg)
    gmax, gsel = first_max(gl)
    g_w = 1.0 / jnp.sum(jnp.exp(gl - gmax), axis=-1, keepdims=True)
    lo = ROUTE_LANE0 + gsel * EXPERTS_PER_GROUP
    el = jnp.where((lane >= lo) & (lane < lo + EXPERTS_PER_GROUP), lg, neg)
    m1, i1 = first_max(el)
    m2, i2 = first_max(jnp.where(lane == i1, neg, el))
    e21 = jnp.exp(m2 - m1)
    w1 = g_w / (1.0 + e21)
    w2 = g_w * e21 / (1.0 + e21)

    sel1 = lane == i1
    sel2 = lane == i2
    onehot = jnp.where(sel1 | sel2, 1.0, 0.0).astype(BF16)
    ri = lax.broadcasted_iota(I32, (tm, tm), 0)
    ci = lax.broadcasted_iota(I32, (tm, tm), 1)
    tri = jnp.where(ci < ri, 1.0, 0.0).astype(BF16)
    base = carry[0:1, :]
    excl = jnp.dot(tri, onehot, preferred_element_type=F32) + base
    r1 = jnp.sum(jnp.where(sel1, excl, 0.0), axis=-1, keepdims=True)
    r2 = jnp.sum(jnp.where(sel2, excl, 0.0), axis=-1, keepdims=True)
    total = base + jnp.sum(jnp.where(sel1 | sel2, 1.0, 0.0), axis=0, keepdims=True)
    carry[...] = jnp.broadcast_to(total, carry.shape)
    cnt_ref[...] = jnp.broadcast_to(total, cnt_ref.shape)

    meta_ref[...] = (jnp.where(lane == 0, i1, 0.0) + jnp.where(lane == 1, i2, 0.0)
                     + jnp.where(lane == 2, r1, 0.0) + jnp.where(lane == 3, r2, 0.0))
    w_ref[...] = jnp.where(lane == 0, w1, 0.0) + jnp.where(lane == 1, w2, 0.0)


def _route(logits, tm=512):
    t = logits.shape[0]
    return pl.pallas_call(
        _route_kernel,
        grid=(t // tm,),
        in_specs=[pl.BlockSpec((tm, LANES), lambda i: (i, 0))],
        out_specs=[pl.BlockSpec((tm, LANES), lambda i: (i, 0)),
                   pl.BlockSpec((tm, LANES), lambda i: (i, 0)),
                   pl.BlockSpec((SUBLANES, LANES), lambda i: (0, 0))],
        out_shape=[jax.ShapeDtypeStruct((t, LANES), F32),
                   jax.ShapeDtypeStruct((t, LANES), F32),
                   jax.ShapeDtypeStruct((SUBLANES, LANES), F32)],
        scratch_shapes=[pltpu.VMEM((SUBLANES, LANES), F32)],
        compiler_params=_cparams(("arbitrary",)),
        name="route_topk",
    )(logits)


def _lane_cumsum(x):
    lane = lax.broadcasted_iota(I32, x.shape, 1)
    sh = 1
    while sh < LANES:
        x = x + jnp.where(lane >= sh, pltpu.roll(x, sh, axis=1), 0)
        sh *= 2
    return x


def _dest_kernel(meta_ref, cnt_ref, dest_ref, be_ref, *, nb):
    tm = meta_ref.shape[0]
    shift = MOE_BLOCK.bit_length() - 1
    cnt = cnt_ref[...].astype(I32)
    padded = ((cnt + (MOE_BLOCK - 1)) >> shift) << shift
    pend_i = _lane_cumsum(padded)
    pend = pend_i.astype(F32)[0:1, :]
    poff = (pend_i - padded).astype(F32)[0:1, :]
    meta = meta_ref[...]
    lane = lax.broadcasted_iota(I32, (tm, LANES), 1).astype(F32)
    pick = lambda col: jnp.sum(jnp.where(lane == col, meta, 0.0), axis=-1, keepdims=True)
    i1, i2, r1, r2 = pick(0), pick(1), pick(2), pick(3)
    d1 = jnp.sum(jnp.where(lane == i1, poff, 0.0), axis=-1, keepdims=True) + r1
    d2 = jnp.sum(jnp.where(lane == i2, poff, 0.0), axis=-1, keepdims=True) + r2
    dest_ref[...] = (jnp.where(lane == 0, d1, 0.0) + jnp.where(lane == 1, d2, 0.0)).astype(I32)

    nrow = be_ref.shape[0]
    blk = lax.broadcasted_iota(I32, (nrow, LANES), 0).astype(F32)
    lane_b = lax.broadcasted_iota(I32, (nrow, LANES), 1).astype(F32)
    is_e = (lane_b >= ROUTE_LANE0) & (lane_b < ROUTE_LANE0 + N_EXPERTS)
    below = jnp.sum(jnp.where(is_e & (pend <= blk * MOE_BLOCK), 1.0, 0.0), axis=-1, keepdims=True)
    block_e = jnp.minimum(below, N_EXPERTS - 1.0)
    used = jnp.sum(jnp.where(lane_b == ROUTE_LANE0 + N_EXPERTS - 1, pend, 0.0),
                   axis=-1, keepdims=True) * (1.0 / MOE_BLOCK)
    be = jnp.where(blk[:, :1] == nb, used, block_e)
    be_ref[...] = jnp.broadcast_to(be, (nrow, LANES)).astype(I32)


def _dest(meta, cnt, nb, tm=512):
    t = meta.shape[0]
    nrow = -(-(nb + 1) // SUBLANES) * SUBLANES
    kern = functools.partial(_dest_kernel, nb=nb)
    return pl.pallas_call(
        kern,
        grid=(t // tm,),
        in_specs=[pl.BlockSpec((tm, LANES), lambda i: (i, 0)),
                  pl.BlockSpec((SUBLANES, LANES), lambda i: (0, 0))],
        out_specs=[pl.BlockSpec((tm, LANES), lambda i: (i, 0)),
                   pl.BlockSpec((nrow, LANES), lambda i: (0, 0))],
        out_shape=[jax.ShapeDtypeStruct((t, LANES), I32),
                   jax.ShapeDtypeStruct((nrow, LANES), I32)],
        compiler_params=_cparams(("arbitrary",)),
        name="route_dest",
    )(meta, cnt)


def _invert_kernel(dest_ref, row_ref, *, t):
    n_slot = row_ref.shape[0]
    n_asg = dest_ref.shape[0]
    spare = 2 * MOE_BLOCK

    def fill(s, c):
        row_ref[s] = 2 * t + (s & (spare - 1))
        return c

    lax.fori_loop(0, n_slot, fill, 0, unroll=8)

    def put(a, c):
        row_ref[dest_ref[a]] = a
        return c

    lax.fori_loop(0, n_asg, put, 0, unroll=8)


def _invert(dest_flat, n_slot, t):
    kern = functools.partial(_invert_kernel, t=t)
    return pl.pallas_call(
        kern,
        in_specs=[pl.BlockSpec(memory_space=pltpu.SMEM)],
        out_specs=pl.BlockSpec(memory_space=pltpu.SMEM),
        out_shape=jax.ShapeDtypeStruct((n_slot,), I32),
        name="route_invert",
    )(dest_flat)


def _moe_up_kernel(row_ref, be_ref, h2_hbm, wg_ref, wu_ref, o_ref, wbuf, xbuf, xb_scr, sem, *, nb):
    b = pl.program_id(0)
    used = be_ref[nb]
    blk = MOE_BLOCK
    fe = wg_ref.shape[2]
    lines = wg_ref.shape[1] // LANES
    t = h2_hbm.shape[0] // lines

    def gather_rows(block, rows):
        slot = block % 2
        base = block * blk
        for r in rows:
            row = row_ref[base + r]
            tok = jnp.where(row >= t, row - t, row)
            tok = jnp.where(tok >= t, 0, tok)
            pltpu.make_async_copy(h2_hbm.at[pl.ds(pl.multiple_of(tok * lines, lines), lines)],
                                  xbuf.at[slot, pl.ds(r * lines, lines)], sem.at[slot]).start()

    @pl.when((b == 0) & (used > 0))
    def _():
        gather_rows(0, range(blk))

    new_expert = (b == 0) | (be_ref[b] != be_ref[jnp.maximum(b - 1, 0)])

    @pl.when(new_expert & (b < used))
    def _():
        wbuf[:, :fe] = wg_ref[0].astype(BF16)
        wbuf[:, fe:] = wu_ref[0].astype(BF16)

    nchunk = fe // MOE_NCHUNK_COLS
    rows_per = blk // nchunk

    def step(prefetch):
        slot = b % 2
        pltpu.make_async_copy(xbuf.at[slot], xbuf.at[slot], sem.at[slot]).wait()
        for j in range(lines):
            xb_scr[:, j * LANES:(j + 1) * LANES] = _from_row_major(xbuf.at[slot], blk, j).astype(BF16)
        for c in range(nchunk):
            if prefetch:
                gather_rows(b + 1, range(c * rows_per, (c + 1) * rows_per))
            cols = slice(c * MOE_NCHUNK_COLS, (c + 1) * MOE_NCHUNK_COLS)
            ucols = slice(fe + c * MOE_NCHUNK_COLS, fe + (c + 1) * MOE_NCHUNK_COLS)
            g = jnp.dot(xb_scr[...], wbuf[:, cols], preferred_element_type=F32)
            u = jnp.dot(xb_scr[...], wbuf[:, ucols], preferred_element_type=F32)
            o_ref[:, cols] = (_silu(g) * u).astype(BF16)

    @pl.when(b + 1 < used)
    def _():
        step(True)

    @pl.when((b < used) & (b + 1 >= used))
    def _():
        step(False)

    @pl.when(b >= used)
    def _():
        o_ref[...] = jnp.zeros(o_ref.shape, BF16)


def _moe_up(slot_row, block_e, h2_rm, w_gate, w_up, nb):
    d, fe = w_gate.shape[1], w_gate.shape[2]
    kern = functools.partial(_moe_up_kernel, nb=nb)
    grid_spec = pltpu.PrefetchScalarGridSpec(
        num_scalar_prefetch=2,
        grid=(nb,),
        in_specs=[pl.BlockSpec(memory_space=pl.ANY),
                  pl.BlockSpec((1, d, fe), lambda b, s, e: (e[b], 0, 0)),
                  pl.BlockSpec((1, d, fe), lambda b, s, e: (e[b], 0, 0))],
        out_specs=pl.BlockSpec((MOE_BLOCK, fe), lambda b, s, e: (b, 0)),
        scratch_shapes=[pltpu.VMEM((d, 2 * fe), BF16),
                        pltpu.VMEM((2, MOE_BLOCK * (d // LANES), LANES), F32),
                        pltpu.VMEM((MOE_BLOCK, d), BF16),
                        pltpu.SemaphoreType.DMA((2,))],
    )
    return pl.pallas_call(
        kern,
        grid_spec=grid_spec,
        out_shape=jax.ShapeDtypeStruct((nb * MOE_BLOCK, fe), BF16),
        compiler_params=_cparams(("arbitrary",)),
        name="moe_up",
    )(slot_row, block_e, h2_rm, w_gate, w_up)


def _moe_down_kernel(row_ref, be_ref, hm_ref, wd_ref, ys_hbm, wbuf, obuf, sem, *, nb, t):
    b = pl.program_id(0)
    used = be_ref[nb]
    blk = MOE_BLOCK
    lines = wbuf.shape[1] // LANES

    def wait_slot(slot):
        pltpu.make_async_copy(obuf.at[slot], obuf.at[slot], sem.at[slot]).wait()

    @pl.when(b == 0)
    def _():
        obuf[...] = jnp.zeros(obuf.shape, F32)
        for half in range(2):
            spare = ys_hbm.at[pl.ds((2 * t + half * blk) * lines, blk * lines)]
            pltpu.make_async_copy(obuf.at[half], spare, sem.at[2]).start()
        for half in range(2):
            spare = ys_hbm.at[pl.ds((2 * t + half * blk) * lines, blk * lines)]
            pltpu.make_async_copy(obuf.at[half], spare, sem.at[2]).wait()

    @pl.when((b >= 2) & (b - 2 < used))
    def _():
        wait_slot(b % 2)

    bc = jnp.minimum(b, nb - 1)
    new_expert = (b == 0) | (be_ref[bc] != be_ref[jnp.maximum(bc - 1, 0)])

    @pl.when(new_expert & (b < used))
    def _():
        wbuf[...] = wd_ref[0].astype(BF16)

    d = wbuf.shape[1]
    nchunk = d // MOE_NCHUNK_COLS
    rows_per = blk // nchunk

    def step(scatter, matmul):
        slot = (b - 1) % 2
        base = (b - 1) * blk
        for c in range(nchunk):
            if scatter:
                for r in range(c * rows_per, (c + 1) * rows_per):
                    dst = pl.multiple_of(row_ref[base + r] * lines, lines)
                    pltpu.make_async_copy(obuf.at[slot, pl.ds(r * lines, lines)],
                                          ys_hbm.at[pl.ds(dst, lines)], sem.at[slot]).start()
            if matmul:
                cols = slice(c * MOE_NCHUNK_COLS, (c + 1) * MOE_NCHUNK_COLS)
                res = jnp.dot(hm_ref[...], wbuf[:, cols], preferred_element_type=F32)
                _to_row_major(obuf.at[b % 2], res, col0=c * (MOE_NCHUNK_COLS // LANES))

    @pl.when((b >= 1) & (b < used))
    def _():
        step(True, True)

    @pl.when((b == 0) & (b < used))
    def _():
        step(False, True)

    @pl.when((b >= 1) & (b == used))
    def _():
        step(True, False)

    @pl.when((b == nb) & (used == nb))
    def _():
        wait_slot((nb - 1) % 2)


def _moe_down(slot_row, block_e, hmid, w_down, nb, t):
    fe, d = w_down.shape[1], w_down.shape[2]
    kern = functools.partial(_moe_down_kernel, nb=nb, t=t)
    grid_spec = pltpu.PrefetchScalarGridSpec(
        num_scalar_prefetch=2,
        grid=(nb + 1,),
        in_specs=[pl.BlockSpec((MOE_BLOCK, fe), lambda b, s, e: (jnp.minimum(b, nb - 1), 0)),
                  pl.BlockSpec((1, fe, d), lambda b, s, e: (e[jnp.minimum(b, nb - 1)], 0, 0))],
        out_specs=pl.BlockSpec(memory_space=pl.ANY),
        scratch_shapes=[pltpu.VMEM((fe, d), BF16),
                        pltpu.VMEM((2, MOE_BLOCK * (d // LANES), LANES), F32),
                        pltpu.SemaphoreType.DMA((3,))],
    )
    return pl.pallas_call(
        kern,
        grid_spec=grid_spec,
        out_shape=jax.ShapeDtypeStruct(((2 * t + 2 * MOE_BLOCK) * (d // LANES), LANES), F32),
        compiler_params=_cparams(("arbitrary",)),
        name="moe_down",
    )(slot_row, block_e, hmid, w_down)


def _final_kernel(x1_ref, y0_ref, y1_ref, w_ref, mod_ref, gf_ref, o_ref, y_scr):
    tm, d = x1_ref.shape
    w = w_ref[...]
    w0 = w[:, 0:1]
    w1 = w[:, 1:2]
    for j in range(d // LANES):
        y_scr[:, j * LANES:(j + 1) * LANES] = (w0 * _from_row_major(y0_ref, tm, j)
                                               + w1 * _from_row_major(y1_ref, tm, j))
    x2 = x1_ref[...] + mod_ref[0][5:6, :] * y_scr[...]
    o_ref[...] = x2 * lax.rsqrt(jnp.mean(x2 * x2, axis=-1, keepdims=True) + NORM_EPS) * gf_ref[...]


def _final(x1, ys, w, mod3, g_final, seq, tm=512):
    t, d = x1.shape
    tpb = seq // tm
    return pl.pallas_call(
        _final_kernel,
        grid=(t // tm,),
        in_specs=[pl.BlockSpec((tm, d), lambda i: (i, 0)),
                  pl.BlockSpec((tm * (d // LANES), LANES), lambda i: (i, 0)),
                  pl.BlockSpec((tm * (d // LANES), LANES), lambda i: (i + t // tm, 0)),
                  pl.BlockSpec((tm, LANES), lambda i: (i, 0)),
                  pl.BlockSpec((1, N_MOD, d), lambda i: (i // tpb, 0, 0)),
                  pl.BlockSpec((1, d), lambda i: (0, 0))],
        out_specs=pl.BlockSpec((tm, d), lambda i: (i, 0)),
        out_shape=jax.ShapeDtypeStruct((t, d), F32),
        scratch_shapes=[pltpu.VMEM((tm, d), F32)],
        compiler_params=_cparams(("arbitrary",)),
        name="combine_final_norm",
    )(x1, ys, ys, w, mod3, g_final.reshape(1, d))


def _rope_tables(seq, dh):
    half = dh // 2
    inv_freq = 1.0 / (ROPE_BASE ** (jnp.arange(half, dtype=F32) * 2.0 / dh))
    ang = jnp.arange(seq, dtype=F32)[:, None] * inv_freq[None, :]
    cos, sin = jnp.cos(ang), jnp.sin(ang)
    return jnp.concatenate([cos, cos], axis=1), jnp.concatenate([-sin, sin], axis=1)


def _layer(x, mod, g_norm1, w_in, ssm_a_re, ssm_a_im, ssm_b_re, ssm_b_im, ssm_c_re, ssm_c_im,
           ssm_d, ssm_log_dt, w_glu, beta_ssm, beta_ret, w_out, g_norm2, w_rg, b_rg, w_re, b_re,
           w_gate, w_up, w_down):
    bn, seq, d = x.shape
    t = bn * seq
    x2d = x.reshape(t, d)
    mod3 = mod.reshape(bn, N_MOD, d)

    sw = w_glu.shape[0]
    u_wide, rest = _inproj(x2d, mod3, g_norm1, w_in.astype(BF16), bn, seq, sw)
    u_tb = u_wide.reshape(seq * bn, sw)

    ab_re, ab_im, bbt_re, bbt_im = _s5_disc(ssm_a_re, ssm_a_im, ssm_log_dt,
                                            jnp.swapaxes(ssm_b_re, 1, 2), jnp.swapaxes(ssm_b_im, 1, 2))
    bz, cz, a_re, a_im = _s5_pack(ab_re, ab_im, bbt_re, bbt_im, ssm_c_re, ssm_c_im)
    ypre_tb = _s5_scan(u_tb, bz, cz, a_re, a_im, ssm_d.reshape(-1), bn, seq)
    ypre_wide = ypre_tb.reshape(seq, bn * sw)

    heads = (d - sw) // RET_HEAD_DIM
    cosf, sinf = _rope_tables(seq, RET_HEAD_DIM)
    gamma = 1.0 - jnp.exp2(-5.0 - jnp.arange(heads, dtype=F32))
    log_g = jnp.broadcast_to(jnp.log(gamma)[:, None, None], (heads, 1, LANES))
    yret = _retention(rest.reshape(bn, seq, rest.shape[1]), cosf, sinf, log_g, beta_ret)

    n_route = N_GROUPS + N_EXPERTS
    wr = jnp.concatenate([w_rg, w_re, jnp.zeros((d, LANES - n_route), F32)], axis=1)
    br = jnp.concatenate([b_rg, b_re, jnp.zeros((LANES - n_route,), F32)]).reshape(1, LANES)
    wr_hi = wr.astype(BF16)
    wr_lo = (wr - wr_hi.astype(F32)).astype(BF16)
    x1, h2, logits = _mix(ypre_wide, yret.reshape(t, d - sw), x2d, mod3, w_glu.astype(BF16),
                          beta_ssm, w_out.astype(BF16), g_norm2, wr_hi, wr_lo, br, bn, seq)

    nb = -(-(t * 2) // MOE_BLOCK) + N_EXPERTS
    meta, w_tok, cnt = _route(logits)
    dest, be = _dest(meta, cnt, nb)
    slot_row = _invert(dest[:, :2].T.reshape(-1), nb * MOE_BLOCK, t)
    block_e = be[:nb + 1, 0]

    hmid = _moe_up(slot_row, block_e, h2, w_gate, w_up, nb)
    ys = _moe_down(slot_row, block_e, hmid, w_down, nb, t)
    return x1, ys, w_tok, mod3


def kernel(x, c, w_ada, b_ada, g_norm1, w_in, ssm_a_re, ssm_a_im, ssm_b_re, ssm_b_im, ssm_c_re,
           ssm_c_im, ssm_d, ssm_log_dt, w_glu, beta_ssm, beta_ret, w_out, g_norm2, w_router_group,
           b_router_group, w_router_expert, b_router_expert, w_gate, w_up, w_down, g_final):
    depth = w_ada.shape[0]
    bn, seq, d = x.shape
    assert depth == 1, "the final norm is fused into the single layer's last kernel"
    l = 0
    mod = _ada(c, w_ada[l], b_ada[l])
    x1, ys, w_tok, mod3 = _layer(
        x, mod, g_norm1[l], w_in[l], ssm_a_re[l], ssm_a_im[l], ssm_b_re[l], ssm_b_im[l],
        ssm_c_re[l], ssm_c_im[l], ssm_d[l], ssm_log_dt[l], w_glu[l], beta_ssm[l], beta_ret[l],
        w_out[l], g_norm2[l], w_router_group[l], b_router_group[l], w_router_expert[l],
        b_router_expert[l], w_gate[l], w_up[l], w_down[l])
    out = _final(x1, ys, w_tok, mod3, g_final, seq)
    return out.reshape(bn, seq, d)
```

```python
import functools
import math

import jax
import jax.numpy as jnp
from jax import lax
from jax.experimental import pallas as pl
from jax.experimental.pallas import tpu as pltpu

F32 = jnp.float32
BF16 = jnp.bfloat16
I32 = jnp.int32

SSM_GROUP = 16
SSM_STATE = 64
RET_HEAD_DIM = 128
ROPE_BASE = 10000.0
N_GROUPS = 4
EXPERTS_PER_GROUP = 8
N_EXPERTS = N_GROUPS * EXPERTS_PER_GROUP
N_MOD = 6
NORM_EPS = 1e-6
GN_EPS = 1e-5

LANES = 128
SUBLANES = 8
VMEM_LIMIT = 56 * 1024 * 1024

ROUTE_LANE0 = N_GROUPS
MOE_BLOCK = 256
MOE_NCHUNK_COLS = 256
RET_CHUNK = 256
S5_TT = 128


def _cparams(sem):
    return pltpu.CompilerParams(dimension_semantics=sem, vmem_limit_bytes=VMEM_LIMIT)


def _silu(x):
    return x * jax.nn.sigmoid(x)


def _ada_kernel(c_ref, w_ref, b_ref, o_ref):
    s = _silu(c_ref[...])
    o_ref[...] = jnp.dot(s.astype(BF16), w_ref[...].astype(BF16),
                         preferred_element_type=F32) + b_ref[...]


def _ada(c, w, b, tn=1024):
    bn, d = c.shape
    n = w.shape[1]
    return pl.pallas_call(
        _ada_kernel,
        grid=(n // tn,),
        in_specs=[pl.BlockSpec((bn, d), lambda j: (0, 0)),
                  pl.BlockSpec((d, tn), lambda j: (0, j)),
                  pl.BlockSpec((1, tn), lambda j: (0, j))],
        out_specs=pl.BlockSpec((bn, tn), lambda j: (0, j)),
        out_shape=jax.ShapeDtypeStruct((bn, n), F32),
        compiler_params=_cparams(("arbitrary",)),
        name="ada_mod",
    )(c, w, b.reshape(1, n))


def _inproj_kernel(x_ref, mod_ref, g_ref, w_ref, o_ref, h_scr):
    @pl.when(pl.program_id(1) == 0)
    def _():
        x = x_ref[...]
        y = x * lax.rsqrt(jnp.mean(x * x, axis=-1, keepdims=True) + NORM_EPS) * g_ref[...]
        m = mod_ref[0]
        h_scr[...] = (y * (1.0 + m[1:2, :]) + m[0:1, :]).astype(BF16)

    o_ref[...] = jnp.dot(h_scr[...], w_ref[...], preferred_element_type=F32).astype(BF16)


def _inproj(x2d, mod3, g1, w_in_bf, seq, tm=1024, tn=1024):
    t, d = x2d.shape
    n = w_in_bf.shape[1]
    tm = min(tm, seq)
    tn = math.gcd(tn, n)
    tpb = seq // tm
    return pl.pallas_call(
        _inproj_kernel,
        grid=(t // tm, n // tn),
        in_specs=[pl.BlockSpec((tm, d), lambda i, j: (i, 0)),
                  pl.BlockSpec((1, N_MOD, d), lambda i, j: (i // tpb, 0, 0)),
                  pl.BlockSpec((1, d), lambda i, j: (0, 0)),
                  pl.BlockSpec((d, tn), lambda i, j: (0, j))],
        out_specs=pl.BlockSpec((tm, tn), lambda i, j: (i, j)),
        out_shape=jax.ShapeDtypeStruct((t, n), BF16),
        scratch_shapes=[pltpu.VMEM((tm, d), BF16)],
        compiler_params=_cparams(("arbitrary", "arbitrary")),
        name="norm_inproj",
    )(x2d, mod3, g1.reshape(1, d), w_in_bf)


def _s5_disc_kernel(are_ref, aim_ref, ldt_ref, bre_ref, bim_ref,
                    abre_ref, abim_ref, bbre_ref, bbim_ref):
    a_re = are_ref[...]
    a_im = aim_ref[...]
    dt = jnp.exp(ldt_ref[...])
    mag = jnp.exp(dt * a_re)
    ab_re = mag * jnp.cos(dt * a_im)
    ab_im = mag * jnp.sin(dt * a_im)
    inv_abs2 = 1.0 / (a_re * a_re + a_im * a_im)
    n_re = ab_re - 1.0
    f_re = (n_re * a_re + ab_im * a_im) * inv_abs2
    f_im = (ab_im * a_re - n_re * a_im) * inv_abs2
    abre_ref[...] = ab_re
    abim_ref[...] = ab_im
    b_re = bre_ref[...]
    b_im = bim_ref[...]
    fr = f_re[:, None, :]
    fi = f_im[:, None, :]
    bbre_ref[...] = fr * b_re - fi * b_im
    bbim_ref[...] = fr * b_im + fi * b_re


def _s5_disc(a_re, a_im, log_dt, bt_re, bt_im):
    g, p = a_re.shape
    h = bt_re.shape[1]
    return pl.pallas_call(
        _s5_disc_kernel,
        out_shape=[jax.ShapeDtypeStruct((g, p), F32), jax.ShapeDtypeStruct((g, p), F32),
                   jax.ShapeDtypeStruct((g, h, p), F32), jax.ShapeDtypeStruct((g, h, p), F32)],
        name="s5_discretise",
    )(a_re, a_im, log_dt.reshape(g, 1), bt_re, bt_im)


def _s5_pack(ab_re, ab_im, bbt_re, bbt_im, c_re, c_im):
    g, h, p = bbt_re.shape
    npair = g // 2
    ppb = LANES // (2 * h)
    eye2 = jnp.eye(2, dtype=F32)
    qsel = jax.nn.one_hot(jnp.arange(npair) % ppb, ppb, dtype=F32)

    def in_mat(bbt):
        b4 = bbt.reshape(npair, 2, h, p)
        return jnp.einsum('ab,xahp->xahbp', eye2, b4).reshape(npair, 2 * h, 2 * p)

    bsmall = jnp.concatenate([in_mat(bbt_re), in_mat(bbt_im)], axis=-1)
    bz = jnp.einsum('xq,xrc->xqrc', qsel, bsmall).reshape(npair, LANES, 4 * p)

    def out_mat(c):
        c4 = jnp.swapaxes(c, 1, 2).reshape(npair, 2, p, h)
        return jnp.einsum('ab,xaph->xapbh', eye2, c4).reshape(npair, 2 * p, 2 * h)

    csmall = jnp.concatenate([out_mat(c_re), -out_mat(c_im)], axis=1)
    cz = jnp.einsum('xq,xrc->xrqc', qsel, csmall).reshape(npair, 4 * p, LANES)
    a_re = ab_re.reshape(npair // ppb, ppb, 2 * p)
    a_im = ab_im.reshape(npair // ppb, ppb, 2 * p)
    return bz.astype(BF16), cz.astype(BF16), a_re, a_im


def _s5_kernel(u_ref, bz_ref, cz_ref, are_ref, aim_ref, d_ref, o_ref, state, buf, *, ppb):
    t = pl.program_id(0)
    kb = pl.program_id(1)
    bn, tt, _ = u_ref.shape
    rt = bn * tt

    @pl.when(t == 0)
    def _():
        state[kb] = jnp.zeros(state.shape[1:], F32)

    u = u_ref[...].reshape(rt, LANES)
    for q in range(ppb):
        bu = jnp.dot(u, bz_ref[q], preferred_element_type=F32)
        for b in range(bn):
            rows = pl.ds(b, tt, stride=bn)
            buf[2 * q, rows, :] = bu[b * tt:(b + 1) * tt, :LANES]
            buf[2 * q + 1, rows, :] = bu[b * tt:(b + 1) * tt, LANES:]

    a_re = [jnp.broadcast_to(are_ref[0, q:q + 1, :], (bn, LANES)) for q in range(ppb)]
    a_im = [jnp.broadcast_to(aim_ref[0, q:q + 1, :], (bn, LANES)) for q in range(ppb)]
    s0 = tuple(state[kb, j] for j in range(2 * ppb))

    def step(i, s):
        rows = pl.ds(pl.multiple_of(i * bn, bn), bn)
        new = []
        for q in range(ppb):
            s_re, s_im = s[2 * q], s[2 * q + 1]
            n_re = a_re[q] * s_re - a_im[q] * s_im + buf[2 * q, rows, :]
            n_im = a_re[q] * s_im + a_im[q] * s_re + buf[2 * q + 1, rows, :]
            buf[2 * q, rows, :] = n_re
            buf[2 * q + 1, rows, :] = n_im
            new += [n_re, n_im]
        return tuple(new)

    s1 = lax.fori_loop(0, tt, step, s0, unroll=4)
    for j in range(2 * ppb):
        state[kb, j] = s1[j]

    acc = None
    for q in range(ppb):
        lhs = jnp.concatenate([buf[2 * q], buf[2 * q + 1]], axis=1).astype(BF16)
        part = jnp.dot(lhs, cz_ref[q], preferred_element_type=F32)
        acc = part if acc is None else acc + part
    buf[2 * ppb] = acc
    d_skip = d_ref[...]
    for b in range(bn):
        o_ref[b] = buf[2 * ppb, pl.ds(b, tt, stride=bn), :] + d_skip * u_ref[b].astype(F32)


def _s5_scan(u3, bz, cz, a_re, a_im, d_skip, tt=S5_TT):
    bn, seq, width = u3.shape[0], u3.shape[1], d_skip.shape[0]
    nkb = width // LANES
    ppb = bz.shape[0] // nkb
    kern = functools.partial(_s5_kernel, ppb=ppb)
    return pl.pallas_call(
        kern,
        grid=(seq // tt, nkb),
        in_specs=[pl.BlockSpec((bn, tt, LANES), lambda t, k: (0, t, k)),
                  pl.BlockSpec((ppb,) + bz.shape[1:], lambda t, k: (k, 0, 0)),
                  pl.BlockSpec((ppb,) + cz.shape[1:], lambda t, k: (k, 0, 0)),
                  pl.BlockSpec((1,) + a_re.shape[1:], lambda t, k: (k, 0, 0)),
                  pl.BlockSpec((1,) + a_im.shape[1:], lambda t, k: (k, 0, 0)),
                  pl.BlockSpec((1, LANES), lambda t, k: (0, k))],
        out_specs=pl.BlockSpec((bn, tt, LANES), lambda t, k: (0, t, k)),
        out_shape=jax.ShapeDtypeStruct((bn, seq, width), F32),
        scratch_shapes=[pltpu.VMEM((nkb, 2 * ppb, bn, LANES), F32),
                        pltpu.VMEM((2 * ppb + 1, bn * tt, LANES), F32)],
        compiler_params=_cparams(("arbitrary", "arbitrary")),
        name="s5_scan",
    )(u3, bz, cz, a_re, a_im, d_skip.reshape(1, width))


def _ret_kernel(q_ref, k_ref, v_ref, g_ref, cq_ref, sq_ref, ck_ref, sk_ref, lg_ref, beta_ref, o_ref,
                q_scr, k_scr, kzt_scr, y_scr, *, chunk):
    seq, dh = q_ref.shape[1], q_ref.shape[2]
    lg = lg_ref[0][:, :1]
    ri = lax.broadcasted_iota(I32, (chunk, chunk), 0)
    ci = lax.broadcasted_iota(I32, (chunk, chunk), 1)
    rel = (ri - ci).astype(F32)
    decay = jnp.where(rel >= 0, jnp.exp(lg * jnp.maximum(rel, 0.0)), 0.0)
    idx = lax.broadcasted_iota(I32, (chunk, 1), 0).astype(F32)
    xi = jnp.exp(lg * (idx + 1.0))
    g_chunk = jnp.exp(lg * float(chunk))
    beta = beta_ref[...]
    half = dh // 2
    pr = lax.broadcasted_iota(I32, (dh, dh), 0)
    pc = lax.broadcasted_iota(I32, (dh, dh), 1)
    swap = jnp.where(((pr + half) & (dh - 1)) == pc, 1.0, 0.0).astype(BF16)

    def rope(x_ref, cos_ref, sin_ref, sl):
        xb = x_ref[0, sl, :]
        rolled = jnp.dot(xb, swap, preferred_element_type=F32).astype(BF16)
        return xb * cos_ref[sl, :] + rolled * sin_ref[sl, :]

    zeta = jnp.exp(lg * (chunk - 1.0 - idx))
    for n in range(seq // chunk):
        sl = slice(n * chunk, (n + 1) * chunk)
        q_scr[sl, :] = rope(q_ref, cq_ref, sq_ref, sl)
        kb = rope(k_ref, ck_ref, sk_ref, sl)
        k_scr[sl, :] = kb
        kzt_scr[:, sl] = (kb.astype(F32) * zeta).T.astype(BF16)

    decay = decay.astype(BF16)
    r = jnp.zeros((dh, dh), F32)
    for n in range(seq // chunk):
        sl = slice(n * chunk, (n + 1) * chunk)
        qb = q_scr[sl, :]
        v = v_ref[0, sl, :]
        s = lax.dot_general(qb, k_scr[sl, :], (((1,), (1,)), ((), ())),
                            preferred_element_type=F32).astype(BF16) * decay
        y = jnp.dot(s, v, preferred_element_type=F32)
        y_scr[sl, :] = y + jnp.dot(qb, r.astype(BF16), preferred_element_type=F32) * xi
        r = r * g_chunk + jnp.dot(kzt_scr[:, sl], v, preferred_element_type=F32)

    y = y_scr[...]
    mu = jnp.mean(y, axis=-1, keepdims=True)
    yc = y - mu
    var = jnp.mean(yc * yc, axis=-1, keepdims=True)
    yn = yc * lax.rsqrt(var + GN_EPS) * beta
    o_ref[0] = (_silu(g_ref[0].astype(F32)) * yn).astype(BF16)


def _retention(proj3, col0, cosf, sinf, log_g, beta_ret, chunk=RET_CHUNK):
    bn, seq, _ = proj3.shape
    dh = RET_HEAD_DIM
    heads = beta_ret.shape[0] // dh
    blk = lambda off: pl.BlockSpec((1, seq, dh), lambda b, h, off=off: (b, 0, col0 + off + h))
    k_scale = dh ** -0.5
    kern = functools.partial(_ret_kernel, chunk=chunk)
    return pl.pallas_call(
        kern,
        grid=(bn, heads),
        in_specs=[blk(0), blk(heads), blk(2 * heads), blk(3 * heads),
                  pl.BlockSpec((seq, dh), lambda b, h: (0, 0)),
                  pl.BlockSpec((seq, dh), lambda b, h: (0, 0)),
                  pl.BlockSpec((seq, dh), lambda b, h: (0, 0)),
                  pl.BlockSpec((seq, dh), lambda b, h: (0, 0)),
                  pl.BlockSpec((1, 1, LANES), lambda b, h: (h, 0, 0)),
                  pl.BlockSpec((1, dh), lambda b, h: (0, h))],
        out_specs=pl.BlockSpec((1, seq, dh), lambda b, h: (b, 0, h)),
        out_shape=jax.ShapeDtypeStruct((bn, seq, heads * dh), BF16),
        scratch_shapes=[pltpu.VMEM((seq, dh), BF16), pltpu.VMEM((seq, dh), BF16),
                        pltpu.VMEM((dh, seq), BF16), pltpu.VMEM((seq, dh), F32)],
        compiler_params=_cparams(("arbitrary", "arbitrary")),
        name="retention",
    )(proj3, proj3, proj3, proj3, cosf.astype(BF16), sinf.astype(BF16),
      (cosf * k_scale).astype(BF16), (sinf * k_scale).astype(BF16),
      log_g, beta_ret.reshape(1, heads * dh))


def _gelu_tanh(x):
    return 0.5 * x * (1.0 + jnp.tanh(math.sqrt(2.0 / math.pi) * (x + 0.044715 * (x * x * x))))


def _split_bf16(x):
    hi = x.astype(BF16)
    lo = (x - hi.astype(F32)).astype(BF16)
    return hi, lo


def _to_row_major(ref, val, col0=0):
    rows = val.shape[0]
    lines = ref.shape[0] // rows
    for j in range(val.shape[1] // LANES):
        ref[pl.ds(col0 + j, rows, stride=lines), :] = val[:, j * LANES:(j + 1) * LANES]


def _from_row_major(ref, rows, j):
    return ref[pl.ds(j, rows, stride=ref.shape[0] // rows), :]


def _mix_kernel(ys_ref, yr_ref, x_ref, mod_ref, wglu_ref, bssm_ref, wout_ref, g2_ref,
                wrh_ref, wrl_ref, br_ref, x1_ref, h2_ref, lg_ref):
    sw = ys_ref.shape[1]
    z = _gelu_tanh(ys_ref[...])
    gate = jax.nn.sigmoid(jnp.dot(z.astype(BF16), wglu_ref[...], preferred_element_type=F32))
    o = z * gate
    y_ssm = o * lax.rsqrt(jnp.mean(o * o, axis=-1, keepdims=True) + NORM_EPS) * bssm_ref[...]
    mixed = (jnp.dot(y_ssm.astype(BF16), wout_ref[:sw, :], preferred_element_type=F32)
             + jnp.dot(yr_ref[...], wout_ref[sw:, :], preferred_element_type=F32))
    m = mod_ref[0]
    x1 = x_ref[...] + m[2:3, :] * mixed
    x1_ref[...] = x1
    y = x1 * lax.rsqrt(jnp.mean(x1 * x1, axis=-1, keepdims=True) + NORM_EPS) * g2_ref[...]
    h2 = y * (1.0 + m[4:5, :]) + m[3:4, :]
    _to_row_major(h2_ref, h2)
    hi, lo = _split_bf16(h2)
    wrh = wrh_ref[...]
    lg_ref[...] = (jnp.dot(hi, wrh, preferred_element_type=F32)
                   + jnp.dot(lo, wrh, preferred_element_type=F32)
                   + jnp.dot(hi, wrl_ref[...], preferred_element_type=F32)
                   + br_ref[...])


def _mix(ypre_tb, yret, x2d, mod3, wglu_bf, beta_ssm, wout_bf, g2, wr_hi, wr_lo, br, bn, seq, tm=256):
    t, d = x2d.shape
    sw = wglu_bf.shape[0]
    tpb = seq // tm
    const = lambda shape: pl.BlockSpec(shape, lambda i: (0,) * len(shape),
                                       pipeline_mode=pl.Buffered(1))
    return pl.pallas_call(
        _mix_kernel,
        grid=(t // tm,),
        in_specs=[pl.BlockSpec((tm, sw), lambda i: (i, 0)),
                  pl.BlockSpec((tm, d - sw), lambda i: (i, 0)),
                  pl.BlockSpec((tm, d), lambda i: (i, 0)),
                  pl.BlockSpec((1, N_MOD, d), lambda i: (i // tpb, 0, 0)),
                  const((sw, sw)), const((1, sw)), const((d, d)), const((1, d)),
                  const((d, LANES)), const((d, LANES)), const((1, LANES))],
        out_specs=[pl.BlockSpec((tm, d), lambda i: (i, 0)),
                   pl.BlockSpec((tm * (d // LANES), LANES), lambda i: (i, 0)),
                   pl.BlockSpec((tm, LANES), lambda i: (i, 0))],
        out_shape=[jax.ShapeDtypeStruct((t, d), F32),
                   jax.ShapeDtypeStruct((t * (d // LANES), LANES), F32),
                   jax.ShapeDtypeStruct((t, LANES), F32)],
        compiler_params=_cparams(("arbitrary",)),
        name="mix_outproj_router",
    )(ypre_tb, yret, x2d, mod3, wglu_bf, beta_ssm.reshape(1, sw), wout_bf, g2.reshape(1, d),
      wr_hi, wr_lo, br)


def _route_kernel(lg_ref, meta_ref, w_ref, cnt_ref, carry):
    i = pl.program_id(0)
    tm = lg_ref.shape[0]

    @pl.when(i == 0)
    def _():
        carry[...] = jnp.zeros(carry.shape, F32)

    lg = lg_ref[...]
    lane = lax.broadcasted_iota(I32, (tm, LANES), 1).astype(F32)
    neg = jnp.float32(-jnp.inf)

    def first_max(vals):
        m = jnp.max(vals, axis=-1, keepdims=True)
        idx = jnp.min(jnp.where(vals == m, lane, float(LANES)), axis=-1, keepdims=True)
        return m, idx

    gl = jnp.where(lane < N_GROUPS, lg, neg)
    gmax, gsel = first_max(gl)
    g_w = 1.0 / jnp.sum(jnp.exp(gl - gmax), axis=-1, keepdims=True)
    lo = ROUTE_LANE0 + gsel * EXPERTS_PER_GROUP
    el = jnp.where((lane >= lo) & (lane < lo + EXPERTS_PER_GROUP), lg, neg)
    m1, i1 = first_max(el)
    m2, i2 = first_max(jnp.where(lane == i1, neg, el))
    e21 = jnp.exp(m2 - m1)
    w1 = g_w / (1.0 + e21)
    w2 = g_w * e21 / (1.0 + e21)

    sel1 = lane == i1
    sel2 = lane == i2
    onehot = jnp.where(sel1 | sel2, 1.0, 0.0).astype(BF16)
    ri = lax.broadcasted_iota(I32, (tm, tm), 0)
    ci = lax.broadcasted_iota(I32, (tm, tm), 1)
    tri = jnp.where(ci < ri, 1.0, 0.0).astype(BF16)
    base = carry[0:1, :]
    excl = jnp.dot(tri, onehot, preferred_element_type=F32) + base
    r1 = jnp.sum(jnp.where(sel1, excl, 0.0), axis=-1, keepdims=True)
    r2 = jnp.sum(jnp.where(sel2, excl, 0.0), axis=-1, keepdims=True)
    total = base + jnp.sum(jnp.where(sel1 | sel2, 1.0, 0.0), axis=0, keepdims=True)
    carry[...] = jnp.broadcast_to(total, carry.shape)
    cnt_ref[...] = jnp.broadcast_to(total, cnt_ref.shape)

    meta_ref[...] = (jnp.where(lane == 0, i1, 0.0) + jnp.where(lane == 1, i2, 0.0)
                     + jnp.where(lane == 2, r1, 0.0) + jnp.where(lane == 3, r2, 0.0))
    w_ref[...] = jnp.where(lane == 0, w1, 0.0) + jnp.where(lane == 1, w2, 0.0)


def _route(logits, tm=512):
    t = logits.shape[0]
    return pl.pallas_call(
        _route_kernel,
        grid=(t // tm,),
        in_specs=[pl.BlockSpec((tm, LANES), lambda i: (i, 0))],
        out_specs=[pl.BlockSpec((tm, LANES), lambda i: (i, 0)),
                   pl.BlockSpec((tm, LANES), lambda i: (i, 0)),
                   pl.BlockSpec((SUBLANES, LANES), lambda i: (0, 0))],
        out_shape=[jax.ShapeDtypeStruct((t, LANES), F32),
                   jax.ShapeDtypeStruct((t, LANES), F32),
                   jax.ShapeDtypeStruct((SUBLANES, LANES), F32)],
        scratch_shapes=[pltpu.VMEM((SUBLANES, LANES), F32)],
        compiler_params=_cparams(("arbitrary",)),
        name="route_topk",
    )(logits)


def _lane_cumsum(x):
    lane = lax.broadcasted_iota(I32, x.shape, 1)
    sh = 1
    while sh < LANES:
        x = x + jnp.where(lane >= sh, pltpu.roll(x, sh, axis=1), 0)
        sh *= 2
    return x


def _dest_kernel(meta_ref, cnt_ref, dest_ref, be_ref, *, nb):
    tm = meta_ref.shape[0]
    shift = MOE_BLOCK.bit_length() - 1
    cnt = cnt_ref[...].astype(I32)
    padded = ((cnt + (MOE_BLOCK - 1)) >> shift) << shift
    pend_i = _lane_cumsum(padded)
    pend = pend_i.astype(F32)[0:1, :]
    poff = (pend_i - padded).astype(F32)[0:1, :]
    meta = meta_ref[...]
    lane = lax.broadcasted_iota(I32, (tm, LANES), 1).astype(F32)
    pick = lambda col: jnp.sum(jnp.where(lane == col, meta, 0.0), axis=-1, keepdims=True)
    i1, i2, r1, r2 = pick(0), pick(1), pick(2), pick(3)
    d1 = jnp.sum(jnp.where(lane == i1, poff, 0.0), axis=-1, keepdims=True) + r1
    d2 = jnp.sum(jnp.where(lane == i2, poff, 0.0), axis=-1, keepdims=True) + r2
    dest_ref[...] = (jnp.where(lane == 0, d1, 0.0) + jnp.where(lane == 1, d2, 0.0)).astype(I32)

    nrow = be_ref.shape[0]
    blk = lax.broadcasted_iota(I32, (nrow, LANES), 0).astype(F32)
    lane_b = lax.broadcasted_iota(I32, (nrow, LANES), 1).astype(F32)
    is_e = (lane_b >= ROUTE_LANE0) & (lane_b < ROUTE_LANE0 + N_EXPERTS)
    below = jnp.sum(jnp.where(is_e & (pend <= blk * MOE_BLOCK), 1.0, 0.0), axis=-1, keepdims=True)
    block_e = jnp.minimum(below, N_EXPERTS - 1.0)
    used = jnp.sum(jnp.where(lane_b == ROUTE_LANE0 + N_EXPERTS - 1, pend, 0.0),
                   axis=-1, keepdims=True) * (1.0 / MOE_BLOCK)
    be = jnp.where(blk[:, :1] == nb, used, block_e)
    be_ref[...] = jnp.broadcast_to(be, (nrow, LANES)).astype(I32)


def _dest(meta, cnt, nb, tm=512):
    t = meta.shape[0]
    nrow = -(-(nb + 1) // SUBLANES) * SUBLANES
    kern = functools.partial(_dest_kernel, nb=nb)
    return pl.pallas_call(
        kern,
        grid=(t // tm,),
        in_specs=[pl.BlockSpec((tm, LANES), lambda i: (i, 0)),
                  pl.BlockSpec((SUBLANES, LANES), lambda i: (0, 0))],
        out_specs=[pl.BlockSpec((tm, LANES), lambda i: (i, 0)),
                   pl.BlockSpec((nrow, LANES), lambda i: (0, 0))],
        out_shape=[jax.ShapeDtypeStruct((t, LANES), I32),
                   jax.ShapeDtypeStruct((nrow, LANES), I32)],
        compiler_params=_cparams(("arbitrary",)),
        name="route_dest",
    )(meta, cnt)


def _invert_kernel(dest_ref, row_ref, *, t):
    n_slot = row_ref.shape[0]
    n_asg = dest_ref.shape[0]
    spare = 2 * MOE_BLOCK

    def fill(s, c):
        row_ref[s] = 2 * t + (s & (spare - 1))
        return c

    lax.fori_loop(0, n_slot, fill, 0, unroll=8)

    def put(a, c):
        row_ref[dest_ref[a]] = a
        return c

    lax.fori_loop(0, n_asg, put, 0, unroll=8)


def _invert(dest_flat, n_slot, t):
    kern = functools.partial(_invert_kernel, t=t)
    return pl.pallas_call(
        kern,
        in_specs=[pl.BlockSpec(memory_space=pltpu.SMEM)],
        out_specs=pl.BlockSpec(memory_space=pltpu.SMEM),
        out_shape=jax.ShapeDtypeStruct((n_slot,), I32),
        name="route_invert",
    )(dest_flat)


def _moe_up_kernel(row_ref, be_ref, h2_hbm, wg_ref, wu_ref, o_ref, wbuf, xbuf, xb_scr, sem, *, nb):
    b = pl.program_id(0)
    used = be_ref[nb]
    blk = MOE_BLOCK
    fe = wg_ref.shape[2]
    lines = wg_ref.shape[1] // LANES
    t = h2_hbm.shape[0] // lines

    def gather_rows(block, rows):
        slot = block % 2
        base = block * blk
        for r in rows:
            row = row_ref[base + r]
            tok = jnp.where(row >= t, row - t, row)
            tok = jnp.where(tok >= t, 0, tok)
            pltpu.make_async_copy(h2_hbm.at[pl.ds(pl.multiple_of(tok * lines, lines), lines)],
                                  xbuf.at[slot, pl.ds(r * lines, lines)],
                                  sem.at[slot]).start(priority=r % 2)

    @pl.when((b == 0) & (used > 0))
    def _():
        gather_rows(0, range(blk))

    new_expert = (b == 0) | (be_ref[b] != be_ref[jnp.maximum(b - 1, 0)])

    @pl.when(new_expert & (b < used))
    def _():
        wbuf[:, :fe] = wg_ref[0].astype(BF16)
        wbuf[:, fe:] = wu_ref[0].astype(BF16)

    nchunk = fe // MOE_NCHUNK_COLS
    rows_per = blk // nchunk

    def step(prefetch):
        slot = b % 2
        pltpu.make_async_copy(xbuf.at[slot], xbuf.at[slot], sem.at[slot]).wait()
        for j in range(lines):
            xb_scr[:, j * LANES:(j + 1) * LANES] = _from_row_major(xbuf.at[slot], blk, j).astype(BF16)
        for c in range(nchunk):
            if prefetch:
                gather_rows(b + 1, range(c * rows_per, (c + 1) * rows_per))
            cols = slice(c * MOE_NCHUNK_COLS, (c + 1) * MOE_NCHUNK_COLS)
            ucols = slice(fe + c * MOE_NCHUNK_COLS, fe + (c + 1) * MOE_NCHUNK_COLS)
            g = jnp.dot(xb_scr[...], wbuf[:, cols], preferred_element_type=F32)
            u = jnp.dot(xb_scr[...], wbuf[:, ucols], preferred_element_type=F32)
            o_ref[:, cols] = (_silu(g) * u).astype(BF16)

    @pl.when(b + 1 < used)
    def _():
        step(True)

    @pl.when((b < used) & (b + 1 >= used))
    def _():
        step(False)

    @pl.when(b >= used)
    def _():
        o_ref[...] = jnp.zeros(o_ref.shape, BF16)


def _moe_up(slot_row, block_e, h2_rm, w_gate, w_up, nb):
    d, fe = w_gate.shape[1], w_gate.shape[2]
    kern = functools.partial(_moe_up_kernel, nb=nb)
    grid_spec = pltpu.PrefetchScalarGridSpec(
        num_scalar_prefetch=2,
        grid=(nb,),
        in_specs=[pl.BlockSpec(memory_space=pl.ANY),
                  pl.BlockSpec((1, d, fe), lambda b, s, e: (e[b], 0, 0)),
                  pl.BlockSpec((1, d, fe), lambda b, s, e: (e[b], 0, 0))],
        out_specs=pl.BlockSpec((MOE_BLOCK, fe), lambda b, s, e: (b, 0)),
        scratch_shapes=[pltpu.VMEM((d, 2 * fe), BF16),
                        pltpu.VMEM((2, MOE_BLOCK * (d // LANES), LANES), F32),
                        pltpu.VMEM((MOE_BLOCK, d), BF16),
                        pltpu.SemaphoreType.DMA((2,))],
    )
    return pl.pallas_call(
        kern,
        grid_spec=grid_spec,
        out_shape=jax.ShapeDtypeStruct((nb * MOE_BLOCK, fe), BF16),
        compiler_params=_cparams(("arbitrary",)),
        name="moe_up",
    )(slot_row, block_e, h2_rm, w_gate, w_up)


def _moe_down_kernel(row_ref, be_ref, hm_ref, wd_ref, ys_hbm, wbuf, obuf, sem, *, nb, t):
    b = pl.program_id(0)
    used = be_ref[nb]
    blk = MOE_BLOCK
    lines = wbuf.shape[1] // LANES

    def wait_slot(slot):
        pltpu.make_async_copy(obuf.at[slot], obuf.at[slot], sem.at[slot]).wait()

    @pl.when(b == 0)
    def _():
        obuf[...] = jnp.zeros(obuf.shape, F32)
        for half in range(2):
            spare = ys_hbm.at[pl.ds((2 * t + half * blk) * lines, blk * lines)]
            pltpu.make_async_copy(obuf.at[half], spare, sem.at[2]).start()
        for half in range(2):
            spare = ys_hbm.at[pl.ds((2 * t + half * blk) * lines, blk * lines)]
            pltpu.make_async_copy(obuf.at[half], spare, sem.at[2]).wait()

    @pl.when((b >= 2) & (b - 2 < used))
    def _():
        wait_slot(b % 2)

    bc = jnp.minimum(b, nb - 1)
    new_expert = (b == 0) | (be_ref[bc] != be_ref[jnp.maximum(bc - 1, 0)])

    @pl.when(new_expert & (b < used))
    def _():
        wbuf[...] = wd_ref[0].astype(BF16)

    d = wbuf.shape[1]
    nchunk = d // MOE_NCHUNK_COLS
    rows_per = blk // nchunk

    def step(scatter, matmul):
        slot = (b - 1) % 2
        base = (b - 1) * blk
        for c in range(nchunk):
            if scatter:
                for r in range(c * rows_per, (c + 1) * rows_per):
                    dst = pl.multiple_of(row_ref[base + r] * lines, lines)
                    pltpu.make_async_copy(obuf.at[slot, pl.ds(r * lines, lines)],
                                          ys_hbm.at[pl.ds(dst, lines)],
                                          sem.at[slot]).start(priority=r % 2)
            if matmul:
                cols = slice(c * MOE_NCHUNK_COLS, (c + 1) * MOE_NCHUNK_COLS)
                res = jnp.dot(hm_ref[...], wbuf[:, cols], preferred_element_type=F32)
                _to_row_major(obuf.at[b % 2], res, col0=c * (MOE_NCHUNK_COLS // LANES))

    @pl.when((b >= 1) & (b < used))
    def _():
        step(True, True)

    @pl.when((b == 0) & (b < used))
    def _():
        step(False, True)

    @pl.when((b >= 1) & (b == used))
    def _():
        step(True, False)

    @pl.when((b == nb) & (used == nb))
    def _():
        wait_slot((nb - 1) % 2)


def _moe_down(slot_row, block_e, hmid, w_down, nb, t):
    fe, d = w_down.shape[1], w_down.shape[2]
    kern = functools.partial(_moe_down_kernel, nb=nb, t=t)
    grid_spec = pltpu.PrefetchScalarGridSpec(
        num_scalar_prefetch=2,
        grid=(nb + 1,),
        in_specs=[pl.BlockSpec((MOE_BLOCK, fe), lambda b, s, e: (jnp.minimum(b, nb - 1), 0)),
                  pl.BlockSpec((1, fe, d), lambda b, s, e: (e[jnp.minimum(b, nb - 1)], 0, 0))],
        out_specs=pl.BlockSpec(memory_space=pl.ANY),
        scratch_shapes=[pltpu.VMEM((fe, d), BF16),
                        pltpu.VMEM((2, MOE_BLOCK * (d // LANES), LANES), F32),
                        pltpu.SemaphoreType.DMA((3,))],
    )
    return pl.pallas_call(
        kern,
        grid_spec=grid_spec,
        out_shape=jax.ShapeDtypeStruct(((2 * t + 2 * MOE_BLOCK) * (d // LANES), LANES), F32),
        compiler_params=_cparams(("arbitrary",)),
        name="moe_down",
    )(slot_row, block_e, hmid, w_down)


def _final_kernel(x1_ref, y0_ref, y1_ref, w_ref, mod_ref, gf_ref, o_ref, y_scr):
    tm, d = x1_ref.shape
    w = w_ref[...]
    w0 = w[:, 0:1]
    w1 = w[:, 1:2]
    for j in range(d // LANES):
        y_scr[:, j * LANES:(j + 1) * LANES] = (w0 * _from_row_major(y0_ref, tm, j)
                                               + w1 * _from_row_major(y1_ref, tm, j))
    x2 = x1_ref[...] + mod_ref[0][5:6, :] * y_scr[...]
    o_ref[...] = x2 * lax.rsqrt(jnp.mean(x2 * x2, axis=-1, keepdims=True) + NORM_EPS) * gf_ref[...]


def _final(x1, ys, w, mod3, g_final, seq, tm=512):
    t, d = x1.shape
    tpb = seq // tm
    return pl.pallas_call(
        _final_kernel,
        grid=(t // tm,),
        in_specs=[pl.BlockSpec((tm, d), lambda i: (i, 0)),
                  pl.BlockSpec((tm * (d // LANES), LANES), lambda i: (i, 0)),
                  pl.BlockSpec((tm * (d // LANES), LANES), lambda i: (i + t // tm, 0)),
                  pl.BlockSpec((tm, LANES), lambda i: (i, 0)),
                  pl.BlockSpec((1, N_MOD, d), lambda i: (i // tpb, 0, 0)),
                  pl.BlockSpec((1, d), lambda i: (0, 0))],
        out_specs=pl.BlockSpec((tm, d), lambda i: (i, 0)),
        out_shape=jax.ShapeDtypeStruct((t, d), F32),
        scratch_shapes=[pltpu.VMEM((tm, d), F32)],
        compiler_params=_cparams(("arbitrary",)),
        name="combine_final_norm",
    )(x1, ys, ys, w, mod3, g_final.reshape(1, d))


def _rope_tables(seq, dh):
    half = dh // 2
    inv_freq = 1.0 / (ROPE_BASE ** (jnp.arange(half, dtype=F32) * 2.0 / dh))
    ang = jnp.arange(seq, dtype=F32)[:, None] * inv_freq[None, :]
    cos, sin = jnp.cos(ang), jnp.sin(ang)
    return jnp.concatenate([cos, cos], axis=1), jnp.concatenate([-sin, sin], axis=1)


def _layer(x, mod, g_norm1, w_in, ssm_a_re, ssm_a_im, ssm_b_re, ssm_b_im, ssm_c_re, ssm_c_im,
           ssm_d, ssm_log_dt, w_glu, beta_ssm, beta_ret, w_out, g_norm2, w_rg, b_rg, w_re, b_re,
           w_gate, w_up, w_down):
    bn, seq, d = x.shape
    t = bn * seq
    x2d = x.reshape(t, d)
    mod3 = mod.reshape(bn, N_MOD, d)

    sw = w_glu.shape[0]
    proj = _inproj(x2d, mod3, g_norm1, w_in.astype(BF16), seq)
    proj3 = proj.reshape(bn, seq, proj.shape[1])

    ab_re, ab_im, bbt_re, bbt_im = _s5_disc(ssm_a_re, ssm_a_im, ssm_log_dt,
                                            jnp.swapaxes(ssm_b_re, 1, 2), jnp.swapaxes(ssm_b_im, 1, 2))
    bz, cz, a_re, a_im = _s5_pack(ab_re, ab_im, bbt_re, bbt_im, ssm_c_re, ssm_c_im)
    ypre = _s5_scan(proj3, bz, cz, a_re, a_im, ssm_d.reshape(-1)).reshape(t, sw)

    heads = (d - sw) // RET_HEAD_DIM
    cosf, sinf = _rope_tables(seq, RET_HEAD_DIM)
    gamma = 1.0 - jnp.exp2(-5.0 - jnp.arange(heads, dtype=F32))
    log_g = jnp.broadcast_to(jnp.log(gamma)[:, None, None], (heads, 1, LANES))
    yret = _retention(proj3, sw // RET_HEAD_DIM, cosf, sinf, log_g, beta_ret)

    n_route = N_GROUPS + N_EXPERTS
    wr = jnp.concatenate([w_rg, w_re, jnp.zeros((d, LANES - n_route), F32)], axis=1)
    br = jnp.concatenate([b_rg, b_re, jnp.zeros((LANES - n_route,), F32)]).reshape(1, LANES)
    wr_hi = wr.astype(BF16)
    wr_lo = (wr - wr_hi.astype(F32)).astype(BF16)
    x1, h2, logits = _mix(ypre, yret.reshape(t, d - sw), x2d, mod3, w_glu.astype(BF16),
                          beta_ssm, w_out.astype(BF16), g_norm2, wr_hi, wr_lo, br, bn, seq)

    nb = -(-(t * 2) // MOE_BLOCK) + N_EXPERTS
    meta, w_tok, cnt = _route(logits)
    dest, be = _dest(meta, cnt, nb)
    slot_row = _invert(dest[:, :2].T.reshape(-1), nb * MOE_BLOCK, t)
    block_e = be[:nb + 1, 0]

    hmid = _moe_up(slot_row, block_e, h2, w_gate, w_up, nb)
    ys = _moe_down(slot_row, block_e, hmid, w_down, nb, t)
    return x1, ys, w_tok, mod3


def kernel(x, c, w_ada, b_ada, g_norm1, w_in, ssm_a_re, ssm_a_im, ssm_b_re, ssm_b_im, ssm_c_re,
           ssm_c_im, ssm_d, ssm_log_dt, w_glu, beta_ssm, beta_ret, w_out, g_norm2, w_router_group,
           b_router_group, w_router_expert, b_router_expert, w_gate, w_up, w_down, g_final):
    depth = w_ada.shape[0]
    bn, seq, d = x.shape
    assert depth == 1, "the final norm is fused into the single layer's last kernel"
    l = 0
    mod = _ada(c, w_ada[l], b_ada[l])
    x1, ys, w_tok, mod3 = _layer(
        x, mod, g_norm1[l], w_in[l], ssm_a_re[l], ssm_a_im[l], ssm_b_re[l], ssm_b_im[l],
        ssm_c_re[l], ssm_c_im[l], ssm_d[l], ssm_log_dt[l], w_glu[l], beta_ssm[l], beta_ret[l],
        w_out[l], g_norm2[l], w_router_group[l], b_router_group[l], w_router_expert[l],
        b_router_expert[l], w_gate[l], w_up[l], w_down[l])
    out = _final(x1, ys, w_tok, mod3, g_final, seq)
    return out.reshape(bn, seq, d)
```

```python
import functools
import math

import jax
import jax.numpy as jnp
from jax import lax
from jax.experimental import pallas as pl
from jax.experimental.pallas import tpu as pltpu

F32 = jnp.float32
BF16 = jnp.bfloat16
I32 = jnp.int32

SSM_GROUP = 16
SSM_STATE = 64
RET_HEAD_DIM = 128
ROPE_BASE = 10000.0
N_GROUPS = 4
EXPERTS_PER_GROUP = 8
N_EXPERTS = N_GROUPS * EXPERTS_PER_GROUP
N_MOD = 6
NORM_EPS = 1e-6
GN_EPS = 1e-5

LANES = 128
SUBLANES = 8
VMEM_LIMIT = 56 * 1024 * 1024

ROUTE_LANE0 = N_GROUPS
MOE_BLOCK = 256
MOE_NCHUNK_COLS = 256
RET_CHUNK = 256
S5_TT = 128


def _cparams(sem):
    return pltpu.CompilerParams(dimension_semantics=sem, vmem_limit_bytes=VMEM_LIMIT)


def _silu(x):
    return x * jax.nn.sigmoid(x)


def _ada_kernel(c_ref, w_ref, b_ref, o_ref):
    s = _silu(c_ref[...])
    o_ref[...] = jnp.dot(s.astype(BF16), w_ref[...].astype(BF16),
                         preferred_element_type=F32) + b_ref[...]


def _ada(c, w, b, tn=1024):
    bn, d = c.shape
    n = w.shape[1]
    return pl.pallas_call(
        _ada_kernel,
        grid=(n // tn,),
        in_specs=[pl.BlockSpec((bn, d), lambda j: (0, 0)),
                  pl.BlockSpec((d, tn), lambda j: (0, j)),
                  pl.BlockSpec((1, tn), lambda j: (0, j))],
        out_specs=pl.BlockSpec((bn, tn), lambda j: (0, j)),
        out_shape=jax.ShapeDtypeStruct((bn, n), F32),
        compiler_params=_cparams(("arbitrary",)),
        name="ada_mod",
    )(c, w, b.reshape(1, n))


def _inproj_kernel(x_ref, mod_ref, g_ref, w_ref, o_ref, h_scr):
    @pl.when(pl.program_id(1) == 0)
    def _():
        x = x_ref[...]
        y = x * lax.rsqrt(jnp.mean(x * x, axis=-1, keepdims=True) + NORM_EPS) * g_ref[...]
        m = mod_ref[0]
        h_scr[...] = (y * (1.0 + m[1:2, :]) + m[0:1, :]).astype(BF16)

    o_ref[...] = jnp.dot(h_scr[...], w_ref[...], preferred_element_type=F32).astype(BF16)


def _inproj(x2d, mod3, g1, w_in_bf, seq, tm=1024, tn=1024):
    t, d = x2d.shape
    n = w_in_bf.shape[1]
    tm = min(tm, seq)
    tn = math.gcd(tn, n)
    tpb = seq // tm
    return pl.pallas_call(
        _inproj_kernel,
        grid=(t // tm, n // tn),
        in_specs=[pl.BlockSpec((tm, d), lambda i, j: (i, 0)),
                  pl.BlockSpec((1, N_MOD, d), lambda i, j: (i // tpb, 0, 0)),
                  pl.BlockSpec((1, d), lambda i, j: (0, 0)),
                  pl.BlockSpec((d, tn), lambda i, j: (0, j))],
        out_specs=pl.BlockSpec((tm, tn), lambda i, j: (i, j)),
        out_shape=jax.ShapeDtypeStruct((t, n), BF16),
        scratch_shapes=[pltpu.VMEM((tm, d), BF16)],
        compiler_params=_cparams(("arbitrary", "arbitrary")),
        name="norm_inproj",
    )(x2d, mod3, g1.reshape(1, d), w_in_bf)


def _s5_disc_kernel(are_ref, aim_ref, ldt_ref, bre_ref, bim_ref,
                    abre_ref, abim_ref, bbre_ref, bbim_ref):
    a_re = are_ref[...]
    a_im = aim_ref[...]
    dt = jnp.exp(ldt_ref[...])
    mag = jnp.exp(dt * a_re)
    ab_re = mag * jnp.cos(dt * a_im)
    ab_im = mag * jnp.sin(dt * a_im)
    inv_abs2 = 1.0 / (a_re * a_re + a_im * a_im)
    n_re = ab_re - 1.0
    f_re = (n_re * a_re + ab_im * a_im) * inv_abs2
    f_im = (ab_im * a_re - n_re * a_im) * inv_abs2
    abre_ref[...] = ab_re
    abim_ref[...] = ab_im
    b_re = bre_ref[...]
    b_im = bim_ref[...]
    fr = f_re[:, None, :]
    fi = f_im[:, None, :]
    bbre_ref[...] = fr * b_re - fi * b_im
    bbim_ref[...] = fr * b_im + fi * b_re


def _s5_disc(a_re, a_im, log_dt, bt_re, bt_im):
    g, p = a_re.shape
    h = bt_re.shape[1]
    return pl.pallas_call(
        _s5_disc_kernel,
        out_shape=[jax.ShapeDtypeStruct((g, p), F32), jax.ShapeDtypeStruct((g, p), F32),
                   jax.ShapeDtypeStruct((g, h, p), F32), jax.ShapeDtypeStruct((g, h, p), F32)],
        name="s5_discretise",
    )(a_re, a_im, log_dt.reshape(g, 1), bt_re, bt_im)


def _s5_pack(ab_re, ab_im, bbt_re, bbt_im, c_re, c_im):
    g, h, p = bbt_re.shape
    npair = g // 2
    ppb = LANES // (2 * h)
    eye2 = jnp.eye(2, dtype=F32)
    qsel = jax.nn.one_hot(jnp.arange(npair) % ppb, ppb, dtype=F32)

    def in_mat(bbt):
        b4 = bbt.reshape(npair, 2, h, p)
        return jnp.einsum('ab,xahp->xahbp', eye2, b4).reshape(npair, 2 * h, 2 * p)

    bsmall = jnp.concatenate([in_mat(bbt_re), in_mat(bbt_im)], axis=-1)
    bz = jnp.einsum('xq,xrc->xqrc', qsel, bsmall).reshape(npair, LANES, 4 * p)

    def out_mat(c):
        c4 = jnp.swapaxes(c, 1, 2).reshape(npair, 2, p, h)
        return jnp.einsum('ab,xaph->xapbh', eye2, c4).reshape(npair, 2 * p, 2 * h)

    csmall = jnp.concatenate([out_mat(c_re), -out_mat(c_im)], axis=1)
    cz = jnp.einsum('xq,xrc->xrqc', qsel, csmall).reshape(npair, 4 * p, LANES)
    a_re = ab_re.reshape(npair // ppb, ppb, 2 * p)
    a_im = ab_im.reshape(npair // ppb, ppb, 2 * p)
    return bz.astype(BF16), cz.astype(BF16), a_re, a_im


def _s5_kernel(u_ref, bz_ref, cz_ref, are_ref, aim_ref, d_ref, o_ref, state, buf, *, ppb):
    t = pl.program_id(0)
    kb = pl.program_id(1)
    bn, tt, _ = u_ref.shape
    rt = bn * tt

    @pl.when(t == 0)
    def _():
        state[kb] = jnp.zeros(state.shape[1:], F32)

    u = u_ref[...].reshape(rt, LANES)
    for q in range(ppb):
        bu = jnp.dot(u, bz_ref[q], preferred_element_type=F32)
        for b in range(bn):
            rows = pl.ds(b, tt, stride=bn)
            buf[2 * q, rows, :] = bu[b * tt:(b + 1) * tt, :LANES]
            buf[2 * q + 1, rows, :] = bu[b * tt:(b + 1) * tt, LANES:]

    a_re = [jnp.broadcast_to(are_ref[0, q:q + 1, :], (bn, LANES)) for q in range(ppb)]
    a_im = [jnp.broadcast_to(aim_ref[0, q:q + 1, :], (bn, LANES)) for q in range(ppb)]
    s0 = tuple(state[kb, j] for j in range(2 * ppb))

    def step(i, s):
        rows = pl.ds(pl.multiple_of(i * bn, bn), bn)
        new = []
        for q in range(ppb):
            s_re, s_im = s[2 * q], s[2 * q + 1]
            n_re = a_re[q] * s_re - a_im[q] * s_im + buf[2 * q, rows, :]
            n_im = a_re[q] * s_im + a_im[q] * s_re + buf[2 * q + 1, rows, :]
            buf[2 * q, rows, :] = n_re
            buf[2 * q + 1, rows, :] = n_im
            new += [n_re, n_im]
        return tuple(new)

    s1 = lax.fori_loop(0, tt, step, s0, unroll=4)
    for j in range(2 * ppb):
        state[kb, j] = s1[j]

    acc = None
    for q in range(ppb):
        lhs = jnp.concatenate([buf[2 * q], buf[2 * q + 1]], axis=1).astype(BF16)
        part = jnp.dot(lhs, cz_ref[q], preferred_element_type=F32)
        acc = part if acc is None else acc + part
    buf[2 * ppb] = acc
    d_skip = d_ref[...]
    for b in range(bn):
        o_ref[b] = buf[2 * ppb, pl.ds(b, tt, stride=bn), :] + d_skip * u_ref[b].astype(F32)


def _s5_scan(u3, bz, cz, a_re, a_im, d_skip, tt=S5_TT):
    bn, seq, width = u3.shape[0], u3.shape[1], d_skip.shape[0]
    nkb = width // LANES
    ppb = bz.shape[0] // nkb
    kern = functools.partial(_s5_kernel, ppb=ppb)
    return pl.pallas_call(
        kern,
        grid=(seq // tt, nkb),
        in_specs=[pl.BlockSpec((bn, tt, LANES), lambda t, k: (0, t, k)),
                  pl.BlockSpec((ppb,) + bz.shape[1:], lambda t, k: (k, 0, 0)),
                  pl.BlockSpec((ppb,) + cz.shape[1:], lambda t, k: (k, 0, 0)),
                  pl.BlockSpec((1,) + a_re.shape[1:], lambda t, k: (k, 0, 0)),
                  pl.BlockSpec((1,) + a_im.shape[1:], lambda t, k: (k, 0, 0)),
                  pl.BlockSpec((1, LANES), lambda t, k: (0, k))],
        out_specs=pl.BlockSpec((bn, tt, LANES), lambda t, k: (0, t, k)),
        out_shape=jax.ShapeDtypeStruct((bn, seq, width), F32),
        scratch_shapes=[pltpu.VMEM((nkb, 2 * ppb, bn, LANES), F32),
                        pltpu.VMEM((2 * ppb + 1, bn * tt, LANES), F32)],
        compiler_params=_cparams(("arbitrary", "arbitrary")),
        name="s5_scan",
    )(u3, bz, cz, a_re, a_im, d_skip.reshape(1, width))


def _ret_kernel(q_ref, k_ref, v_ref, g_ref, cq_ref, sq_ref, ck_ref, sk_ref, lg_ref, beta_ref, o_ref,
                q_scr, k_scr, kzt_scr, y_scr, *, chunk):
    seq, dh = q_ref.shape[1], q_ref.shape[2]
    lg = lg_ref[0][:, :1]
    ri = lax.broadcasted_iota(I32, (chunk, chunk), 0)
    ci = lax.broadcasted_iota(I32, (chunk, chunk), 1)
    rel = (ri - ci).astype(F32)
    decay = jnp.where(rel >= 0, jnp.exp(lg * jnp.maximum(rel, 0.0)), 0.0)
    idx = lax.broadcasted_iota(I32, (chunk, 1), 0).astype(F32)
    xi = jnp.exp(lg * (idx + 1.0))
    g_chunk = jnp.exp(lg * float(chunk))
    beta = beta_ref[...]
    half = dh // 2
    pr = lax.broadcasted_iota(I32, (dh, dh), 0)
    pc = lax.broadcasted_iota(I32, (dh, dh), 1)
    swap = jnp.where(((pr + half) & (dh - 1)) == pc, 1.0, 0.0).astype(BF16)

    def rope(x_ref, cos_ref, sin_ref, sl):
        xb = x_ref[0, sl, :]
        rolled = jnp.dot(xb, swap, preferred_element_type=F32).astype(BF16)
        return xb * cos_ref[sl, :] + rolled * sin_ref[sl, :]

    zeta = jnp.exp(lg * (chunk - 1.0 - idx))
    for n in range(seq // chunk):
        sl = slice(n * chunk, (n + 1) * chunk)
        q_scr[sl, :] = rope(q_ref, cq_ref, sq_ref, sl)
        kb = rope(k_ref, ck_ref, sk_ref, sl)
        k_scr[sl, :] = kb
        kzt_scr[:, sl] = (kb.astype(F32) * zeta).T.astype(BF16)

    decay = decay.astype(BF16)
    r = jnp.zeros((dh, dh), F32)
    for n in range(seq // chunk):
        sl = slice(n * chunk, (n + 1) * chunk)
        qb = q_scr[sl, :]
        v = v_ref[0, sl, :]
        s = lax.dot_general(qb, k_scr[sl, :], (((1,), (1,)), ((), ())),
                            preferred_element_type=F32).astype(BF16) * decay
        y = jnp.dot(s, v, preferred_element_type=F32)
        y_scr[sl, :] = y + jnp.dot(qb, r.astype(BF16), preferred_element_type=F32) * xi
        r = r * g_chunk + jnp.dot(kzt_scr[:, sl], v, preferred_element_type=F32)

    y = y_scr[...]
    mu = jnp.mean(y, axis=-1, keepdims=True)
    yc = y - mu
    var = jnp.mean(yc * yc, axis=-1, keepdims=True)
    yn = yc * lax.rsqrt(var + GN_EPS) * beta
    o_ref[0] = (_silu(g_ref[0].astype(F32)) * yn).astype(BF16)


def _retention(proj3, col0, cosf, sinf, log_g, beta_ret, chunk=RET_CHUNK):
    bn, seq, _ = proj3.shape
    dh = RET_HEAD_DIM
    heads = beta_ret.shape[0] // dh
    blk = lambda off: pl.BlockSpec((1, seq, dh), lambda b, h, off=off: (b, 0, col0 + off + h))
    k_scale = dh ** -0.5
    kern = functools.partial(_ret_kernel, chunk=chunk)
    return pl.pallas_call(
        kern,
        grid=(bn, heads),
        in_specs=[blk(0), blk(heads), blk(2 * heads), blk(3 * heads),
                  pl.BlockSpec((seq, dh), lambda b, h: (0, 0)),
                  pl.BlockSpec((seq, dh), lambda b, h: (0, 0)),
                  pl.BlockSpec((seq, dh), lambda b, h: (0, 0)),
                  pl.BlockSpec((seq, dh), lambda b, h: (0, 0)),
                  pl.BlockSpec((1, 1, LANES), lambda b, h: (h, 0, 0)),
                  pl.BlockSpec((1, dh), lambda b, h: (0, h))],
        out_specs=pl.BlockSpec((1, seq, dh), lambda b, h: (b, 0, h)),
        out_shape=jax.ShapeDtypeStruct((bn, seq, heads * dh), BF16),
        scratch_shapes=[pltpu.VMEM((seq, dh), BF16), pltpu.VMEM((seq, dh), BF16),
                        pltpu.VMEM((dh, seq), BF16), pltpu.VMEM((seq, dh), F32)],
        compiler_params=_cparams(("arbitrary", "arbitrary")),
        name="retention",
    )(proj3, proj3, proj3, proj3, cosf.astype(BF16), sinf.astype(BF16),
      (cosf * k_scale).astype(BF16), (sinf * k_scale).astype(BF16),
      log_g, beta_ret.reshape(1, heads * dh))


def _gelu_tanh(x):
    return 0.5 * x * (1.0 + jnp.tanh(math.sqrt(2.0 / math.pi) * (x + 0.044715 * (x * x * x))))


def _split_bf16(x):
    hi = x.astype(BF16)
    lo = (x - hi.astype(F32)).astype(BF16)
    return hi, lo


def _mix_kernel(ys_ref, yr_ref, x_ref, mod_ref, wglu_ref, bssm_ref, wout_ref, g2_ref,
                wrh_ref, wrl_ref, br_ref, x1_ref, h2_ref, lg_ref):
    sw = ys_ref.shape[1]
    z = _gelu_tanh(ys_ref[...])
    gate = jax.nn.sigmoid(jnp.dot(z.astype(BF16), wglu_ref[...], preferred_element_type=F32))
    o = z * gate
    y_ssm = o * lax.rsqrt(jnp.mean(o * o, axis=-1, keepdims=True) + NORM_EPS) * bssm_ref[...]
    mixed = (jnp.dot(y_ssm.astype(BF16), wout_ref[:sw, :], preferred_element_type=F32)
             + jnp.dot(yr_ref[...], wout_ref[sw:, :], preferred_element_type=F32))
    m = mod_ref[0]
    x1 = x_ref[...] + m[2:3, :] * mixed
    x1_ref[...] = x1
    y = x1 * lax.rsqrt(jnp.mean(x1 * x1, axis=-1, keepdims=True) + NORM_EPS) * g2_ref[...]
    h2 = y * (1.0 + m[4:5, :]) + m[3:4, :]
    h2_ref[...] = h2
    hi, lo = _split_bf16(h2)
    wrh = wrh_ref[...]
    lg_ref[...] = (jnp.dot(hi, wrh, preferred_element_type=F32)
                   + jnp.dot(lo, wrh, preferred_element_type=F32)
                   + jnp.dot(hi, wrl_ref[...], preferred_element_type=F32)
                   + br_ref[...])


def _mix(ypre, yret, x2d, mod3, wglu_bf, beta_ssm, wout_bf, g2, wr_hi, wr_lo, br, seq, tm=256):
    t, d = x2d.shape
    sw = wglu_bf.shape[0]
    tpb = seq // tm
    const = lambda shape: pl.BlockSpec(shape, lambda i: (0,) * len(shape),
                                       pipeline_mode=pl.Buffered(1))
    return pl.pallas_call(
        _mix_kernel,
        grid=(t // tm,),
        in_specs=[pl.BlockSpec((tm, sw), lambda i: (i, 0)),
                  pl.BlockSpec((tm, d - sw), lambda i: (i, 0)),
                  pl.BlockSpec((tm, d), lambda i: (i, 0)),
                  pl.BlockSpec((1, N_MOD, d), lambda i: (i // tpb, 0, 0)),
                  const((sw, sw)), const((1, sw)), const((d, d)), const((1, d)),
                  const((d, LANES)), const((d, LANES)), const((1, LANES))],
        out_specs=[pl.BlockSpec((tm, d), lambda i: (i, 0)),
                   pl.BlockSpec((tm, d), lambda i: (i, 0)),
                   pl.BlockSpec((tm, LANES), lambda i: (i, 0))],
        out_shape=[jax.ShapeDtypeStruct((t, d), F32),
                   jax.ShapeDtypeStruct((t, d), F32),
                   jax.ShapeDtypeStruct((t, LANES), F32)],
        compiler_params=_cparams(("arbitrary",)),
        name="mix_outproj_router",
    )(ypre, yret, x2d, mod3, wglu_bf, beta_ssm.reshape(1, sw), wout_bf, g2.reshape(1, d),
      wr_hi, wr_lo, br)


def _route_kernel(lg_ref, meta_ref, w_ref, cnt_ref, carry):
    i = pl.program_id(0)
    tm = lg_ref.shape[0]

    @pl.when(i == 0)
    def _():
        carry[...] = jnp.zeros(carry.shape, F32)

    lg = lg_ref[...]
    lane = lax.broadcasted_iota(I32, (tm, LANES), 1).astype(F32)
    neg = jnp.float32(-jnp.inf)

    def first_max(vals):
        m = jnp.max(vals, axis=-1, keepdims=True)
        idx = jnp.min(jnp.where(vals == m, lane, float(LANES)), axis=-1, keepdims=True)
        return m, idx

    gl = jnp.where(lane < N_GROUPS, lg, neg)
    gmax, gsel = first_max(gl)
    g_w = 1.0 / jnp.sum(jnp.exp(gl - gmax), axis=-1, keepdims=True)
    lo = ROUTE_LANE0 + gsel * EXPERTS_PER_GROUP
    el = jnp.where((lane >= lo) & (lane < lo + EXPERTS_PER_GROUP), lg, neg)
    m1, i1 = first_max(el)
    m2, i2 = first_max(jnp.where(lane == i1, neg, el))
    e21 = jnp.exp(m2 - m1)
    w1 = g_w / (1.0 + e21)
    w2 = g_w * e21 / (1.0 + e21)

    sel1 = lane == i1
    sel2 = lane == i2
    onehot = jnp.where(sel1 | sel2, 1.0, 0.0).astype(BF16)
    ri = lax.broadcasted_iota(I32, (tm, tm), 0)
    ci = lax.broadcasted_iota(I32, (tm, tm), 1)
    tri = jnp.where(ci < ri, 1.0, 0.0).astype(BF16)
    base = carry[0:1, :]
    excl = jnp.dot(tri, onehot, preferred_element_type=F32) + base
    r1 = jnp.sum(jnp.where(sel1, excl, 0.0), axis=-1, keepdims=True)
    r2 = jnp.sum(jnp.where(sel2, excl, 0.0), axis=-1, keepdims=True)
    total = base + jnp.sum(jnp.where(sel1 | sel2, 1.0, 0.0), axis=0, keepdims=True)
    carry[...] = jnp.broadcast_to(total, carry.shape)
    cnt_ref[...] = jnp.broadcast_to(total, cnt_ref.shape)

    meta_ref[...] = (jnp.where(lane == 0, i1, 0.0) + jnp.where(lane == 1, i2, 0.0)
                     + jnp.where(lane == 2, r1, 0.0) + jnp.where(lane == 3, r2, 0.0))
    w_ref[...] = jnp.where(lane == 0, w1, 0.0) + jnp.where(lane == 1, w2, 0.0)


def _route(logits, tm=512):
    t = logits.shape[0]
    return pl.pallas_call(
        _route_kernel,
        grid=(t // tm,),
        in_specs=[pl.BlockSpec((tm, LANES), lambda i: (i, 0))],
        out_specs=[pl.BlockSpec((tm, LANES), lambda i: (i, 0)),
                   pl.BlockSpec((tm, LANES), lambda i: (i, 0)),
                   pl.BlockSpec((SUBLANES, LANES), lambda i: (0, 0))],
        out_shape=[jax.ShapeDtypeStruct((t, LANES), F32),
                   jax.ShapeDtypeStruct((t, LANES), F32),
                   jax.ShapeDtypeStruct((SUBLANES, LANES), F32)],
        scratch_shapes=[pltpu.VMEM((SUBLANES, LANES), F32)],
        compiler_params=_cparams(("arbitrary",)),
        name="route_topk",
    )(logits)


def _lane_cumsum(x):
    lane = lax.broadcasted_iota(I32, x.shape, 1)
    sh = 1
    while sh < LANES:
        x = x + jnp.where(lane >= sh, pltpu.roll(x, sh, axis=1), 0)
        sh *= 2
    return x


def _dest_kernel(meta_ref, cnt_ref, dest_ref, be_ref, *, nb):
    tm = meta_ref.shape[0]
    shift = MOE_BLOCK.bit_length() - 1
    cnt = cnt_ref[...].astype(I32)
    padded = ((cnt + (MOE_BLOCK - 1)) >> shift) << shift
    pend_i = _lane_cumsum(padded)
    pend = pend_i.astype(F32)[0:1, :]
    poff = (pend_i - padded).astype(F32)[0:1, :]
    meta = meta_ref[...]
    lane = lax.broadcasted_iota(I32, (tm, LANES), 1).astype(F32)
    pick = lambda col: jnp.sum(jnp.where(lane == col, meta, 0.0), axis=-1, keepdims=True)
    i1, i2, r1, r2 = pick(0), pick(1), pick(2), pick(3)
    d1 = jnp.sum(jnp.where(lane == i1, poff, 0.0), axis=-1, keepdims=True) + r1
    d2 = jnp.sum(jnp.where(lane == i2, poff, 0.0), axis=-1, keepdims=True) + r2
    dest_ref[...] = (jnp.where(lane == 0, d1, 0.0) + jnp.where(lane == 1, d2, 0.0)).astype(I32)

    nrow = be_ref.shape[0]
    blk = lax.broadcasted_iota(I32, (nrow, LANES), 0).astype(F32)
    lane_b = lax.broadcasted_iota(I32, (nrow, LANES), 1).astype(F32)
    is_e = (lane_b >= ROUTE_LANE0) & (lane_b < ROUTE_LANE0 + N_EXPERTS)
    below = jnp.sum(jnp.where(is_e & (pend <= blk * MOE_BLOCK), 1.0, 0.0), axis=-1, keepdims=True)
    block_e = jnp.minimum(below, N_EXPERTS - 1.0)
    used = jnp.sum(jnp.where(lane_b == ROUTE_LANE0 + N_EXPERTS - 1, pend, 0.0),
                   axis=-1, keepdims=True) * (1.0 / MOE_BLOCK)
    be = jnp.where(blk[:, :1] == nb, used, block_e)
    be_ref[...] = jnp.broadcast_to(be, (nrow, LANES)).astype(I32)


def _dest(meta, cnt, nb, tm=512):
    t = meta.shape[0]
    nrow = -(-(nb + 1) // SUBLANES) * SUBLANES
    kern = functools.partial(_dest_kernel, nb=nb)
    return pl.pallas_call(
        kern,
        grid=(t // tm,),
        in_specs=[pl.BlockSpec((tm, LANES), lambda i: (i, 0)),
                  pl.BlockSpec((SUBLANES, LANES), lambda i: (0, 0))],
        out_specs=[pl.BlockSpec((tm, LANES), lambda i: (i, 0)),
                   pl.BlockSpec((nrow, LANES), lambda i: (0, 0))],
        out_shape=[jax.ShapeDtypeStruct((t, LANES), I32),
                   jax.ShapeDtypeStruct((nrow, LANES), I32)],
        compiler_params=_cparams(("arbitrary",)),
        name="route_dest",
    )(meta, cnt)


def _invert_kernel(dest_ref, fill_hbm, row_ref, sem):
    cp = pltpu.make_async_copy(fill_hbm, row_ref, sem)
    cp.start()
    cp.wait()

    def put(a, c):
        row_ref[dest_ref[a]] = a
        return c

    lax.fori_loop(0, dest_ref.shape[0], put, 0, unroll=8)


def _invert(dest_flat, n_slot, t):
    spare = 2 * MOE_BLOCK
    fill = 2 * t + (jnp.arange(n_slot, dtype=I32) & (spare - 1))
    return pl.pallas_call(
        _invert_kernel,
        in_specs=[pl.BlockSpec(memory_space=pltpu.SMEM), pl.BlockSpec(memory_space=pl.ANY)],
        out_specs=pl.BlockSpec(memory_space=pltpu.SMEM),
        out_shape=jax.ShapeDtypeStruct((n_slot,), I32),
        scratch_shapes=[pltpu.SemaphoreType.DMA(())],
        name="route_invert",
    )(dest_flat, fill)


def _expert_weights(b, used, be_ref, w_hbms, wst, wsem, run_ref, load):
    def fetch(e, p):
        for i, w in enumerate(w_hbms):
            pltpu.make_async_copy(w.at[e], wst.at[p, i], wsem.at[p]).start()

    @pl.when((b == 0) & (used > 0))
    def _():
        run_ref[0] = 0
        fetch(be_ref[0], 0)

    bc = jnp.minimum(b, be_ref.shape[0] - 2)
    new_expert = (b == 0) | (be_ref[bc] != be_ref[jnp.maximum(bc - 1, 0)])

    @pl.when(new_expert & (b < used))
    def _():
        p = run_ref[0]
        for i, w in enumerate(w_hbms):
            pltpu.make_async_copy(w.at[0], wst.at[p, i], wsem.at[p]).wait()
        load(p)
        e = be_ref[b]
        nxt = lax.while_loop(lambda i: (i < used) & (be_ref[i] == e), lambda i: i + 1, b + 1)

        @pl.when(nxt < used)
        def _():
            fetch(be_ref[nxt], 1 - p)

        run_ref[0] = 1 - p


def _moe_up_kernel(row_ref, be_ref, h2_hbm, wg_hbm, wu_hbm, o_ref, wst, wbuf, xbuf, xb_scr, sem,
                   wsem, run_ref, *, nb):
    b = pl.program_id(0)
    used = be_ref[nb]
    blk = MOE_BLOCK
    fe = wg_hbm.shape[2]
    t = h2_hbm.shape[0]

    def gather_rows(block, rows):
        slot = block % 2
        base = block * blk
        for r in rows:
            row = row_ref[base + r]
            tok = jnp.where(row >= t, row - t, row)
            tok = jnp.where(tok >= t, 0, tok)
            pltpu.make_async_copy(h2_hbm.at[pl.ds(tok, 1)], xbuf.at[slot, pl.ds(r, 1)],
                                  sem.at[slot]).start(priority=r % 2)

    @pl.when((b == 0) & (used > 0))
    def _():
        gather_rows(0, range(blk))

    def load(p):
        wbuf[:, :fe] = wst[p, 0].astype(BF16)
        wbuf[:, fe:] = wst[p, 1].astype(BF16)

    _expert_weights(b, used, be_ref, (wg_hbm, wu_hbm), wst, wsem, run_ref, load)

    nchunk = fe // MOE_NCHUNK_COLS
    rows_per = blk // nchunk

    def step(prefetch):
        slot = b % 2
        pltpu.make_async_copy(xbuf.at[slot], xbuf.at[slot], sem.at[slot]).wait()
        xb_scr[...] = xbuf[slot].astype(BF16)
        for c in range(nchunk):
            if prefetch:
                gather_rows(b + 1, range(c * rows_per, (c + 1) * rows_per))
            cols = slice(c * MOE_NCHUNK_COLS, (c + 1) * MOE_NCHUNK_COLS)
            ucols = slice(fe + c * MOE_NCHUNK_COLS, fe + (c + 1) * MOE_NCHUNK_COLS)
            g = jnp.dot(xb_scr[...], wbuf[:, cols], preferred_element_type=F32)
            u = jnp.dot(xb_scr[...], wbuf[:, ucols], preferred_element_type=F32)
            o_ref[:, cols] = (_silu(g) * u).astype(BF16)

    @pl.when(b + 1 < used)
    def _():
        step(True)

    @pl.when((b < used) & (b + 1 >= used))
    def _():
        step(False)

    @pl.when(b >= used)
    def _():
        o_ref[...] = jnp.zeros(o_ref.shape, BF16)


def _moe_up(slot_row, block_e, h2, w_gate, w_up, nb):
    d, fe = w_gate.shape[1], w_gate.shape[2]
    kern = functools.partial(_moe_up_kernel, nb=nb)
    grid_spec = pltpu.PrefetchScalarGridSpec(
        num_scalar_prefetch=2,
        grid=(nb,),
        in_specs=[pl.BlockSpec(memory_space=pl.ANY),
                  pl.BlockSpec(memory_space=pl.ANY),
                  pl.BlockSpec(memory_space=pl.ANY)],
        out_specs=pl.BlockSpec((MOE_BLOCK, fe), lambda b, s, e: (b, 0)),
        scratch_shapes=[pltpu.VMEM((2, 2, d, fe), F32),
                        pltpu.VMEM((d, 2 * fe), BF16),
                        pltpu.VMEM((2, MOE_BLOCK, d), F32),
                        pltpu.VMEM((MOE_BLOCK, d), BF16),
                        pltpu.SemaphoreType.DMA((2,)),
                        pltpu.SemaphoreType.DMA((2,)),
                        pltpu.SMEM((1,), I32)],
    )
    return pl.pallas_call(
        kern,
        grid_spec=grid_spec,
        out_shape=jax.ShapeDtypeStruct((nb * MOE_BLOCK, fe), BF16),
        compiler_params=_cparams(("arbitrary",)),
        name="moe_up",
    )(slot_row, block_e, h2, w_gate, w_up)


def _moe_down_kernel(row_ref, be_ref, hm_ref, wd_hbm, ys_hbm, wst, wbuf, obuf, sem, wsem, run_ref,
                     *, nb, t):
    b = pl.program_id(0)
    used = be_ref[nb]
    blk = MOE_BLOCK

    def wait_slot(slot):
        pltpu.make_async_copy(obuf.at[slot], obuf.at[slot], sem.at[slot]).wait()

    @pl.when(b == 0)
    def _():
        obuf[...] = jnp.zeros(obuf.shape, F32)
        for half in range(2):
            spare = ys_hbm.at[pl.ds(2 * t + half * blk, blk)]
            pltpu.make_async_copy(obuf.at[half], spare, sem.at[2]).start()
        for half in range(2):
            spare = ys_hbm.at[pl.ds(2 * t + half * blk, blk)]
            pltpu.make_async_copy(obuf.at[half], spare, sem.at[2]).wait()

    @pl.when((b >= 2) & (b - 2 < used))
    def _():
        wait_slot(b % 2)

    def load(p):
        wbuf[...] = wst[p, 0].astype(BF16)

    _expert_weights(b, used, be_ref, (wd_hbm,), wst, wsem, run_ref, load)

    d = wbuf.shape[1]
    nchunk = d // MOE_NCHUNK_COLS
    rows_per = blk // nchunk

    def step(scatter, matmul):
        slot = (b - 1) % 2
        base = (b - 1) * blk
        for c in range(nchunk):
            if scatter:
                for r in range(c * rows_per, (c + 1) * rows_per):
                    pltpu.make_async_copy(obuf.at[slot, pl.ds(r, 1)],
                                          ys_hbm.at[pl.ds(row_ref[base + r], 1)],
                                          sem.at[slot]).start(priority=r % 2)
            if matmul:
                cols = slice(c * MOE_NCHUNK_COLS, (c + 1) * MOE_NCHUNK_COLS)
                obuf[b % 2, :, cols] = jnp.dot(hm_ref[...], wbuf[:, cols],
                                               preferred_element_type=F32)

    @pl.when((b >= 1) & (b < used))
    def _():
        step(True, True)

    @pl.when((b == 0) & (b < used))
    def _():
        step(False, True)

    @pl.when((b >= 1) & (b == used))
    def _():
        step(True, False)

    @pl.when((b == nb) & (used == nb))
    def _():
        wait_slot((nb - 1) % 2)


def _moe_down(slot_row, block_e, hmid, w_down, nb, t):
    fe, d = w_down.shape[1], w_down.shape[2]
    kern = functools.partial(_moe_down_kernel, nb=nb, t=t)
    grid_spec = pltpu.PrefetchScalarGridSpec(
        num_scalar_prefetch=2,
        grid=(nb + 1,),
        in_specs=[pl.BlockSpec((MOE_BLOCK, fe), lambda b, s, e: (jnp.minimum(b, nb - 1), 0)),
                  pl.BlockSpec(memory_space=pl.ANY)],
        out_specs=pl.BlockSpec(memory_space=pl.ANY),
        scratch_shapes=[pltpu.VMEM((2, 1, fe, d), F32),
                        pltpu.VMEM((fe, d), BF16),
                        pltpu.VMEM((2, MOE_BLOCK, d), F32),
                        pltpu.SemaphoreType.DMA((3,)),
                        pltpu.SemaphoreType.DMA((2,)),
                        pltpu.SMEM((1,), I32)],
    )
    return pl.pallas_call(
        kern,
        grid_spec=grid_spec,
        out_shape=jax.ShapeDtypeStruct((2 * t + 2 * MOE_BLOCK, d), F32),
        compiler_params=_cparams(("arbitrary",)),
        name="moe_down",
    )(slot_row, block_e, hmid, w_down)


def _final_kernel(x1_ref, y0_ref, y1_ref, w_ref, mod_ref, gf_ref, o_ref):
    w = w_ref[...]
    y = w[:, 0:1] * y0_ref[...] + w[:, 1:2] * y1_ref[...]
    x2 = x1_ref[...] + mod_ref[0][5:6, :] * y
    o_ref[...] = x2 * lax.rsqrt(jnp.mean(x2 * x2, axis=-1, keepdims=True) + NORM_EPS) * gf_ref[...]


def _final(x1, ys, w, mod3, g_final, seq, tm=512):
    t, d = x1.shape
    tpb = seq // tm
    return pl.pallas_call(
        _final_kernel,
        grid=(t // tm,),
        in_specs=[pl.BlockSpec((tm, d), lambda i: (i, 0)),
                  pl.BlockSpec((tm, d), lambda i: (i, 0)),
                  pl.BlockSpec((tm, d), lambda i: (i + t // tm, 0)),
                  pl.BlockSpec((tm, LANES), lambda i: (i, 0)),
                  pl.BlockSpec((1, N_MOD, d), lambda i: (i // tpb, 0, 0)),
                  pl.BlockSpec((1, d), lambda i: (0, 0))],
        out_specs=pl.BlockSpec((tm, d), lambda i: (i, 0)),
        out_shape=jax.ShapeDtypeStruct((t, d), F32),
        compiler_params=_cparams(("arbitrary",)),
        name="combine_final_norm",
    )(x1, ys, ys, w, mod3, g_final.reshape(1, d))


def _rope_tables(seq, dh):
    half = dh // 2
    inv_freq = 1.0 / (ROPE_BASE ** (jnp.arange(half, dtype=F32) * 2.0 / dh))
    ang = jnp.arange(seq, dtype=F32)[:, None] * inv_freq[None, :]
    cos, sin = jnp.cos(ang), jnp.sin(ang)
    return jnp.concatenate([cos, cos], axis=1), jnp.concatenate([-sin, sin], axis=1)


def _layer(x, mod, g_norm1, w_in, ssm_a_re, ssm_a_im, ssm_b_re, ssm_b_im, ssm_c_re, ssm_c_im,
           ssm_d, ssm_log_dt, w_glu, beta_ssm, beta_ret, w_out, g_norm2, w_rg, b_rg, w_re, b_re,
           w_gate, w_up, w_down):
    bn, seq, d = x.shape
    t = bn * seq
    x2d = x.reshape(t, d)
    mod3 = mod.reshape(bn, N_MOD, d)

    sw = w_glu.shape[0]
    proj = _inproj(x2d, mod3, g_norm1, w_in.astype(BF16), seq)
    proj3 = proj.reshape(bn, seq, proj.shape[1])

    ab_re, ab_im, bbt_re, bbt_im = _s5_disc(ssm_a_re, ssm_a_im, ssm_log_dt,
                                            jnp.swapaxes(ssm_b_re, 1, 2), jnp.swapaxes(ssm_b_im, 1, 2))
    bz, cz, a_re, a_im = _s5_pack(ab_re, ab_im, bbt_re, bbt_im, ssm_c_re, ssm_c_im)
    ypre = _s5_scan(proj3, bz, cz, a_re, a_im, ssm_d.reshape(-1)).reshape(t, sw)

    heads = (d - sw) // RET_HEAD_DIM
    cosf, sinf = _rope_tables(seq, RET_HEAD_DIM)
    gamma = 1.0 - jnp.exp2(-5.0 - jnp.arange(heads, dtype=F32))
    log_g = jnp.broadcast_to(jnp.log(gamma)[:, None, None], (heads, 1, LANES))
    yret = _retention(proj3, sw // RET_HEAD_DIM, cosf, sinf, log_g, beta_ret)

    n_route = N_GROUPS + N_EXPERTS
    wr = jnp.concatenate([w_rg, w_re, jnp.zeros((d, LANES - n_route), F32)], axis=1)
    br = jnp.concatenate([b_rg, b_re, jnp.zeros((LANES - n_route,), F32)]).reshape(1, LANES)
    wr_hi = wr.astype(BF16)
    wr_lo = (wr - wr_hi.astype(F32)).astype(BF16)
    x1, h2, logits = _mix(ypre, yret.reshape(t, d - sw), x2d, mod3, w_glu.astype(BF16),
                          beta_ssm, w_out.astype(BF16), g_norm2, wr_hi, wr_lo, br, seq)

    nb = -(-(t * 2) // MOE_BLOCK) + N_EXPERTS
    meta, w_tok, cnt = _route(logits)
    dest, be = _dest(meta, cnt, nb)
    slot_row = _invert(dest[:, :2].T.reshape(-1), nb * MOE_BLOCK, t)
    block_e = be[:nb + 1, 0]

    hmid = _moe_up(slot_row, block_e, h2, w_gate, w_up, nb)
    ys = _moe_down(slot_row, block_e, hmid, w_down, nb, t)
    return x1, ys, w_tok, mod3


def kernel(x, c, w_ada, b_ada, g_norm1, w_in, ssm_a_re, ssm_a_im, ssm_b_re, ssm_b_im, ssm_c_re,
           ssm_c_im, ssm_d, ssm_log_dt, w_glu, beta_ssm, beta_ret, w_out, g_norm2, w_router_group,
           b_router_group, w_router_expert, b_router_expert, w_gate, w_up, w_down, g_final):
    depth = w_ada.shape[0]
    bn, seq, d = x.shape
    assert depth == 1, "the final norm is fused into the single layer's last kernel"
    l = 0
    mod = _ada(c, w_ada[l], b_ada[l])
    x1, ys, w_tok, mod3 = _layer(
        x, mod, g_norm1[l], w_in[l], ssm_a_re[l], ssm_a_im[l], ssm_b_re[l], ssm_b_im[l],
        ssm_c_re[l], ssm_c_im[l], ssm_d[l], ssm_log_dt[l], w_glu[l], beta_ssm[l], beta_ret[l],
        w_out[l], g_norm2[l], w_router_group[l], b_router_group[l], w_router_expert[l],
        b_router_expert[l], w_gate[l], w_up[l], w_down[l])
    out = _final(x1, ys, w_tok, mod3, g_final, seq)
    return out.reshape(bn, seq, d)
```

```python
import functools
import math

import jax
import jax.numpy as jnp
from jax import lax
from jax.experimental import pallas as pl
from jax.experimental.pallas import tpu as pltpu

F32 = jnp.float32
BF16 = jnp.bfloat16
I32 = jnp.int32

SSM_GROUP = 16
SSM_STATE = 64
RET_HEAD_DIM = 128
ROPE_BASE = 10000.0
N_GROUPS = 4
EXPERTS_PER_GROUP = 8
N_EXPERTS = N_GROUPS * EXPERTS_PER_GROUP
N_MOD = 6
NORM_EPS = 1e-6
GN_EPS = 1e-5

LANES = 128
SUBLANES = 8
VMEM_LIMIT = 56 * 1024 * 1024

ROUTE_LANE0 = N_GROUPS
MOE_BLOCK = 256
MOE_NCHUNK_COLS = 256
RET_CHUNK = 256
S5_TT = 128


def _cparams(sem):
    return pltpu.CompilerParams(dimension_semantics=sem, vmem_limit_bytes=VMEM_LIMIT)


def _silu(x):
    return x * jax.nn.sigmoid(x)


def _ada_kernel(c_ref, w_ref, b_ref, o_ref):
    s = _silu(c_ref[...])
    o_ref[...] = jnp.dot(s.astype(BF16), w_ref[...].astype(BF16),
                         preferred_element_type=F32) + b_ref[...]


def _ada(c, w, b, tn=1024):
    bn, d = c.shape
    n = w.shape[1]
    return pl.pallas_call(
        _ada_kernel,
        grid=(n // tn,),
        in_specs=[pl.BlockSpec((bn, d), lambda j: (0, 0)),
                  pl.BlockSpec((d, tn), lambda j: (0, j)),
                  pl.BlockSpec((1, tn), lambda j: (0, j))],
        out_specs=pl.BlockSpec((bn, tn), lambda j: (0, j)),
        out_shape=jax.ShapeDtypeStruct((bn, n), F32),
        compiler_params=_cparams(("arbitrary",)),
        name="ada_mod",
    )(c, w, b.reshape(1, n))


def _inproj_kernel(x_ref, mod_ref, g_ref, w_ref, o_ref, h_scr):
    @pl.when(pl.program_id(1) == 0)
    def _():
        x = x_ref[...]
        y = x * lax.rsqrt(jnp.mean(x * x, axis=-1, keepdims=True) + NORM_EPS) * g_ref[...]
        m = mod_ref[0]
        h_scr[...] = (y * (1.0 + m[1:2, :]) + m[0:1, :]).astype(BF16)

    o_ref[...] = jnp.dot(h_scr[...], w_ref[...], preferred_element_type=F32).astype(BF16)


def _inproj(x2d, mod3, g1, w_in_bf, seq, tm=1024, tn=1024):
    t, d = x2d.shape
    n = w_in_bf.shape[1]
    tm = min(tm, seq)
    tn = math.gcd(tn, n)
    tpb = seq // tm
    return pl.pallas_call(
        _inproj_kernel,
        grid=(t // tm, n // tn),
        in_specs=[pl.BlockSpec((tm, d), lambda i, j: (i, 0)),
                  pl.BlockSpec((1, N_MOD, d), lambda i, j: (i // tpb, 0, 0)),
                  pl.BlockSpec((1, d), lambda i, j: (0, 0)),
                  pl.BlockSpec((d, tn), lambda i, j: (0, j))],
        out_specs=pl.BlockSpec((tm, tn), lambda i, j: (i, j)),
        out_shape=jax.ShapeDtypeStruct((t, n), BF16),
        scratch_shapes=[pltpu.VMEM((tm, d), BF16)],
        compiler_params=_cparams(("arbitrary", "arbitrary")),
        name="norm_inproj",
    )(x2d, mod3, g1.reshape(1, d), w_in_bf)


def _s5_disc_kernel(are_ref, aim_ref, ldt_ref, bre_ref, bim_ref,
                    abre_ref, abim_ref, bbre_ref, bbim_ref):
    a_re = are_ref[...]
    a_im = aim_ref[...]
    dt = jnp.exp(ldt_ref[...])
    mag = jnp.exp(dt * a_re)
    ab_re = mag * jnp.cos(dt * a_im)
    ab_im = mag * jnp.sin(dt * a_im)
    inv_abs2 = 1.0 / (a_re * a_re + a_im * a_im)
    n_re = ab_re - 1.0
    f_re = (n_re * a_re + ab_im * a_im) * inv_abs2
    f_im = (ab_im * a_re - n_re * a_im) * inv_abs2
    abre_ref[...] = ab_re
    abim_ref[...] = ab_im
    b_re = bre_ref[...]
    b_im = bim_ref[...]
    fr = f_re[:, None, :]
    fi = f_im[:, None, :]
    bbre_ref[...] = fr * b_re - fi * b_im
    bbim_ref[...] = fr * b_im + fi * b_re


def _s5_disc(a_re, a_im, log_dt, bt_re, bt_im):
    g, p = a_re.shape
    h = bt_re.shape[1]
    return pl.pallas_call(
        _s5_disc_kernel,
        out_shape=[jax.ShapeDtypeStruct((g, p), F32), jax.ShapeDtypeStruct((g, p), F32),
                   jax.ShapeDtypeStruct((g, h, p), F32), jax.ShapeDtypeStruct((g, h, p), F32)],
        name="s5_discretise",
    )(a_re, a_im, log_dt.reshape(g, 1), bt_re, bt_im)


def _s5_pack(ab_re, ab_im, bbt_re, bbt_im, c_re, c_im):
    g, h, p = bbt_re.shape
    npair = g // 2
    ppb = LANES // (2 * h)
    eye2 = jnp.eye(2, dtype=F32)
    qsel = jax.nn.one_hot(jnp.arange(npair) % ppb, ppb, dtype=F32)

    def in_mat(bbt):
        b4 = bbt.reshape(npair, 2, h, p)
        return jnp.einsum('ab,xahp->xahbp', eye2, b4).reshape(npair, 2 * h, 2 * p)

    bsmall = jnp.concatenate([in_mat(bbt_re), in_mat(bbt_im)], axis=-1)
    bz = jnp.einsum('xq,xrc->xqrc', qsel, bsmall).reshape(npair, LANES, 4 * p)

    def out_mat(c):
        c4 = jnp.swapaxes(c, 1, 2).reshape(npair, 2, p, h)
        return jnp.einsum('ab,xaph->xapbh', eye2, c4).reshape(npair, 2 * p, 2 * h)

    csmall = jnp.concatenate([out_mat(c_re), -out_mat(c_im)], axis=1)
    cz = jnp.einsum('xq,xrc->xrqc', qsel, csmall).reshape(npair, 4 * p, LANES)
    a_re = ab_re.reshape(npair // ppb, ppb, 2 * p)
    a_im = ab_im.reshape(npair // ppb, ppb, 2 * p)
    return bz.astype(BF16), cz.astype(BF16), a_re, a_im


def _s5_kernel(u_ref, bz_ref, cz_ref, are_ref, aim_ref, d_ref, o_ref, state, buf, *, ppb):
    t = pl.program_id(0)
    kb = pl.program_id(1)
    bn, tt, _ = u_ref.shape
    rt = bn * tt

    @pl.when(t == 0)
    def _():
        state[kb] = jnp.zeros(state.shape[1:], F32)

    u = u_ref[...].reshape(rt, LANES)
    for q in range(ppb):
        bu = jnp.dot(u, bz_ref[q], preferred_element_type=F32)
        for b in range(bn):
            rows = pl.ds(b, tt, stride=bn)
            buf[2 * q, rows, :] = bu[b * tt:(b + 1) * tt, :LANES]
            buf[2 * q + 1, rows, :] = bu[b * tt:(b + 1) * tt, LANES:]

    a_re = [jnp.broadcast_to(are_ref[0, q:q + 1, :], (bn, LANES)) for q in range(ppb)]
    a_im = [jnp.broadcast_to(aim_ref[0, q:q + 1, :], (bn, LANES)) for q in range(ppb)]
    s0 = tuple(state[kb, j] for j in range(2 * ppb))

    def step(i, s):
        rows = pl.ds(pl.multiple_of(i * bn, bn), bn)
        new = []
        for q in range(ppb):
            s_re, s_im = s[2 * q], s[2 * q + 1]
            n_re = a_re[q] * s_re - a_im[q] * s_im + buf[2 * q, rows, :]
            n_im = a_re[q] * s_im + a_im[q] * s_re + buf[2 * q + 1, rows, :]
            buf[2 * q, rows, :] = n_re
            buf[2 * q + 1, rows, :] = n_im
            new += [n_re, n_im]
        return tuple(new)

    s1 = lax.fori_loop(0, tt, step, s0, unroll=4)
    for j in range(2 * ppb):
        state[kb, j] = s1[j]

    acc = None
    for q in range(ppb):
        lhs = jnp.concatenate([buf[2 * q], buf[2 * q + 1]], axis=1).astype(BF16)
        part = jnp.dot(lhs, cz_ref[q], preferred_element_type=F32)
        acc = part if acc is None else acc + part
    buf[2 * ppb] = acc
    d_skip = d_ref[...]
    for b in range(bn):
        o_ref[b] = buf[2 * ppb, pl.ds(b, tt, stride=bn), :] + d_skip * u_ref[b].astype(F32)


def _s5_scan(u3, bz, cz, a_re, a_im, d_skip, tt=S5_TT):
    bn, seq, width = u3.shape[0], u3.shape[1], d_skip.shape[0]
    nkb = width // LANES
    ppb = bz.shape[0] // nkb
    kern = functools.partial(_s5_kernel, ppb=ppb)
    return pl.pallas_call(
        kern,
        grid=(seq // tt, nkb),
        in_specs=[pl.BlockSpec((bn, tt, LANES), lambda t, k: (0, t, k)),
                  pl.BlockSpec((ppb,) + bz.shape[1:], lambda t, k: (k, 0, 0)),
                  pl.BlockSpec((ppb,) + cz.shape[1:], lambda t, k: (k, 0, 0)),
                  pl.BlockSpec((1,) + a_re.shape[1:], lambda t, k: (k, 0, 0)),
                  pl.BlockSpec((1,) + a_im.shape[1:], lambda t, k: (k, 0, 0)),
                  pl.BlockSpec((1, LANES), lambda t, k: (0, k))],
        out_specs=pl.BlockSpec((bn, tt, LANES), lambda t, k: (0, t, k)),
        out_shape=jax.ShapeDtypeStruct((bn, seq, width), F32),
        scratch_shapes=[pltpu.VMEM((nkb, 2 * ppb, bn, LANES), F32),
                        pltpu.VMEM((2 * ppb + 1, bn * tt, LANES), F32)],
        compiler_params=_cparams(("arbitrary", "arbitrary")),
        name="s5_scan",
    )(u3, bz, cz, a_re, a_im, d_skip.reshape(1, width))


def _ret_kernel(q_ref, k_ref, v_ref, g_ref, cq_ref, sq_ref, ck_ref, sk_ref, lg_ref, beta_ref, o_ref,
                q_scr, k_scr, kzt_scr, y_scr, *, chunk):
    seq, dh = q_ref.shape[1], q_ref.shape[2]
    lg = lg_ref[0][:, :1]
    ri = lax.broadcasted_iota(I32, (chunk, chunk), 0)
    ci = lax.broadcasted_iota(I32, (chunk, chunk), 1)
    rel = (ri - ci).astype(F32)
    decay = jnp.where(rel >= 0, jnp.exp(lg * jnp.maximum(rel, 0.0)), 0.0)
    idx = lax.broadcasted_iota(I32, (chunk, 1), 0).astype(F32)
    xi = jnp.exp(lg * (idx + 1.0))
    g_chunk = jnp.exp(lg * float(chunk))
    beta = beta_ref[...]
    half = dh // 2
    pr = lax.broadcasted_iota(I32, (dh, dh), 0)
    pc = lax.broadcasted_iota(I32, (dh, dh), 1)
    swap = jnp.where(((pr + half) & (dh - 1)) == pc, 1.0, 0.0).astype(BF16)

    def rope(x_ref, cos_ref, sin_ref, sl):
        xb = x_ref[0, sl, :]
        rolled = jnp.dot(xb, swap, preferred_element_type=F32).astype(BF16)
        return xb * cos_ref[sl, :] + rolled * sin_ref[sl, :]

    zeta = jnp.exp(lg * (chunk - 1.0 - idx))
    for n in range(seq // chunk):
        sl = slice(n * chunk, (n + 1) * chunk)
        q_scr[sl, :] = rope(q_ref, cq_ref, sq_ref, sl)
        kb = rope(k_ref, ck_ref, sk_ref, sl)
        k_scr[sl, :] = kb
        kzt_scr[:, sl] = (kb.astype(F32) * zeta).T.astype(BF16)

    decay = decay.astype(BF16)
    r = jnp.zeros((dh, dh), F32)
    for n in range(seq // chunk):
        sl = slice(n * chunk, (n + 1) * chunk)
        qb = q_scr[sl, :]
        v = v_ref[0, sl, :]
        s = lax.dot_general(qb, k_scr[sl, :], (((1,), (1,)), ((), ())),
                            preferred_element_type=F32).astype(BF16) * decay
        y = jnp.dot(s, v, preferred_element_type=F32)
        y_scr[sl, :] = y + jnp.dot(qb, r.astype(BF16), preferred_element_type=F32) * xi
        r = r * g_chunk + jnp.dot(kzt_scr[:, sl], v, preferred_element_type=F32)

    y = y_scr[...]
    mu = jnp.mean(y, axis=-1, keepdims=True)
    yc = y - mu
    var = jnp.mean(yc * yc, axis=-1, keepdims=True)
    yn = yc * lax.rsqrt(var + GN_EPS) * beta
    o_ref[0] = (_silu(g_ref[0].astype(F32)) * yn).astype(BF16)


def _retention(proj3, col0, cosf, sinf, log_g, beta_ret, chunk=RET_CHUNK):
    bn, seq, _ = proj3.shape
    dh = RET_HEAD_DIM
    heads = beta_ret.shape[0] // dh
    blk = lambda off: pl.BlockSpec((1, seq, dh), lambda b, h, off=off: (b, 0, col0 + off + h))
    k_scale = dh ** -0.5
    kern = functools.partial(_ret_kernel, chunk=chunk)
    return pl.pallas_call(
        kern,
        grid=(bn, heads),
        in_specs=[blk(0), blk(heads), blk(2 * heads), blk(3 * heads),
                  pl.BlockSpec((seq, dh), lambda b, h: (0, 0)),
                  pl.BlockSpec((seq, dh), lambda b, h: (0, 0)),
                  pl.BlockSpec((seq, dh), lambda b, h: (0, 0)),
                  pl.BlockSpec((seq, dh), lambda b, h: (0, 0)),
                  pl.BlockSpec((1, 1, LANES), lambda b, h: (h, 0, 0)),
                  pl.BlockSpec((1, dh), lambda b, h: (0, h))],
        out_specs=pl.BlockSpec((1, seq, dh), lambda b, h: (b, 0, h)),
        out_shape=jax.ShapeDtypeStruct((bn, seq, heads * dh), BF16),
        scratch_shapes=[pltpu.VMEM((seq, dh), BF16), pltpu.VMEM((seq, dh), BF16),
                        pltpu.VMEM((dh, seq), BF16), pltpu.VMEM((seq, dh), F32)],
        compiler_params=_cparams(("arbitrary", "arbitrary")),
        name="retention",
    )(proj3, proj3, proj3, proj3, cosf.astype(BF16), sinf.astype(BF16),
      (cosf * k_scale).astype(BF16), (sinf * k_scale).astype(BF16),
      log_g, beta_ret.reshape(1, heads * dh))


def _gelu_tanh(x):
    return 0.5 * x * (1.0 + jnp.tanh(math.sqrt(2.0 / math.pi) * (x + 0.044715 * (x * x * x))))


def _split_bf16(x):
    hi = x.astype(BF16)
    lo = (x - hi.astype(F32)).astype(BF16)
    return hi, lo


def _mix_kernel(ys_ref, yr_ref, x_ref, mod_ref, wglu_ref, bssm_ref, wout_ref, g2_ref,
                wrh_ref, wrl_ref, br_ref, x1_ref, h2_ref, lg_ref):
    sw = ys_ref.shape[1]
    z = _gelu_tanh(ys_ref[...])
    gate = jax.nn.sigmoid(jnp.dot(z.astype(BF16), wglu_ref[...], preferred_element_type=F32))
    o = z * gate
    y_ssm = o * lax.rsqrt(jnp.mean(o * o, axis=-1, keepdims=True) + NORM_EPS) * bssm_ref[...]
    mixed = (jnp.dot(y_ssm.astype(BF16), wout_ref[:sw, :], preferred_element_type=F32)
             + jnp.dot(yr_ref[...], wout_ref[sw:, :], preferred_element_type=F32))
    m = mod_ref[0]
    x1 = x_ref[...] + m[2:3, :] * mixed
    x1_ref[...] = x1
    y = x1 * lax.rsqrt(jnp.mean(x1 * x1, axis=-1, keepdims=True) + NORM_EPS) * g2_ref[...]
    h2 = y * (1.0 + m[4:5, :]) + m[3:4, :]
    h2_ref[...] = h2
    hi, lo = _split_bf16(h2)
    wrh = wrh_ref[...]
    lg_ref[...] = (jnp.dot(hi, wrh, preferred_element_type=F32)
                   + jnp.dot(lo, wrh, preferred_element_type=F32)
                   + jnp.dot(hi, wrl_ref[...], preferred_element_type=F32)
                   + br_ref[...])


def _mix(ypre, yret, x2d, mod3, wglu_bf, beta_ssm, wout_bf, g2, wr_hi, wr_lo, br, seq, tm=256):
    t, d = x2d.shape
    sw = wglu_bf.shape[0]
    tpb = seq // tm
    const = lambda shape: pl.BlockSpec(shape, lambda i: (0,) * len(shape),
                                       pipeline_mode=pl.Buffered(1))
    return pl.pallas_call(
        _mix_kernel,
        grid=(t // tm,),
        in_specs=[pl.BlockSpec((tm, sw), lambda i: (i, 0)),
                  pl.BlockSpec((tm, d - sw), lambda i: (i, 0)),
                  pl.BlockSpec((tm, d), lambda i: (i, 0)),
                  pl.BlockSpec((1, N_MOD, d), lambda i: (i // tpb, 0, 0)),
                  const((sw, sw)), const((1, sw)), const((d, d)), const((1, d)),
                  const((d, LANES)), const((d, LANES)), const((1, LANES))],
        out_specs=[pl.BlockSpec((tm, d), lambda i: (i, 0)),
                   pl.BlockSpec((tm, d), lambda i: (i, 0)),
                   pl.BlockSpec((tm, LANES), lambda i: (i, 0))],
        out_shape=[jax.ShapeDtypeStruct((t, d), F32),
                   jax.ShapeDtypeStruct((t, d), F32),
                   jax.ShapeDtypeStruct((t, LANES), F32)],
        compiler_params=_cparams(("arbitrary",)),
        name="mix_outproj_router",
    )(ypre, yret, x2d, mod3, wglu_bf, beta_ssm.reshape(1, sw), wout_bf, g2.reshape(1, d),
      wr_hi, wr_lo, br)


def _route_kernel(lg_ref, meta_ref, w_ref, cnt_ref, carry):
    i = pl.program_id(0)
    tm = lg_ref.shape[0]

    @pl.when(i == 0)
    def _():
        carry[...] = jnp.zeros(carry.shape, F32)

    lg = lg_ref[...]
    lane = lax.broadcasted_iota(I32, (tm, LANES), 1).astype(F32)
    neg = jnp.float32(-jnp.inf)

    def first_max(vals):
        m = jnp.max(vals, axis=-1, keepdims=True)
        idx = jnp.min(jnp.where(vals == m, lane, float(LANES)), axis=-1, keepdims=True)
        return m, idx

    gl = jnp.where(lane < N_GROUPS, lg, neg)
    gmax, gsel = first_max(gl)
    g_w = 1.0 / jnp.sum(jnp.exp(gl - gmax), axis=-1, keepdims=True)
    lo = ROUTE_LANE0 + gsel * EXPERTS_PER_GROUP
    el = jnp.where((lane >= lo) & (lane < lo + EXPERTS_PER_GROUP), lg, neg)
    m1, i1 = first_max(el)
    m2, i2 = first_max(jnp.where(lane == i1, neg, el))
    e21 = jnp.exp(m2 - m1)
    w1 = g_w / (1.0 + e21)
    w2 = g_w * e21 / (1.0 + e21)

    sel1 = lane == i1
    sel2 = lane == i2
    onehot = jnp.where(sel1 | sel2, 1.0, 0.0).astype(BF16)
    ri = lax.broadcasted_iota(I32, (tm, tm), 0)
    ci = lax.broadcasted_iota(I32, (tm, tm), 1)
    tri = jnp.where(ci < ri, 1.0, 0.0).astype(BF16)
    base = carry[0:1, :]
    excl = jnp.dot(tri, onehot, preferred_element_type=F32) + base
    r1 = jnp.sum(jnp.where(sel1, excl, 0.0), axis=-1, keepdims=True)
    r2 = jnp.sum(jnp.where(sel2, excl, 0.0), axis=-1, keepdims=True)
    total = base + jnp.sum(jnp.where(sel1 | sel2, 1.0, 0.0), axis=0, keepdims=True)
    carry[...] = jnp.broadcast_to(total, carry.shape)
    cnt_ref[...] = jnp.broadcast_to(total, cnt_ref.shape)

    meta_ref[...] = (jnp.where(lane == 0, i1, 0.0) + jnp.where(lane == 1, i2, 0.0)
                     + jnp.where(lane == 2, r1, 0.0) + jnp.where(lane == 3, r2, 0.0))
    w_ref[...] = jnp.where(lane == 0, w1, 0.0) + jnp.where(lane == 1, w2, 0.0)


def _route(logits, tm=512):
    t = logits.shape[0]
    return pl.pallas_call(
        _route_kernel,
        grid=(t // tm,),
        in_specs=[pl.BlockSpec((tm, LANES), lambda i: (i, 0))],
        out_specs=[pl.BlockSpec((tm, LANES), lambda i: (i, 0)),
                   pl.BlockSpec((tm, LANES), lambda i: (i, 0)),
                   pl.BlockSpec((SUBLANES, LANES), lambda i: (0, 0))],
        out_shape=[jax.ShapeDtypeStruct((t, LANES), F32),
                   jax.ShapeDtypeStruct((t, LANES), F32),
                   jax.ShapeDtypeStruct((SUBLANES, LANES), F32)],
        scratch_shapes=[pltpu.VMEM((SUBLANES, LANES), F32)],
        compiler_params=_cparams(("arbitrary",)),
        name="route_topk",
    )(logits)


def _lane_cumsum(x):
    lane = lax.broadcasted_iota(I32, x.shape, 1)
    sh = 1
    while sh < LANES:
        x = x + jnp.where(lane >= sh, pltpu.roll(x, sh, axis=1), 0)
        sh *= 2
    return x


def _dest_kernel(meta_ref, cnt_ref, dest_ref, be_ref, *, nb):
    tm = meta_ref.shape[0]
    shift = MOE_BLOCK.bit_length() - 1
    cnt = cnt_ref[...].astype(I32)
    padded = ((cnt + (MOE_BLOCK - 1)) >> shift) << shift
    pend_i = _lane_cumsum(padded)
    pend = pend_i.astype(F32)[0:1, :]
    poff = (pend_i - padded).astype(F32)[0:1, :]
    meta = meta_ref[...]
    lane = lax.broadcasted_iota(I32, (tm, LANES), 1).astype(F32)
    pick = lambda col: jnp.sum(jnp.where(lane == col, meta, 0.0), axis=-1, keepdims=True)
    i1, i2, r1, r2 = pick(0), pick(1), pick(2), pick(3)
    d1 = jnp.sum(jnp.where(lane == i1, poff, 0.0), axis=-1, keepdims=True) + r1
    d2 = jnp.sum(jnp.where(lane == i2, poff, 0.0), axis=-1, keepdims=True) + r2
    dest_ref[...] = (jnp.where(lane == 0, d1, 0.0) + jnp.where(lane == 1, d2, 0.0)).astype(I32)

    nrow = be_ref.shape[0]
    blk = lax.broadcasted_iota(I32, (nrow, LANES), 0).astype(F32)
    lane_b = lax.broadcasted_iota(I32, (nrow, LANES), 1).astype(F32)
    is_e = (lane_b >= ROUTE_LANE0) & (lane_b < ROUTE_LANE0 + N_EXPERTS)
    below = jnp.sum(jnp.where(is_e & (pend <= blk * MOE_BLOCK), 1.0, 0.0), axis=-1, keepdims=True)
    block_e = jnp.minimum(below, N_EXPERTS - 1.0)
    used = jnp.sum(jnp.where(lane_b == ROUTE_LANE0 + N_EXPERTS - 1, pend, 0.0),
                   axis=-1, keepdims=True) * (1.0 / MOE_BLOCK)
    be = jnp.where(blk[:, :1] == nb, used, block_e)
    be_ref[...] = jnp.broadcast_to(be, (nrow, LANES)).astype(I32)


def _dest(meta, cnt, nb, tm=512):
    t = meta.shape[0]
    nrow = -(-(nb + 1) // SUBLANES) * SUBLANES
    kern = functools.partial(_dest_kernel, nb=nb)
    return pl.pallas_call(
        kern,
        grid=(t // tm,),
        in_specs=[pl.BlockSpec((tm, LANES), lambda i: (i, 0)),
                  pl.BlockSpec((SUBLANES, LANES), lambda i: (0, 0))],
        out_specs=[pl.BlockSpec((tm, LANES), lambda i: (i, 0)),
                   pl.BlockSpec((nrow, LANES), lambda i: (0, 0))],
        out_shape=[jax.ShapeDtypeStruct((t, LANES), I32),
                   jax.ShapeDtypeStruct((nrow, LANES), I32)],
        compiler_params=_cparams(("arbitrary",)),
        name="route_dest",
    )(meta, cnt)


def _invert_kernel(dest_ref, fill_hbm, row_ref, sem):
    cp = pltpu.make_async_copy(fill_hbm, row_ref, sem)
    cp.start()
    cp.wait()

    def put(a, c):
        row_ref[dest_ref[a]] = a
        return c

    lax.fori_loop(0, dest_ref.shape[0], put, 0, unroll=8)


def _invert(dest_flat, n_slot, t):
    spare = 2 * MOE_BLOCK
    fill = 2 * t + (jnp.arange(n_slot, dtype=I32) & (spare - 1))
    return pl.pallas_call(
        _invert_kernel,
        in_specs=[pl.BlockSpec(memory_space=pltpu.SMEM), pl.BlockSpec(memory_space=pl.ANY)],
        out_specs=pl.BlockSpec(memory_space=pltpu.SMEM),
        out_shape=jax.ShapeDtypeStruct((n_slot,), I32),
        scratch_shapes=[pltpu.SemaphoreType.DMA(())],
        name="route_invert",
    )(dest_flat, fill)


def _expert_weights(b, used, be_ref, w_hbms, wst, wsem, run_ref, load):
    def fetch(e, p):
        for i, w in enumerate(w_hbms):
            pltpu.make_async_copy(w.at[e], wst.at[p, i], wsem.at[p]).start(priority=1)

    @pl.when((b == 0) & (used > 0))
    def _():
        run_ref[0] = 0
        fetch(be_ref[0], 0)

    bc = jnp.minimum(b, be_ref.shape[0] - 2)
    new_expert = (b == 0) | (be_ref[bc] != be_ref[jnp.maximum(bc - 1, 0)])

    @pl.when(new_expert & (b < used))
    def _():
        p = run_ref[0]
        for i, w in enumerate(w_hbms):
            pltpu.make_async_copy(w.at[0], wst.at[p, i], wsem.at[p]).wait()
        load(p)
        e = be_ref[b]
        nxt = lax.while_loop(lambda i: (i < used) & (be_ref[i] == e), lambda i: i + 1, b + 1)

        @pl.when(nxt < used)
        def _():
            fetch(be_ref[nxt], 1 - p)

        run_ref[0] = 1 - p


def _moe_up_kernel(row_ref, be_ref, h2_hbm, wg_hbm, wu_hbm, o_ref, wst, wbuf, xbuf, xb_scr, sem,
                   wsem, run_ref, *, nb):
    b = pl.program_id(0)
    used = be_ref[nb]
    blk = MOE_BLOCK
    fe = wg_hbm.shape[2]
    t = h2_hbm.shape[0]

    def gather_rows(block, rows):
        slot = block % 2
        base = block * blk
        for r in rows:
            row = row_ref[base + r]
            tok = jnp.where(row >= t, row - t, row)
            tok = jnp.where(tok >= t, 0, tok)
            pltpu.make_async_copy(h2_hbm.at[pl.ds(tok, 1)], xbuf.at[slot, pl.ds(r, 1)],
                                  sem.at[slot]).start()

    @pl.when((b == 0) & (used > 0))
    def _():
        gather_rows(0, range(blk))

    def load(p):
        wbuf[:, :fe] = wst[p, 0].astype(BF16)
        wbuf[:, fe:] = wst[p, 1].astype(BF16)

    _expert_weights(b, used, be_ref, (wg_hbm, wu_hbm), wst, wsem, run_ref, load)

    nchunk = fe // MOE_NCHUNK_COLS
    rows_per = blk // nchunk

    def step(prefetch):
        slot = b % 2
        pltpu.make_async_copy(xbuf.at[slot], xbuf.at[slot], sem.at[slot]).wait()
        xb_scr[...] = xbuf[slot].astype(BF16)
        for c in range(nchunk):
            if prefetch:
                gather_rows(b + 1, range(c * rows_per, (c + 1) * rows_per))
            cols = slice(c * MOE_NCHUNK_COLS, (c + 1) * MOE_NCHUNK_COLS)
            ucols = slice(fe + c * MOE_NCHUNK_COLS, fe + (c + 1) * MOE_NCHUNK_COLS)
            g = jnp.dot(xb_scr[...], wbuf[:, cols], preferred_element_type=F32)
            u = jnp.dot(xb_scr[...], wbuf[:, ucols], preferred_element_type=F32)
            o_ref[:, cols] = (_silu(g) * u).astype(BF16)

    @pl.when(b + 1 < used)
    def _():
        step(True)

    @pl.when((b < used) & (b + 1 >= used))
    def _():
        step(False)

    @pl.when(b >= used)
    def _():
        o_ref[...] = jnp.zeros(o_ref.shape, BF16)


def _moe_up(slot_row, block_e, h2, w_gate, w_up, nb):
    d, fe = w_gate.shape[1], w_gate.shape[2]
    kern = functools.partial(_moe_up_kernel, nb=nb)
    grid_spec = pltpu.PrefetchScalarGridSpec(
        num_scalar_prefetch=2,
        grid=(nb,),
        in_specs=[pl.BlockSpec(memory_space=pl.ANY),
                  pl.BlockSpec(memory_space=pl.ANY),
                  pl.BlockSpec(memory_space=pl.ANY)],
        out_specs=pl.BlockSpec((MOE_BLOCK, fe), lambda b, s, e: (b, 0)),
        scratch_shapes=[pltpu.VMEM((2, 2, d, fe), F32),
                        pltpu.VMEM((d, 2 * fe), BF16),
                        pltpu.VMEM((2, MOE_BLOCK, d), F32),
                        pltpu.VMEM((MOE_BLOCK, d), BF16),
                        pltpu.SemaphoreType.DMA((2,)),
                        pltpu.SemaphoreType.DMA((2,)),
                        pltpu.SMEM((1,), I32)],
    )
    return pl.pallas_call(
        kern,
        grid_spec=grid_spec,
        out_shape=jax.ShapeDtypeStruct((nb * MOE_BLOCK, fe), BF16),
        compiler_params=_cparams(("arbitrary",)),
        name="moe_up",
    )(slot_row, block_e, h2, w_gate, w_up)


def _moe_down_kernel(row_ref, be_ref, hm_ref, wd_hbm, ys_hbm, wst, wbuf, obuf, sem, wsem, run_ref,
                     *, nb, t):
    b = pl.program_id(0)
    used = be_ref[nb]
    blk = MOE_BLOCK

    def wait_slot(slot):
        pltpu.make_async_copy(obuf.at[slot], obuf.at[slot], sem.at[slot]).wait()

    @pl.when(b == 0)
    def _():
        obuf[...] = jnp.zeros(obuf.shape, F32)
        for half in range(2):
            spare = ys_hbm.at[pl.ds(2 * t + half * blk, blk)]
            pltpu.make_async_copy(obuf.at[half], spare, sem.at[2]).start()
        for half in range(2):
            spare = ys_hbm.at[pl.ds(2 * t + half * blk, blk)]
            pltpu.make_async_copy(obuf.at[half], spare, sem.at[2]).wait()

    @pl.when((b >= 2) & (b - 2 < used))
    def _():
        wait_slot(b % 2)

    def load(p):
        wbuf[...] = wst[p, 0].astype(BF16)

    _expert_weights(b, used, be_ref, (wd_hbm,), wst, wsem, run_ref, load)

    d = wbuf.shape[1]
    nchunk = d // MOE_NCHUNK_COLS
    rows_per = blk // nchunk

    def step(scatter, matmul):
        slot = (b - 1) % 2
        base = (b - 1) * blk
        for c in range(nchunk):
            if scatter:
                for r in range(c * rows_per, (c + 1) * rows_per):
                    pltpu.make_async_copy(obuf.at[slot, pl.ds(r, 1)],
                                          ys_hbm.at[pl.ds(row_ref[base + r], 1)],
                                          sem.at[slot]).start(priority=r % 2)
            if matmul:
                cols = slice(c * MOE_NCHUNK_COLS, (c + 1) * MOE_NCHUNK_COLS)
                obuf[b % 2, :, cols] = jnp.dot(hm_ref[...], wbuf[:, cols],
                                               preferred_element_type=F32)

    @pl.when((b >= 1) & (b < used))
    def _():
        step(True, True)

    @pl.when((b == 0) & (b < used))
    def _():
        step(False, True)

    @pl.when((b >= 1) & (b == used))
    def _():
        step(True, False)

    @pl.when((b == nb) & (used == nb))
    def _():
        wait_slot((nb - 1) % 2)


def _moe_down(slot_row, block_e, hmid, w_down, nb, t):
    fe, d = w_down.shape[1], w_down.shape[2]
    kern = functools.partial(_moe_down_kernel, nb=nb, t=t)
    grid_spec = pltpu.PrefetchScalarGridSpec(
        num_scalar_prefetch=2,
        grid=(nb + 1,),
        in_specs=[pl.BlockSpec((MOE_BLOCK, fe), lambda b, s, e: (jnp.minimum(b, nb - 1), 0)),
                  pl.BlockSpec(memory_space=pl.ANY)],
        out_specs=pl.BlockSpec(memory_space=pl.ANY),
        scratch_shapes=[pltpu.VMEM((2, 1, fe, d), F32),
                        pltpu.VMEM((fe, d), BF16),
                        pltpu.VMEM((2, MOE_BLOCK, d), F32),
                        pltpu.SemaphoreType.DMA((3,)),
                        pltpu.SemaphoreType.DMA((2,)),
                        pltpu.SMEM((1,), I32)],
    )
    return pl.pallas_call(
        kern,
        grid_spec=grid_spec,
        out_shape=jax.ShapeDtypeStruct((2 * t + 2 * MOE_BLOCK, d), F32),
        compiler_params=_cparams(("arbitrary",)),
        name="moe_down",
    )(slot_row, block_e, hmid, w_down)


def _final_kernel(x1_ref, y0_ref, y1_ref, w_ref, mod_ref, gf_ref, o_ref):
    w = w_ref[...]
    y = w[:, 0:1] * y0_ref[...] + w[:, 1:2] * y1_ref[...]
    x2 = x1_ref[...] + mod_ref[0][5:6, :] * y
    o_ref[...] = x2 * lax.rsqrt(jnp.mean(x2 * x2, axis=-1, keepdims=True) + NORM_EPS) * gf_ref[...]


def _final(x1, ys, w, mod3, g_final, seq, tm=512):
    t, d = x1.shape
    tpb = seq // tm
    return pl.pallas_call(
        _final_kernel,
        grid=(t // tm,),
        in_specs=[pl.BlockSpec((tm, d), lambda i: (i, 0)),
                  pl.BlockSpec((tm, d), lambda i: (i, 0)),
                  pl.BlockSpec((tm, d), lambda i: (i + t // tm, 0)),
                  pl.BlockSpec((tm, LANES), lambda i: (i, 0)),
                  pl.BlockSpec((1, N_MOD, d), lambda i: (i // tpb, 0, 0)),
                  pl.BlockSpec((1, d), lambda i: (0, 0))],
        out_specs=pl.BlockSpec((tm, d), lambda i: (i, 0)),
        out_shape=jax.ShapeDtypeStruct((t, d), F32),
        compiler_params=_cparams(("arbitrary",)),
        name="combine_final_norm",
    )(x1, ys, ys, w, mod3, g_final.reshape(1, d))


def _rope_tables(seq, dh):
    half = dh // 2
    inv_freq = 1.0 / (ROPE_BASE ** (jnp.arange(half, dtype=F32) * 2.0 / dh))
    ang = jnp.arange(seq, dtype=F32)[:, None] * inv_freq[None, :]
    cos, sin = jnp.cos(ang), jnp.sin(ang)
    return jnp.concatenate([cos, cos], axis=1), jnp.concatenate([-sin, sin], axis=1)


def _layer(x, mod, g_norm1, w_in, ssm_a_re, ssm_a_im, ssm_b_re, ssm_b_im, ssm_c_re, ssm_c_im,
           ssm_d, ssm_log_dt, w_glu, beta_ssm, beta_ret, w_out, g_norm2, w_rg, b_rg, w_re, b_re,
           w_gate, w_up, w_down):
    bn, seq, d = x.shape
    t = bn * seq
    x2d = x.reshape(t, d)
    mod3 = mod.reshape(bn, N_MOD, d)

    sw = w_glu.shape[0]
    proj = _inproj(x2d, mod3, g_norm1, w_in.astype(BF16), seq)
    proj3 = proj.reshape(bn, seq, proj.shape[1])

    ab_re, ab_im, bbt_re, bbt_im = _s5_disc(ssm_a_re, ssm_a_im, ssm_log_dt,
                                            jnp.swapaxes(ssm_b_re, 1, 2), jnp.swapaxes(ssm_b_im, 1, 2))
    bz, cz, a_re, a_im = _s5_pack(ab_re, ab_im, bbt_re, bbt_im, ssm_c_re, ssm_c_im)
    ypre = _s5_scan(proj3, bz, cz, a_re, a_im, ssm_d.reshape(-1)).reshape(t, sw)

    heads = (d - sw) // RET_HEAD_DIM
    cosf, sinf = _rope_tables(seq, RET_HEAD_DIM)
    gamma = 1.0 - jnp.exp2(-5.0 - jnp.arange(heads, dtype=F32))
    log_g = jnp.broadcast_to(jnp.log(gamma)[:, None, None], (heads, 1, LANES))
    yret = _retention(proj3, sw // RET_HEAD_DIM, cosf, sinf, log_g, beta_ret)

    n_route = N_GROUPS + N_EXPERTS
    wr = jnp.concatenate([w_rg, w_re, jnp.zeros((d, LANES - n_route), F32)], axis=1)
    br = jnp.concatenate([b_rg, b_re, jnp.zeros((LANES - n_route,), F32)]).reshape(1, LANES)
    wr_hi = wr.astype(BF16)
    wr_lo = (wr - wr_hi.astype(F32)).astype(BF16)
    x1, h2, logits = _mix(ypre, yret.reshape(t, d - sw), x2d, mod3, w_glu.astype(BF16),
                          beta_ssm, w_out.astype(BF16), g_norm2, wr_hi, wr_lo, br, seq)

    nb = -(-(t * 2) // MOE_BLOCK) + N_EXPERTS
    meta, w_tok, cnt = _route(logits)
    dest, be = _dest(meta, cnt, nb)
    slot_row = _invert(dest[:, :2].T.reshape(-1), nb * MOE_BLOCK, t)
    block_e = be[:nb + 1, 0]

    hmid = _moe_up(slot_row, block_e, h2, w_gate, w_up, nb)
    ys = _moe_down(slot_row, block_e, hmid, w_down, nb, t)
    return x1, ys, w_tok, mod3


def kernel(x, c, w_ada, b_ada, g_norm1, w_in, ssm_a_re, ssm_a_im, ssm_b_re, ssm_b_im, ssm_c_re,
           ssm_c_im, ssm_d, ssm_log_dt, w_glu, beta_ssm, beta_ret, w_out, g_norm2, w_router_group,
           b_router_group, w_router_expert, b_router_expert, w_gate, w_up, w_down, g_final):
    depth = w_ada.shape[0]
    bn, seq, d = x.shape
    assert depth == 1, "the final norm is fused into the single layer's last kernel"
    l = 0
    mod = _ada(c, w_ada[l], b_ada[l])
    x1, ys, w_tok, mod3 = _layer(
        x, mod, g_norm1[l], w_in[l], ssm_a_re[l], ssm_a_im[l], ssm_b_re[l], ssm_b_im[l],
        ssm_c_re[l], ssm_c_im[l], ssm_d[l], ssm_log_dt[l], w_glu[l], beta_ssm[l], beta_ret[l],
        w_out[l], g_norm2[l], w_router_group[l], b_router_group[l], w_router_expert[l],
        b_router_expert[l], w_gate[l], w_up[l], w_down[l])
    out = _final(x1, ys, w_tok, mod3, g_final, seq)
    return out.reshape(bn, seq, d)
```

```python
import functools
import math

import jax
import jax.numpy as jnp
from jax import lax
from jax.experimental import pallas as pl
from jax.experimental.pallas import tpu as pltpu

F32 = jnp.float32
BF16 = jnp.bfloat16
I32 = jnp.int32

SSM_GROUP = 16
SSM_STATE = 64
RET_HEAD_DIM = 128
ROPE_BASE = 10000.0
N_GROUPS = 4
EXPERTS_PER_GROUP = 8
N_EXPERTS = N_GROUPS * EXPERTS_PER_GROUP
N_MOD = 6
NORM_EPS = 1e-6
GN_EPS = 1e-5

LANES = 128
SUBLANES = 8
VMEM_LIMIT = 56 * 1024 * 1024

ROUTE_LANE0 = N_GROUPS
MOE_BLOCK = 256
MOE_NCHUNK_COLS = 256
MOE_GATHER_DEPTH = 4
MOE_SCATTER_DEPTH = 3
MOE_SPARE_SETS = 4
RET_CHUNK = 256
S5_TT = 128


def _cparams(sem):
    return pltpu.CompilerParams(dimension_semantics=sem, vmem_limit_bytes=VMEM_LIMIT)


def _silu(x):
    return x * jax.nn.sigmoid(x)


def _ada_kernel(c_ref, w_ref, b_ref, o_ref):
    s = _silu(c_ref[...])
    o_ref[...] = jnp.dot(s.astype(BF16), w_ref[...].astype(BF16),
                         preferred_element_type=F32) + b_ref[...]


def _ada(c, w, b, tn=1024):
    bn, d = c.shape
    n = w.shape[1]
    return pl.pallas_call(
        _ada_kernel,
        grid=(n // tn,),
        in_specs=[pl.BlockSpec((bn, d), lambda j: (0, 0)),
                  pl.BlockSpec((d, tn), lambda j: (0, j)),
                  pl.BlockSpec((1, tn), lambda j: (0, j))],
        out_specs=pl.BlockSpec((bn, tn), lambda j: (0, j)),
        out_shape=jax.ShapeDtypeStruct((bn, n), F32),
        compiler_params=_cparams(("arbitrary",)),
        name="ada_mod",
    )(c, w, b.reshape(1, n))


def _inproj_kernel(x_ref, mod_ref, g_ref, w_ref, o_ref, h_scr):
    @pl.when(pl.program_id(1) == 0)
    def _():
        x = x_ref[...]
        y = x * lax.rsqrt(jnp.mean(x * x, axis=-1, keepdims=True) + NORM_EPS) * g_ref[...]
        m = mod_ref[0]
        h_scr[...] = (y * (1.0 + m[1:2, :]) + m[0:1, :]).astype(BF16)

    o_ref[...] = jnp.dot(h_scr[...], w_ref[...], preferred_element_type=F32).astype(BF16)


def _inproj(x2d, mod3, g1, w_in_bf, seq, tm=1024, tn=1024):
    t, d = x2d.shape
    n = w_in_bf.shape[1]
    tm = min(tm, seq)
    tn = math.gcd(tn, n)
    tpb = seq // tm
    return pl.pallas_call(
        _inproj_kernel,
        grid=(t // tm, n // tn),
        in_specs=[pl.BlockSpec((tm, d), lambda i, j: (i, 0)),
                  pl.BlockSpec((1, N_MOD, d), lambda i, j: (i // tpb, 0, 0)),
                  pl.BlockSpec((1, d), lambda i, j: (0, 0)),
                  pl.BlockSpec((d, tn), lambda i, j: (0, j))],
        out_specs=pl.BlockSpec((tm, tn), lambda i, j: (i, j)),
        out_shape=jax.ShapeDtypeStruct((t, n), BF16),
        scratch_shapes=[pltpu.VMEM((tm, d), BF16)],
        compiler_params=_cparams(("arbitrary", "arbitrary")),
        name="norm_inproj",
    )(x2d, mod3, g1.reshape(1, d), w_in_bf)


def _s5_disc_kernel(are_ref, aim_ref, ldt_ref, bre_ref, bim_ref,
                    abre_ref, abim_ref, bbre_ref, bbim_ref):
    a_re = are_ref[...]
    a_im = aim_ref[...]
    dt = jnp.exp(ldt_ref[...])
    mag = jnp.exp(dt * a_re)
    ab_re = mag * jnp.cos(dt * a_im)
    ab_im = mag * jnp.sin(dt * a_im)
    inv_abs2 = 1.0 / (a_re * a_re + a_im * a_im)
    n_re = ab_re - 1.0
    f_re = (n_re * a_re + ab_im * a_im) * inv_abs2
    f_im = (ab_im * a_re - n_re * a_im) * inv_abs2
    abre_ref[...] = ab_re
    abim_ref[...] = ab_im
    b_re = bre_ref[...]
    b_im = bim_ref[...]
    fr = f_re[:, None, :]
    fi = f_im[:, None, :]
    bbre_ref[...] = fr * b_re - fi * b_im
    bbim_ref[...] = fr * b_im + fi * b_re


def _s5_disc(a_re, a_im, log_dt, bt_re, bt_im):
    g, p = a_re.shape
    h = bt_re.shape[1]
    return pl.pallas_call(
        _s5_disc_kernel,
        out_shape=[jax.ShapeDtypeStruct((g, p), F32), jax.ShapeDtypeStruct((g, p), F32),
                   jax.ShapeDtypeStruct((g, h, p), F32), jax.ShapeDtypeStruct((g, h, p), F32)],
        name="s5_discretise",
    )(a_re, a_im, log_dt.reshape(g, 1), bt_re, bt_im)


def _s5_pack(ab_re, ab_im, bbt_re, bbt_im, c_re, c_im):
    g, h, p = bbt_re.shape
    npair = g // 2
    ppb = LANES // (2 * h)
    eye2 = jnp.eye(2, dtype=F32)
    qsel = jax.nn.one_hot(jnp.arange(npair) % ppb, ppb, dtype=F32)

    def in_mat(bbt):
        b4 = bbt.reshape(npair, 2, h, p)
        return jnp.einsum('ab,xahp->xahbp', eye2, b4).reshape(npair, 2 * h, 2 * p)

    bsmall = jnp.concatenate([in_mat(bbt_re), in_mat(bbt_im)], axis=-1)
    bz = jnp.einsum('xq,xrc->xqrc', qsel, bsmall).reshape(npair, LANES, 4 * p)

    def out_mat(c):
        c4 = jnp.swapaxes(c, 1, 2).reshape(npair, 2, p, h)
        return jnp.einsum('ab,xaph->xapbh', eye2, c4).reshape(npair, 2 * p, 2 * h)

    csmall = jnp.concatenate([out_mat(c_re), -out_mat(c_im)], axis=1)
    cz = jnp.einsum('xq,xrc->xrqc', qsel, csmall).reshape(npair, 4 * p, LANES)
    a_re = ab_re.reshape(npair // ppb, ppb, 2 * p)
    a_im = ab_im.reshape(npair // ppb, ppb, 2 * p)
    return bz.astype(BF16), cz.astype(BF16), a_re, a_im


def _s5_kernel(u_ref, bz_ref, cz_ref, are_ref, aim_ref, d_ref, o_ref, state, buf, *, ppb):
    t = pl.program_id(0)
    kb = pl.program_id(1)
    bn, tt, _ = u_ref.shape
    rt = bn * tt

    @pl.when(t == 0)
    def _():
        state[kb] = jnp.zeros(state.shape[1:], F32)

    u = u_ref[...].reshape(rt, LANES)
    for q in range(ppb):
        bu = jnp.dot(u, bz_ref[q], preferred_element_type=F32)
        for b in range(bn):
            rows = pl.ds(b, tt, stride=bn)
            buf[2 * q, rows, :] = bu[b * tt:(b + 1) * tt, :LANES]
            buf[2 * q + 1, rows, :] = bu[b * tt:(b + 1) * tt, LANES:]

    a_re = [jnp.broadcast_to(are_ref[0, q:q + 1, :], (bn, LANES)) for q in range(ppb)]
    a_im = [jnp.broadcast_to(aim_ref[0, q:q + 1, :], (bn, LANES)) for q in range(ppb)]
    s0 = tuple(state[kb, j] for j in range(2 * ppb))

    def step(i, s):
        rows = pl.ds(pl.multiple_of(i * bn, bn), bn)
        new = []
        for q in range(ppb):
            s_re, s_im = s[2 * q], s[2 * q + 1]
            n_re = a_re[q] * s_re - a_im[q] * s_im + buf[2 * q, rows, :]
            n_im = a_re[q] * s_im + a_im[q] * s_re + buf[2 * q + 1, rows, :]
            buf[2 * q, rows, :] = n_re
            buf[2 * q + 1, rows, :] = n_im
            new += [n_re, n_im]
        return tuple(new)

    s1 = lax.fori_loop(0, tt, step, s0, unroll=4)
    for j in range(2 * ppb):
        state[kb, j] = s1[j]

    acc = None
    for q in range(ppb):
        lhs = jnp.concatenate([buf[2 * q], buf[2 * q + 1]], axis=1).astype(BF16)
        part = jnp.dot(lhs, cz_ref[q], preferred_element_type=F32)
        acc = part if acc is None else acc + part
    buf[2 * ppb] = acc
    d_skip = d_ref[...]
    for b in range(bn):
        o_ref[b] = buf[2 * ppb, pl.ds(b, tt, stride=bn), :] + d_skip * u_ref[b].astype(F32)


def _s5_scan(u3, bz, cz, a_re, a_im, d_skip, tt=S5_TT):
    bn, seq, width = u3.shape[0], u3.shape[1], d_skip.shape[0]
    nkb = width // LANES
    ppb = bz.shape[0] // nkb
    kern = functools.partial(_s5_kernel, ppb=ppb)
    return pl.pallas_call(
        kern,
        grid=(seq // tt, nkb),
        in_specs=[pl.BlockSpec((bn, tt, LANES), lambda t, k: (0, t, k)),
                  pl.BlockSpec((ppb,) + bz.shape[1:], lambda t, k: (k, 0, 0)),
                  pl.BlockSpec((ppb,) + cz.shape[1:], lambda t, k: (k, 0, 0)),
                  pl.BlockSpec((1,) + a_re.shape[1:], lambda t, k: (k, 0, 0)),
                  pl.BlockSpec((1,) + a_im.shape[1:], lambda t, k: (k, 0, 0)),
                  pl.BlockSpec((1, LANES), lambda t, k: (0, k))],
        out_specs=pl.BlockSpec((bn, tt, LANES), lambda t, k: (0, t, k)),
        out_shape=jax.ShapeDtypeStruct((bn, seq, width), F32),
        scratch_shapes=[pltpu.VMEM((nkb, 2 * ppb, bn, LANES), F32),
                        pltpu.VMEM((2 * ppb + 1, bn * tt, LANES), F32)],
        compiler_params=_cparams(("arbitrary", "arbitrary")),
        name="s5_scan",
    )(u3, bz, cz, a_re, a_im, d_skip.reshape(1, width))


def _ret_kernel(q_ref, k_ref, v_ref, g_ref, cq_ref, sq_ref, ck_ref, sk_ref, lg_ref, beta_ref, o_ref,
                q_scr, k_scr, kzt_scr, y_scr, *, chunk):
    seq, dh = q_ref.shape[1], q_ref.shape[2]
    lg = lg_ref[0][:, :1]
    ri = lax.broadcasted_iota(I32, (chunk, chunk), 0)
    ci = lax.broadcasted_iota(I32, (chunk, chunk), 1)
    rel = (ri - ci).astype(F32)
    decay = jnp.where(rel >= 0, jnp.exp(lg * jnp.maximum(rel, 0.0)), 0.0)
    idx = lax.broadcasted_iota(I32, (chunk, 1), 0).astype(F32)
    xi = jnp.exp(lg * (idx + 1.0))
    g_chunk = jnp.exp(lg * float(chunk))
    beta = beta_ref[...]
    half = dh // 2
    pr = lax.broadcasted_iota(I32, (dh, dh), 0)
    pc = lax.broadcasted_iota(I32, (dh, dh), 1)
    swap = jnp.where(((pr + half) & (dh - 1)) == pc, 1.0, 0.0).astype(BF16)

    def rope(x_ref, cos_ref, sin_ref, sl):
        xb = x_ref[0, sl, :]
        rolled = jnp.dot(xb, swap, preferred_element_type=F32).astype(BF16)
        return xb * cos_ref[sl, :] + rolled * sin_ref[sl, :]

    zeta = jnp.exp(lg * (chunk - 1.0 - idx))
    for n in range(seq // chunk):
        sl = slice(n * chunk, (n + 1) * chunk)
        q_scr[sl, :] = rope(q_ref, cq_ref, sq_ref, sl)
        kb = rope(k_ref, ck_ref, sk_ref, sl)
        k_scr[sl, :] = kb
        kzt_scr[:, sl] = (kb.astype(F32) * zeta).T.astype(BF16)

    decay = decay.astype(BF16)
    r = jnp.zeros((dh, dh), F32)
    for n in range(seq // chunk):
        sl = slice(n * chunk, (n + 1) * chunk)
        qb = q_scr[sl, :]
        v = v_ref[0, sl, :]
        s = lax.dot_general(qb, k_scr[sl, :], (((1,), (1,)), ((), ())),
                            preferred_element_type=F32).astype(BF16) * decay
        y = jnp.dot(s, v, preferred_element_type=F32)
        y_scr[sl, :] = y + jnp.dot(qb, r.astype(BF16), preferred_element_type=F32) * xi
        r = r * g_chunk + jnp.dot(kzt_scr[:, sl], v, preferred_element_type=F32)

    y = y_scr[...]
    mu = jnp.mean(y, axis=-1, keepdims=True)
    yc = y - mu
    var = jnp.mean(yc * yc, axis=-1, keepdims=True)
    yn = yc * lax.rsqrt(var + GN_EPS) * beta
    o_ref[0] = (_silu(g_ref[0].astype(F32)) * yn).astype(BF16)


def _retention(proj3, col0, cosf, sinf, log_g, beta_ret, chunk=RET_CHUNK):
    bn, seq, _ = proj3.shape
    dh = RET_HEAD_DIM
    heads = beta_ret.shape[0] // dh
    blk = lambda off: pl.BlockSpec((1, seq, dh), lambda b, h, off=off: (b, 0, col0 + off + h))
    k_scale = dh ** -0.5
    kern = functools.partial(_ret_kernel, chunk=chunk)
    return pl.pallas_call(
        kern,
        grid=(bn, heads),
        in_specs=[blk(0), blk(heads), blk(2 * heads), blk(3 * heads),
                  pl.BlockSpec((seq, dh), lambda b, h: (0, 0)),
                  pl.BlockSpec((seq, dh), lambda b, h: (0, 0)),
                  pl.BlockSpec((seq, dh), lambda b, h: (0, 0)),
                  pl.BlockSpec((seq, dh), lambda b, h: (0, 0)),
                  pl.BlockSpec((1, 1, LANES), lambda b, h: (h, 0, 0)),
                  pl.BlockSpec((1, dh), lambda b, h: (0, h))],
        out_specs=pl.BlockSpec((1, seq, dh), lambda b, h: (b, 0, h)),
        out_shape=jax.ShapeDtypeStruct((bn, seq, heads * dh), BF16),
        scratch_shapes=[pltpu.VMEM((seq, dh), BF16), pltpu.VMEM((seq, dh), BF16),
                        pltpu.VMEM((dh, seq), BF16), pltpu.VMEM((seq, dh), F32)],
        compiler_params=_cparams(("arbitrary", "arbitrary")),
        name="retention",
    )(proj3, proj3, proj3, proj3, cosf.astype(BF16), sinf.astype(BF16),
      (cosf * k_scale).astype(BF16), (sinf * k_scale).astype(BF16),
      log_g, beta_ret.reshape(1, heads * dh))


def _gelu_tanh(x):
    return 0.5 * x * (1.0 + jnp.tanh(math.sqrt(2.0 / math.pi) * (x + 0.044715 * (x * x * x))))


def _split_bf16(x):
    hi = x.astype(BF16)
    lo = (x - hi.astype(F32)).astype(BF16)
    return hi, lo


def _mix_kernel(ys_ref, yr_ref, x_ref, mod_ref, wglu_ref, bssm_ref, wout_ref, g2_ref,
                wrh_ref, wrl_ref, br_ref, x1_ref, h2_ref, lg_ref):
    sw = ys_ref.shape[1]
    z = _gelu_tanh(ys_ref[...])
    gate = jax.nn.sigmoid(jnp.dot(z.astype(BF16), wglu_ref[...], preferred_element_type=F32))
    o = z * gate
    y_ssm = o * lax.rsqrt(jnp.mean(o * o, axis=-1, keepdims=True) + NORM_EPS) * bssm_ref[...]
    mixed = (jnp.dot(y_ssm.astype(BF16), wout_ref[:sw, :], preferred_element_type=F32)
             + jnp.dot(yr_ref[...], wout_ref[sw:, :], preferred_element_type=F32))
    m = mod_ref[0]
    x1 = x_ref[...] + m[2:3, :] * mixed
    x1_ref[...] = x1
    y = x1 * lax.rsqrt(jnp.mean(x1 * x1, axis=-1, keepdims=True) + NORM_EPS) * g2_ref[...]
    h2 = y * (1.0 + m[4:5, :]) + m[3:4, :]
    h2_ref[...] = h2
    hi, lo = _split_bf16(h2)
    wrh = wrh_ref[...]
    lg_ref[...] = (jnp.dot(hi, wrh, preferred_element_type=F32)
                   + jnp.dot(lo, wrh, preferred_element_type=F32)
                   + jnp.dot(hi, wrl_ref[...], preferred_element_type=F32)
                   + br_ref[...])


def _mix(ypre, yret, x2d, mod3, wglu_bf, beta_ssm, wout_bf, g2, wr_hi, wr_lo, br, seq, tm=256):
    t, d = x2d.shape
    sw = wglu_bf.shape[0]
    tpb = seq // tm
    const = lambda shape: pl.BlockSpec(shape, lambda i: (0,) * len(shape),
                                       pipeline_mode=pl.Buffered(1))
    return pl.pallas_call(
        _mix_kernel,
        grid=(t // tm,),
        in_specs=[pl.BlockSpec((tm, sw), lambda i: (i, 0)),
                  pl.BlockSpec((tm, d - sw), lambda i: (i, 0)),
                  pl.BlockSpec((tm, d), lambda i: (i, 0)),
                  pl.BlockSpec((1, N_MOD, d), lambda i: (i // tpb, 0, 0)),
                  const((sw, sw)), const((1, sw)), const((d, d)), const((1, d)),
                  const((d, LANES)), const((d, LANES)), const((1, LANES))],
        out_specs=[pl.BlockSpec((tm, d), lambda i: (i, 0)),
                   pl.BlockSpec((tm, d), lambda i: (i, 0)),
                   pl.BlockSpec((tm, LANES), lambda i: (i, 0))],
        out_shape=[jax.ShapeDtypeStruct((t, d), F32),
                   jax.ShapeDtypeStruct((t, d), F32),
                   jax.ShapeDtypeStruct((t, LANES), F32)],
        compiler_params=_cparams(("arbitrary",)),
        name="mix_outproj_router",
    )(ypre, yret, x2d, mod3, wglu_bf, beta_ssm.reshape(1, sw), wout_bf, g2.reshape(1, d),
      wr_hi, wr_lo, br)


def _route_kernel(lg_ref, meta_ref, w_ref, cnt_ref, carry):
    i = pl.program_id(0)
    tm = lg_ref.shape[0]

    @pl.when(i == 0)
    def _():
        carry[...] = jnp.zeros(carry.shape, F32)

    lg = lg_ref[...]
    lane = lax.broadcasted_iota(I32, (tm, LANES), 1).astype(F32)
    neg = jnp.float32(-jnp.inf)

    def first_max(vals):
        m = jnp.max(vals, axis=-1, keepdims=True)
        idx = jnp.min(jnp.where(vals == m, lane, float(LANES)), axis=-1, keepdims=True)
        return m, idx

    gl = jnp.where(lane < N_GROUPS, lg, neg)
    gmax, gsel = first_max(gl)
    g_w = 1.0 / jnp.sum(jnp.exp(gl - gmax), axis=-1, keepdims=True)
    lo = ROUTE_LANE0 + gsel * EXPERTS_PER_GROUP
    el = jnp.where((lane >= lo) & (lane < lo + EXPERTS_PER_GROUP), lg, neg)
    m1, i1 = first_max(el)
    m2, i2 = first_max(jnp.where(lane == i1, neg, el))
    e21 = jnp.exp(m2 - m1)
    w1 = g_w / (1.0 + e21)
    w2 = g_w * e21 / (1.0 + e21)

    sel1 = lane == i1
    sel2 = lane == i2
    onehot = jnp.where(sel1 | sel2, 1.0, 0.0).astype(BF16)
    ri = lax.broadcasted_iota(I32, (tm, tm), 0)
    ci = lax.broadcasted_iota(I32, (tm, tm), 1)
    tri = jnp.where(ci < ri, 1.0, 0.0).astype(BF16)
    base = carry[0:1, :]
    excl = jnp.dot(tri, onehot, preferred_element_type=F32) + base
    r1 = jnp.sum(jnp.where(sel1, excl, 0.0), axis=-1, keepdims=True)
    r2 = jnp.sum(jnp.where(sel2, excl, 0.0), axis=-1, keepdims=True)
    total = base + jnp.sum(jnp.where(sel1 | sel2, 1.0, 0.0), axis=0, keepdims=True)
    carry[...] = jnp.broadcast_to(total, carry.shape)
    cnt_ref[...] = jnp.broadcast_to(total, cnt_ref.shape)

    meta_ref[...] = (jnp.where(lane == 0, i1, 0.0) + jnp.where(lane == 1, i2, 0.0)
                     + jnp.where(lane == 2, r1, 0.0) + jnp.where(lane == 3, r2, 0.0))
    w_ref[...] = jnp.where(lane == 0, w1, 0.0) + jnp.where(lane == 1, w2, 0.0)


def _route(logits, tm=512):
    t = logits.shape[0]
    return pl.pallas_call(
        _route_kernel,
        grid=(t // tm,),
        in_specs=[pl.BlockSpec((tm, LANES), lambda i: (i, 0))],
        out_specs=[pl.BlockSpec((tm, LANES), lambda i: (i, 0)),
                   pl.BlockSpec((tm, LANES), lambda i: (i, 0)),
                   pl.BlockSpec((SUBLANES, LANES), lambda i: (0, 0))],
        out_shape=[jax.ShapeDtypeStruct((t, LANES), F32),
                   jax.ShapeDtypeStruct((t, LANES), F32),
                   jax.ShapeDtypeStruct((SUBLANES, LANES), F32)],
        scratch_shapes=[pltpu.VMEM((SUBLANES, LANES), F32)],
        compiler_params=_cparams(("arbitrary",)),
        name="route_topk",
    )(logits)


def _lane_cumsum(x):
    lane = lax.broadcasted_iota(I32, x.shape, 1)
    sh = 1
    while sh < LANES:
        x = x + jnp.where(lane >= sh, pltpu.roll(x, sh, axis=1), 0)
        sh *= 2
    return x


def _dest_kernel(meta_ref, cnt_ref, dest_ref, be_ref, *, nb):
    tm = meta_ref.shape[0]
    shift = MOE_BLOCK.bit_length() - 1
    cnt = cnt_ref[...].astype(I32)
    padded = ((cnt + (MOE_BLOCK - 1)) >> shift) << shift
    pend_i = _lane_cumsum(padded)
    pend = pend_i.astype(F32)[0:1, :]
    poff = (pend_i - padded).astype(F32)[0:1, :]
    meta = meta_ref[...]
    lane = lax.broadcasted_iota(I32, (tm, LANES), 1).astype(F32)
    pick = lambda col: jnp.sum(jnp.where(lane == col, meta, 0.0), axis=-1, keepdims=True)
    i1, i2, r1, r2 = pick(0), pick(1), pick(2), pick(3)
    d1 = jnp.sum(jnp.where(lane == i1, poff, 0.0), axis=-1, keepdims=True) + r1
    d2 = jnp.sum(jnp.where(lane == i2, poff, 0.0), axis=-1, keepdims=True) + r2
    dest_ref[...] = (jnp.where(lane == 0, d1, 0.0) + jnp.where(lane == 1, d2, 0.0)).astype(I32)

    nrow = be_ref.shape[0]
    blk = lax.broadcasted_iota(I32, (nrow, LANES), 0).astype(F32)
    lane_b = lax.broadcasted_iota(I32, (nrow, LANES), 1).astype(F32)
    is_e = (lane_b >= ROUTE_LANE0) & (lane_b < ROUTE_LANE0 + N_EXPERTS)
    below = jnp.sum(jnp.where(is_e & (pend <= blk * MOE_BLOCK), 1.0, 0.0), axis=-1, keepdims=True)
    block_e = jnp.minimum(below, N_EXPERTS - 1.0)
    used = jnp.sum(jnp.where(lane_b == ROUTE_LANE0 + N_EXPERTS - 1, pend, 0.0),
                   axis=-1, keepdims=True) * (1.0 / MOE_BLOCK)
    be = jnp.where(blk[:, :1] == nb, used, block_e)
    be_ref[...] = jnp.broadcast_to(be, (nrow, LANES)).astype(I32)


def _dest(meta, cnt, nb, tm=512):
    t = meta.shape[0]
    nrow = -(-(nb + 1) // SUBLANES) * SUBLANES
    kern = functools.partial(_dest_kernel, nb=nb)
    return pl.pallas_call(
        kern,
        grid=(t // tm,),
        in_specs=[pl.BlockSpec((tm, LANES), lambda i: (i, 0)),
                  pl.BlockSpec((SUBLANES, LANES), lambda i: (0, 0))],
        out_specs=[pl.BlockSpec((tm, LANES), lambda i: (i, 0)),
                   pl.BlockSpec((nrow, LANES), lambda i: (0, 0))],
        out_shape=[jax.ShapeDtypeStruct((t, LANES), I32),
                   jax.ShapeDtypeStruct((nrow, LANES), I32)],
        compiler_params=_cparams(("arbitrary",)),
        name="route_dest",
    )(meta, cnt)


def _invert_kernel(dest_ref, fill_hbm, row_ref, sem):
    cp = pltpu.make_async_copy(fill_hbm, row_ref, sem)
    cp.start()
    cp.wait()

    def put(a, c):
        row_ref[dest_ref[a]] = a
        return c

    lax.fori_loop(0, dest_ref.shape[0], put, 0, unroll=8)


def _invert(dest_flat, n_slot, t):
    spare = MOE_SPARE_SETS * MOE_BLOCK
    fill = 2 * t + (jnp.arange(n_slot, dtype=I32) & (spare - 1))
    return pl.pallas_call(
        _invert_kernel,
        in_specs=[pl.BlockSpec(memory_space=pltpu.SMEM), pl.BlockSpec(memory_space=pl.ANY)],
        out_specs=pl.BlockSpec(memory_space=pltpu.SMEM),
        out_shape=jax.ShapeDtypeStruct((n_slot,), I32),
        scratch_shapes=[pltpu.SemaphoreType.DMA(())],
        name="route_invert",
    )(dest_flat, fill)


def _expert_weights(b, used, be_ref, w_hbms, wst, wsem, run_ref, load):
    def fetch(e, p):
        for i, w in enumerate(w_hbms):
            pltpu.make_async_copy(w.at[e], wst.at[p, i], wsem.at[p]).start(priority=1)

    @pl.when((b == 0) & (used > 0))
    def _():
        run_ref[0] = 0
        fetch(be_ref[0], 0)

    bc = jnp.minimum(b, be_ref.shape[0] - 2)
    new_expert = (b == 0) | (be_ref[bc] != be_ref[jnp.maximum(bc - 1, 0)])

    @pl.when(new_expert & (b < used))
    def _():
        p = run_ref[0]
        for i, w in enumerate(w_hbms):
            pltpu.make_async_copy(w.at[0], wst.at[p, i], wsem.at[p]).wait()
        load(p)
        e = be_ref[b]
        nxt = lax.while_loop(lambda i: (i < used) & (be_ref[i] == e), lambda i: i + 1, b + 1)

        @pl.when(nxt < used)
        def _():
            fetch(be_ref[nxt], 1 - p)

        run_ref[0] = 1 - p


def _moe_up_kernel(row_ref, be_ref, h2_hbm, wg_hbm, wu_hbm, o_ref, wst, wbuf, xbuf, xb_scr, sem,
                   wsem, run_ref, *, nb):
    b = pl.program_id(0)
    used = be_ref[nb]
    blk = MOE_BLOCK
    fe = wg_hbm.shape[2]
    t = h2_hbm.shape[0]

    def gather_rows(block, rows):
        slot = block % MOE_GATHER_DEPTH
        base = block * blk
        for r in rows:
            row = row_ref[base + r]
            tok = jnp.where(row >= t, row - t, row)
            tok = jnp.where(tok >= t, 0, tok)
            pltpu.make_async_copy(h2_hbm.at[pl.ds(tok, 1)], xbuf.at[slot, pl.ds(r, 1)],
                                  sem.at[slot]).start()

    ahead = MOE_GATHER_DEPTH - 1

    @pl.when(b == 0)
    def _():
        for first in range(ahead):
            @pl.when(first < used)
            def _():
                gather_rows(first, range(blk))

    def load(p):
        wbuf[:, :fe] = wst[p, 0].astype(BF16)
        wbuf[:, fe:] = wst[p, 1].astype(BF16)

    _expert_weights(b, used, be_ref, (wg_hbm, wu_hbm), wst, wsem, run_ref, load)

    nchunk = fe // MOE_NCHUNK_COLS
    rows_per = blk // nchunk

    def step(prefetch):
        slot = b % MOE_GATHER_DEPTH
        pltpu.make_async_copy(xbuf.at[slot], xbuf.at[slot], sem.at[slot]).wait()
        xb_scr[...] = xbuf[slot].astype(BF16)
        for c in range(nchunk):
            if prefetch:
                gather_rows(b + ahead, range(c * rows_per, (c + 1) * rows_per))
            cols = slice(c * MOE_NCHUNK_COLS, (c + 1) * MOE_NCHUNK_COLS)
            ucols = slice(fe + c * MOE_NCHUNK_COLS, fe + (c + 1) * MOE_NCHUNK_COLS)
            g = jnp.dot(xb_scr[...], wbuf[:, cols], preferred_element_type=F32)
            u = jnp.dot(xb_scr[...], wbuf[:, ucols], preferred_element_type=F32)
            o_ref[:, cols] = (_silu(g) * u).astype(BF16)

    @pl.when(b + ahead < used)
    def _():
        step(True)

    @pl.when((b < used) & (b + ahead >= used))
    def _():
        step(False)

    @pl.when(b >= used)
    def _():
        o_ref[...] = jnp.zeros(o_ref.shape, BF16)


def _moe_up(slot_row, block_e, h2, w_gate, w_up, nb):
    d, fe = w_gate.shape[1], w_gate.shape[2]
    kern = functools.partial(_moe_up_kernel, nb=nb)
    grid_spec = pltpu.PrefetchScalarGridSpec(
        num_scalar_prefetch=2,
        grid=(nb,),
        in_specs=[pl.BlockSpec(memory_space=pl.ANY),
                  pl.BlockSpec(memory_space=pl.ANY),
                  pl.BlockSpec(memory_space=pl.ANY)],
        out_specs=pl.BlockSpec((MOE_BLOCK, fe), lambda b, s, e: (b, 0)),
        scratch_shapes=[pltpu.VMEM((2, 2, d, fe), F32),
                        pltpu.VMEM((d, 2 * fe), BF16),
                        pltpu.VMEM((MOE_GATHER_DEPTH, MOE_BLOCK, d), F32),
                        pltpu.VMEM((MOE_BLOCK, d), BF16),
                        pltpu.SemaphoreType.DMA((MOE_GATHER_DEPTH,)),
                        pltpu.SemaphoreType.DMA((2,)),
                        pltpu.SMEM((1,), I32)],
    )
    return pl.pallas_call(
        kern,
        grid_spec=grid_spec,
        out_shape=jax.ShapeDtypeStruct((nb * MOE_BLOCK, fe), BF16),
        compiler_params=_cparams(("arbitrary",)),
        name="moe_up",
    )(slot_row, block_e, h2, w_gate, w_up)


def _moe_down_kernel(row_ref, be_ref, hm_ref, wd_hbm, ys_hbm, wst, wbuf, obuf, sem, wsem, run_ref,
                     *, nb, t):
    b = pl.program_id(0)
    used = be_ref[nb]
    blk = MOE_BLOCK

    def wait_slot(slot):
        pltpu.make_async_copy(obuf.at[slot], obuf.at[slot], sem.at[slot]).wait()

    depth = obuf.shape[0]

    @pl.when(b == 0)
    def _():
        obuf[0] = jnp.zeros(obuf.shape[1:], F32)
        for k in range(MOE_SPARE_SETS):
            spare = ys_hbm.at[pl.ds(2 * t + k * blk, blk)]
            pltpu.make_async_copy(obuf.at[0], spare, sem.at[depth]).start()
        for k in range(MOE_SPARE_SETS):
            spare = ys_hbm.at[pl.ds(2 * t + k * blk, blk)]
            pltpu.make_async_copy(obuf.at[0], spare, sem.at[depth]).wait()

    @pl.when((b >= depth) & (b - depth < used))
    def _():
        wait_slot(b % depth)

    def load(p):
        wbuf[...] = wst[p, 0].astype(BF16)

    _expert_weights(b, used, be_ref, (wd_hbm,), wst, wsem, run_ref, load)

    d = wbuf.shape[1]
    nchunk = d // MOE_NCHUNK_COLS
    rows_per = blk // nchunk

    def step(scatter, matmul):
        slot = (b - 1) % depth
        base = (b - 1) * blk
        for c in range(nchunk):
            if scatter:
                for r in range(c * rows_per, (c + 1) * rows_per):
                    pltpu.make_async_copy(obuf.at[slot, pl.ds(r, 1)],
                                          ys_hbm.at[pl.ds(row_ref[base + r], 1)],
                                          sem.at[slot]).start(priority=r % 2)
            if matmul:
                cols = slice(c * MOE_NCHUNK_COLS, (c + 1) * MOE_NCHUNK_COLS)
                obuf[b % depth, :, cols] = jnp.dot(hm_ref[...], wbuf[:, cols],
                                                   preferred_element_type=F32)

    @pl.when((b >= 1) & (b < used))
    def _():
        step(True, True)

    @pl.when((b == 0) & (b < used))
    def _():
        step(False, True)

    @pl.when((b >= 1) & (b == used))
    def _():
        step(True, False)

    @pl.when(b == nb)
    def _():
        for k in range(1, depth):
            @pl.when(nb - k < used)
            def _():
                wait_slot((nb - k) % depth)


def _moe_down(slot_row, block_e, hmid, w_down, nb, t):
    fe, d = w_down.shape[1], w_down.shape[2]
    kern = functools.partial(_moe_down_kernel, nb=nb, t=t)
    grid_spec = pltpu.PrefetchScalarGridSpec(
        num_scalar_prefetch=2,
        grid=(nb + 1,),
        in_specs=[pl.BlockSpec((MOE_BLOCK, fe), lambda b, s, e: (jnp.minimum(b, nb - 1), 0)),
                  pl.BlockSpec(memory_space=pl.ANY)],
        out_specs=pl.BlockSpec(memory_space=pl.ANY),
        scratch_shapes=[pltpu.VMEM((2, 1, fe, d), F32),
                        pltpu.VMEM((fe, d), BF16),
                        pltpu.VMEM((MOE_SCATTER_DEPTH, MOE_BLOCK, d), F32),
                        pltpu.SemaphoreType.DMA((MOE_SCATTER_DEPTH + 1,)),
                        pltpu.SemaphoreType.DMA((2,)),
                        pltpu.SMEM((1,), I32)],
    )
    return pl.pallas_call(
        kern,
        grid_spec=grid_spec,
        out_shape=jax.ShapeDtypeStruct((2 * t + MOE_SPARE_SETS * MOE_BLOCK, d), F32),
        compiler_params=_cparams(("arbitrary",)),
        name="moe_down",
    )(slot_row, block_e, hmid, w_down)


def _final_kernel(x1_ref, y0_ref, y1_ref, w_ref, mod_ref, gf_ref, o_ref):
    w = w_ref[...]
    y = w[:, 0:1] * y0_ref[...] + w[:, 1:2] * y1_ref[...]
    x2 = x1_ref[...] + mod_ref[0][5:6, :] * y
    o_ref[...] = x2 * lax.rsqrt(jnp.mean(x2 * x2, axis=-1, keepdims=True) + NORM_EPS) * gf_ref[...]


def _final(x1, ys, w, mod3, g_final, seq, tm=512):
    t, d = x1.shape
    tpb = seq // tm
    return pl.pallas_call(
        _final_kernel,
        grid=(t // tm,),
        in_specs=[pl.BlockSpec((tm, d), lambda i: (i, 0)),
                  pl.BlockSpec((tm, d), lambda i: (i, 0)),
                  pl.BlockSpec((tm, d), lambda i: (i + t // tm, 0)),
                  pl.BlockSpec((tm, LANES), lambda i: (i, 0)),
                  pl.BlockSpec((1, N_MOD, d), lambda i: (i // tpb, 0, 0)),
                  pl.BlockSpec((1, d), lambda i: (0, 0))],
        out_specs=pl.BlockSpec((tm, d), lambda i: (i, 0)),
        out_shape=jax.ShapeDtypeStruct((t, d), F32),
        compiler_params=_cparams(("arbitrary",)),
        name="combine_final_norm",
    )(x1, ys, ys, w, mod3, g_final.reshape(1, d))


def _rope_tables(seq, dh):
    half = dh // 2
    inv_freq = 1.0 / (ROPE_BASE ** (jnp.arange(half, dtype=F32) * 2.0 / dh))
    ang = jnp.arange(seq, dtype=F32)[:, None] * inv_freq[None, :]
    cos, sin = jnp.cos(ang), jnp.sin(ang)
    return jnp.concatenate([cos, cos], axis=1), jnp.concatenate([-sin, sin], axis=1)


def _layer(x, mod, g_norm1, w_in, ssm_a_re, ssm_a_im, ssm_b_re, ssm_b_im, ssm_c_re, ssm_c_im,
           ssm_d, ssm_log_dt, w_glu, beta_ssm, beta_ret, w_out, g_norm2, w_rg, b_rg, w_re, b_re,
           w_gate, w_up, w_down):
    bn, seq, d = x.shape
    t = bn * seq
    x2d = x.reshape(t, d)
    mod3 = mod.reshape(bn, N_MOD, d)

    sw = w_glu.shape[0]
    proj = _inproj(x2d, mod3, g_norm1, w_in.astype(BF16), seq)
    proj3 = proj.reshape(bn, seq, proj.shape[1])

    ab_re, ab_im, bbt_re, bbt_im = _s5_disc(ssm_a_re, ssm_a_im, ssm_log_dt,
                                            jnp.swapaxes(ssm_b_re, 1, 2), jnp.swapaxes(ssm_b_im, 1, 2))
    bz, cz, a_re, a_im = _s5_pack(ab_re, ab_im, bbt_re, bbt_im, ssm_c_re, ssm_c_im)
    ypre = _s5_scan(proj3, bz, cz, a_re, a_im, ssm_d.reshape(-1)).reshape(t, sw)

    heads = (d - sw) // RET_HEAD_DIM
    cosf, sinf = _rope_tables(seq, RET_HEAD_DIM)
    gamma = 1.0 - jnp.exp2(-5.0 - jnp.arange(heads, dtype=F32))
    log_g = jnp.broadcast_to(jnp.log(gamma)[:, None, None], (heads, 1, LANES))
    yret = _retention(proj3, sw // RET_HEAD_DIM, cosf, sinf, log_g, beta_ret)

    n_route = N_GROUPS + N_EXPERTS
    wr = jnp.concatenate([w_rg, w_re, jnp.zeros((d, LANES - n_route), F32)], axis=1)
    br = jnp.concatenate([b_rg, b_re, jnp.zeros((LANES - n_route,), F32)]).reshape(1, LANES)
    wr_hi = wr.astype(BF16)
    wr_lo = (wr - wr_hi.astype(F32)).astype(BF16)
    x1, h2, logits = _mix(ypre, yret.reshape(t, d - sw), x2d, mod3, w_glu.astype(BF16),
                          beta_ssm, w_out.astype(BF16), g_norm2, wr_hi, wr_lo, br, seq)

    nb = -(-(t * 2) // MOE_BLOCK) + N_EXPERTS
    meta, w_tok, cnt = _route(logits)
    dest, be = _dest(meta, cnt, nb)
    slot_row = _invert(dest[:, :2].T.reshape(-1), nb * MOE_BLOCK, t)
    block_e = be[:nb + 1, 0]

    hmid = _moe_up(slot_row, block_e, h2, w_gate, w_up, nb)
    ys = _moe_down(slot_row, block_e, hmid, w_down, nb, t)
    return x1, ys, w_tok, mod3


def kernel(x, c, w_ada, b_ada, g_norm1, w_in, ssm_a_re, ssm_a_im, ssm_b_re, ssm_b_im, ssm_c_re,
           ssm_c_im, ssm_d, ssm_log_dt, w_glu, beta_ssm, beta_ret, w_out, g_norm2, w_router_group,
           b_router_group, w_router_expert, b_router_expert, w_gate, w_up, w_down, g_final):
    depth = w_ada.shape[0]
    bn, seq, d = x.shape
    assert depth == 1, "the final norm is fused into the single layer's last kernel"
    l = 0
    mod = _ada(c, w_ada[l], b_ada[l])
    x1, ys, w_tok, mod3 = _layer(
        x, mod, g_norm1[l], w_in[l], ssm_a_re[l], ssm_a_im[l], ssm_b_re[l], ssm_b_im[l],
        ssm_c_re[l], ssm_c_im[l], ssm_d[l], ssm_log_dt[l], w_glu[l], beta_ssm[l], beta_ret[l],
        w_out[l], g_norm2[l], w_router_group[l], b_router_group[l], w_router_expert[l],
        b_router_expert[l], w_gate[l], w_up[l], w_down[l])
    out = _final(x1, ys, w_tok, mod3, g_final, seq)
    return out.reshape(bn, seq, d)
```

```python
import functools
import math

import jax
import jax.numpy as jnp
from jax import lax
from jax.experimental import pallas as pl
from jax.experimental.pallas import tpu as pltpu

F32 = jnp.float32
BF16 = jnp.bfloat16
I32 = jnp.int32

SSM_GROUP = 16
SSM_STATE = 64
RET_HEAD_DIM = 128
ROPE_BASE = 10000.0
N_GROUPS = 4
EXPERTS_PER_GROUP = 8
N_EXPERTS = N_GROUPS * EXPERTS_PER_GROUP
N_MOD = 6
NORM_EPS = 1e-6
GN_EPS = 1e-5

LANES = 128
SUBLANES = 8
VMEM_LIMIT = 56 * 1024 * 1024

ROUTE_LANE0 = N_GROUPS
MOE_BLOCK = 256
MOE_NCHUNK_COLS = 256
MOE_GATHER_DEPTH = 4
MOE_SCATTER_DEPTH = 3
MOE_SPARE_SETS = 4
RET_CHUNK = 256
S5_TT = 128
S5_BLOCKS_PER_STEP = 4


def _cparams(sem):
    return pltpu.CompilerParams(dimension_semantics=sem, vmem_limit_bytes=VMEM_LIMIT)


def _silu(x):
    return x * jax.nn.sigmoid(x)


def _ada_kernel(c_ref, w_ref, b_ref, o_ref):
    s = _silu(c_ref[...])
    o_ref[...] = jnp.dot(s.astype(BF16), w_ref[...].astype(BF16),
                         preferred_element_type=F32) + b_ref[...]


def _ada(c, w, b, tn=1024):
    bn, d = c.shape
    n = w.shape[1]
    return pl.pallas_call(
        _ada_kernel,
        grid=(n // tn,),
        in_specs=[pl.BlockSpec((bn, d), lambda j: (0, 0)),
                  pl.BlockSpec((d, tn), lambda j: (0, j)),
                  pl.BlockSpec((1, tn), lambda j: (0, j))],
        out_specs=pl.BlockSpec((bn, tn), lambda j: (0, j)),
        out_shape=jax.ShapeDtypeStruct((bn, n), F32),
        compiler_params=_cparams(("arbitrary",)),
        name="ada_mod",
    )(c, w, b.reshape(1, n))


def _inproj_kernel(x0_ref, xn_ref, mod0_ref, modn_ref, g_ref, w_ref, o_ref, h_scr):
    i = pl.program_id(0)
    j = pl.program_id(1)
    nj = pl.num_programs(1)

    def normed(x, m):
        y = x * lax.rsqrt(jnp.mean(x * x, axis=-1, keepdims=True) + NORM_EPS) * g_ref[...]
        return (y * (1.0 + m[1:2, :]) + m[0:1, :]).astype(BF16)

    @pl.when((i == 0) & (j == 0))
    def _():
        h_scr[0] = normed(x0_ref[...], mod0_ref[0])

    rows = xn_ref.shape[0] // nj
    sl = pl.ds(pl.multiple_of(j * rows, rows), rows)
    h_scr[(i + 1) % 2, sl, :] = normed(xn_ref[sl, :], modn_ref[0])

    o_ref[...] = jnp.dot(h_scr[i % 2], w_ref[...], preferred_element_type=F32).astype(BF16)


def _inproj(x2d, mod3, g1, w_in_bf, seq, tm=512, tn=1280):
    t, d = x2d.shape
    n = w_in_bf.shape[1]
    tm = min(tm, seq)
    tn = math.gcd(tn, n)
    tpb = seq // tm
    last = t // tm - 1
    nxt = lambda i: jnp.minimum(i + 1, last)
    return pl.pallas_call(
        _inproj_kernel,
        grid=(t // tm, n // tn),
        in_specs=[pl.BlockSpec((tm, d), lambda i, j: (0, 0), pipeline_mode=pl.Buffered(1)),
                  pl.BlockSpec((tm, d), lambda i, j: (nxt(i), 0)),
                  pl.BlockSpec((1, N_MOD, d), lambda i, j: (0, 0, 0)),
                  pl.BlockSpec((1, N_MOD, d), lambda i, j: (nxt(i) // tpb, 0, 0)),
                  pl.BlockSpec((1, d), lambda i, j: (0, 0)),
                  pl.BlockSpec((d, tn), lambda i, j: (0, j))],
        out_specs=pl.BlockSpec((tm, tn), lambda i, j: (i, j)),
        out_shape=jax.ShapeDtypeStruct((t, n), BF16),
        scratch_shapes=[pltpu.VMEM((2, tm, d), BF16)],
        compiler_params=_cparams(("arbitrary", "arbitrary")),
        name="norm_inproj",
    )(x2d, x2d, mod3, mod3, g1.reshape(1, d), w_in_bf)


def _s5_disc_kernel(are_ref, aim_ref, ldt_ref, bre_ref, bim_ref,
                    abre_ref, abim_ref, bbre_ref, bbim_ref):
    a_re = are_ref[...]
    a_im = aim_ref[...]
    dt = jnp.exp(ldt_ref[...])
    mag = jnp.exp(dt * a_re)
    ab_re = mag * jnp.cos(dt * a_im)
    ab_im = mag * jnp.sin(dt * a_im)
    inv_abs2 = 1.0 / (a_re * a_re + a_im * a_im)
    n_re = ab_re - 1.0
    f_re = (n_re * a_re + ab_im * a_im) * inv_abs2
    f_im = (ab_im * a_re - n_re * a_im) * inv_abs2
    abre_ref[...] = ab_re
    abim_ref[...] = ab_im
    b_re = bre_ref[...]
    b_im = bim_ref[...]
    fr = f_re[:, None, :]
    fi = f_im[:, None, :]
    bbre_ref[...] = fr * b_re - fi * b_im
    bbim_ref[...] = fr * b_im + fi * b_re


def _s5_disc(a_re, a_im, log_dt, bt_re, bt_im):
    g, p = a_re.shape
    h = bt_re.shape[1]
    return pl.pallas_call(
        _s5_disc_kernel,
        out_shape=[jax.ShapeDtypeStruct((g, p), F32), jax.ShapeDtypeStruct((g, p), F32),
                   jax.ShapeDtypeStruct((g, h, p), F32), jax.ShapeDtypeStruct((g, h, p), F32)],
        name="s5_discretise",
    )(a_re, a_im, log_dt.reshape(g, 1), bt_re, bt_im)


def _s5_pack(ab_re, ab_im, bbt_re, bbt_im, c_re, c_im):
    g, h, p = bbt_re.shape
    npair = g // 2
    ppb = LANES // (2 * h)
    eye2 = jnp.eye(2, dtype=F32)
    qsel = jax.nn.one_hot(jnp.arange(npair) % ppb, ppb, dtype=F32)

    def in_mat(bbt):
        b4 = bbt.reshape(npair, 2, h, p)
        return jnp.einsum('ab,xahp->xahbp', eye2, b4).reshape(npair, 2 * h, 2 * p)

    bsmall = jnp.concatenate([in_mat(bbt_re), in_mat(bbt_im)], axis=-1)
    bz = jnp.einsum('xq,xrc->xqrc', qsel, bsmall).reshape(npair, LANES, 4 * p)

    def out_mat(c):
        c4 = jnp.swapaxes(c, 1, 2).reshape(npair, 2, p, h)
        return jnp.einsum('ab,xaph->xapbh', eye2, c4).reshape(npair, 2 * p, 2 * h)

    csmall = jnp.concatenate([out_mat(c_re), -out_mat(c_im)], axis=1)
    cz = jnp.einsum('xq,xrc->xrqc', qsel, csmall).reshape(npair, 4 * p, LANES)
    a_re = ab_re.reshape(npair // ppb, ppb, 2 * p)
    a_im = ab_im.reshape(npair // ppb, ppb, 2 * p)
    return bz.astype(BF16), cz.astype(BF16), a_re, a_im


def _s5_kernel(u_ref, bz_ref, cz_ref, are_ref, aim_ref, d_ref, o_ref, state, *bufs, ppb):
    t = pl.program_id(0)
    j = pl.program_id(1)
    bn, tt, width = u_ref.shape
    nblk = width // LANES
    rt = bn * tt

    @pl.when(t == 0)
    def _():
        for i in range(nblk):
            state[j * nblk + i] = jnp.zeros(state.shape[1:], F32)

    for i in range(nblk):
        buf = bufs[i]
        kb = j * nblk + i
        lanes = slice(i * LANES, (i + 1) * LANES)

        u = u_ref[:, :, lanes].reshape(rt, LANES)
        for q in range(ppb):
            bu = jnp.dot(u, bz_ref[i * ppb + q], preferred_element_type=F32)
            for b in range(bn):
                rows = pl.ds(b, tt, stride=bn)
                buf[2 * q, rows, :] = bu[b * tt:(b + 1) * tt, :LANES]
                buf[2 * q + 1, rows, :] = bu[b * tt:(b + 1) * tt, LANES:]

        a_re = [jnp.broadcast_to(are_ref[i, q:q + 1, :], (bn, LANES)) for q in range(ppb)]
        a_im = [jnp.broadcast_to(aim_ref[i, q:q + 1, :], (bn, LANES)) for q in range(ppb)]
        s = [state[kb, k] for k in range(2 * ppb)]
        for step in range(tt):
            rows = slice(step * bn, (step + 1) * bn)
            for q in range(ppb):
                s_re, s_im = s[2 * q], s[2 * q + 1]
                n_re = a_re[q] * s_re - a_im[q] * s_im + buf[2 * q, rows, :]
                n_im = a_re[q] * s_im + a_im[q] * s_re + buf[2 * q + 1, rows, :]
                buf[2 * q, rows, :] = n_re
                buf[2 * q + 1, rows, :] = n_im
                s[2 * q], s[2 * q + 1] = n_re, n_im
        for k in range(2 * ppb):
            state[kb, k] = s[k]

        acc = None
        for q in range(ppb):
            lhs = jnp.concatenate([buf[2 * q], buf[2 * q + 1]], axis=1).astype(BF16)
            part = jnp.dot(lhs, cz_ref[i * ppb + q], preferred_element_type=F32)
            acc = part if acc is None else acc + part
        buf[2 * ppb] = acc
        d_skip = d_ref[:, lanes]
        for b in range(bn):
            o_ref[b, :, lanes] = (buf[2 * ppb, pl.ds(b, tt, stride=bn), :]
                                  + d_skip * u_ref[b, :, lanes].astype(F32))


def _s5_scan(u3, bz, cz, a_re, a_im, d_skip, tt=S5_TT, nblk=S5_BLOCKS_PER_STEP):
    bn, seq, width = u3.shape[0], u3.shape[1], d_skip.shape[0]
    nkb = width // LANES
    ppb = bz.shape[0] // nkb
    nblk = math.gcd(nblk, nkb)
    bw = nblk * LANES
    kern = functools.partial(_s5_kernel, ppb=ppb)
    return pl.pallas_call(
        kern,
        grid=(seq // tt, nkb // nblk),
        in_specs=[pl.BlockSpec((bn, tt, bw), lambda t, k: (0, t, k)),
                  pl.BlockSpec((nblk * ppb,) + bz.shape[1:], lambda t, k: (k, 0, 0)),
                  pl.BlockSpec((nblk * ppb,) + cz.shape[1:], lambda t, k: (k, 0, 0)),
                  pl.BlockSpec((nblk,) + a_re.shape[1:], lambda t, k: (k, 0, 0)),
                  pl.BlockSpec((nblk,) + a_im.shape[1:], lambda t, k: (k, 0, 0)),
                  pl.BlockSpec((1, bw), lambda t, k: (0, k))],
        out_specs=pl.BlockSpec((bn, tt, bw), lambda t, k: (0, t, k)),
        out_shape=jax.ShapeDtypeStruct((bn, seq, width), F32),
        scratch_shapes=[pltpu.VMEM((nkb, 2 * ppb, bn, LANES), F32)]
        + [pltpu.VMEM((2 * ppb + 1, bn * tt, LANES), F32) for _ in range(nblk)],
        compiler_params=_cparams(("arbitrary", "arbitrary")),
        name="s5_scan",
    )(u3, bz, cz, a_re, a_im, d_skip.reshape(1, width))


def _ret_kernel(q_ref, k_ref, v_ref, g_ref, cq_ref, sq_ref, ck_ref, sk_ref, lg_ref, beta_ref, o_ref,
                q_scr, k_scr, kzt_scr, y_scr, *, chunk):
    seq, dh = q_ref.shape[1], q_ref.shape[2]
    lg = lg_ref[0][:, :1]
    ri = lax.broadcasted_iota(I32, (chunk, chunk), 0)
    ci = lax.broadcasted_iota(I32, (chunk, chunk), 1)
    rel = (ri - ci).astype(F32)
    decay = jnp.where(rel >= 0, jnp.exp(lg * jnp.maximum(rel, 0.0)), 0.0)
    idx = lax.broadcasted_iota(I32, (chunk, 1), 0).astype(F32)
    xi = jnp.exp(lg * (idx + 1.0))
    g_chunk = jnp.exp(lg * float(chunk))
    beta = beta_ref[...]
    half = dh // 2
    pr = lax.broadcasted_iota(I32, (dh, dh), 0)
    pc = lax.broadcasted_iota(I32, (dh, dh), 1)
    swap = jnp.where(((pr + half) & (dh - 1)) == pc, 1.0, 0.0).astype(BF16)

    def rope(x_ref, cos_ref, sin_ref, sl):
        xb = x_ref[0, sl, :]
        rolled = jnp.dot(xb, swap, preferred_element_type=F32).astype(BF16)
        return xb * cos_ref[sl, :] + rolled * sin_ref[sl, :]

    zeta = jnp.exp(lg * (chunk - 1.0 - idx))
    for n in range(seq // chunk):
        sl = slice(n * chunk, (n + 1) * chunk)
        q_scr[sl, :] = rope(q_ref, cq_ref, sq_ref, sl)
        kb = rope(k_ref, ck_ref, sk_ref, sl)
        k_scr[sl, :] = kb
        kzt_scr[:, sl] = (kb.astype(F32) * zeta).T.astype(BF16)

    decay = decay.astype(BF16)
    r = jnp.zeros((dh, dh), F32)
    for n in range(seq // chunk):
        sl = slice(n * chunk, (n + 1) * chunk)
        qb = q_scr[sl, :]
        v = v_ref[0, sl, :]
        s = lax.dot_general(qb, k_scr[sl, :], (((1,), (1,)), ((), ())),
                            preferred_element_type=F32).astype(BF16) * decay
        y = jnp.dot(s, v, preferred_element_type=F32)
        y_scr[sl, :] = y + jnp.dot(qb, r.astype(BF16), preferred_element_type=F32) * xi
        r = r * g_chunk + jnp.dot(kzt_scr[:, sl], v, preferred_element_type=F32)

    y = y_scr[...]
    mu = jnp.mean(y, axis=-1, keepdims=True)
    yc = y - mu
    var = jnp.mean(yc * yc, axis=-1, keepdims=True)
    yn = yc * lax.rsqrt(var + GN_EPS) * beta
    o_ref[0] = (_silu(g_ref[0].astype(F32)) * yn).astype(BF16)


def _retention(proj3, col0, cosf, sinf, log_g, beta_ret, chunk=RET_CHUNK):
    bn, seq, _ = proj3.shape
    dh = RET_HEAD_DIM
    heads = beta_ret.shape[0] // dh
    blk = lambda off: pl.BlockSpec((1, seq, dh), lambda b, h, off=off: (b, 0, col0 + off + h))
    k_scale = dh ** -0.5
    kern = functools.partial(_ret_kernel, chunk=chunk)
    return pl.pallas_call(
        kern,
        grid=(bn, heads),
        in_specs=[blk(0), blk(heads), blk(2 * heads), blk(3 * heads),
                  pl.BlockSpec((seq, dh), lambda b, h: (0, 0)),
                  pl.BlockSpec((seq, dh), lambda b, h: (0, 0)),
                  pl.BlockSpec((seq, dh), lambda b, h: (0, 0)),
                  pl.BlockSpec((seq, dh), lambda b, h: (0, 0)),
                  pl.BlockSpec((1, 1, LANES), lambda b, h: (h, 0, 0)),
                  pl.BlockSpec((1, dh), lambda b, h: (0, h))],
        out_specs=pl.BlockSpec((1, seq, dh), lambda b, h: (b, 0, h)),
        out_shape=jax.ShapeDtypeStruct((bn, seq, heads * dh), BF16),
        scratch_shapes=[pltpu.VMEM((seq, dh), BF16), pltpu.VMEM((seq, dh), BF16),
                        pltpu.VMEM((dh, seq), BF16), pltpu.VMEM((seq, dh), F32)],
        compiler_params=_cparams(("arbitrary", "arbitrary")),
        name="retention",
    )(proj3, proj3, proj3, proj3, cosf.astype(BF16), sinf.astype(BF16),
      (cosf * k_scale).astype(BF16), (sinf * k_scale).astype(BF16),
      log_g, beta_ret.reshape(1, heads * dh))


def _gelu_tanh(x):
    return 0.5 * x * (1.0 + jnp.tanh(math.sqrt(2.0 / math.pi) * (x + 0.044715 * (x * x * x))))


def _split_bf16(x):
    hi = x.astype(BF16)
    lo = (x - hi.astype(F32)).astype(BF16)
    return hi, lo


def _mix_kernel(ys_ref, yr_ref, x_ref, mod_ref, wglu_ref, bssm_ref, wout_ref, g2_ref,
                wrh_ref, wrl_ref, br_ref, x1_ref, h2_ref, lg_ref):
    sw = ys_ref.shape[1]
    z = _gelu_tanh(ys_ref[...])
    gate = jax.nn.sigmoid(jnp.dot(z.astype(BF16), wglu_ref[...], preferred_element_type=F32))
    o = z * gate
    y_ssm = o * lax.rsqrt(jnp.mean(o * o, axis=-1, keepdims=True) + NORM_EPS) * bssm_ref[...]
    mixed = (jnp.dot(y_ssm.astype(BF16), wout_ref[:sw, :], preferred_element_type=F32)
             + jnp.dot(yr_ref[...], wout_ref[sw:, :], preferred_element_type=F32))
    m = mod_ref[0]
    x1 = x_ref[...] + m[2:3, :] * mixed
    x1_ref[...] = x1
    y = x1 * lax.rsqrt(jnp.mean(x1 * x1, axis=-1, keepdims=True) + NORM_EPS) * g2_ref[...]
    h2 = y * (1.0 + m[4:5, :]) + m[3:4, :]
    h2_ref[...] = h2
    hi, lo = _split_bf16(h2)
    wrh = wrh_ref[...]
    lg_ref[...] = (jnp.dot(hi, wrh, preferred_element_type=F32)
                   + jnp.dot(lo, wrh, preferred_element_type=F32)
                   + jnp.dot(hi, wrl_ref[...], preferred_element_type=F32)
                   + br_ref[...])


def _mix(ypre, yret, x2d, mod3, wglu_bf, beta_ssm, wout_bf, g2, wr_hi, wr_lo, br, seq, tm=256):
    t, d = x2d.shape
    sw = wglu_bf.shape[0]
    tpb = seq // tm
    const = lambda shape: pl.BlockSpec(shape, lambda i: (0,) * len(shape),
                                       pipeline_mode=pl.Buffered(1))
    return pl.pallas_call(
        _mix_kernel,
        grid=(t // tm,),
        in_specs=[pl.BlockSpec((tm, sw), lambda i: (i, 0)),
                  pl.BlockSpec((tm, d - sw), lambda i: (i, 0)),
                  pl.BlockSpec((tm, d), lambda i: (i, 0)),
                  pl.BlockSpec((1, N_MOD, d), lambda i: (i // tpb, 0, 0)),
                  const((sw, sw)), const((1, sw)), const((d, d)), const((1, d)),
                  const((d, LANES)), const((d, LANES)), const((1, LANES))],
        out_specs=[pl.BlockSpec((tm, d), lambda i: (i, 0)),
                   pl.BlockSpec((tm, d), lambda i: (i, 0)),
                   pl.BlockSpec((tm, LANES), lambda i: (i, 0))],
        out_shape=[jax.ShapeDtypeStruct((t, d), F32),
                   jax.ShapeDtypeStruct((t, d), F32),
                   jax.ShapeDtypeStruct((t, LANES), F32)],
        compiler_params=_cparams(("arbitrary",)),
        name="mix_outproj_router",
    )(ypre, yret, x2d, mod3, wglu_bf, beta_ssm.reshape(1, sw), wout_bf, g2.reshape(1, d),
      wr_hi, wr_lo, br)


def _route_kernel(lg_ref, meta_ref, w_ref, cnt_ref, carry):
    i = pl.program_id(0)
    tm = lg_ref.shape[0]

    @pl.when(i == 0)
    def _():
        carry[...] = jnp.zeros(carry.shape, F32)

    lg = lg_ref[...]
    lane = lax.broadcasted_iota(I32, (tm, LANES), 1).astype(F32)
    neg = jnp.float32(-jnp.inf)

    def first_max(vals):
        m = jnp.max(vals, axis=-1, keepdims=True)
        idx = jnp.min(jnp.where(vals == m, lane, float(LANES)), axis=-1, keepdims=True)
        return m, idx

    gl = jnp.where(lane < N_GROUPS, lg, neg)
    gmax, gsel = first_max(gl)
    g_w = 1.0 / jnp.sum(jnp.exp(gl - gmax), axis=-1, keepdims=True)
    lo = ROUTE_LANE0 + gsel * EXPERTS_PER_GROUP
    el = jnp.where((lane >= lo) & (lane < lo + EXPERTS_PER_GROUP), lg, neg)
    m1, i1 = first_max(el)
    m2, i2 = first_max(jnp.where(lane == i1, neg, el))
    e21 = jnp.exp(m2 - m1)
    w1 = g_w / (1.0 + e21)
    w2 = g_w * e21 / (1.0 + e21)

    sel1 = lane == i1
    sel2 = lane == i2
    onehot = jnp.where(sel1 | sel2, 1.0, 0.0).astype(BF16)
    ri = lax.broadcasted_iota(I32, (tm, tm), 0)
    ci = lax.broadcasted_iota(I32, (tm, tm), 1)
    tri = jnp.where(ci < ri, 1.0, 0.0).astype(BF16)
    base = carry[0:1, :]
    excl = jnp.dot(tri, onehot, preferred_element_type=F32) + base
    r1 = jnp.sum(jnp.where(sel1, excl, 0.0), axis=-1, keepdims=True)
    r2 = jnp.sum(jnp.where(sel2, excl, 0.0), axis=-1, keepdims=True)
    total = base + jnp.sum(jnp.where(sel1 | sel2, 1.0, 0.0), axis=0, keepdims=True)
    carry[...] = jnp.broadcast_to(total, carry.shape)
    cnt_ref[...] = jnp.broadcast_to(total, cnt_ref.shape)

    meta_ref[...] = (jnp.where(lane == 0, i1, 0.0) + jnp.where(lane == 1, i2, 0.0)
                     + jnp.where(lane == 2, r1, 0.0) + jnp.where(lane == 3, r2, 0.0))
    w_ref[...] = jnp.where(lane == 0, w1, 0.0) + jnp.where(lane == 1, w2, 0.0)


def _route(logits, tm=512):
    t = logits.shape[0]
    return pl.pallas_call(
        _route_kernel,
        grid=(t // tm,),
        in_specs=[pl.BlockSpec((tm, LANES), lambda i: (i, 0))],
        out_specs=[pl.BlockSpec((tm, LANES), lambda i: (i, 0)),
                   pl.BlockSpec((tm, LANES), lambda i: (i, 0)),
                   pl.BlockSpec((SUBLANES, LANES), lambda i: (0, 0))],
        out_shape=[jax.ShapeDtypeStruct((t, LANES), F32),
                   jax.ShapeDtypeStruct((t, LANES), F32),
                   jax.ShapeDtypeStruct((SUBLANES, LANES), F32)],
        scratch_shapes=[pltpu.VMEM((SUBLANES, LANES), F32)],
        compiler_params=_cparams(("arbitrary",)),
        name="route_topk",
    )(logits)


def _lane_cumsum(x):
    lane = lax.broadcasted_iota(I32, x.shape, 1)
    sh = 1
    while sh < LANES:
        x = x + jnp.where(lane >= sh, pltpu.roll(x, sh, axis=1), 0)
        sh *= 2
    return x


def _dest_kernel(meta_ref, cnt_ref, dest_ref, be_ref, *, nb):
    tm = meta_ref.shape[0]
    shift = MOE_BLOCK.bit_length() - 1
    cnt = cnt_ref[...].astype(I32)
    padded = ((cnt + (MOE_BLOCK - 1)) >> shift) << shift
    pend_i = _lane_cumsum(padded)
    pend = pend_i.astype(F32)[0:1, :]
    poff = (pend_i - padded).astype(F32)[0:1, :]
    meta = meta_ref[...]
    lane = lax.broadcasted_iota(I32, (tm, LANES), 1).astype(F32)
    pick = lambda col: jnp.sum(jnp.where(lane == col, meta, 0.0), axis=-1, keepdims=True)
    i1, i2, r1, r2 = pick(0), pick(1), pick(2), pick(3)
    d1 = jnp.sum(jnp.where(lane == i1, poff, 0.0), axis=-1, keepdims=True) + r1
    d2 = jnp.sum(jnp.where(lane == i2, poff, 0.0), axis=-1, keepdims=True) + r2
    dest_ref[...] = (jnp.where(lane == 0, d1, 0.0) + jnp.where(lane == 1, d2, 0.0)).astype(I32)

    nrow = be_ref.shape[0]
    blk = lax.broadcasted_iota(I32, (nrow, LANES), 0).astype(F32)
    lane_b = lax.broadcasted_iota(I32, (nrow, LANES), 1).astype(F32)
    is_e = (lane_b >= ROUTE_LANE0) & (lane_b < ROUTE_LANE0 + N_EXPERTS)
    below = jnp.sum(jnp.where(is_e & (pend <= blk * MOE_BLOCK), 1.0, 0.0), axis=-1, keepdims=True)
    block_e = jnp.minimum(below, N_EXPERTS - 1.0)
    used = jnp.sum(jnp.where(lane_b == ROUTE_LANE0 + N_EXPERTS - 1, pend, 0.0),
                   axis=-1, keepdims=True) * (1.0 / MOE_BLOCK)
    be = jnp.where(blk[:, :1] == nb, used, block_e)
    be_ref[...] = jnp.broadcast_to(be, (nrow, LANES)).astype(I32)


def _dest(meta, cnt, nb, tm=512):
    t = meta.shape[0]
    nrow = -(-(nb + 1) // SUBLANES) * SUBLANES
    kern = functools.partial(_dest_kernel, nb=nb)
    return pl.pallas_call(
        kern,
        grid=(t // tm,),
        in_specs=[pl.BlockSpec((tm, LANES), lambda i: (i, 0)),
                  pl.BlockSpec((SUBLANES, LANES), lambda i: (0, 0))],
        out_specs=[pl.BlockSpec((tm, LANES), lambda i: (i, 0)),
                   pl.BlockSpec((nrow, LANES), lambda i: (0, 0))],
        out_shape=[jax.ShapeDtypeStruct((t, LANES), I32),
                   jax.ShapeDtypeStruct((nrow, LANES), I32)],
        compiler_params=_cparams(("arbitrary",)),
        name="route_dest",
    )(meta, cnt)


def _invert_kernel(dest_ref, fill_hbm, row_ref, sem):
    cp = pltpu.make_async_copy(fill_hbm, row_ref, sem)
    cp.start()
    cp.wait()

    def put(a, c):
        row_ref[dest_ref[a]] = a
        return c

    lax.fori_loop(0, dest_ref.shape[0], put, 0, unroll=8)


def _invert(dest_flat, n_slot, t):
    spare = MOE_SPARE_SETS * MOE_BLOCK
    fill = 2 * t + (jnp.arange(n_slot, dtype=I32) & (spare - 1))
    return pl.pallas_call(
        _invert_kernel,
        in_specs=[pl.BlockSpec(memory_space=pltpu.SMEM), pl.BlockSpec(memory_space=pl.ANY)],
        out_specs=pl.BlockSpec(memory_space=pltpu.SMEM),
        out_shape=jax.ShapeDtypeStruct((n_slot,), I32),
        scratch_shapes=[pltpu.SemaphoreType.DMA(())],
        name="route_invert",
    )(dest_flat, fill)


def _expert_weights(b, used, be_ref, w_hbms, wst, wsem, run_ref, load):
    def fetch(e, p):
        for i, w in enumerate(w_hbms):
            pltpu.make_async_copy(w.at[e], wst.at[p, i], wsem.at[p]).start(priority=1)

    @pl.when((b == 0) & (used > 0))
    def _():
        run_ref[0] = 0
        fetch(be_ref[0], 0)

    bc = jnp.minimum(b, be_ref.shape[0] - 2)
    new_expert = (b == 0) | (be_ref[bc] != be_ref[jnp.maximum(bc - 1, 0)])

    @pl.when(new_expert & (b < used))
    def _():
        p = run_ref[0]
        for i, w in enumerate(w_hbms):
            pltpu.make_async_copy(w.at[0], wst.at[p, i], wsem.at[p]).wait()
        load(p)
        e = be_ref[b]
        nxt = lax.while_loop(lambda i: (i < used) & (be_ref[i] == e), lambda i: i + 1, b + 1)

        @pl.when(nxt < used)
        def _():
            fetch(be_ref[nxt], 1 - p)

        run_ref[0] = 1 - p


def _moe_up_kernel(row_ref, be_ref, h2_hbm, wg_hbm, wu_hbm, o_ref, wst, wbuf, xbuf, xb_scr, sem,
                   wsem, run_ref, *, nb):
    b = pl.program_id(0)
    used = be_ref[nb]
    blk = MOE_BLOCK
    fe = wg_hbm.shape[2]
    t = h2_hbm.shape[0]

    def gather_rows(block, rows):
        slot = block % MOE_GATHER_DEPTH
        base = block * blk
        for r in rows:
            row = row_ref[base + r]
            tok = jnp.where(row >= t, row - t, row)
            tok = jnp.where(tok >= t, 0, tok)
            pltpu.make_async_copy(h2_hbm.at[pl.ds(tok, 1)], xbuf.at[slot, pl.ds(r, 1)],
                                  sem.at[slot]).start()

    ahead = MOE_GATHER_DEPTH - 1

    @pl.when(b == 0)
    def _():
        for first in range(ahead):
            @pl.when(first < used)
            def _():
                gather_rows(first, range(blk))

    def load(p):
        wbuf[:, :fe] = wst[p, 0].astype(BF16)
        wbuf[:, fe:] = wst[p, 1].astype(BF16)

    _expert_weights(b, used, be_ref, (wg_hbm, wu_hbm), wst, wsem, run_ref, load)

    nchunk = fe // MOE_NCHUNK_COLS
    rows_per = blk // nchunk

    def step(prefetch):
        slot = b % MOE_GATHER_DEPTH
        pltpu.make_async_copy(xbuf.at[slot], xbuf.at[slot], sem.at[slot]).wait()
        xb_scr[...] = xbuf[slot].astype(BF16)
        for c in range(nchunk):
            if prefetch:
                gather_rows(b + ahead, range(c * rows_per, (c + 1) * rows_per))
            cols = slice(c * MOE_NCHUNK_COLS, (c + 1) * MOE_NCHUNK_COLS)
            ucols = slice(fe + c * MOE_NCHUNK_COLS, fe + (c + 1) * MOE_NCHUNK_COLS)
            g = jnp.dot(xb_scr[...], wbuf[:, cols], preferred_element_type=F32)
            u = jnp.dot(xb_scr[...], wbuf[:, ucols], preferred_element_type=F32)
            o_ref[:, cols] = (_silu(g) * u).astype(BF16)

    @pl.when(b + ahead < used)
    def _():
        step(True)

    @pl.when((b < used) & (b + ahead >= used))
    def _():
        step(False)

    @pl.when(b >= used)
    def _():
        o_ref[...] = jnp.zeros(o_ref.shape, BF16)


def _moe_up(slot_row, block_e, h2, w_gate, w_up, nb):
    d, fe = w_gate.shape[1], w_gate.shape[2]
    kern = functools.partial(_moe_up_kernel, nb=nb)
    grid_spec = pltpu.PrefetchScalarGridSpec(
        num_scalar_prefetch=2,
        grid=(nb,),
        in_specs=[pl.BlockSpec(memory_space=pl.ANY),
                  pl.BlockSpec(memory_space=pl.ANY),
                  pl.BlockSpec(memory_space=pl.ANY)],
        out_specs=pl.BlockSpec((MOE_BLOCK, fe), lambda b, s, e: (b, 0)),
        scratch_shapes=[pltpu.VMEM((2, 2, d, fe), F32),
                        pltpu.VMEM((d, 2 * fe), BF16),
                        pltpu.VMEM((MOE_GATHER_DEPTH, MOE_BLOCK, d), F32),
                        pltpu.VMEM((MOE_BLOCK, d), BF16),
                        pltpu.SemaphoreType.DMA((MOE_GATHER_DEPTH,)),
                        pltpu.SemaphoreType.DMA((2,)),
                        pltpu.SMEM((1,), I32)],
    )
    return pl.pallas_call(
        kern,
        grid_spec=grid_spec,
        out_shape=jax.ShapeDtypeStruct((nb * MOE_BLOCK, fe), BF16),
        compiler_params=_cparams(("arbitrary",)),
        name="moe_up",
    )(slot_row, block_e, h2, w_gate, w_up)


def _moe_down_kernel(row_ref, be_ref, hm_ref, wd_hbm, ys_hbm, wst, wbuf, obuf, sem, wsem, run_ref,
                     *, nb, t):
    b = pl.program_id(0)
    used = be_ref[nb]
    blk = MOE_BLOCK

    def wait_slot(slot):
        pltpu.make_async_copy(obuf.at[slot], obuf.at[slot], sem.at[slot]).wait()

    depth = obuf.shape[0]

    @pl.when(b == 0)
    def _():
        obuf[0] = jnp.zeros(obuf.shape[1:], F32)
        for k in range(MOE_SPARE_SETS):
            spare = ys_hbm.at[pl.ds(2 * t + k * blk, blk)]
            pltpu.make_async_copy(obuf.at[0], spare, sem.at[depth]).start()
        for k in range(MOE_SPARE_SETS):
            spare = ys_hbm.at[pl.ds(2 * t + k * blk, blk)]
            pltpu.make_async_copy(obuf.at[0], spare, sem.at[depth]).wait()

    @pl.when((b >= depth) & (b - depth < used))
    def _():
        wait_slot(b % depth)

    def load(p):
        wbuf[...] = wst[p, 0].astype(BF16)

    _expert_weights(b, used, be_ref, (wd_hbm,), wst, wsem, run_ref, load)

    d = wbuf.shape[1]
    nchunk = d // MOE_NCHUNK_COLS
    rows_per = blk // nchunk

    def step(scatter, matmul):
        slot = (b - 1) % depth
        base = (b - 1) * blk
        for c in range(nchunk):
            if scatter:
                for r in range(c * rows_per, (c + 1) * rows_per):
                    pltpu.make_async_copy(obuf.at[slot, pl.ds(r, 1)],
                                          ys_hbm.at[pl.ds(row_ref[base + r], 1)],
                                          sem.at[slot]).start(priority=r % 2)
            if matmul:
                cols = slice(c * MOE_NCHUNK_COLS, (c + 1) * MOE_NCHUNK_COLS)
                obuf[b % depth, :, cols] = jnp.dot(hm_ref[...], wbuf[:, cols],
                                                   preferred_element_type=F32)

    @pl.when((b >= 1) & (b < used))
    def _():
        step(True, True)

    @pl.when((b == 0) & (b < used))
    def _():
        step(False, True)

    @pl.when((b >= 1) & (b == used))
    def _():
        step(True, False)

    @pl.when(b == nb)
    def _():
        for k in range(1, depth):
            @pl.when(nb - k < used)
            def _():
                wait_slot((nb - k) % depth)


def _moe_down(slot_row, block_e, hmid, w_down, nb, t):
    fe, d = w_down.shape[1], w_down.shape[2]
    kern = functools.partial(_moe_down_kernel, nb=nb, t=t)
    grid_spec = pltpu.PrefetchScalarGridSpec(
        num_scalar_prefetch=2,
        grid=(nb + 1,),
        in_specs=[pl.BlockSpec((MOE_BLOCK, fe), lambda b, s, e: (jnp.minimum(b, nb - 1), 0)),
                  pl.BlockSpec(memory_space=pl.ANY)],
        out_specs=pl.BlockSpec(memory_space=pl.ANY),
        scratch_shapes=[pltpu.VMEM((2, 1, fe, d), F32),
                        pltpu.VMEM((fe, d), BF16),
                        pltpu.VMEM((MOE_SCATTER_DEPTH, MOE_BLOCK, d), F32),
                        pltpu.SemaphoreType.DMA((MOE_SCATTER_DEPTH + 1,)),
                        pltpu.SemaphoreType.DMA((2,)),
                        pltpu.SMEM((1,), I32)],
    )
    return pl.pallas_call(
        kern,
        grid_spec=grid_spec,
        out_shape=jax.ShapeDtypeStruct((2 * t + MOE_SPARE_SETS * MOE_BLOCK, d), F32),
        compiler_params=_cparams(("arbitrary",)),
        name="moe_down",
    )(slot_row, block_e, hmid, w_down)


def _final_kernel(x1_ref, y0_ref, y1_ref, w_ref, mod_ref, gf_ref, o_ref):
    w = w_ref[...]
    y = w[:, 0:1] * y0_ref[...] + w[:, 1:2] * y1_ref[...]
    x2 = x1_ref[...] + mod_ref[0][5:6, :] * y
    o_ref[...] = x2 * lax.rsqrt(jnp.mean(x2 * x2, axis=-1, keepdims=True) + NORM_EPS) * gf_ref[...]


def _final(x1, ys, w, mod3, g_final, seq, tm=512):
    t, d = x1.shape
    tpb = seq // tm
    return pl.pallas_call(
        _final_kernel,
        grid=(t // tm,),
        in_specs=[pl.BlockSpec((tm, d), lambda i: (i, 0)),
                  pl.BlockSpec((tm, d), lambda i: (i, 0)),
                  pl.BlockSpec((tm, d), lambda i: (i + t // tm, 0)),
                  pl.BlockSpec((tm, LANES), lambda i: (i, 0)),
                  pl.BlockSpec((1, N_MOD, d), lambda i: (i // tpb, 0, 0)),
                  pl.BlockSpec((1, d), lambda i: (0, 0))],
        out_specs=pl.BlockSpec((tm, d), lambda i: (i, 0)),
        out_shape=jax.ShapeDtypeStruct((t, d), F32),
        compiler_params=_cparams(("arbitrary",)),
        name="combine_final_norm",
    )(x1, ys, ys, w, mod3, g_final.reshape(1, d))


def _rope_tables(seq, dh):
    half = dh // 2
    inv_freq = 1.0 / (ROPE_BASE ** (jnp.arange(half, dtype=F32) * 2.0 / dh))
    ang = jnp.arange(seq, dtype=F32)[:, None] * inv_freq[None, :]
    cos, sin = jnp.cos(ang), jnp.sin(ang)
    return jnp.concatenate([cos, cos], axis=1), jnp.concatenate([-sin, sin], axis=1)


def _layer(x, mod, g_norm1, w_in, ssm_a_re, ssm_a_im, ssm_b_re, ssm_b_im, ssm_c_re, ssm_c_im,
           ssm_d, ssm_log_dt, w_glu, beta_ssm, beta_ret, w_out, g_norm2, w_rg, b_rg, w_re, b_re,
           w_gate, w_up, w_down):
    bn, seq, d = x.shape
    t = bn * seq
    x2d = x.reshape(t, d)
    mod3 = mod.reshape(bn, N_MOD, d)

    sw = w_glu.shape[0]
    proj = _inproj(x2d, mod3, g_norm1, w_in.astype(BF16), seq)
    proj3 = proj.reshape(bn, seq, proj.shape[1])

    ab_re, ab_im, bbt_re, bbt_im = _s5_disc(ssm_a_re, ssm_a_im, ssm_log_dt,
                                            jnp.swapaxes(ssm_b_re, 1, 2), jnp.swapaxes(ssm_b_im, 1, 2))
    bz, cz, a_re, a_im = _s5_pack(ab_re, ab_im, bbt_re, bbt_im, ssm_c_re, ssm_c_im)
    ypre = _s5_scan(proj3, bz, cz, a_re, a_im, ssm_d.reshape(-1)).reshape(t, sw)

    heads = (d - sw) // RET_HEAD_DIM
    cosf, sinf = _rope_tables(seq, RET_HEAD_DIM)
    gamma = 1.0 - jnp.exp2(-5.0 - jnp.arange(heads, dtype=F32))
    log_g = jnp.broadcast_to(jnp.log(gamma)[:, None, None], (heads, 1, LANES))
    yret = _retention(proj3, sw // RET_HEAD_DIM, cosf, sinf, log_g, beta_ret)

    n_route = N_GROUPS + N_EXPERTS
    wr = jnp.concatenate([w_rg, w_re, jnp.zeros((d, LANES - n_route), F32)], axis=1)
    br = jnp.concatenate([b_rg, b_re, jnp.zeros((LANES - n_route,), F32)]).reshape(1, LANES)
    wr_hi = wr.astype(BF16)
    wr_lo = (wr - wr_hi.astype(F32)).astype(BF16)
    x1, h2, logits = _mix(ypre, yret.reshape(t, d - sw), x2d, mod3, w_glu.astype(BF16),
                          beta_ssm, w_out.astype(BF16), g_norm2, wr_hi, wr_lo, br, seq)

    nb = -(-(t * 2) // MOE_BLOCK) + N_EXPERTS
    meta, w_tok, cnt = _route(logits)
    dest, be = _dest(meta, cnt, nb)
    slot_row = _invert(dest[:, :2].T.reshape(-1), nb * MOE_BLOCK, t)
    block_e = be[:nb + 1, 0]

    hmid = _moe_up(slot_row, block_e, h2, w_gate, w_up, nb)
    ys = _moe_down(slot_row, block_e, hmid, w_down, nb, t)
    return x1, ys, w_tok, mod3


def kernel(x, c, w_ada, b_ada, g_norm1, w_in, ssm_a_re, ssm_a_im, ssm_b_re, ssm_b_im, ssm_c_re,
           ssm_c_im, ssm_d, ssm_log_dt, w_glu, beta_ssm, beta_ret, w_out, g_norm2, w_router_group,
           b_router_group, w_router_expert, b_router_expert, w_gate, w_up, w_down, g_final):
    depth = w_ada.shape[0]
    bn, seq, d = x.shape
    assert depth == 1, "the final norm is fused into the single layer's last kernel"
    l = 0
    mod = _ada(c, w_ada[l], b_ada[l])
    x1, ys, w_tok, mod3 = _layer(
        x, mod, g_norm1[l], w_in[l], ssm_a_re[l], ssm_a_im[l], ssm_b_re[l], ssm_b_im[l],
        ssm_c_re[l], ssm_c_im[l], ssm_d[l], ssm_log_dt[l], w_glu[l], beta_ssm[l], beta_ret[l],
        w_out[l], g_norm2[l], w_router_group[l], b_router_group[l], w_router_expert[l],
        b_router_expert[l], w_gate[l], w_up[l], w_down[l])
    out = _final(x1, ys, w_tok, mod3, g_final, seq)
    return out.reshape(bn, seq, d)
```

```python
import functools
import math

import jax
import jax.numpy as jnp
from jax import lax
from jax.experimental import pallas as pl
from jax.experimental.pallas import tpu as pltpu

F32 = jnp.float32
BF16 = jnp.bfloat16
I32 = jnp.int32

SSM_GROUP = 16
SSM_STATE = 64
RET_HEAD_DIM = 128
ROPE_BASE = 10000.0
N_GROUPS = 4
EXPERTS_PER_GROUP = 8
N_EXPERTS = N_GROUPS * EXPERTS_PER_GROUP
N_MOD = 6
NORM_EPS = 1e-6
GN_EPS = 1e-5

LANES = 128
SUBLANES = 8
VMEM_LIMIT = 56 * 1024 * 1024

ROUTE_LANE0 = N_GROUPS
MOE_BLOCK = 256
MOE_NCHUNK_COLS = 256
MOE_GATHER_DEPTH = 4
MOE_SCATTER_DEPTH = 3
MOE_SPARE_SETS = 4
RET_CHUNK = 256
S5_TT = 128
S5_BLOCKS_PER_STEP = 4


def _cparams(sem):
    return pltpu.CompilerParams(dimension_semantics=sem, vmem_limit_bytes=VMEM_LIMIT)


def _silu(x):
    return x * jax.nn.sigmoid(x)


def _pack_bf16_pairs(lo, hi):
    lo_w = pltpu.bitcast(lo.astype(BF16).astype(F32), jnp.uint32)
    hi_w = pltpu.bitcast(hi.astype(BF16).astype(F32), jnp.uint32)
    return (lo_w >> 16) | (hi_w & jnp.uint32(0xFFFF0000))


def _unpack_bf16_pairs(w):
    return (pltpu.bitcast(w << 16, F32), pltpu.bitcast(w & jnp.uint32(0xFFFF0000), F32))


def _ada_kernel(c_ref, w_ref, b_ref, o_ref):
    s = _silu(c_ref[...])
    o_ref[...] = jnp.dot(s.astype(BF16), w_ref[...].astype(BF16),
                         preferred_element_type=F32) + b_ref[...]


def _ada(c, w, b, tn=1024):
    bn, d = c.shape
    n = w.shape[1]
    return pl.pallas_call(
        _ada_kernel,
        grid=(n // tn,),
        in_specs=[pl.BlockSpec((bn, d), lambda j: (0, 0)),
                  pl.BlockSpec((d, tn), lambda j: (0, j)),
                  pl.BlockSpec((1, tn), lambda j: (0, j))],
        out_specs=pl.BlockSpec((bn, tn), lambda j: (0, j)),
        out_shape=jax.ShapeDtypeStruct((bn, n), F32),
        compiler_params=_cparams(("arbitrary",)),
        name="ada_mod",
    )(c, w, b.reshape(1, n))


def _inproj_kernel(x_ref, mod_ref, g_ref, w_ref, o_ref, h_scr):
    @pl.when(pl.program_id(1) == 0)
    def _():
        x = x_ref[...]
        y = x * lax.rsqrt(jnp.mean(x * x, axis=-1, keepdims=True) + NORM_EPS) * g_ref[...]
        m = mod_ref[0]
        h_scr[...] = (y * (1.0 + m[1:2, :]) + m[0:1, :]).astype(BF16)

    o_ref[...] = jnp.dot(h_scr[...], w_ref[...], preferred_element_type=F32).astype(BF16)


def _inproj(x2d, mod3, g1, w_in_bf, seq, tm=1024, tn=1024):
    t, d = x2d.shape
    n = w_in_bf.shape[1]
    tm = min(tm, seq)
    tn = math.gcd(tn, n)
    tpb = seq // tm
    return pl.pallas_call(
        _inproj_kernel,
        grid=(t // tm, n // tn),
        in_specs=[pl.BlockSpec((tm, d), lambda i, j: (i, 0)),
                  pl.BlockSpec((1, N_MOD, d), lambda i, j: (i // tpb, 0, 0)),
                  pl.BlockSpec((1, d), lambda i, j: (0, 0)),
                  pl.BlockSpec((d, tn), lambda i, j: (0, j))],
        out_specs=pl.BlockSpec((tm, tn), lambda i, j: (i, j)),
        out_shape=jax.ShapeDtypeStruct((t, n), BF16),
        scratch_shapes=[pltpu.VMEM((tm, d), BF16)],
        compiler_params=_cparams(("arbitrary", "arbitrary")),
        name="norm_inproj",
    )(x2d, mod3, g1.reshape(1, d), w_in_bf)


def _s5_disc_kernel(are_ref, aim_ref, ldt_ref, bre_ref, bim_ref,
                    abre_ref, abim_ref, bbre_ref, bbim_ref):
    a_re = are_ref[...]
    a_im = aim_ref[...]
    dt = jnp.exp(ldt_ref[...])
    mag = jnp.exp(dt * a_re)
    ab_re = mag * jnp.cos(dt * a_im)
    ab_im = mag * jnp.sin(dt * a_im)
    inv_abs2 = 1.0 / (a_re * a_re + a_im * a_im)
    n_re = ab_re - 1.0
    f_re = (n_re * a_re + ab_im * a_im) * inv_abs2
    f_im = (ab_im * a_re - n_re * a_im) * inv_abs2
    abre_ref[...] = ab_re
    abim_ref[...] = ab_im
    b_re = bre_ref[...]
    b_im = bim_ref[...]
    fr = f_re[:, None, :]
    fi = f_im[:, None, :]
    bbre_ref[...] = fr * b_re - fi * b_im
    bbim_ref[...] = fr * b_im + fi * b_re


def _s5_disc(a_re, a_im, log_dt, bt_re, bt_im):
    g, p = a_re.shape
    h = bt_re.shape[1]
    return pl.pallas_call(
        _s5_disc_kernel,
        out_shape=[jax.ShapeDtypeStruct((g, p), F32), jax.ShapeDtypeStruct((g, p), F32),
                   jax.ShapeDtypeStruct((g, h, p), F32), jax.ShapeDtypeStruct((g, h, p), F32)],
        name="s5_discretise",
    )(a_re, a_im, log_dt.reshape(g, 1), bt_re, bt_im)


def _s5_pack(ab_re, ab_im, bbt_re, bbt_im, c_re, c_im):
    g, h, p = bbt_re.shape
    npair = g // 2
    ppb = LANES // (2 * h)
    eye2 = jnp.eye(2, dtype=F32)
    qsel = jax.nn.one_hot(jnp.arange(npair) % ppb, ppb, dtype=F32)

    def in_mat(bbt):
        b4 = bbt.reshape(npair, 2, h, p)
        return jnp.einsum('ab,xahp->xahbp', eye2, b4).reshape(npair, 2 * h, 2 * p)

    bsmall = jnp.concatenate([in_mat(bbt_re), in_mat(bbt_im)], axis=-1)
    bz = jnp.einsum('xq,xrc->xqrc', qsel, bsmall).reshape(npair, LANES, 4 * p)

    def out_mat(c):
        c4 = jnp.swapaxes(c, 1, 2).reshape(npair, 2, p, h)
        return jnp.einsum('ab,xaph->xapbh', eye2, c4).reshape(npair, 2 * p, 2 * h)

    csmall = jnp.concatenate([out_mat(c_re), -out_mat(c_im)], axis=1)
    cz = jnp.einsum('xq,xrc->xrqc', qsel, csmall).reshape(npair, 4 * p, LANES)
    a_re = ab_re.reshape(npair // ppb, ppb, 2 * p)
    a_im = ab_im.reshape(npair // ppb, ppb, 2 * p)
    return bz.astype(BF16), cz.astype(BF16), a_re, a_im


def _s5_kernel(u_ref, bz_ref, cz_ref, are_ref, aim_ref, d_ref, o_ref, state, *bufs, ppb):
    t = pl.program_id(0)
    j = pl.program_id(1)
    bn, tt, width = u_ref.shape
    nblk = width // LANES
    rt = bn * tt

    @pl.when(t == 0)
    def _():
        for i in range(nblk):
            state[j * nblk + i] = jnp.zeros(state.shape[1:], F32)

    for i in range(nblk):
        buf = bufs[i]
        kb = j * nblk + i
        lanes = slice(i * LANES, (i + 1) * LANES)

        u = u_ref[:, :, lanes].reshape(rt, LANES)
        for q in range(ppb):
            bu = jnp.dot(u, bz_ref[i * ppb + q], preferred_element_type=F32)
            for b in range(bn):
                rows = pl.ds(b, tt, stride=bn)
                buf[2 * q, rows, :] = bu[b * tt:(b + 1) * tt, :LANES]
                buf[2 * q + 1, rows, :] = bu[b * tt:(b + 1) * tt, LANES:]

        a_re = [jnp.broadcast_to(are_ref[i, q:q + 1, :], (bn, LANES)) for q in range(ppb)]
        a_im = [jnp.broadcast_to(aim_ref[i, q:q + 1, :], (bn, LANES)) for q in range(ppb)]
        s = [state[kb, k] for k in range(2 * ppb)]
        for step in range(tt):
            rows = slice(step * bn, (step + 1) * bn)
            for q in range(ppb):
                s_re, s_im = s[2 * q], s[2 * q + 1]
                n_re = a_re[q] * s_re - a_im[q] * s_im + buf[2 * q, rows, :]
                n_im = a_re[q] * s_im + a_im[q] * s_re + buf[2 * q + 1, rows, :]
                buf[2 * q, rows, :] = n_re
                buf[2 * q + 1, rows, :] = n_im
                s[2 * q], s[2 * q + 1] = n_re, n_im
        for k in range(2 * ppb):
            state[kb, k] = s[k]

        acc = None
        for q in range(ppb):
            lhs = jnp.concatenate([buf[2 * q], buf[2 * q + 1]], axis=1).astype(BF16)
            part = jnp.dot(lhs, cz_ref[i * ppb + q], preferred_element_type=F32)
            acc = part if acc is None else acc + part
        buf[2 * ppb] = acc
        d_skip = d_ref[:, lanes]
        for b in range(bn):
            o_ref[b, :, lanes] = (buf[2 * ppb, pl.ds(b, tt, stride=bn), :]
                                  + d_skip * u_ref[b, :, lanes].astype(F32))


def _s5_scan(u3, bz, cz, a_re, a_im, d_skip, tt=S5_TT, nblk=S5_BLOCKS_PER_STEP):
    bn, seq, width = u3.shape[0], u3.shape[1], d_skip.shape[0]
    nkb = width // LANES
    ppb = bz.shape[0] // nkb
    nblk = math.gcd(nblk, nkb)
    bw = nblk * LANES
    kern = functools.partial(_s5_kernel, ppb=ppb)
    return pl.pallas_call(
        kern,
        grid=(seq // tt, nkb // nblk),
        in_specs=[pl.BlockSpec((bn, tt, bw), lambda t, k: (0, t, k)),
                  pl.BlockSpec((nblk * ppb,) + bz.shape[1:], lambda t, k: (k, 0, 0)),
                  pl.BlockSpec((nblk * ppb,) + cz.shape[1:], lambda t, k: (k, 0, 0)),
                  pl.BlockSpec((nblk,) + a_re.shape[1:], lambda t, k: (k, 0, 0)),
                  pl.BlockSpec((nblk,) + a_im.shape[1:], lambda t, k: (k, 0, 0)),
                  pl.BlockSpec((1, bw), lambda t, k: (0, k))],
        out_specs=pl.BlockSpec((bn, tt, bw), lambda t, k: (0, t, k)),
        out_shape=jax.ShapeDtypeStruct((bn, seq, width), F32),
        scratch_shapes=[pltpu.VMEM((nkb, 2 * ppb, bn, LANES), F32)]
        + [pltpu.VMEM((2 * ppb + 1, bn * tt, LANES), F32) for _ in range(nblk)],
        compiler_params=_cparams(("arbitrary", "arbitrary")),
        name="s5_scan",
    )(u3, bz, cz, a_re, a_im, d_skip.reshape(1, width))


def _ret_kernel(q_ref, k_ref, v_ref, g_ref, cq_ref, sq_ref, ck_ref, sk_ref, lg_ref, beta_ref, o_ref,
                q_scr, k_scr, kzt_scr, y_scr, *, chunk):
    seq, dh = q_ref.shape[1], q_ref.shape[2]
    lg = lg_ref[0][:, :1]
    ri = lax.broadcasted_iota(I32, (chunk, chunk), 0)
    ci = lax.broadcasted_iota(I32, (chunk, chunk), 1)
    rel = (ri - ci).astype(F32)
    decay = jnp.where(rel >= 0, jnp.exp(lg * jnp.maximum(rel, 0.0)), 0.0)
    idx = lax.broadcasted_iota(I32, (chunk, 1), 0).astype(F32)
    xi = jnp.exp(lg * (idx + 1.0))
    g_chunk = jnp.exp(lg * float(chunk))
    beta = beta_ref[...]
    half = dh // 2
    pr = lax.broadcasted_iota(I32, (dh, dh), 0)
    pc = lax.broadcasted_iota(I32, (dh, dh), 1)
    swap = jnp.where(((pr + half) & (dh - 1)) == pc, 1.0, 0.0).astype(BF16)

    def rope(x_ref, cos_ref, sin_ref, sl):
        xb = x_ref[0, sl, :]
        rolled = jnp.dot(xb, swap, preferred_element_type=F32).astype(BF16)
        return xb * cos_ref[sl, :] + rolled * sin_ref[sl, :]

    zeta = jnp.exp(lg * (chunk - 1.0 - idx))
    for n in range(seq // chunk):
        sl = slice(n * chunk, (n + 1) * chunk)
        q_scr[sl, :] = rope(q_ref, cq_ref, sq_ref, sl)
        kb = rope(k_ref, ck_ref, sk_ref, sl)
        k_scr[sl, :] = kb
        kzt_scr[:, sl] = (kb.astype(F32) * zeta).T.astype(BF16)

    decay = decay.astype(BF16)
    r = jnp.zeros((dh, dh), F32)
    for n in range(seq // chunk):
        sl = slice(n * chunk, (n + 1) * chunk)
        qb = q_scr[sl, :]
        v = v_ref[0, sl, :]
        s = lax.dot_general(qb, k_scr[sl, :], (((1,), (1,)), ((), ())),
                            preferred_element_type=F32).astype(BF16) * decay
        y = jnp.dot(s, v, preferred_element_type=F32)
        y_scr[sl, :] = y + jnp.dot(qb, r.astype(BF16), preferred_element_type=F32) * xi
        r = r * g_chunk + jnp.dot(kzt_scr[:, sl], v, preferred_element_type=F32)

    y = y_scr[...]
    mu = jnp.mean(y, axis=-1, keepdims=True)
    yc = y - mu
    var = jnp.mean(yc * yc, axis=-1, keepdims=True)
    yn = yc * lax.rsqrt(var + GN_EPS) * beta
    o_ref[0] = (_silu(g_ref[0].astype(F32)) * yn).astype(BF16)


def _retention(proj3, col0, cosf, sinf, log_g, beta_ret, chunk=RET_CHUNK):
    bn, seq, _ = proj3.shape
    dh = RET_HEAD_DIM
    heads = beta_ret.shape[0] // dh
    blk = lambda off: pl.BlockSpec((1, seq, dh), lambda b, h, off=off: (b, 0, col0 + off + h))
    k_scale = dh ** -0.5
    kern = functools.partial(_ret_kernel, chunk=chunk)
    return pl.pallas_call(
        kern,
        grid=(bn, heads),
        in_specs=[blk(0), blk(heads), blk(2 * heads), blk(3 * heads),
                  pl.BlockSpec((seq, dh), lambda b, h: (0, 0)),
                  pl.BlockSpec((seq, dh), lambda b, h: (0, 0)),
                  pl.BlockSpec((seq, dh), lambda b, h: (0, 0)),
                  pl.BlockSpec((seq, dh), lambda b, h: (0, 0)),
                  pl.BlockSpec((1, 1, LANES), lambda b, h: (h, 0, 0)),
                  pl.BlockSpec((1, dh), lambda b, h: (0, h))],
        out_specs=pl.BlockSpec((1, seq, dh), lambda b, h: (b, 0, h)),
        out_shape=jax.ShapeDtypeStruct((bn, seq, heads * dh), BF16),
        scratch_shapes=[pltpu.VMEM((seq, dh), BF16), pltpu.VMEM((seq, dh), BF16),
                        pltpu.VMEM((dh, seq), BF16), pltpu.VMEM((seq, dh), F32)],
        compiler_params=_cparams(("arbitrary", "arbitrary")),
        name="retention",
    )(proj3, proj3, proj3, proj3, cosf.astype(BF16), sinf.astype(BF16),
      (cosf * k_scale).astype(BF16), (sinf * k_scale).astype(BF16),
      log_g, beta_ret.reshape(1, heads * dh))


def _gelu_tanh(x):
    return 0.5 * x * (1.0 + jnp.tanh(math.sqrt(2.0 / math.pi) * (x + 0.044715 * (x * x * x))))


def _split_bf16(x):
    hi = x.astype(BF16)
    lo = (x - hi.astype(F32)).astype(BF16)
    return hi, lo


def _mix_kernel(ys_ref, yr_ref, x_ref, mod_ref, wglu_ref, bssm_ref, wout_ref, g2_ref,
                wrh_ref, wrl_ref, br_ref, x1_ref, h2_ref, lg_ref):
    sw = ys_ref.shape[1]
    z = _gelu_tanh(ys_ref[...])
    gate = jax.nn.sigmoid(jnp.dot(z.astype(BF16), wglu_ref[...], preferred_element_type=F32))
    o = z * gate
    y_ssm = o * lax.rsqrt(jnp.mean(o * o, axis=-1, keepdims=True) + NORM_EPS) * bssm_ref[...]
    mixed = (jnp.dot(y_ssm.astype(BF16), wout_ref[:sw, :], preferred_element_type=F32)
             + jnp.dot(yr_ref[...], wout_ref[sw:, :], preferred_element_type=F32))
    m = mod_ref[0]
    x1 = x_ref[...] + m[2:3, :] * mixed
    x1_ref[...] = x1
    y = x1 * lax.rsqrt(jnp.mean(x1 * x1, axis=-1, keepdims=True) + NORM_EPS) * g2_ref[...]
    h2 = y * (1.0 + m[4:5, :]) + m[3:4, :]
    half = h2.shape[1] // 2
    h2_ref[...] = _pack_bf16_pairs(h2[:, :half], h2[:, half:])
    hi, lo = _split_bf16(h2)
    wrh = wrh_ref[...]
    lg_ref[...] = (jnp.dot(hi, wrh, preferred_element_type=F32)
                   + jnp.dot(lo, wrh, preferred_element_type=F32)
                   + jnp.dot(hi, wrl_ref[...], preferred_element_type=F32)
                   + br_ref[...])


def _mix(ypre, yret, x2d, mod3, wglu_bf, beta_ssm, wout_bf, g2, wr_hi, wr_lo, br, seq, tm=256):
    t, d = x2d.shape
    sw = wglu_bf.shape[0]
    tpb = seq // tm
    const = lambda shape: pl.BlockSpec(shape, lambda i: (0,) * len(shape),
                                       pipeline_mode=pl.Buffered(1))
    return pl.pallas_call(
        _mix_kernel,
        grid=(t // tm,),
        in_specs=[pl.BlockSpec((tm, sw), lambda i: (i, 0)),
                  pl.BlockSpec((tm, d - sw), lambda i: (i, 0)),
                  pl.BlockSpec((tm, d), lambda i: (i, 0)),
                  pl.BlockSpec((1, N_MOD, d), lambda i: (i // tpb, 0, 0)),
                  const((sw, sw)), const((1, sw)), const((d, d)), const((1, d)),
                  const((d, LANES)), const((d, LANES)), const((1, LANES))],
        out_specs=[pl.BlockSpec((tm, d), lambda i: (i, 0)),
                   pl.BlockSpec((tm, d // 2), lambda i: (i, 0)),
                   pl.BlockSpec((tm, LANES), lambda i: (i, 0))],
        out_shape=[jax.ShapeDtypeStruct((t, d), F32),
                   jax.ShapeDtypeStruct((t, d // 2), jnp.uint32),
                   jax.ShapeDtypeStruct((t, LANES), F32)],
        compiler_params=_cparams(("arbitrary",)),
        name="mix_outproj_router",
    )(ypre, yret, x2d, mod3, wglu_bf, beta_ssm.reshape(1, sw), wout_bf, g2.reshape(1, d),
      wr_hi, wr_lo, br)


def _route_kernel(lg_ref, meta_ref, w_ref, cnt_ref, carry):
    i = pl.program_id(0)
    tm = lg_ref.shape[0]

    @pl.when(i == 0)
    def _():
        carry[...] = jnp.zeros(carry.shape, F32)

    lg = lg_ref[...]
    lane = lax.broadcasted_iota(I32, (tm, LANES), 1).astype(F32)
    neg = jnp.float32(-jnp.inf)

    def first_max(vals):
        m = jnp.max(vals, axis=-1, keepdims=True)
        idx = jnp.min(jnp.where(vals == m, lane, float(LANES)), axis=-1, keepdims=True)
        return m, idx

    gl = jnp.where(lane < N_GROUPS, lg, neg)
    gmax, gsel = first_max(gl)
    g_w = 1.0 / jnp.sum(jnp.exp(gl - gmax), axis=-1, keepdims=True)
    lo = ROUTE_LANE0 + gsel * EXPERTS_PER_GROUP
    el = jnp.where((lane >= lo) & (lane < lo + EXPERTS_PER_GROUP), lg, neg)
    m1, i1 = first_max(el)
    m2, i2 = first_max(jnp.where(lane == i1, neg, el))
    e21 = jnp.exp(m2 - m1)
    w1 = g_w / (1.0 + e21)
    w2 = g_w * e21 / (1.0 + e21)

    sel1 = lane == i1
    sel2 = lane == i2
    onehot = jnp.where(sel1 | sel2, 1.0, 0.0).astype(BF16)
    ri = lax.broadcasted_iota(I32, (tm, tm), 0)
    ci = lax.broadcasted_iota(I32, (tm, tm), 1)
    tri = jnp.where(ci < ri, 1.0, 0.0).astype(BF16)
    base = carry[0:1, :]
    excl = jnp.dot(tri, onehot, preferred_element_type=F32) + base
    r1 = jnp.sum(jnp.where(sel1, excl, 0.0), axis=-1, keepdims=True)
    r2 = jnp.sum(jnp.where(sel2, excl, 0.0), axis=-1, keepdims=True)
    total = base + jnp.sum(jnp.where(sel1 | sel2, 1.0, 0.0), axis=0, keepdims=True)
    carry[...] = jnp.broadcast_to(total, carry.shape)
    cnt_ref[...] = jnp.broadcast_to(total, cnt_ref.shape)

    meta_ref[...] = (jnp.where(lane == 0, i1, 0.0) + jnp.where(lane == 1, i2, 0.0)
                     + jnp.where(lane == 2, r1, 0.0) + jnp.where(lane == 3, r2, 0.0))
    w_ref[...] = jnp.where(lane == 0, w1, 0.0) + jnp.where(lane == 1, w2, 0.0)


def _route(logits, tm=512):
    t = logits.shape[0]
    return pl.pallas_call(
        _route_kernel,
        grid=(t // tm,),
        in_specs=[pl.BlockSpec((tm, LANES), lambda i: (i, 0))],
        out_specs=[pl.BlockSpec((tm, LANES), lambda i: (i, 0)),
                   pl.BlockSpec((tm, LANES), lambda i: (i, 0)),
                   pl.BlockSpec((SUBLANES, LANES), lambda i: (0, 0))],
        out_shape=[jax.ShapeDtypeStruct((t, LANES), F32),
                   jax.ShapeDtypeStruct((t, LANES), F32),
                   jax.ShapeDtypeStruct((SUBLANES, LANES), F32)],
        scratch_shapes=[pltpu.VMEM((SUBLANES, LANES), F32)],
        compiler_params=_cparams(("arbitrary",)),
        name="route_topk",
    )(logits)


def _lane_cumsum(x):
    lane = lax.broadcasted_iota(I32, x.shape, 1)
    sh = 1
    while sh < LANES:
        x = x + jnp.where(lane >= sh, pltpu.roll(x, sh, axis=1), 0)
        sh *= 2
    return x


def _dest_kernel(meta_ref, cnt_ref, dest_ref, be_ref, *, nb):
    tm = meta_ref.shape[0]
    shift = MOE_BLOCK.bit_length() - 1
    cnt = cnt_ref[...].astype(I32)
    padded = ((cnt + (MOE_BLOCK - 1)) >> shift) << shift
    pend_i = _lane_cumsum(padded)
    pend = pend_i.astype(F32)[0:1, :]
    poff = (pend_i - padded).astype(F32)[0:1, :]
    meta = meta_ref[...]
    lane = lax.broadcasted_iota(I32, (tm, LANES), 1).astype(F32)
    pick = lambda col: jnp.sum(jnp.where(lane == col, meta, 0.0), axis=-1, keepdims=True)
    i1, i2, r1, r2 = pick(0), pick(1), pick(2), pick(3)
    d1 = jnp.sum(jnp.where(lane == i1, poff, 0.0), axis=-1, keepdims=True) + r1
    d2 = jnp.sum(jnp.where(lane == i2, poff, 0.0), axis=-1, keepdims=True) + r2
    dest_ref[...] = (jnp.where(lane == 0, d1, 0.0) + jnp.where(lane == 1, d2, 0.0)).astype(I32)

    nrow = be_ref.shape[0]
    blk = lax.broadcasted_iota(I32, (nrow, LANES), 0).astype(F32)
    lane_b = lax.broadcasted_iota(I32, (nrow, LANES), 1).astype(F32)
    is_e = (lane_b >= ROUTE_LANE0) & (lane_b < ROUTE_LANE0 + N_EXPERTS)
    below = jnp.sum(jnp.where(is_e & (pend <= blk * MOE_BLOCK), 1.0, 0.0), axis=-1, keepdims=True)
    block_e = jnp.minimum(below, N_EXPERTS - 1.0)
    used = jnp.sum(jnp.where(lane_b == ROUTE_LANE0 + N_EXPERTS - 1, pend, 0.0),
                   axis=-1, keepdims=True) * (1.0 / MOE_BLOCK)
    be = jnp.where(blk[:, :1] == nb, used, block_e)
    be_ref[...] = jnp.broadcast_to(be, (nrow, LANES)).astype(I32)


def _dest(meta, cnt, nb, tm=512):
    t = meta.shape[0]
    nrow = -(-(nb + 1) // SUBLANES) * SUBLANES
    kern = functools.partial(_dest_kernel, nb=nb)
    return pl.pallas_call(
        kern,
        grid=(t // tm,),
        in_specs=[pl.BlockSpec((tm, LANES), lambda i: (i, 0)),
                  pl.BlockSpec((SUBLANES, LANES), lambda i: (0, 0))],
        out_specs=[pl.BlockSpec((tm, LANES), lambda i: (i, 0)),
                   pl.BlockSpec((nrow, LANES), lambda i: (0, 0))],
        out_shape=[jax.ShapeDtypeStruct((t, LANES), I32),
                   jax.ShapeDtypeStruct((nrow, LANES), I32)],
        compiler_params=_cparams(("arbitrary",)),
        name="route_dest",
    )(meta, cnt)


def _invert_kernel(dest_ref, fill_hbm, row_ref, sem):
    cp = pltpu.make_async_copy(fill_hbm, row_ref, sem)
    cp.start()
    cp.wait()

    def put(a, c):
        row_ref[dest_ref[a]] = a
        return c

    lax.fori_loop(0, dest_ref.shape[0], put, 0, unroll=8)


def _invert(dest_flat, n_slot, t):
    spare = MOE_SPARE_SETS * MOE_BLOCK
    fill = 2 * t + (jnp.arange(n_slot, dtype=I32) & (spare - 1))
    return pl.pallas_call(
        _invert_kernel,
        in_specs=[pl.BlockSpec(memory_space=pltpu.SMEM), pl.BlockSpec(memory_space=pl.ANY)],
        out_specs=pl.BlockSpec(memory_space=pltpu.SMEM),
        out_shape=jax.ShapeDtypeStruct((n_slot,), I32),
        scratch_shapes=[pltpu.SemaphoreType.DMA(())],
        name="route_invert",
    )(dest_flat, fill)


def _expert_weights(b, used, be_ref, w_hbms, wst, wsem, run_ref, load):
    def fetch(e, p):
        for i, w in enumerate(w_hbms):
            pltpu.make_async_copy(w.at[e], wst.at[p, i], wsem.at[p]).start(priority=1)

    @pl.when((b == 0) & (used > 0))
    def _():
        run_ref[0] = 0
        fetch(be_ref[0], 0)

    bc = jnp.minimum(b, be_ref.shape[0] - 2)
    new_expert = (b == 0) | (be_ref[bc] != be_ref[jnp.maximum(bc - 1, 0)])

    @pl.when(new_expert & (b < used))
    def _():
        p = run_ref[0]
        for i, w in enumerate(w_hbms):
            pltpu.make_async_copy(w.at[0], wst.at[p, i], wsem.at[p]).wait()
        load(p)
        e = be_ref[b]
        nxt = lax.while_loop(lambda i: (i < used) & (be_ref[i] == e), lambda i: i + 1, b + 1)

        @pl.when(nxt < used)
        def _():
            fetch(be_ref[nxt], 1 - p)

        run_ref[0] = 1 - p


def _moe_up_kernel(row_ref, be_ref, h2_hbm, wg_hbm, wu_hbm, o_ref, wst, wbuf, xbuf, xb_scr, sem,
                   wsem, run_ref, *, nb):
    b = pl.program_id(0)
    used = be_ref[nb]
    blk = MOE_BLOCK
    fe = wg_hbm.shape[2]
    t = h2_hbm.shape[0]

    def gather_rows(block, rows):
        slot = block % MOE_GATHER_DEPTH
        base = block * blk
        for r in rows:
            row = row_ref[base + r]
            tok = jnp.where(row >= t, row - t, row)
            tok = jnp.where(tok >= t, 0, tok)
            pltpu.make_async_copy(h2_hbm.at[pl.ds(tok, 1)], xbuf.at[slot, pl.ds(r, 1)],
                                  sem.at[slot]).start()

    ahead = MOE_GATHER_DEPTH - 1

    @pl.when(b == 0)
    def _():
        for first in range(ahead):
            @pl.when(first < used)
            def _():
                gather_rows(first, range(blk))

    def load(p):
        wbuf[:, :fe] = wst[p, 0].astype(BF16)
        wbuf[:, fe:] = wst[p, 1].astype(BF16)

    _expert_weights(b, used, be_ref, (wg_hbm, wu_hbm), wst, wsem, run_ref, load)

    nchunk = fe // MOE_NCHUNK_COLS
    rows_per = blk // nchunk

    def step(prefetch):
        slot = b % MOE_GATHER_DEPTH
        pltpu.make_async_copy(xbuf.at[slot], xbuf.at[slot], sem.at[slot]).wait()
        lo, hi = _unpack_bf16_pairs(xbuf[slot])
        half = lo.shape[1]
        xb_scr[:, :half] = lo.astype(BF16)
        xb_scr[:, half:] = hi.astype(BF16)
        for c in range(nchunk):
            if prefetch:
                gather_rows(b + ahead, range(c * rows_per, (c + 1) * rows_per))
            cols = slice(c * MOE_NCHUNK_COLS, (c + 1) * MOE_NCHUNK_COLS)
            ucols = slice(fe + c * MOE_NCHUNK_COLS, fe + (c + 1) * MOE_NCHUNK_COLS)
            g = jnp.dot(xb_scr[...], wbuf[:, cols], preferred_element_type=F32)
            u = jnp.dot(xb_scr[...], wbuf[:, ucols], preferred_element_type=F32)
            o_ref[:, cols] = (_silu(g) * u).astype(BF16)

    @pl.when(b + ahead < used)
    def _():
        step(True)

    @pl.when((b < used) & (b + ahead >= used))
    def _():
        step(False)

    @pl.when(b >= used)
    def _():
        o_ref[...] = jnp.zeros(o_ref.shape, BF16)


def _moe_up(slot_row, block_e, h2, w_gate, w_up, nb):
    d, fe = w_gate.shape[1], w_gate.shape[2]
    kern = functools.partial(_moe_up_kernel, nb=nb)
    grid_spec = pltpu.PrefetchScalarGridSpec(
        num_scalar_prefetch=2,
        grid=(nb,),
        in_specs=[pl.BlockSpec(memory_space=pl.ANY),
                  pl.BlockSpec(memory_space=pl.ANY),
                  pl.BlockSpec(memory_space=pl.ANY)],
        out_specs=pl.BlockSpec((MOE_BLOCK, fe), lambda b, s, e: (b, 0)),
        scratch_shapes=[pltpu.VMEM((2, 2, d, fe), F32),
                        pltpu.VMEM((d, 2 * fe), BF16),
                        pltpu.VMEM((MOE_GATHER_DEPTH, MOE_BLOCK, d // 2), jnp.uint32),
                        pltpu.VMEM((MOE_BLOCK, d), BF16),
                        pltpu.SemaphoreType.DMA((MOE_GATHER_DEPTH,)),
                        pltpu.SemaphoreType.DMA((2,)),
                        pltpu.SMEM((1,), I32)],
    )
    return pl.pallas_call(
        kern,
        grid_spec=grid_spec,
        out_shape=jax.ShapeDtypeStruct((nb * MOE_BLOCK, fe), BF16),
        compiler_params=_cparams(("arbitrary",)),
        name="moe_up",
    )(slot_row, block_e, h2, w_gate, w_up)


def _moe_down_kernel(row_ref, be_ref, hm_ref, wd_hbm, ys_hbm, wst, wbuf, obuf, sem, wsem, run_ref,
                     *, nb, t):
    b = pl.program_id(0)
    used = be_ref[nb]
    blk = MOE_BLOCK

    def wait_slot(slot):
        pltpu.make_async_copy(obuf.at[slot], obuf.at[slot], sem.at[slot]).wait()

    depth = obuf.shape[0]

    @pl.when(b == 0)
    def _():
        obuf[0] = jnp.zeros(obuf.shape[1:], jnp.uint32)
        for k in range(MOE_SPARE_SETS):
            spare = ys_hbm.at[pl.ds(2 * t + k * blk, blk)]
            pltpu.make_async_copy(obuf.at[0], spare, sem.at[depth]).start()
        for k in range(MOE_SPARE_SETS):
            spare = ys_hbm.at[pl.ds(2 * t + k * blk, blk)]
            pltpu.make_async_copy(obuf.at[0], spare, sem.at[depth]).wait()

    @pl.when((b >= depth) & (b - depth < used))
    def _():
        wait_slot(b % depth)

    def load(p):
        wbuf[...] = wst[p, 0].astype(BF16)

    _expert_weights(b, used, be_ref, (wd_hbm,), wst, wsem, run_ref, load)

    half = wbuf.shape[1] // 2
    nchunk = half // MOE_NCHUNK_COLS
    rows_per = blk // nchunk

    def step(scatter, matmul):
        slot = (b - 1) % depth
        base = (b - 1) * blk
        for c in range(nchunk):
            if scatter:
                for r in range(c * rows_per, (c + 1) * rows_per):
                    pltpu.make_async_copy(obuf.at[slot, pl.ds(r, 1)],
                                          ys_hbm.at[pl.ds(row_ref[base + r], 1)],
                                          sem.at[slot]).start(priority=r % 2)
            if matmul:
                cols = slice(c * MOE_NCHUNK_COLS, (c + 1) * MOE_NCHUNK_COLS)
                hcols = slice(half + c * MOE_NCHUNK_COLS, half + (c + 1) * MOE_NCHUNK_COLS)
                lo = jnp.dot(hm_ref[...], wbuf[:, cols], preferred_element_type=F32)
                hi = jnp.dot(hm_ref[...], wbuf[:, hcols], preferred_element_type=F32)
                obuf[b % depth, :, cols] = _pack_bf16_pairs(lo, hi)

    @pl.when((b >= 1) & (b < used))
    def _():
        step(True, True)

    @pl.when((b == 0) & (b < used))
    def _():
        step(False, True)

    @pl.when((b >= 1) & (b == used))
    def _():
        step(True, False)

    @pl.when(b == nb)
    def _():
        for k in range(1, depth):
            @pl.when(nb - k < used)
            def _():
                wait_slot((nb - k) % depth)


def _moe_down(slot_row, block_e, hmid, w_down, nb, t):
    fe, d = w_down.shape[1], w_down.shape[2]
    kern = functools.partial(_moe_down_kernel, nb=nb, t=t)
    grid_spec = pltpu.PrefetchScalarGridSpec(
        num_scalar_prefetch=2,
        grid=(nb + 1,),
        in_specs=[pl.BlockSpec((MOE_BLOCK, fe), lambda b, s, e: (jnp.minimum(b, nb - 1), 0)),
                  pl.BlockSpec(memory_space=pl.ANY)],
        out_specs=pl.BlockSpec(memory_space=pl.ANY),
        scratch_shapes=[pltpu.VMEM((2, 1, fe, d), F32),
                        pltpu.VMEM((fe, d), BF16),
                        pltpu.VMEM((MOE_SCATTER_DEPTH, MOE_BLOCK, d // 2), jnp.uint32),
                        pltpu.SemaphoreType.DMA((MOE_SCATTER_DEPTH + 1,)),
                        pltpu.SemaphoreType.DMA((2,)),
                        pltpu.SMEM((1,), I32)],
    )
    return pl.pallas_call(
        kern,
        grid_spec=grid_spec,
        out_shape=jax.ShapeDtypeStruct((2 * t + MOE_SPARE_SETS * MOE_BLOCK, d // 2), jnp.uint32),
        compiler_params=_cparams(("arbitrary",)),
        name="moe_down",
    )(slot_row, block_e, hmid, w_down)


def _final_kernel(x1_ref, y0_ref, y1_ref, w_ref, mod_ref, gf_ref, o_ref):
    w = w_ref[...]
    lo0, hi0 = _unpack_bf16_pairs(y0_ref[...])
    lo1, hi1 = _unpack_bf16_pairs(y1_ref[...])
    y = jnp.concatenate([w[:, 0:1] * lo0 + w[:, 1:2] * lo1,
                         w[:, 0:1] * hi0 + w[:, 1:2] * hi1], axis=1)
    x2 = x1_ref[...] + mod_ref[0][5:6, :] * y
    o_ref[...] = x2 * lax.rsqrt(jnp.mean(x2 * x2, axis=-1, keepdims=True) + NORM_EPS) * gf_ref[...]


def _final(x1, ys, w, mod3, g_final, seq, tm=512):
    t, d = x1.shape
    tpb = seq // tm
    return pl.pallas_call(
        _final_kernel,
        grid=(t // tm,),
        in_specs=[pl.BlockSpec((tm, d), lambda i: (i, 0)),
                  pl.BlockSpec((tm, d // 2), lambda i: (i, 0)),
                  pl.BlockSpec((tm, d // 2), lambda i: (i + t // tm, 0)),
                  pl.BlockSpec((tm, LANES), lambda i: (i, 0)),
                  pl.BlockSpec((1, N_MOD, d), lambda i: (i // tpb, 0, 0)),
                  pl.BlockSpec((1, d), lambda i: (0, 0))],
        out_specs=pl.BlockSpec((tm, d), lambda i: (i, 0)),
        out_shape=jax.ShapeDtypeStruct((t, d), F32),
        compiler_params=_cparams(("arbitrary",)),
        name="combine_final_norm",
    )(x1, ys, ys, w, mod3, g_final.reshape(1, d))


def _rope_tables(seq, dh):
    half = dh // 2
    inv_freq = 1.0 / (ROPE_BASE ** (jnp.arange(half, dtype=F32) * 2.0 / dh))
    ang = jnp.arange(seq, dtype=F32)[:, None] * inv_freq[None, :]
    cos, sin = jnp.cos(ang), jnp.sin(ang)
    return jnp.concatenate([cos, cos], axis=1), jnp.concatenate([-sin, sin], axis=1)


def _layer(x, mod, g_norm1, w_in, ssm_a_re, ssm_a_im, ssm_b_re, ssm_b_im, ssm_c_re, ssm_c_im,
           ssm_d, ssm_log_dt, w_glu, beta_ssm, beta_ret, w_out, g_norm2, w_rg, b_rg, w_re, b_re,
           w_gate, w_up, w_down):
    bn, seq, d = x.shape
    t = bn * seq
    x2d = x.reshape(t, d)
    mod3 = mod.reshape(bn, N_MOD, d)

    sw = w_glu.shape[0]
    proj = _inproj(x2d, mod3, g_norm1, w_in.astype(BF16), seq)
    proj3 = proj.reshape(bn, seq, proj.shape[1])

    ab_re, ab_im, bbt_re, bbt_im = _s5_disc(ssm_a_re, ssm_a_im, ssm_log_dt,
                                            jnp.swapaxes(ssm_b_re, 1, 2), jnp.swapaxes(ssm_b_im, 1, 2))
    bz, cz, a_re, a_im = _s5_pack(ab_re, ab_im, bbt_re, bbt_im, ssm_c_re, ssm_c_im)
    ypre = _s5_scan(proj3, bz, cz, a_re, a_im, ssm_d.reshape(-1)).reshape(t, sw)

    heads = (d - sw) // RET_HEAD_DIM
    cosf, sinf = _rope_tables(seq, RET_HEAD_DIM)
    gamma = 1.0 - jnp.exp2(-5.0 - jnp.arange(heads, dtype=F32))
    log_g = jnp.broadcast_to(jnp.log(gamma)[:, None, None], (heads, 1, LANES))
    yret = _retention(proj3, sw // RET_HEAD_DIM, cosf, sinf, log_g, beta_ret)

    n_route = N_GROUPS + N_EXPERTS
    wr = jnp.concatenate([w_rg, w_re, jnp.zeros((d, LANES - n_route), F32)], axis=1)
    br = jnp.concatenate([b_rg, b_re, jnp.zeros((LANES - n_route,), F32)]).reshape(1, LANES)
    wr_hi = wr.astype(BF16)
    wr_lo = (wr - wr_hi.astype(F32)).astype(BF16)
    x1, h2, logits = _mix(ypre, yret.reshape(t, d - sw), x2d, mod3, w_glu.astype(BF16),
                          beta_ssm, w_out.astype(BF16), g_norm2, wr_hi, wr_lo, br, seq)

    nb = -(-(t * 2) // MOE_BLOCK) + N_EXPERTS
    meta, w_tok, cnt = _route(logits)
    dest, be = _dest(meta, cnt, nb)
    slot_row = _invert(dest[:, :2].T.reshape(-1), nb * MOE_BLOCK, t)
    block_e = be[:nb + 1, 0]

    hmid = _moe_up(slot_row, block_e, h2, w_gate, w_up, nb)
    ys = _moe_down(slot_row, block_e, hmid, w_down, nb, t)
    return x1, ys, w_tok, mod3


def kernel(x, c, w_ada, b_ada, g_norm1, w_in, ssm_a_re, ssm_a_im, ssm_b_re, ssm_b_im, ssm_c_re,
           ssm_c_im, ssm_d, ssm_log_dt, w_glu, beta_ssm, beta_ret, w_out, g_norm2, w_router_group,
           b_router_group, w_router_expert, b_router_expert, w_gate, w_up, w_down, g_final):
    depth = w_ada.shape[0]
    bn, seq, d = x.shape
    assert depth == 1, "the final norm is fused into the single layer's last kernel"
    l = 0
    mod = _ada(c, w_ada[l], b_ada[l])
    x1, ys, w_tok, mod3 = _layer(
        x, mod, g_norm1[l], w_in[l], ssm_a_re[l], ssm_a_im[l], ssm_b_re[l], ssm_b_im[l],
        ssm_c_re[l], ssm_c_im[l], ssm_d[l], ssm_log_dt[l], w_glu[l], beta_ssm[l], beta_ret[l],
        w_out[l], g_norm2[l], w_router_group[l], b_router_group[l], w_router_expert[l],
        b_router_expert[l], w_gate[l], w_up[l], w_down[l])
    out = _final(x1, ys, w_tok, mod3, g_final, seq)
    return out.reshape(bn, seq, d)
```

```python
import functools
import math

import jax
import jax.numpy as jnp
from jax import lax
from jax.experimental import pallas as pl
from jax.experimental.pallas import tpu as pltpu

F32 = jnp.float32
BF16 = jnp.bfloat16
I32 = jnp.int32

SSM_GROUP = 16
SSM_STATE = 64
RET_HEAD_DIM = 128
ROPE_BASE = 10000.0
N_GROUPS = 4
EXPERTS_PER_GROUP = 8
N_EXPERTS = N_GROUPS * EXPERTS_PER_GROUP
N_MOD = 6
NORM_EPS = 1e-6
GN_EPS = 1e-5

LANES = 128
SUBLANES = 8
VMEM_LIMIT = 56 * 1024 * 1024

ROUTE_LANE0 = N_GROUPS
MOE_BLOCK = 256
MOE_NCHUNK_COLS = 256
MOE_GATHER_DEPTH = 4
MOE_SCATTER_DEPTH = 3
MOE_SPARE_SETS = 4
RET_CHUNK = 256
S5_TT = 128
S5_BLOCKS_PER_STEP = 4


def _cparams(sem):
    return pltpu.CompilerParams(dimension_semantics=sem, vmem_limit_bytes=VMEM_LIMIT)


def _silu(x):
    return x * jax.nn.sigmoid(x)


def _pack_bf16_pairs(lo, hi):
    lo_w = pltpu.bitcast(lo.astype(BF16).astype(F32), jnp.uint32)
    hi_w = pltpu.bitcast(hi.astype(BF16).astype(F32), jnp.uint32)
    return (lo_w >> 16) | (hi_w & jnp.uint32(0xFFFF0000))


def _unpack_bf16_pairs(w):
    return (pltpu.bitcast(w << 16, F32), pltpu.bitcast(w & jnp.uint32(0xFFFF0000), F32))


def _ada_kernel(c_ref, w_ref, b_ref, o_ref):
    s = _silu(c_ref[...])
    o_ref[...] = jnp.dot(s.astype(BF16), w_ref[...].astype(BF16),
                         preferred_element_type=F32) + b_ref[...]


def _ada(c, w, b, tn=1024):
    bn, d = c.shape
    n = w.shape[1]
    return pl.pallas_call(
        _ada_kernel,
        grid=(n // tn,),
        in_specs=[pl.BlockSpec((bn, d), lambda j: (0, 0)),
                  pl.BlockSpec((d, tn), lambda j: (0, j)),
                  pl.BlockSpec((1, tn), lambda j: (0, j))],
        out_specs=pl.BlockSpec((bn, tn), lambda j: (0, j)),
        out_shape=jax.ShapeDtypeStruct((bn, n), F32),
        compiler_params=_cparams(("arbitrary",)),
        name="ada_mod",
    )(c, w, b.reshape(1, n))


def _inproj_kernel(x_ref, mod_ref, g_ref, w_ref, o_ref, h_scr):
    @pl.when(pl.program_id(1) == 0)
    def _():
        x = x_ref[...]
        y = x * lax.rsqrt(jnp.mean(x * x, axis=-1, keepdims=True) + NORM_EPS) * g_ref[...]
        m = mod_ref[0]
        h_scr[...] = (y * (1.0 + m[1:2, :]) + m[0:1, :]).astype(BF16)

    o_ref[...] = jnp.dot(h_scr[...], w_ref[...], preferred_element_type=F32).astype(BF16)


def _inproj(x2d, mod3, g1, w_in_bf, seq, tm=1024, tn=1024):
    t, d = x2d.shape
    n = w_in_bf.shape[1]
    tm = min(tm, seq)
    tn = math.gcd(tn, n)
    tpb = seq // tm
    return pl.pallas_call(
        _inproj_kernel,
        grid=(t // tm, n // tn),
        in_specs=[pl.BlockSpec((tm, d), lambda i, j: (i, 0)),
                  pl.BlockSpec((1, N_MOD, d), lambda i, j: (i // tpb, 0, 0)),
                  pl.BlockSpec((1, d), lambda i, j: (0, 0)),
                  pl.BlockSpec((d, tn), lambda i, j: (0, j))],
        out_specs=pl.BlockSpec((tm, tn), lambda i, j: (i, j)),
        out_shape=jax.ShapeDtypeStruct((t, n), BF16),
        scratch_shapes=[pltpu.VMEM((tm, d), BF16)],
        compiler_params=_cparams(("arbitrary", "arbitrary")),
        name="norm_inproj",
    )(x2d, mod3, g1.reshape(1, d), w_in_bf)


def _s5_disc_kernel(are_ref, aim_ref, ldt_ref, bre_ref, bim_ref,
                    abre_ref, abim_ref, bbre_ref, bbim_ref):
    a_re = are_ref[...]
    a_im = aim_ref[...]
    dt = jnp.exp(ldt_ref[...])
    mag = jnp.exp(dt * a_re)
    ab_re = mag * jnp.cos(dt * a_im)
    ab_im = mag * jnp.sin(dt * a_im)
    inv_abs2 = 1.0 / (a_re * a_re + a_im * a_im)
    n_re = ab_re - 1.0
    f_re = (n_re * a_re + ab_im * a_im) * inv_abs2
    f_im = (ab_im * a_re - n_re * a_im) * inv_abs2
    abre_ref[...] = ab_re
    abim_ref[...] = ab_im
    b_re = bre_ref[...]
    b_im = bim_ref[...]
    fr = f_re[:, None, :]
    fi = f_im[:, None, :]
    bbre_ref[...] = fr * b_re - fi * b_im
    bbim_ref[...] = fr * b_im + fi * b_re


def _s5_disc(a_re, a_im, log_dt, bt_re, bt_im):
    g, p = a_re.shape
    h = bt_re.shape[1]
    return pl.pallas_call(
        _s5_disc_kernel,
        out_shape=[jax.ShapeDtypeStruct((g, p), F32), jax.ShapeDtypeStruct((g, p), F32),
                   jax.ShapeDtypeStruct((g, h, p), F32), jax.ShapeDtypeStruct((g, h, p), F32)],
        name="s5_discretise",
    )(a_re, a_im, log_dt.reshape(g, 1), bt_re, bt_im)


def _s5_pack(ab_re, ab_im, bbt_re, bbt_im, c_re, c_im):
    g, h, p = bbt_re.shape
    npair = g // 2
    ppb = LANES // (2 * h)
    eye2 = jnp.eye(2, dtype=F32)
    qsel = jax.nn.one_hot(jnp.arange(npair) % ppb, ppb, dtype=F32)

    def in_mat(bbt):
        b4 = bbt.reshape(npair, 2, h, p)
        return jnp.einsum('ab,xahp->xahbp', eye2, b4).reshape(npair, 2 * h, 2 * p)

    bsmall = jnp.concatenate([in_mat(bbt_re), in_mat(bbt_im)], axis=-1)
    bz = jnp.einsum('xq,xrc->xqrc', qsel, bsmall).reshape(npair, LANES, 4 * p)

    def out_mat(c):
        c4 = jnp.swapaxes(c, 1, 2).reshape(npair, 2, p, h)
        return jnp.einsum('ab,xaph->xapbh', eye2, c4).reshape(npair, 2 * p, 2 * h)

    csmall = jnp.concatenate([out_mat(c_re), -out_mat(c_im)], axis=1)
    cz = jnp.einsum('xq,xrc->xrqc', qsel, csmall).reshape(npair, 4 * p, LANES)
    a_re = ab_re.reshape(npair // ppb, ppb, 2 * p)
    a_im = ab_im.reshape(npair // ppb, ppb, 2 * p)
    return bz.astype(BF16), cz.astype(BF16), a_re, a_im


def _s5_kernel(u_ref, bz_ref, cz_ref, are_ref, aim_ref, d_ref, o_ref, state, *bufs, ppb):
    t = pl.program_id(0)
    j = pl.program_id(1)
    bn, tt, width = u_ref.shape
    nblk = width // LANES
    rt = bn * tt

    @pl.when(t == 0)
    def _():
        for i in range(nblk):
            state[j * nblk + i] = jnp.zeros(state.shape[1:], F32)

    for i in range(nblk):
        buf = bufs[i]
        kb = j * nblk + i
        lanes = slice(i * LANES, (i + 1) * LANES)

        u = u_ref[:, :, lanes].reshape(rt, LANES)
        for q in range(ppb):
            bu = jnp.dot(u, bz_ref[i * ppb + q], preferred_element_type=F32)
            for b in range(bn):
                rows = pl.ds(b, tt, stride=bn)
                buf[2 * q, rows, :] = bu[b * tt:(b + 1) * tt, :LANES]
                buf[2 * q + 1, rows, :] = bu[b * tt:(b + 1) * tt, LANES:]

        a_re = [jnp.broadcast_to(are_ref[i, q:q + 1, :], (bn, LANES)) for q in range(ppb)]
        a_im = [jnp.broadcast_to(aim_ref[i, q:q + 1, :], (bn, LANES)) for q in range(ppb)]
        s = [state[kb, k] for k in range(2 * ppb)]
        for step in range(tt):
            rows = slice(step * bn, (step + 1) * bn)
            for q in range(ppb):
                s_re, s_im = s[2 * q], s[2 * q + 1]
                n_re = a_re[q] * s_re - a_im[q] * s_im + buf[2 * q, rows, :]
                n_im = a_re[q] * s_im + a_im[q] * s_re + buf[2 * q + 1, rows, :]
                buf[2 * q, rows, :] = n_re
                buf[2 * q + 1, rows, :] = n_im
                s[2 * q], s[2 * q + 1] = n_re, n_im
        for k in range(2 * ppb):
            state[kb, k] = s[k]

        acc = None
        for q in range(ppb):
            lhs = jnp.concatenate([buf[2 * q], buf[2 * q + 1]], axis=1).astype(BF16)
            part = jnp.dot(lhs, cz_ref[i * ppb + q], preferred_element_type=F32)
            acc = part if acc is None else acc + part
        buf[2 * ppb] = acc
        d_skip = d_ref[:, lanes]
        for b in range(bn):
            o_ref[b, :, lanes] = (buf[2 * ppb, pl.ds(b, tt, stride=bn), :]
                                  + d_skip * u_ref[b, :, lanes].astype(F32))


def _s5_scan(u3, bz, cz, a_re, a_im, d_skip, tt=S5_TT, nblk=S5_BLOCKS_PER_STEP):
    bn, seq, width = u3.shape[0], u3.shape[1], d_skip.shape[0]
    nkb = width // LANES
    ppb = bz.shape[0] // nkb
    nblk = math.gcd(nblk, nkb)
    bw = nblk * LANES
    kern = functools.partial(_s5_kernel, ppb=ppb)
    return pl.pallas_call(
        kern,
        grid=(seq // tt, nkb // nblk),
        in_specs=[pl.BlockSpec((bn, tt, bw), lambda t, k: (0, t, k)),
                  pl.BlockSpec((nblk * ppb,) + bz.shape[1:], lambda t, k: (k, 0, 0)),
                  pl.BlockSpec((nblk * ppb,) + cz.shape[1:], lambda t, k: (k, 0, 0)),
                  pl.BlockSpec((nblk,) + a_re.shape[1:], lambda t, k: (k, 0, 0)),
                  pl.BlockSpec((nblk,) + a_im.shape[1:], lambda t, k: (k, 0, 0)),
                  pl.BlockSpec((1, bw), lambda t, k: (0, k))],
        out_specs=pl.BlockSpec((bn, tt, bw), lambda t, k: (0, t, k)),
        out_shape=jax.ShapeDtypeStruct((bn, seq, width), F32),
        scratch_shapes=[pltpu.VMEM((nkb, 2 * ppb, bn, LANES), F32)]
        + [pltpu.VMEM((2 * ppb + 1, bn * tt, LANES), F32) for _ in range(nblk)],
        compiler_params=_cparams(("arbitrary", "arbitrary")),
        name="s5_scan",
    )(u3, bz, cz, a_re, a_im, d_skip.reshape(1, width))


def _ret_kernel(q_ref, k_ref, v_ref, g_ref, cq_ref, sq_ref, ck_ref, sk_ref, lg_ref, beta_ref, o_ref,
                q_scr, k_scr, kzt_scr, y_scr, *, chunk):
    seq, dh = q_ref.shape[1], q_ref.shape[2]
    lg = lg_ref[0][:, :1]
    ri = lax.broadcasted_iota(I32, (chunk, chunk), 0)
    ci = lax.broadcasted_iota(I32, (chunk, chunk), 1)
    rel = (ri - ci).astype(F32)
    decay = jnp.where(rel >= 0, jnp.exp(lg * jnp.maximum(rel, 0.0)), 0.0)
    idx = lax.broadcasted_iota(I32, (chunk, 1), 0).astype(F32)
    xi = jnp.exp(lg * (idx + 1.0))
    g_chunk = jnp.exp(lg * float(chunk))
    beta = beta_ref[...]
    half = dh // 2
    pr = lax.broadcasted_iota(I32, (dh, dh), 0)
    pc = lax.broadcasted_iota(I32, (dh, dh), 1)
    swap = jnp.where(((pr + half) & (dh - 1)) == pc, 1.0, 0.0).astype(BF16)

    def rope(x_ref, cos_ref, sin_ref, sl):
        xb = x_ref[0, sl, :]
        rolled = jnp.dot(xb, swap, preferred_element_type=F32).astype(BF16)
        return xb * cos_ref[sl, :] + rolled * sin_ref[sl, :]

    zeta = jnp.exp(lg * (chunk - 1.0 - idx))
    for n in range(seq // chunk):
        sl = slice(n * chunk, (n + 1) * chunk)
        q_scr[sl, :] = rope(q_ref, cq_ref, sq_ref, sl)
        kb = rope(k_ref, ck_ref, sk_ref, sl)
        k_scr[sl, :] = kb
        kzt_scr[:, sl] = (kb.astype(F32) * zeta).T.astype(BF16)

    decay = decay.astype(BF16)
    r = jnp.zeros((dh, dh), F32)
    for n in range(seq // chunk):
        sl = slice(n * chunk, (n + 1) * chunk)
        qb = q_scr[sl, :]
        v = v_ref[0, sl, :]
        s = lax.dot_general(qb, k_scr[sl, :], (((1,), (1,)), ((), ())),
                            preferred_element_type=F32).astype(BF16) * decay
        y = jnp.dot(s, v, preferred_element_type=F32)
        y_scr[sl, :] = y + jnp.dot(qb, r.astype(BF16), preferred_element_type=F32) * xi
        r = r * g_chunk + jnp.dot(kzt_scr[:, sl], v, preferred_element_type=F32)

    y = y_scr[...]
    mu = jnp.mean(y, axis=-1, keepdims=True)
    yc = y - mu
    var = jnp.mean(yc * yc, axis=-1, keepdims=True)
    yn = yc * lax.rsqrt(var + GN_EPS) * beta
    o_ref[0] = (_silu(g_ref[0].astype(F32)) * yn).astype(BF16)


def _retention(proj3, col0, cosf, sinf, log_g, beta_ret, chunk=RET_CHUNK):
    bn, seq, _ = proj3.shape
    dh = RET_HEAD_DIM
    heads = beta_ret.shape[0] // dh
    blk = lambda off: pl.BlockSpec((1, seq, dh), lambda b, h, off=off: (b, 0, col0 + off + h))
    k_scale = dh ** -0.5
    kern = functools.partial(_ret_kernel, chunk=chunk)
    return pl.pallas_call(
        kern,
        grid=(bn, heads),
        in_specs=[blk(0), blk(heads), blk(2 * heads), blk(3 * heads),
                  pl.BlockSpec((seq, dh), lambda b, h: (0, 0)),
                  pl.BlockSpec((seq, dh), lambda b, h: (0, 0)),
                  pl.BlockSpec((seq, dh), lambda b, h: (0, 0)),
                  pl.BlockSpec((seq, dh), lambda b, h: (0, 0)),
                  pl.BlockSpec((1, 1, LANES), lambda b, h: (h, 0, 0)),
                  pl.BlockSpec((1, dh), lambda b, h: (0, h))],
        out_specs=pl.BlockSpec((1, seq, dh), lambda b, h: (b, 0, h)),
        out_shape=jax.ShapeDtypeStruct((bn, seq, heads * dh), BF16),
        scratch_shapes=[pltpu.VMEM((seq, dh), BF16), pltpu.VMEM((seq, dh), BF16),
                        pltpu.VMEM((dh, seq), BF16), pltpu.VMEM((seq, dh), F32)],
        compiler_params=_cparams(("arbitrary", "arbitrary")),
        name="retention",
    )(proj3, proj3, proj3, proj3, cosf.astype(BF16), sinf.astype(BF16),
      (cosf * k_scale).astype(BF16), (sinf * k_scale).astype(BF16),
      log_g, beta_ret.reshape(1, heads * dh))


def _gelu_tanh(x):
    return 0.5 * x * (1.0 + jnp.tanh(math.sqrt(2.0 / math.pi) * (x + 0.044715 * (x * x * x))))


def _route_rows(lg, carry):
    tm = lg.shape[0]
    lane = lax.broadcasted_iota(I32, (tm, LANES), 1).astype(F32)
    neg = jnp.float32(-jnp.inf)

    def first_max(vals):
        m = jnp.max(vals, axis=-1, keepdims=True)
        idx = jnp.min(jnp.where(vals == m, lane, float(LANES)), axis=-1, keepdims=True)
        return m, idx

    gl = jnp.where(lane < N_GROUPS, lg, neg)
    gmax, gsel = first_max(gl)
    g_w = 1.0 / jnp.sum(jnp.exp(gl - gmax), axis=-1, keepdims=True)
    lo = ROUTE_LANE0 + gsel * EXPERTS_PER_GROUP
    el = jnp.where((lane >= lo) & (lane < lo + EXPERTS_PER_GROUP), lg, neg)
    m1, i1 = first_max(el)
    m2, i2 = first_max(jnp.where(lane == i1, neg, el))
    e21 = jnp.exp(m2 - m1)
    w1 = g_w / (1.0 + e21)
    w2 = g_w * e21 / (1.0 + e21)

    sel1 = lane == i1
    sel2 = lane == i2
    onehot = jnp.where(sel1 | sel2, 1.0, 0.0).astype(BF16)
    ri = lax.broadcasted_iota(I32, (tm, tm), 0)
    ci = lax.broadcasted_iota(I32, (tm, tm), 1)
    tri = jnp.where(ci < ri, 1.0, 0.0).astype(BF16)
    base = carry[0:1, :]
    excl = jnp.dot(tri, onehot, preferred_element_type=F32) + base
    r1 = jnp.sum(jnp.where(sel1, excl, 0.0), axis=-1, keepdims=True)
    r2 = jnp.sum(jnp.where(sel2, excl, 0.0), axis=-1, keepdims=True)
    total = base + jnp.sum(jnp.where(sel1 | sel2, 1.0, 0.0), axis=0, keepdims=True)
    meta = (jnp.where(lane == 0, i1, 0.0) + jnp.where(lane == 1, i2, 0.0)
            + jnp.where(lane == 2, r1, 0.0) + jnp.where(lane == 3, r2, 0.0))
    w = jnp.where(lane == 0, w1, 0.0) + jnp.where(lane == 1, w2, 0.0)
    return meta, w, total


def _mix_kernel(ys_ref, yr_ref, x_ref, mod_ref, wglu_ref, bssm_ref, wout_ref, g2_ref,
                wr_ref, br_ref, x1_ref, h2_ref, meta_ref, w_ref, cnt_ref, carry):
    @pl.when(pl.program_id(0) == 0)
    def _():
        carry[...] = jnp.zeros(carry.shape, F32)

    sw = ys_ref.shape[1]
    z = _gelu_tanh(ys_ref[...])
    gate = jax.nn.sigmoid(jnp.dot(z.astype(BF16), wglu_ref[...], preferred_element_type=F32))
    o = z * gate
    y_ssm = o * lax.rsqrt(jnp.mean(o * o, axis=-1, keepdims=True) + NORM_EPS) * bssm_ref[...]
    mixed = (jnp.dot(y_ssm.astype(BF16), wout_ref[:sw, :], preferred_element_type=F32)
             + jnp.dot(yr_ref[...], wout_ref[sw:, :], preferred_element_type=F32))
    m = mod_ref[0]
    x1 = x_ref[...] + m[2:3, :] * mixed
    x1_ref[...] = x1
    y = x1 * lax.rsqrt(jnp.mean(x1 * x1, axis=-1, keepdims=True) + NORM_EPS) * g2_ref[...]
    h2 = y * (1.0 + m[4:5, :]) + m[3:4, :]
    half = h2.shape[1] // 2
    h2_ref[...] = _pack_bf16_pairs(h2[:, :half], h2[:, half:])
    logits = jnp.dot(h2.astype(BF16), wr_ref[...], preferred_element_type=F32) + br_ref[...]
    meta, w, total = _route_rows(logits, carry)
    meta_ref[...] = meta
    w_ref[...] = w
    carry[...] = jnp.broadcast_to(total, carry.shape)
    cnt_ref[...] = jnp.broadcast_to(total, cnt_ref.shape)


def _mix(ypre, yret, x2d, mod3, wglu_bf, beta_ssm, wout_bf, g2, wr_bf, br, seq, tm=256):
    t, d = x2d.shape
    sw = wglu_bf.shape[0]
    tpb = seq // tm
    const = lambda shape: pl.BlockSpec(shape, lambda i: (0,) * len(shape),
                                       pipeline_mode=pl.Buffered(1))
    return pl.pallas_call(
        _mix_kernel,
        grid=(t // tm,),
        in_specs=[pl.BlockSpec((tm, sw), lambda i: (i, 0)),
                  pl.BlockSpec((tm, d - sw), lambda i: (i, 0)),
                  pl.BlockSpec((tm, d), lambda i: (i, 0)),
                  pl.BlockSpec((1, N_MOD, d), lambda i: (i // tpb, 0, 0)),
                  const((sw, sw)), const((1, sw)), const((d, d)), const((1, d)),
                  const((d, LANES)), const((1, LANES))],
        out_specs=[pl.BlockSpec((tm, d), lambda i: (i, 0)),
                   pl.BlockSpec((tm, d // 2), lambda i: (i, 0)),
                   pl.BlockSpec((tm, LANES), lambda i: (i, 0)),
                   pl.BlockSpec((tm, LANES), lambda i: (i, 0)),
                   pl.BlockSpec((SUBLANES, LANES), lambda i: (0, 0))],
        out_shape=[jax.ShapeDtypeStruct((t, d), F32),
                   jax.ShapeDtypeStruct((t, d // 2), jnp.uint32),
                   jax.ShapeDtypeStruct((t, LANES), F32),
                   jax.ShapeDtypeStruct((t, LANES), F32),
                   jax.ShapeDtypeStruct((SUBLANES, LANES), F32)],
        scratch_shapes=[pltpu.VMEM((SUBLANES, LANES), F32)],
        compiler_params=_cparams(("arbitrary",)),
        name="mix_outproj_router",
    )(ypre, yret, x2d, mod3, wglu_bf, beta_ssm.reshape(1, sw), wout_bf, g2.reshape(1, d),
      wr_bf, br)


def _lane_cumsum(x):
    lane = lax.broadcasted_iota(I32, x.shape, 1)
    sh = 1
    while sh < LANES:
        x = x + jnp.where(lane >= sh, pltpu.roll(x, sh, axis=1), 0)
        sh *= 2
    return x


def _dest_kernel(meta_ref, cnt_ref, dest_ref, be_ref, *, nb):
    tm = meta_ref.shape[0]
    shift = MOE_BLOCK.bit_length() - 1
    cnt = cnt_ref[...].astype(I32)
    padded = ((cnt + (MOE_BLOCK - 1)) >> shift) << shift
    pend_i = _lane_cumsum(padded)
    pend = pend_i.astype(F32)[0:1, :]
    poff = (pend_i - padded).astype(F32)[0:1, :]
    meta = meta_ref[...]
    lane = lax.broadcasted_iota(I32, (tm, LANES), 1).astype(F32)
    pick = lambda col: jnp.sum(jnp.where(lane == col, meta, 0.0), axis=-1, keepdims=True)
    i1, i2, r1, r2 = pick(0), pick(1), pick(2), pick(3)
    d1 = jnp.sum(jnp.where(lane == i1, poff, 0.0), axis=-1, keepdims=True) + r1
    d2 = jnp.sum(jnp.where(lane == i2, poff, 0.0), axis=-1, keepdims=True) + r2
    dest_ref[...] = (jnp.where(lane == 0, d1, 0.0) + jnp.where(lane == 1, d2, 0.0)).astype(I32)

    nrow = be_ref.shape[0]
    blk = lax.broadcasted_iota(I32, (nrow, LANES), 0).astype(F32)
    lane_b = lax.broadcasted_iota(I32, (nrow, LANES), 1).astype(F32)
    is_e = (lane_b >= ROUTE_LANE0) & (lane_b < ROUTE_LANE0 + N_EXPERTS)
    below = jnp.sum(jnp.where(is_e & (pend <= blk * MOE_BLOCK), 1.0, 0.0), axis=-1, keepdims=True)
    block_e = jnp.minimum(below, N_EXPERTS - 1.0)
    used = jnp.sum(jnp.where(lane_b == ROUTE_LANE0 + N_EXPERTS - 1, pend, 0.0),
                   axis=-1, keepdims=True) * (1.0 / MOE_BLOCK)
    be = jnp.where(blk[:, :1] == nb, used, block_e)
    be_ref[...] = jnp.broadcast_to(be, (nrow, LANES)).astype(I32)


def _dest(meta, cnt, nb, tm=512):
    t = meta.shape[0]
    nrow = -(-(nb + 1) // SUBLANES) * SUBLANES
    kern = functools.partial(_dest_kernel, nb=nb)
    return pl.pallas_call(
        kern,
        grid=(t // tm,),
        in_specs=[pl.BlockSpec((tm, LANES), lambda i: (i, 0)),
                  pl.BlockSpec((SUBLANES, LANES), lambda i: (0, 0))],
        out_specs=[pl.BlockSpec((tm, LANES), lambda i: (i, 0)),
                   pl.BlockSpec((nrow, LANES), lambda i: (0, 0))],
        out_shape=[jax.ShapeDtypeStruct((t, LANES), I32),
                   jax.ShapeDtypeStruct((nrow, LANES), I32)],
        compiler_params=_cparams(("arbitrary",)),
        name="route_dest",
    )(meta, cnt)


def _invert_kernel(dest_ref, fill_hbm, row_ref, sem):
    cp = pltpu.make_async_copy(fill_hbm, row_ref, sem)
    cp.start()
    cp.wait()

    def put(a, c):
        row_ref[dest_ref[a]] = a
        return c

    lax.fori_loop(0, dest_ref.shape[0], put, 0, unroll=8)


def _invert(dest_flat, n_slot, t):
    spare = MOE_SPARE_SETS * MOE_BLOCK
    fill = 2 * t + (jnp.arange(n_slot, dtype=I32) & (spare - 1))
    return pl.pallas_call(
        _invert_kernel,
        in_specs=[pl.BlockSpec(memory_space=pltpu.SMEM), pl.BlockSpec(memory_space=pl.ANY)],
        out_specs=pl.BlockSpec(memory_space=pltpu.SMEM),
        out_shape=jax.ShapeDtypeStruct((n_slot,), I32),
        scratch_shapes=[pltpu.SemaphoreType.DMA(())],
        name="route_invert",
    )(dest_flat, fill)


def _expert_weights(b, used, be_ref, w_hbms, wst, wsem, run_ref, load):
    def fetch(e, p):
        for i, w in enumerate(w_hbms):
            pltpu.make_async_copy(w.at[e], wst.at[p, i], wsem.at[p]).start(priority=1)

    @pl.when((b == 0) & (used > 0))
    def _():
        run_ref[0] = 0
        fetch(be_ref[0], 0)

    bc = jnp.minimum(b, be_ref.shape[0] - 2)
    new_expert = (b == 0) | (be_ref[bc] != be_ref[jnp.maximum(bc - 1, 0)])

    @pl.when(new_expert & (b < used))
    def _():
        p = run_ref[0]
        for i, w in enumerate(w_hbms):
            pltpu.make_async_copy(w.at[0], wst.at[p, i], wsem.at[p]).wait()
        load(p)
        e = be_ref[b]
        nxt = lax.while_loop(lambda i: (i < used) & (be_ref[i] == e), lambda i: i + 1, b + 1)

        @pl.when(nxt < used)
        def _():
            fetch(be_ref[nxt], 1 - p)

        run_ref[0] = 1 - p


def _moe_up_kernel(row_ref, be_ref, h2_hbm, wg_hbm, wu_hbm, o_ref, wst, wbuf, xbuf, xb_scr, sem,
                   wsem, run_ref, *, nb):
    b = pl.program_id(0)
    used = be_ref[nb]
    blk = MOE_BLOCK
    fe = wg_hbm.shape[2]
    t = h2_hbm.shape[0]

    def gather_rows(block, rows):
        slot = block % MOE_GATHER_DEPTH
        base = block * blk
        for r in rows:
            row = row_ref[base + r]
            tok = jnp.where(row >= t, row - t, row)
            tok = jnp.where(tok >= t, 0, tok)
            pltpu.make_async_copy(h2_hbm.at[pl.ds(tok, 1)], xbuf.at[slot, pl.ds(r, 1)],
                                  sem.at[slot]).start()

    ahead = MOE_GATHER_DEPTH - 1

    @pl.when(b == 0)
    def _():
        for first in range(ahead):
            @pl.when(first < used)
            def _():
                gather_rows(first, range(blk))

    def load(p):
        wbuf[:, :fe] = wst[p, 0].astype(BF16)
        wbuf[:, fe:] = wst[p, 1].astype(BF16)

    _expert_weights(b, used, be_ref, (wg_hbm, wu_hbm), wst, wsem, run_ref, load)

    nchunk = fe // MOE_NCHUNK_COLS
    rows_per = blk // nchunk

    def step(prefetch):
        slot = b % MOE_GATHER_DEPTH
        pltpu.make_async_copy(xbuf.at[slot], xbuf.at[slot], sem.at[slot]).wait()
        lo, hi = _unpack_bf16_pairs(xbuf[slot])
        half = lo.shape[1]
        xb_scr[:, :half] = lo.astype(BF16)
        xb_scr[:, half:] = hi.astype(BF16)
        for c in range(nchunk):
            if prefetch:
                gather_rows(b + ahead, range(c * rows_per, (c + 1) * rows_per))
            cols = slice(c * MOE_NCHUNK_COLS, (c + 1) * MOE_NCHUNK_COLS)
            ucols = slice(fe + c * MOE_NCHUNK_COLS, fe + (c + 1) * MOE_NCHUNK_COLS)
            g = jnp.dot(xb_scr[...], wbuf[:, cols], preferred_element_type=F32)
            u = jnp.dot(xb_scr[...], wbuf[:, ucols], preferred_element_type=F32)
            o_ref[:, cols] = (_silu(g) * u).astype(BF16)

    @pl.when(b + ahead < used)
    def _():
        step(True)

    @pl.when((b < used) & (b + ahead >= used))
    def _():
        step(False)

    @pl.when(b >= used)
    def _():
        o_ref[...] = jnp.zeros(o_ref.shape, BF16)


def _moe_up(slot_row, block_e, h2, w_gate, w_up, nb):
    d, fe = w_gate.shape[1], w_gate.shape[2]
    kern = functools.partial(_moe_up_kernel, nb=nb)
    grid_spec = pltpu.PrefetchScalarGridSpec(
        num_scalar_prefetch=2,
        grid=(nb,),
        in_specs=[pl.BlockSpec(memory_space=pl.ANY),
                  pl.BlockSpec(memory_space=pl.ANY),
                  pl.BlockSpec(memory_space=pl.ANY)],
        out_specs=pl.BlockSpec((MOE_BLOCK, fe), lambda b, s, e: (b, 0)),
        scratch_shapes=[pltpu.VMEM((2, 2, d, fe), F32),
                        pltpu.VMEM((d, 2 * fe), BF16),
                        pltpu.VMEM((MOE_GATHER_DEPTH, MOE_BLOCK, d // 2), jnp.uint32),
                        pltpu.VMEM((MOE_BLOCK, d), BF16),
                        pltpu.SemaphoreType.DMA((MOE_GATHER_DEPTH,)),
                        pltpu.SemaphoreType.DMA((2,)),
                        pltpu.SMEM((1,), I32)],
    )
    return pl.pallas_call(
        kern,
        grid_spec=grid_spec,
        out_shape=jax.ShapeDtypeStruct((nb * MOE_BLOCK, fe), BF16),
        compiler_params=_cparams(("arbitrary",)),
        name="moe_up",
    )(slot_row, block_e, h2, w_gate, w_up)


def _moe_down_kernel(row_ref, be_ref, hm_ref, wd_hbm, ys_hbm, wst, wbuf, obuf, sem, wsem, run_ref,
                     *, nb, t):
    b = pl.program_id(0)
    used = be_ref[nb]
    blk = MOE_BLOCK

    def wait_slot(slot):
        pltpu.make_async_copy(obuf.at[slot], obuf.at[slot], sem.at[slot]).wait()

    depth = obuf.shape[0]

    @pl.when(b == 0)
    def _():
        obuf[0] = jnp.zeros(obuf.shape[1:], jnp.uint32)
        for k in range(MOE_SPARE_SETS):
            spare = ys_hbm.at[pl.ds(2 * t + k * blk, blk)]
            pltpu.make_async_copy(obuf.at[0], spare, sem.at[depth]).start()
        for k in range(MOE_SPARE_SETS):
            spare = ys_hbm.at[pl.ds(2 * t + k * blk, blk)]
            pltpu.make_async_copy(obuf.at[0], spare, sem.at[depth]).wait()

    @pl.when((b >= depth) & (b - depth < used))
    def _():
        wait_slot(b % depth)

    def load(p):
        wbuf[...] = wst[p, 0].astype(BF16)

    _expert_weights(b, used, be_ref, (wd_hbm,), wst, wsem, run_ref, load)

    half = wbuf.shape[1] // 2
    nchunk = half // MOE_NCHUNK_COLS
    rows_per = blk // nchunk

    def step(scatter, matmul):
        slot = (b - 1) % depth
        base = (b - 1) * blk
        for c in range(nchunk):
            if scatter:
                for r in range(c * rows_per, (c + 1) * rows_per):
                    pltpu.make_async_copy(obuf.at[slot, pl.ds(r, 1)],
                                          ys_hbm.at[pl.ds(row_ref[base + r], 1)],
                                          sem.at[slot]).start(priority=r % 2)
            if matmul:
                cols = slice(c * MOE_NCHUNK_COLS, (c + 1) * MOE_NCHUNK_COLS)
                hcols = slice(half + c * MOE_NCHUNK_COLS, half + (c + 1) * MOE_NCHUNK_COLS)
                lo = jnp.dot(hm_ref[...], wbuf[:, cols], preferred_element_type=F32)
                hi = jnp.dot(hm_ref[...], wbuf[:, hcols], preferred_element_type=F32)
                obuf[b % depth, :, cols] = _pack_bf16_pairs(lo, hi)

    @pl.when((b >= 1) & (b < used))
    def _():
        step(True, True)

    @pl.when((b == 0) & (b < used))
    def _():
        step(False, True)

    @pl.when((b >= 1) & (b == used))
    def _():
        step(True, False)

    @pl.when(b == nb)
    def _():
        for k in range(1, depth):
            @pl.when(nb - k < used)
            def _():
                wait_slot((nb - k) % depth)


def _moe_down(slot_row, block_e, hmid, w_down, nb, t):
    fe, d = w_down.shape[1], w_down.shape[2]
    kern = functools.partial(_moe_down_kernel, nb=nb, t=t)
    grid_spec = pltpu.PrefetchScalarGridSpec(
        num_scalar_prefetch=2,
        grid=(nb + 1,),
        in_specs=[pl.BlockSpec((MOE_BLOCK, fe), lambda b, s, e: (jnp.minimum(b, nb - 1), 0)),
                  pl.BlockSpec(memory_space=pl.ANY)],
        out_specs=pl.BlockSpec(memory_space=pl.ANY),
        scratch_shapes=[pltpu.VMEM((2, 1, fe, d), F32),
                        pltpu.VMEM((fe, d), BF16),
                        pltpu.VMEM((MOE_SCATTER_DEPTH, MOE_BLOCK, d // 2), jnp.uint32),
                        pltpu.SemaphoreType.DMA((MOE_SCATTER_DEPTH + 1,)),
                        pltpu.SemaphoreType.DMA((2,)),
                        pltpu.SMEM((1,), I32)],
    )
    return pl.pallas_call(
        kern,
        grid_spec=grid_spec,
        out_shape=jax.ShapeDtypeStruct((2 * t + MOE_SPARE_SETS * MOE_BLOCK, d // 2), jnp.uint32),
        compiler_params=_cparams(("arbitrary",)),
        name="moe_down",
    )(slot_row, block_e, hmid, w_down)


def _final_kernel(x1_ref, y0_ref, y1_ref, w_ref, mod_ref, gf_ref, o_ref):
    w = w_ref[...]
    lo0, hi0 = _unpack_bf16_pairs(y0_ref[...])
    lo1, hi1 = _unpack_bf16_pairs(y1_ref[...])
    y = jnp.concatenate([w[:, 0:1] * lo0 + w[:, 1:2] * lo1,
                         w[:, 0:1] * hi0 + w[:, 1:2] * hi1], axis=1)
    x2 = x1_ref[...] + mod_ref[0][5:6, :] * y
    o_ref[...] = x2 * lax.rsqrt(jnp.mean(x2 * x2, axis=-1, keepdims=True) + NORM_EPS) * gf_ref[...]


def _final(x1, ys, w, mod3, g_final, seq, tm=512):
    t, d = x1.shape
    tpb = seq // tm
    return pl.pallas_call(
        _final_kernel,
        grid=(t // tm,),
        in_specs=[pl.BlockSpec((tm, d), lambda i: (i, 0)),
                  pl.BlockSpec((tm, d // 2), lambda i: (i, 0)),
                  pl.BlockSpec((tm, d // 2), lambda i: (i + t // tm, 0)),
                  pl.BlockSpec((tm, LANES), lambda i: (i, 0)),
                  pl.BlockSpec((1, N_MOD, d), lambda i: (i // tpb, 0, 0)),
                  pl.BlockSpec((1, d), lambda i: (0, 0))],
        out_specs=pl.BlockSpec((tm, d), lambda i: (i, 0)),
        out_shape=jax.ShapeDtypeStruct((t, d), F32),
        compiler_params=_cparams(("arbitrary",)),
        name="combine_final_norm",
    )(x1, ys, ys, w, mod3, g_final.reshape(1, d))


def _rope_tables(seq, dh):
    half = dh // 2
    inv_freq = 1.0 / (ROPE_BASE ** (jnp.arange(half, dtype=F32) * 2.0 / dh))
    ang = jnp.arange(seq, dtype=F32)[:, None] * inv_freq[None, :]
    cos, sin = jnp.cos(ang), jnp.sin(ang)
    return jnp.concatenate([cos, cos], axis=1), jnp.concatenate([-sin, sin], axis=1)


def _layer(x, mod, g_norm1, w_in, ssm_a_re, ssm_a_im, ssm_b_re, ssm_b_im, ssm_c_re, ssm_c_im,
           ssm_d, ssm_log_dt, w_glu, beta_ssm, beta_ret, w_out, g_norm2, w_rg, b_rg, w_re, b_re,
           w_gate, w_up, w_down):
    bn, seq, d = x.shape
    t = bn * seq
    x2d = x.reshape(t, d)
    mod3 = mod.reshape(bn, N_MOD, d)

    sw = w_glu.shape[0]
    proj = _inproj(x2d, mod3, g_norm1, w_in.astype(BF16), seq)
    proj3 = proj.reshape(bn, seq, proj.shape[1])

    ab_re, ab_im, bbt_re, bbt_im = _s5_disc(ssm_a_re, ssm_a_im, ssm_log_dt,
                                            jnp.swapaxes(ssm_b_re, 1, 2), jnp.swapaxes(ssm_b_im, 1, 2))
    bz, cz, a_re, a_im = _s5_pack(ab_re, ab_im, bbt_re, bbt_im, ssm_c_re, ssm_c_im)
    ypre = _s5_scan(proj3, bz, cz, a_re, a_im, ssm_d.reshape(-1)).reshape(t, sw)

    heads = (d - sw) // RET_HEAD_DIM
    cosf, sinf = _rope_tables(seq, RET_HEAD_DIM)
    gamma = 1.0 - jnp.exp2(-5.0 - jnp.arange(heads, dtype=F32))
    log_g = jnp.broadcast_to(jnp.log(gamma)[:, None, None], (heads, 1, LANES))
    yret = _retention(proj3, sw // RET_HEAD_DIM, cosf, sinf, log_g, beta_ret)

    n_route = N_GROUPS + N_EXPERTS
    wr = jnp.concatenate([w_rg, w_re, jnp.zeros((d, LANES - n_route), F32)], axis=1)
    br = jnp.concatenate([b_rg, b_re, jnp.zeros((LANES - n_route,), F32)]).reshape(1, LANES)
    x1, h2, meta, w_tok, cnt = _mix(ypre, yret.reshape(t, d - sw), x2d, mod3, w_glu.astype(BF16),
                                    beta_ssm, w_out.astype(BF16), g_norm2, wr.astype(BF16), br, seq)

    nb = -(-(t * 2) // MOE_BLOCK) + N_EXPERTS
    dest, be = _dest(meta, cnt, nb)
    slot_row = _invert(dest[:, :2].T.reshape(-1), nb * MOE_BLOCK, t)
    block_e = be[:nb + 1, 0]

    hmid = _moe_up(slot_row, block_e, h2, w_gate, w_up, nb)
    ys = _moe_down(slot_row, block_e, hmid, w_down, nb, t)
    return x1, ys, w_tok, mod3


def kernel(x, c, w_ada, b_ada, g_norm1, w_in, ssm_a_re, ssm_a_im, ssm_b_re, ssm_b_im, ssm_c_re,
           ssm_c_im, ssm_d, ssm_log_dt, w_glu, beta_ssm, beta_ret, w_out, g_norm2, w_router_group,
           b_router_group, w_router_expert, b_router_expert, w_gate, w_up, w_down, g_final):
    depth = w_ada.shape[0]
    bn, seq, d = x.shape
    assert depth == 1, "the final norm is fused into the single layer's last kernel"
    l = 0
    mod = _ada(c, w_ada[l], b_ada[l])
    x1, ys, w_tok, mod3 = _layer(
        x, mod, g_norm1[l], w_in[l], ssm_a_re[l], ssm_a_im[l], ssm_b_re[l], ssm_b_im[l],
        ssm_c_re[l], ssm_c_im[l], ssm_d[l], ssm_log_dt[l], w_glu[l], beta_ssm[l], beta_ret[l],
        w_out[l], g_norm2[l], w_router_group[l], b_router_group[l], w_router_expert[l],
        b_router_expert[l], w_gate[l], w_up[l], w_down[l])
    out = _final(x1, ys, w_tok, mod3, g_final, seq)
    return out.reshape(bn, seq, d)
```

```python
import functools
import math

import jax
import jax.numpy as jnp
from jax import lax
from jax.experimental import pallas as pl
from jax.experimental.pallas import tpu as pltpu

F32 = jnp.float32
BF16 = jnp.bfloat16
I32 = jnp.int32

SSM_GROUP = 16
SSM_STATE = 64
RET_HEAD_DIM = 128
ROPE_BASE = 10000.0
N_GROUPS = 4
EXPERTS_PER_GROUP = 8
N_EXPERTS = N_GROUPS * EXPERTS_PER_GROUP
N_MOD = 6
NORM_EPS = 1e-6
GN_EPS = 1e-5

LANES = 128
SUBLANES = 8
VMEM_LIMIT = 56 * 1024 * 1024

ROUTE_LANE0 = N_GROUPS
MOE_BLOCK = 256
MOE_NCHUNK_COLS = 256
MOE_GATHER_DEPTH = 4
MOE_SCATTER_DEPTH = 3
MOE_SPARE_SETS = 4
RET_CHUNK = 256
INPROJ_SLICES = 4
S5_TT = 128
S5_BLOCKS_PER_STEP = 4


def _cparams(sem):
    return pltpu.CompilerParams(dimension_semantics=sem, vmem_limit_bytes=VMEM_LIMIT)


def _silu(x):
    return x * jax.nn.sigmoid(x)


def _pack_bf16_pairs(lo, hi):
    lo_w = pltpu.bitcast(lo.astype(BF16).astype(F32), jnp.uint32)
    hi_w = pltpu.bitcast(hi.astype(BF16).astype(F32), jnp.uint32)
    return (lo_w >> 16) | (hi_w & jnp.uint32(0xFFFF0000))


def _unpack_bf16_pairs(w):
    return (pltpu.bitcast(w << 16, F32), pltpu.bitcast(w & jnp.uint32(0xFFFF0000), F32))


def _ada_kernel(c_ref, w_ref, b_ref, o_ref):
    s = _silu(c_ref[...])
    o_ref[...] = jnp.dot(s.astype(BF16), w_ref[...].astype(BF16),
                         preferred_element_type=F32) + b_ref[...]


def _ada(c, w, b, tn=1024):
    bn, d = c.shape
    n = w.shape[1]
    return pl.pallas_call(
        _ada_kernel,
        grid=(n // tn,),
        in_specs=[pl.BlockSpec((bn, d), lambda j: (0, 0)),
                  pl.BlockSpec((d, tn), lambda j: (0, j)),
                  pl.BlockSpec((1, tn), lambda j: (0, j))],
        out_specs=pl.BlockSpec((bn, tn), lambda j: (0, j)),
        out_shape=jax.ShapeDtypeStruct((bn, n), F32),
        compiler_params=_cparams(("arbitrary",)),
        name="ada_mod",
    )(c, w, b.reshape(1, n))


def _inproj_kernel(x0_ref, xn_ref, mod0_ref, modn_ref, g_ref, w_ref, o_ref, h_even, h_odd, *,
                   nslices):
    i = pl.program_id(0)
    j = pl.program_id(1)

    def normed(x, m):
        y = x * lax.rsqrt(jnp.mean(x * x, axis=-1, keepdims=True) + NORM_EPS) * g_ref[...]
        return (y * (1.0 + m[1:2, :]) + m[0:1, :]).astype(BF16)

    @pl.when((i == 0) & (j == 0))
    def _():
        h_even[...] = normed(x0_ref[...], mod0_ref[0])

    rows = xn_ref.shape[0] // nslices
    start = pl.multiple_of(jnp.minimum(j, nslices - 1) * rows, rows)
    sl = pl.ds(start, rows)

    def step(h_cur, h_next):
        h_next[sl, :] = normed(xn_ref[sl, :], modn_ref[0])
        o_ref[...] = jnp.dot(h_cur[...], w_ref[...], preferred_element_type=F32).astype(BF16)

    @pl.when(i % 2 == 0)
    def _():
        step(h_even, h_odd)

    @pl.when(i % 2 == 1)
    def _():
        step(h_odd, h_even)


def _inproj(x2d, mod3, g1, w_in_bf, seq, tm=1024, tn=1024):
    t, d = x2d.shape
    n = w_in_bf.shape[1]
    tm = min(tm, seq)
    tn = math.gcd(tn, n)
    kern = functools.partial(_inproj_kernel, nslices=min(INPROJ_SLICES, n // tn))
    tpb = seq // tm
    last = t // tm - 1
    nxt = lambda i: jnp.minimum(i + 1, last)
    return pl.pallas_call(
        kern,
        grid=(t // tm, n // tn),
        in_specs=[pl.BlockSpec((tm, d), lambda i, j: (0, 0), pipeline_mode=pl.Buffered(1)),
                  pl.BlockSpec((tm, d), lambda i, j: (nxt(i), 0)),
                  pl.BlockSpec((1, N_MOD, d), lambda i, j: (0, 0, 0)),
                  pl.BlockSpec((1, N_MOD, d), lambda i, j: (nxt(i) // tpb, 0, 0)),
                  pl.BlockSpec((1, d), lambda i, j: (0, 0)),
                  pl.BlockSpec((d, tn), lambda i, j: (0, j))],
        out_specs=pl.BlockSpec((tm, tn), lambda i, j: (i, j)),
        out_shape=jax.ShapeDtypeStruct((t, n), BF16),
        scratch_shapes=[pltpu.VMEM((tm, d), BF16), pltpu.VMEM((tm, d), BF16)],
        compiler_params=_cparams(("arbitrary", "arbitrary")),
        name="norm_inproj",
    )(x2d, x2d, mod3, mod3, g1.reshape(1, d), w_in_bf)


def _s5_disc_kernel(are_ref, aim_ref, ldt_ref, bre_ref, bim_ref,
                    abre_ref, abim_ref, bbre_ref, bbim_ref):
    a_re = are_ref[...]
    a_im = aim_ref[...]
    dt = jnp.exp(ldt_ref[...])
    mag = jnp.exp(dt * a_re)
    ab_re = mag * jnp.cos(dt * a_im)
    ab_im = mag * jnp.sin(dt * a_im)
    inv_abs2 = 1.0 / (a_re * a_re + a_im * a_im)
    n_re = ab_re - 1.0
    f_re = (n_re * a_re + ab_im * a_im) * inv_abs2
    f_im = (ab_im * a_re - n_re * a_im) * inv_abs2
    abre_ref[...] = ab_re
    abim_ref[...] = ab_im
    b_re = bre_ref[...]
    b_im = bim_ref[...]
    fr = f_re[:, None, :]
    fi = f_im[:, None, :]
    bbre_ref[...] = fr * b_re - fi * b_im
    bbim_ref[...] = fr * b_im + fi * b_re


def _s5_disc(a_re, a_im, log_dt, bt_re, bt_im):
    g, p = a_re.shape
    h = bt_re.shape[1]
    return pl.pallas_call(
        _s5_disc_kernel,
        out_shape=[jax.ShapeDtypeStruct((g, p), F32), jax.ShapeDtypeStruct((g, p), F32),
                   jax.ShapeDtypeStruct((g, h, p), F32), jax.ShapeDtypeStruct((g, h, p), F32)],
        name="s5_discretise",
    )(a_re, a_im, log_dt.reshape(g, 1), bt_re, bt_im)


def _s5_pack(ab_re, ab_im, bbt_re, bbt_im, c_re, c_im):
    g, h, p = bbt_re.shape
    npair = g // 2
    ppb = LANES // (2 * h)
    eye2 = jnp.eye(2, dtype=F32)
    qsel = jax.nn.one_hot(jnp.arange(npair) % ppb, ppb, dtype=F32)

    def in_mat(bbt):
        b4 = bbt.reshape(npair, 2, h, p)
        return jnp.einsum('ab,xahp->xahbp', eye2, b4).reshape(npair, 2 * h, 2 * p)

    bsmall = jnp.concatenate([in_mat(bbt_re), in_mat(bbt_im)], axis=-1)
    bz = jnp.einsum('xq,xrc->xqrc', qsel, bsmall).reshape(npair, LANES, 4 * p)

    def out_mat(c):
        c4 = jnp.swapaxes(c, 1, 2).reshape(npair, 2, p, h)
        return jnp.einsum('ab,xaph->xapbh', eye2, c4).reshape(npair, 2 * p, 2 * h)

    csmall = jnp.concatenate([out_mat(c_re), -out_mat(c_im)], axis=1)
    cz = jnp.einsum('xq,xrc->xrqc', qsel, csmall).reshape(npair, 4 * p, LANES)
    a_re = ab_re.reshape(npair // ppb, ppb, 2 * p)
    a_im = ab_im.reshape(npair // ppb, ppb, 2 * p)
    return bz.astype(BF16), cz.astype(BF16), a_re, a_im


def _s5_kernel(u_ref, bz_ref, cz_ref, are_ref, aim_ref, d_ref, o_ref, state, *bufs, ppb):
    t = pl.program_id(0)
    j = pl.program_id(1)
    bn, tt, width = u_ref.shape
    nblk = width // LANES
    rt = bn * tt

    @pl.when(t == 0)
    def _():
        for i in range(nblk):
            state[j * nblk + i] = jnp.zeros(state.shape[1:], F32)

    for i in range(nblk):
        buf = bufs[i]
        kb = j * nblk + i
        lanes = slice(i * LANES, (i + 1) * LANES)

        u = u_ref[:, :, lanes].reshape(rt, LANES)
        for q in range(ppb):
            bu = jnp.dot(u, bz_ref[i * ppb + q], preferred_element_type=F32)
            for b in range(bn):
                rows = pl.ds(b, tt, stride=bn)
                buf[2 * q, rows, :] = bu[b * tt:(b + 1) * tt, :LANES]
                buf[2 * q + 1, rows, :] = bu[b * tt:(b + 1) * tt, LANES:]

        a_re = [jnp.broadcast_to(are_ref[i, q:q + 1, :], (bn, LANES)) for q in range(ppb)]
        a_im = [jnp.broadcast_to(aim_ref[i, q:q + 1, :], (bn, LANES)) for q in range(ppb)]
        s = [state[kb, k] for k in range(2 * ppb)]
        for step in range(tt):
            rows = slice(step * bn, (step + 1) * bn)
            for q in range(ppb):
                s_re, s_im = s[2 * q], s[2 * q + 1]
                n_re = a_re[q] * s_re - a_im[q] * s_im + buf[2 * q, rows, :]
                n_im = a_re[q] * s_im + a_im[q] * s_re + buf[2 * q + 1, rows, :]
                buf[2 * q, rows, :] = n_re
                buf[2 * q + 1, rows, :] = n_im
                s[2 * q], s[2 * q + 1] = n_re, n_im
        for k in range(2 * ppb):
            state[kb, k] = s[k]

        acc = None
        for q in range(ppb):
            lhs = jnp.concatenate([buf[2 * q], buf[2 * q + 1]], axis=1).astype(BF16)
            part = jnp.dot(lhs, cz_ref[i * ppb + q], preferred_element_type=F32)
            acc = part if acc is None else acc + part
        buf[2 * ppb] = acc
        d_skip = d_ref[:, lanes]
        for b in range(bn):
            o_ref[b, :, lanes] = (buf[2 * ppb, pl.ds(b, tt, stride=bn), :]
                                  + d_skip * u_ref[b, :, lanes].astype(F32))


def _s5_scan(u3, bz, cz, a_re, a_im, d_skip, tt=S5_TT, nblk=S5_BLOCKS_PER_STEP):
    bn, seq, width = u3.shape[0], u3.shape[1], d_skip.shape[0]
    nkb = width // LANES
    ppb = bz.shape[0] // nkb
    nblk = math.gcd(nblk, nkb)
    bw = nblk * LANES
    kern = functools.partial(_s5_kernel, ppb=ppb)
    return pl.pallas_call(
        kern,
        grid=(seq // tt, nkb // nblk),
        in_specs=[pl.BlockSpec((bn, tt, bw), lambda t, k: (0, t, k)),
                  pl.BlockSpec((nblk * ppb,) + bz.shape[1:], lambda t, k: (k, 0, 0)),
                  pl.BlockSpec((nblk * ppb,) + cz.shape[1:], lambda t, k: (k, 0, 0)),
                  pl.BlockSpec((nblk,) + a_re.shape[1:], lambda t, k: (k, 0, 0)),
                  pl.BlockSpec((nblk,) + a_im.shape[1:], lambda t, k: (k, 0, 0)),
                  pl.BlockSpec((1, bw), lambda t, k: (0, k))],
        out_specs=pl.BlockSpec((bn, tt, bw), lambda t, k: (0, t, k)),
        out_shape=jax.ShapeDtypeStruct((bn, seq, width), F32),
        scratch_shapes=[pltpu.VMEM((nkb, 2 * ppb, bn, LANES), F32)]
        + [pltpu.VMEM((2 * ppb + 1, bn * tt, LANES), F32) for _ in range(nblk)],
        compiler_params=_cparams(("arbitrary", "arbitrary")),
        name="s5_scan",
    )(u3, bz, cz, a_re, a_im, d_skip.reshape(1, width))


def _ret_kernel(q_ref, k_ref, v_ref, g_ref, cq_ref, sq_ref, ck_ref, sk_ref, lg_ref, beta_ref, o_ref,
                q_scr, k_scr, kzt_scr, y_scr, *, chunk):
    seq, dh = q_ref.shape[1], q_ref.shape[2]
    lg = lg_ref[0][:, :1]
    ri = lax.broadcasted_iota(I32, (chunk, chunk), 0)
    ci = lax.broadcasted_iota(I32, (chunk, chunk), 1)
    rel = (ri - ci).astype(F32)
    decay = jnp.where(rel >= 0, jnp.exp(lg * jnp.maximum(rel, 0.0)), 0.0)
    idx = lax.broadcasted_iota(I32, (chunk, 1), 0).astype(F32)
    xi = jnp.exp(lg * (idx + 1.0))
    g_chunk = jnp.exp(lg * float(chunk))
    beta = beta_ref[...]
    half = dh // 2
    pr = lax.broadcasted_iota(I32, (dh, dh), 0)
    pc = lax.broadcasted_iota(I32, (dh, dh), 1)
    swap = jnp.where(((pr + half) & (dh - 1)) == pc, 1.0, 0.0).astype(BF16)

    def rope(x_ref, cos_ref, sin_ref, sl):
        xb = x_ref[0, sl, :]
        rolled = jnp.dot(xb, swap, preferred_element_type=F32).astype(BF16)
        return xb * cos_ref[sl, :] + rolled * sin_ref[sl, :]

    zeta = jnp.exp(lg * (chunk - 1.0 - idx))
    for n in range(seq // chunk):
        sl = slice(n * chunk, (n + 1) * chunk)
        q_scr[sl, :] = rope(q_ref, cq_ref, sq_ref, sl)
        kb = rope(k_ref, ck_ref, sk_ref, sl)
        k_scr[sl, :] = kb
        kzt_scr[:, sl] = (kb.astype(F32) * zeta).T.astype(BF16)

    decay = decay.astype(BF16)
    r = jnp.zeros((dh, dh), F32)
    for n in range(seq // chunk):
        sl = slice(n * chunk, (n + 1) * chunk)
        qb = q_scr[sl, :]
        v = v_ref[0, sl, :]
        s = lax.dot_general(qb, k_scr[sl, :], (((1,), (1,)), ((), ())),
                            preferred_element_type=F32).astype(BF16) * decay
        y = jnp.dot(s, v, preferred_element_type=F32)
        y_scr[sl, :] = y + jnp.dot(qb, r.astype(BF16), preferred_element_type=F32) * xi
        r = r * g_chunk + jnp.dot(kzt_scr[:, sl], v, preferred_element_type=F32)

    y = y_scr[...]
    mu = jnp.mean(y, axis=-1, keepdims=True)
    yc = y - mu
    var = jnp.mean(yc * yc, axis=-1, keepdims=True)
    yn = yc * lax.rsqrt(var + GN_EPS) * beta
    o_ref[0] = (_silu(g_ref[0].astype(F32)) * yn).astype(BF16)


def _retention(proj3, col0, cosf, sinf, log_g, beta_ret, chunk=RET_CHUNK):
    bn, seq, _ = proj3.shape
    dh = RET_HEAD_DIM
    heads = beta_ret.shape[0] // dh
    blk = lambda off: pl.BlockSpec((1, seq, dh), lambda b, h, off=off: (b, 0, col0 + off + h))
    k_scale = dh ** -0.5
    kern = functools.partial(_ret_kernel, chunk=chunk)
    return pl.pallas_call(
        kern,
        grid=(bn, heads),
        in_specs=[blk(0), blk(heads), blk(2 * heads), blk(3 * heads),
                  pl.BlockSpec((seq, dh), lambda b, h: (0, 0)),
                  pl.BlockSpec((seq, dh), lambda b, h: (0, 0)),
                  pl.BlockSpec((seq, dh), lambda b, h: (0, 0)),
                  pl.BlockSpec((seq, dh), lambda b, h: (0, 0)),
                  pl.BlockSpec((1, 1, LANES), lambda b, h: (h, 0, 0)),
                  pl.BlockSpec((1, dh), lambda b, h: (0, h))],
        out_specs=pl.BlockSpec((1, seq, dh), lambda b, h: (b, 0, h)),
        out_shape=jax.ShapeDtypeStruct((bn, seq, heads * dh), BF16),
        scratch_shapes=[pltpu.VMEM((seq, dh), BF16), pltpu.VMEM((seq, dh), BF16),
                        pltpu.VMEM((dh, seq), BF16), pltpu.VMEM((seq, dh), F32)],
        compiler_params=_cparams(("arbitrary", "arbitrary")),
        name="retention",
    )(proj3, proj3, proj3, proj3, cosf.astype(BF16), sinf.astype(BF16),
      (cosf * k_scale).astype(BF16), (sinf * k_scale).astype(BF16),
      log_g, beta_ret.reshape(1, heads * dh))


def _gelu_tanh(x):
    return 0.5 * x * (1.0 + jnp.tanh(math.sqrt(2.0 / math.pi) * (x + 0.044715 * (x * x * x))))


def _route_rows(lg, carry):
    tm = lg.shape[0]
    lane = lax.broadcasted_iota(I32, (tm, LANES), 1).astype(F32)
    neg = jnp.float32(-jnp.inf)

    def first_max(vals):
        m = jnp.max(vals, axis=-1, keepdims=True)
        idx = jnp.min(jnp.where(vals == m, lane, float(LANES)), axis=-1, keepdims=True)
        return m, idx

    gl = jnp.where(lane < N_GROUPS, lg, neg)
    gmax, gsel = first_max(gl)
    g_w = 1.0 / jnp.sum(jnp.exp(gl - gmax), axis=-1, keepdims=True)
    lo = ROUTE_LANE0 + gsel * EXPERTS_PER_GROUP
    el = jnp.where((lane >= lo) & (lane < lo + EXPERTS_PER_GROUP), lg, neg)
    m1, i1 = first_max(el)
    m2, i2 = first_max(jnp.where(lane == i1, neg, el))
    e21 = jnp.exp(m2 - m1)
    w1 = g_w / (1.0 + e21)
    w2 = g_w * e21 / (1.0 + e21)

    sel1 = lane == i1
    sel2 = lane == i2
    onehot = jnp.where(sel1 | sel2, 1.0, 0.0).astype(BF16)
    ri = lax.broadcasted_iota(I32, (tm, tm), 0)
    ci = lax.broadcasted_iota(I32, (tm, tm), 1)
    tri = jnp.where(ci < ri, 1.0, 0.0).astype(BF16)
    base = carry[0:1, :]
    excl = jnp.dot(tri, onehot, preferred_element_type=F32) + base
    r1 = jnp.sum(jnp.where(sel1, excl, 0.0), axis=-1, keepdims=True)
    r2 = jnp.sum(jnp.where(sel2, excl, 0.0), axis=-1, keepdims=True)
    total = base + jnp.sum(jnp.where(sel1 | sel2, 1.0, 0.0), axis=0, keepdims=True)
    meta = (jnp.where(lane == 0, i1, 0.0) + jnp.where(lane == 1, i2, 0.0)
            + jnp.where(lane == 2, r1, 0.0) + jnp.where(lane == 3, r2, 0.0))
    w = jnp.where(lane == 0, w1, 0.0) + jnp.where(lane == 1, w2, 0.0)
    return meta, w, total


def _mix_kernel(ys_ref, yr_ref, x_ref, mod_ref, wglu_ref, bssm_ref, wout_ref, g2_ref,
                wr_ref, br_ref, x1_ref, h2_ref, meta_ref, w_ref, cnt_ref, carry):
    @pl.when(pl.program_id(0) == 0)
    def _():
        carry[...] = jnp.zeros(carry.shape, F32)

    sw = ys_ref.shape[1]
    z = _gelu_tanh(ys_ref[...])
    gate = jax.nn.sigmoid(jnp.dot(z.astype(BF16), wglu_ref[...], preferred_element_type=F32))
    o = z * gate
    y_ssm = o * lax.rsqrt(jnp.mean(o * o, axis=-1, keepdims=True) + NORM_EPS) * bssm_ref[...]
    mixed = (jnp.dot(y_ssm.astype(BF16), wout_ref[:sw, :], preferred_element_type=F32)
             + jnp.dot(yr_ref[...], wout_ref[sw:, :], preferred_element_type=F32))
    m = mod_ref[0]
    x1 = x_ref[...] + m[2:3, :] * mixed
    x1_ref[...] = x1
    y = x1 * lax.rsqrt(jnp.mean(x1 * x1, axis=-1, keepdims=True) + NORM_EPS) * g2_ref[...]
    h2 = y * (1.0 + m[4:5, :]) + m[3:4, :]
    half = h2.shape[1] // 2
    h2_ref[...] = _pack_bf16_pairs(h2[:, :half], h2[:, half:])
    logits = jnp.dot(h2.astype(BF16), wr_ref[...], preferred_element_type=F32) + br_ref[...]
    meta, w, total = _route_rows(logits, carry)
    meta_ref[...] = meta
    w_ref[...] = w
    carry[...] = jnp.broadcast_to(total, carry.shape)
    cnt_ref[...] = jnp.broadcast_to(total, cnt_ref.shape)


def _mix(ypre, yret, x2d, mod3, wglu_bf, beta_ssm, wout_bf, g2, wr_bf, br, seq, tm=256):
    t, d = x2d.shape
    sw = wglu_bf.shape[0]
    tpb = seq // tm
    const = lambda shape: pl.BlockSpec(shape, lambda i: (0,) * len(shape),
                                       pipeline_mode=pl.Buffered(1))
    return pl.pallas_call(
        _mix_kernel,
        grid=(t // tm,),
        in_specs=[pl.BlockSpec((tm, sw), lambda i: (i, 0)),
                  pl.BlockSpec((tm, d - sw), lambda i: (i, 0)),
                  pl.BlockSpec((tm, d), lambda i: (i, 0)),
                  pl.BlockSpec((1, N_MOD, d), lambda i: (i // tpb, 0, 0)),
                  const((sw, sw)), const((1, sw)), const((d, d)), const((1, d)),
                  const((d, LANES)), const((1, LANES))],
        out_specs=[pl.BlockSpec((tm, d), lambda i: (i, 0)),
                   pl.BlockSpec((tm, d // 2), lambda i: (i, 0)),
                   pl.BlockSpec((tm, LANES), lambda i: (i, 0)),
                   pl.BlockSpec((tm, LANES), lambda i: (i, 0)),
                   pl.BlockSpec((SUBLANES, LANES), lambda i: (0, 0))],
        out_shape=[jax.ShapeDtypeStruct((t, d), F32),
                   jax.ShapeDtypeStruct((t, d // 2), jnp.uint32),
                   jax.ShapeDtypeStruct((t, LANES), F32),
                   jax.ShapeDtypeStruct((t, LANES), F32),
                   jax.ShapeDtypeStruct((SUBLANES, LANES), F32)],
        scratch_shapes=[pltpu.VMEM((SUBLANES, LANES), F32)],
        compiler_params=_cparams(("arbitrary",)),
        name="mix_outproj_router",
    )(ypre, yret, x2d, mod3, wglu_bf, beta_ssm.reshape(1, sw), wout_bf, g2.reshape(1, d),
      wr_bf, br)


def _lane_cumsum(x):
    lane = lax.broadcasted_iota(I32, x.shape, 1)
    sh = 1
    while sh < LANES:
        x = x + jnp.where(lane >= sh, pltpu.roll(x, sh, axis=1), 0)
        sh *= 2
    return x


def _dest_kernel(meta_ref, cnt_ref, dest_ref, be_ref, *, nb):
    tm = meta_ref.shape[0]
    shift = MOE_BLOCK.bit_length() - 1
    cnt = cnt_ref[...].astype(I32)
    padded = ((cnt + (MOE_BLOCK - 1)) >> shift) << shift
    pend_i = _lane_cumsum(padded)
    pend = pend_i.astype(F32)[0:1, :]
    poff = (pend_i - padded).astype(F32)[0:1, :]
    meta = meta_ref[...]
    lane = lax.broadcasted_iota(I32, (tm, LANES), 1).astype(F32)
    pick = lambda col: jnp.sum(jnp.where(lane == col, meta, 0.0), axis=-1, keepdims=True)
    i1, i2, r1, r2 = pick(0), pick(1), pick(2), pick(3)
    d1 = jnp.sum(jnp.where(lane == i1, poff, 0.0), axis=-1, keepdims=True) + r1
    d2 = jnp.sum(jnp.where(lane == i2, poff, 0.0), axis=-1, keepdims=True) + r2
    dest_ref[...] = (jnp.where(lane == 0, d1, 0.0) + jnp.where(lane == 1, d2, 0.0)).astype(I32)

    nrow = be_ref.shape[0]
    blk = lax.broadcasted_iota(I32, (nrow, LANES), 0).astype(F32)
    lane_b = lax.broadcasted_iota(I32, (nrow, LANES), 1).astype(F32)
    is_e = (lane_b >= ROUTE_LANE0) & (lane_b < ROUTE_LANE0 + N_EXPERTS)
    below = jnp.sum(jnp.where(is_e & (pend <= blk * MOE_BLOCK), 1.0, 0.0), axis=-1, keepdims=True)
    block_e = jnp.minimum(below, N_EXPERTS - 1.0)
    used = jnp.sum(jnp.where(lane_b == ROUTE_LANE0 + N_EXPERTS - 1, pend, 0.0),
                   axis=-1, keepdims=True) * (1.0 / MOE_BLOCK)
    be = jnp.where(blk[:, :1] == nb, used, block_e)
    be_ref[...] = jnp.broadcast_to(be, (nrow, LANES)).astype(I32)


def _dest(meta, cnt, nb, tm=512):
    t = meta.shape[0]
    nrow = -(-(nb + 1) // SUBLANES) * SUBLANES
    kern = functools.partial(_dest_kernel, nb=nb)
    return pl.pallas_call(
        kern,
        grid=(t // tm,),
        in_specs=[pl.BlockSpec((tm, LANES), lambda i: (i, 0)),
                  pl.BlockSpec((SUBLANES, LANES), lambda i: (0, 0))],
        out_specs=[pl.BlockSpec((tm, LANES), lambda i: (i, 0)),
                   pl.BlockSpec((nrow, LANES), lambda i: (0, 0))],
        out_shape=[jax.ShapeDtypeStruct((t, LANES), I32),
                   jax.ShapeDtypeStruct((nrow, LANES), I32)],
        compiler_params=_cparams(("arbitrary",)),
        name="route_dest",
    )(meta, cnt)


def _invert_kernel(dest_ref, fill_hbm, row_ref, sem):
    cp = pltpu.make_async_copy(fill_hbm, row_ref, sem)
    cp.start()
    cp.wait()

    def put(a, c):
        row_ref[dest_ref[a]] = a
        return c

    lax.fori_loop(0, dest_ref.shape[0], put, 0, unroll=8)


def _invert(dest_flat, n_slot, t):
    spare = MOE_SPARE_SETS * MOE_BLOCK
    fill = 2 * t + (jnp.arange(n_slot, dtype=I32) & (spare - 1))
    return pl.pallas_call(
        _invert_kernel,
        in_specs=[pl.BlockSpec(memory_space=pltpu.SMEM), pl.BlockSpec(memory_space=pl.ANY)],
        out_specs=pl.BlockSpec(memory_space=pltpu.SMEM),
        out_shape=jax.ShapeDtypeStruct((n_slot,), I32),
        scratch_shapes=[pltpu.SemaphoreType.DMA(())],
        name="route_invert",
    )(dest_flat, fill)


def _expert_weights(b, used, be_ref, w_hbms, wst, wsem, run_ref, load):
    def fetch(e, p):
        for i, w in enumerate(w_hbms):
            pltpu.make_async_copy(w.at[e], wst.at[p, i], wsem.at[p]).start(priority=1)

    @pl.when((b == 0) & (used > 0))
    def _():
        run_ref[0] = 0
        fetch(be_ref[0], 0)

    bc = jnp.minimum(b, be_ref.shape[0] - 2)
    new_expert = (b == 0) | (be_ref[bc] != be_ref[jnp.maximum(bc - 1, 0)])

    @pl.when(new_expert & (b < used))
    def _():
        p = run_ref[0]
        for i, w in enumerate(w_hbms):
            pltpu.make_async_copy(w.at[0], wst.at[p, i], wsem.at[p]).wait()
        load(p)
        e = be_ref[b]
        nxt = lax.while_loop(lambda i: (i < used) & (be_ref[i] == e), lambda i: i + 1, b + 1)

        @pl.when(nxt < used)
        def _():
            fetch(be_ref[nxt], 1 - p)

        run_ref[0] = 1 - p


def _moe_up_kernel(row_ref, be_ref, h2_hbm, wg_hbm, wu_hbm, o_ref, wst, wbuf, xbuf, xb_scr, sem,
                   wsem, run_ref, *, nb):
    b = pl.program_id(0)
    used = be_ref[nb]
    blk = MOE_BLOCK
    fe = wg_hbm.shape[2]
    t = h2_hbm.shape[0]

    def gather_rows(block, rows):
        slot = block % MOE_GATHER_DEPTH
        base = block * blk
        for r in rows:
            row = row_ref[base + r]
            tok = jnp.where(row >= t, row - t, row)
            tok = jnp.where(tok >= t, 0, tok)
            pltpu.make_async_copy(h2_hbm.at[pl.ds(tok, 1)], xbuf.at[slot, pl.ds(r, 1)],
                                  sem.at[slot]).start()

    ahead = MOE_GATHER_DEPTH - 1

    @pl.when(b == 0)
    def _():
        for first in range(ahead):
            @pl.when(first < used)
            def _():
                gather_rows(first, range(blk))

    def load(p):
        wbuf[:, :fe] = wst[p, 0].astype(BF16)
        wbuf[:, fe:] = wst[p, 1].astype(BF16)

    _expert_weights(b, used, be_ref, (wg_hbm, wu_hbm), wst, wsem, run_ref, load)

    nchunk = fe // MOE_NCHUNK_COLS
    rows_per = blk // nchunk

    def step(prefetch):
        slot = b % MOE_GATHER_DEPTH
        pltpu.make_async_copy(xbuf.at[slot], xbuf.at[slot], sem.at[slot]).wait()
        lo, hi = _unpack_bf16_pairs(xbuf[slot])
        half = lo.shape[1]
        xb_scr[:, :half] = lo.astype(BF16)
        xb_scr[:, half:] = hi.astype(BF16)
        for c in range(nchunk):
            if prefetch:
                gather_rows(b + ahead, range(c * rows_per, (c + 1) * rows_per))
            cols = slice(c * MOE_NCHUNK_COLS, (c + 1) * MOE_NCHUNK_COLS)
            ucols = slice(fe + c * MOE_NCHUNK_COLS, fe + (c + 1) * MOE_NCHUNK_COLS)
            g = jnp.dot(xb_scr[...], wbuf[:, cols], preferred_element_type=F32)
            u = jnp.dot(xb_scr[...], wbuf[:, ucols], preferred_element_type=F32)
            o_ref[:, cols] = (_silu(g) * u).astype(BF16)

    @pl.when(b + ahead < used)
    def _():
        step(True)

    @pl.when((b < used) & (b + ahead >= used))
    def _():
        step(False)

    @pl.when(b >= used)
    def _():
        o_ref[...] = jnp.zeros(o_ref.shape, BF16)


def _moe_up(slot_row, block_e, h2, w_gate, w_up, nb):
    d, fe = w_gate.shape[1], w_gate.shape[2]
    kern = functools.partial(_moe_up_kernel, nb=nb)
    grid_spec = pltpu.PrefetchScalarGridSpec(
        num_scalar_prefetch=2,
        grid=(nb,),
        in_specs=[pl.BlockSpec(memory_space=pl.ANY),
                  pl.BlockSpec(memory_space=pl.ANY),
                  pl.BlockSpec(memory_space=pl.ANY)],
        out_specs=pl.BlockSpec((MOE_BLOCK, fe), lambda b, s, e: (b, 0)),
        scratch_shapes=[pltpu.VMEM((2, 2, d, fe), F32),
                        pltpu.VMEM((d, 2 * fe), BF16),
                        pltpu.VMEM((MOE_GATHER_DEPTH, MOE_BLOCK, d // 2), jnp.uint32),
                        pltpu.VMEM((MOE_BLOCK, d), BF16),
                        pltpu.SemaphoreType.DMA((MOE_GATHER_DEPTH,)),
                        pltpu.SemaphoreType.DMA((2,)),
                        pltpu.SMEM((1,), I32)],
    )
    return pl.pallas_call(
        kern,
        grid_spec=grid_spec,
        out_shape=jax.ShapeDtypeStruct((nb * MOE_BLOCK, fe), BF16),
        compiler_params=_cparams(("arbitrary",)),
        name="moe_up",
    )(slot_row, block_e, h2, w_gate, w_up)


def _moe_down_kernel(row_ref, be_ref, hm_ref, wd_hbm, ys_hbm, wst, wbuf, obuf, sem, wsem, run_ref,
                     *, nb, t):
    b = pl.program_id(0)
    used = be_ref[nb]
    blk = MOE_BLOCK

    def wait_slot(slot):
        pltpu.make_async_copy(obuf.at[slot], obuf.at[slot], sem.at[slot]).wait()

    depth = obuf.shape[0]

    @pl.when(b == 0)
    def _():
        obuf[0] = jnp.zeros(obuf.shape[1:], jnp.uint32)
        for k in range(MOE_SPARE_SETS):
            spare = ys_hbm.at[pl.ds(2 * t + k * blk, blk)]
            pltpu.make_async_copy(obuf.at[0], spare, sem.at[depth]).start()
        for k in range(MOE_SPARE_SETS):
            spare = ys_hbm.at[pl.ds(2 * t + k * blk, blk)]
            pltpu.make_async_copy(obuf.at[0], spare, sem.at[depth]).wait()

    @pl.when((b >= depth) & (b - depth < used))
    def _():
        wait_slot(b % depth)

    def load(p):
        wbuf[...] = wst[p, 0].astype(BF16)

    _expert_weights(b, used, be_ref, (wd_hbm,), wst, wsem, run_ref, load)

    half = wbuf.shape[1] // 2
    nchunk = half // MOE_NCHUNK_COLS
    rows_per = blk // nchunk

    def step(scatter, matmul):
        slot = (b - 1) % depth
        base = (b - 1) * blk
        for c in range(nchunk):
            if scatter:
                for r in range(c * rows_per, (c + 1) * rows_per):
                    pltpu.make_async_copy(obuf.at[slot, pl.ds(r, 1)],
                                          ys_hbm.at[pl.ds(row_ref[base + r], 1)],
                                          sem.at[slot]).start(priority=r % 2)
            if matmul:
                cols = slice(c * MOE_NCHUNK_COLS, (c + 1) * MOE_NCHUNK_COLS)
                hcols = slice(half + c * MOE_NCHUNK_COLS, half + (c + 1) * MOE_NCHUNK_COLS)
                lo = jnp.dot(hm_ref[...], wbuf[:, cols], preferred_element_type=F32)
                hi = jnp.dot(hm_ref[...], wbuf[:, hcols], preferred_element_type=F32)
                obuf[b % depth, :, cols] = _pack_bf16_pairs(lo, hi)

    @pl.when((b >= 1) & (b < used))
    def _():
        step(True, True)

    @pl.when((b == 0) & (b < used))
    def _():
        step(False, True)

    @pl.when((b >= 1) & (b == used))
    def _():
        step(True, False)

    @pl.when(b == nb)
    def _():
        for k in range(1, depth):
            @pl.when(nb - k < used)
            def _():
                wait_slot((nb - k) % depth)


def _moe_down(slot_row, block_e, hmid, w_down, nb, t):
    fe, d = w_down.shape[1], w_down.shape[2]
    kern = functools.partial(_moe_down_kernel, nb=nb, t=t)
    grid_spec = pltpu.PrefetchScalarGridSpec(
        num_scalar_prefetch=2,
        grid=(nb + 1,),
        in_specs=[pl.BlockSpec((MOE_BLOCK, fe), lambda b, s, e: (jnp.minimum(b, nb - 1), 0)),
                  pl.BlockSpec(memory_space=pl.ANY)],
        out_specs=pl.BlockSpec(memory_space=pl.ANY),
        scratch_shapes=[pltpu.VMEM((2, 1, fe, d), F32),
                        pltpu.VMEM((fe, d), BF16),
                        pltpu.VMEM((MOE_SCATTER_DEPTH, MOE_BLOCK, d // 2), jnp.uint32),
                        pltpu.SemaphoreType.DMA((MOE_SCATTER_DEPTH + 1,)),
                        pltpu.SemaphoreType.DMA((2,)),
                        pltpu.SMEM((1,), I32)],
    )
    return pl.pallas_call(
        kern,
        grid_spec=grid_spec,
        out_shape=jax.ShapeDtypeStruct((2 * t + MOE_SPARE_SETS * MOE_BLOCK, d // 2), jnp.uint32),
        compiler_params=_cparams(("arbitrary",)),
        name="moe_down",
    )(slot_row, block_e, hmid, w_down)


def _final_kernel(x1_ref, y0_ref, y1_ref, w_ref, mod_ref, gf_ref, o_ref):
    w = w_ref[...]
    lo0, hi0 = _unpack_bf16_pairs(y0_ref[...])
    lo1, hi1 = _unpack_bf16_pairs(y1_ref[...])
    y = jnp.concatenate([w[:, 0:1] * lo0 + w[:, 1:2] * lo1,
                         w[:, 0:1] * hi0 + w[:, 1:2] * hi1], axis=1)
    x2 = x1_ref[...] + mod_ref[0][5:6, :] * y
    o_ref[...] = x2 * lax.rsqrt(jnp.mean(x2 * x2, axis=-1, keepdims=True) + NORM_EPS) * gf_ref[...]


def _final(x1, ys, w, mod3, g_final, seq, tm=512):
    t, d = x1.shape
    tpb = seq // tm
    return pl.pallas_call(
        _final_kernel,
        grid=(t // tm,),
        in_specs=[pl.BlockSpec((tm, d), lambda i: (i, 0)),
                  pl.BlockSpec((tm, d // 2), lambda i: (i, 0)),
                  pl.BlockSpec((tm, d // 2), lambda i: (i + t // tm, 0)),
                  pl.BlockSpec((tm, LANES), lambda i: (i, 0)),
                  pl.BlockSpec((1, N_MOD, d), lambda i: (i // tpb, 0, 0)),
                  pl.BlockSpec((1, d), lambda i: (0, 0))],
        out_specs=pl.BlockSpec((tm, d), lambda i: (i, 0)),
        out_shape=jax.ShapeDtypeStruct((t, d), F32),
        compiler_params=_cparams(("arbitrary",)),
        name="combine_final_norm",
    )(x1, ys, ys, w, mod3, g_final.reshape(1, d))


def _rope_tables(seq, dh):
    half = dh // 2
    inv_freq = 1.0 / (ROPE_BASE ** (jnp.arange(half, dtype=F32) * 2.0 / dh))
    ang = jnp.arange(seq, dtype=F32)[:, None] * inv_freq[None, :]
    cos, sin = jnp.cos(ang), jnp.sin(ang)
    return jnp.concatenate([cos, cos], axis=1), jnp.concatenate([-sin, sin], axis=1)


def _layer(x, mod, g_norm1, w_in, ssm_a_re, ssm_a_im, ssm_b_re, ssm_b_im, ssm_c_re, ssm_c_im,
           ssm_d, ssm_log_dt, w_glu, beta_ssm, beta_ret, w_out, g_norm2, w_rg, b_rg, w_re, b_re,
           w_gate, w_up, w_down):
    bn, seq, d = x.shape
    t = bn * seq
    x2d = x.reshape(t, d)
    mod3 = mod.reshape(bn, N_MOD, d)

    sw = w_glu.shape[0]
    proj = _inproj(x2d, mod3, g_norm1, w_in.astype(BF16), seq)
    proj3 = proj.reshape(bn, seq, proj.shape[1])

    ab_re, ab_im, bbt_re, bbt_im = _s5_disc(ssm_a_re, ssm_a_im, ssm_log_dt,
                                            jnp.swapaxes(ssm_b_re, 1, 2), jnp.swapaxes(ssm_b_im, 1, 2))
    bz, cz, a_re, a_im = _s5_pack(ab_re, ab_im, bbt_re, bbt_im, ssm_c_re, ssm_c_im)
    ypre = _s5_scan(proj3, bz, cz, a_re, a_im, ssm_d.reshape(-1)).reshape(t, sw)

    heads = (d - sw) // RET_HEAD_DIM
    cosf, sinf = _rope_tables(seq, RET_HEAD_DIM)
    gamma = 1.0 - jnp.exp2(-5.0 - jnp.arange(heads, dtype=F32))
    log_g = jnp.broadcast_to(jnp.log(gamma)[:, None, None], (heads, 1, LANES))
    yret = _retention(proj3, sw // RET_HEAD_DIM, cosf, sinf, log_g, beta_ret)

    n_route = N_GROUPS + N_EXPERTS
    wr = jnp.concatenate([w_rg, w_re, jnp.zeros((d, LANES - n_route), F32)], axis=1)
    br = jnp.concatenate([b_rg, b_re, jnp.zeros((LANES - n_route,), F32)]).reshape(1, LANES)
    x1, h2, meta, w_tok, cnt = _mix(ypre, yret.reshape(t, d - sw), x2d, mod3, w_glu.astype(BF16),
                                    beta_ssm, w_out.astype(BF16), g_norm2, wr.astype(BF16), br, seq)

    nb = -(-(t * 2) // MOE_BLOCK) + N_EXPERTS
    dest, be = _dest(meta, cnt, nb)
    slot_row = _invert(dest[:, :2].T.reshape(-1), nb * MOE_BLOCK, t)
    block_e = be[:nb + 1, 0]

    hmid = _moe_up(slot_row, block_e, h2, w_gate, w_up, nb)
    ys = _moe_down(slot_row, block_e, hmid, w_down, nb, t)
    return x1, ys, w_tok, mod3


def kernel(x, c, w_ada, b_ada, g_norm1, w_in, ssm_a_re, ssm_a_im, ssm_b_re, ssm_b_im, ssm_c_re,
           ssm_c_im, ssm_d, ssm_log_dt, w_glu, beta_ssm, beta_ret, w_out, g_norm2, w_router_group,
           b_router_group, w_router_expert, b_router_expert, w_gate, w_up, w_down, g_final):
    depth = w_ada.shape[0]
    bn, seq, d = x.shape
    assert depth == 1, "the final norm is fused into the single layer's last kernel"
    l = 0
    mod = _ada(c, w_ada[l], b_ada[l])
    x1, ys, w_tok, mod3 = _layer(
        x, mod, g_norm1[l], w_in[l], ssm_a_re[l], ssm_a_im[l], ssm_b_re[l], ssm_b_im[l],
        ssm_c_re[l], ssm_c_im[l], ssm_d[l], ssm_log_dt[l], w_glu[l], beta_ssm[l], beta_ret[l],
        w_out[l], g_norm2[l], w_router_group[l], b_router_group[l], w_router_expert[l],
        b_router_expert[l], w_gate[l], w_up[l], w_down[l])
    out = _final(x1, ys, w_tok, mod3, g_final, seq)
    return out.reshape(bn, seq, d)
```

```python
import functools
import math

import jax
import jax.numpy as jnp
from jax import lax
from jax.experimental import pallas as pl
from jax.experimental.pallas import tpu as pltpu

F32 = jnp.float32
BF16 = jnp.bfloat16
I32 = jnp.int32

SSM_GROUP = 16
SSM_STATE = 64
RET_HEAD_DIM = 128
ROPE_BASE = 10000.0
N_GROUPS = 4
EXPERTS_PER_GROUP = 8
N_EXPERTS = N_GROUPS * EXPERTS_PER_GROUP
N_MOD = 6
NORM_EPS = 1e-6
GN_EPS = 1e-5

LANES = 128
SUBLANES = 8
VMEM_LIMIT = 56 * 1024 * 1024

ROUTE_LANE0 = N_GROUPS
MOE_BLOCK = 256
MOE_NCHUNK_COLS = 256
MOE_GATHER_DEPTH = 4
MOE_SCATTER_DEPTH = 3
MOE_SPARE_SETS = 4
RET_CHUNK = 256
INPROJ_SLICES = 4
MIX_SUBTILES = 2
S5_TT = 128
S5_BLOCKS_PER_STEP = 4


def _cparams(sem):
    return pltpu.CompilerParams(dimension_semantics=sem, vmem_limit_bytes=VMEM_LIMIT)


def _silu(x):
    return x * jax.nn.sigmoid(x)


def _pack_bf16_pairs(lo, hi):
    lo_w = pltpu.bitcast(lo.astype(BF16).astype(F32), jnp.uint32)
    hi_w = pltpu.bitcast(hi.astype(BF16).astype(F32), jnp.uint32)
    return (lo_w >> 16) | (hi_w & jnp.uint32(0xFFFF0000))


def _unpack_bf16_pairs(w):
    return (pltpu.bitcast(w << 16, F32), pltpu.bitcast(w & jnp.uint32(0xFFFF0000), F32))


def _ada_kernel(c_ref, w_ref, b_ref, o_ref):
    s = _silu(c_ref[...])
    o_ref[...] = jnp.dot(s.astype(BF16), w_ref[...].astype(BF16),
                         preferred_element_type=F32) + b_ref[...]


def _ada(c, w, b, tn=1024):
    bn, d = c.shape
    n = w.shape[1]
    return pl.pallas_call(
        _ada_kernel,
        grid=(n // tn,),
        in_specs=[pl.BlockSpec((bn, d), lambda j: (0, 0)),
                  pl.BlockSpec((d, tn), lambda j: (0, j)),
                  pl.BlockSpec((1, tn), lambda j: (0, j))],
        out_specs=pl.BlockSpec((bn, tn), lambda j: (0, j)),
        out_shape=jax.ShapeDtypeStruct((bn, n), F32),
        compiler_params=_cparams(("arbitrary",)),
        name="ada_mod",
    )(c, w, b.reshape(1, n))


def _inproj_kernel(x0_ref, xn_ref, mod0_ref, modn_ref, g_ref, w_ref, o_ref, h_even, h_odd, *,
                   nslices):
    i = pl.program_id(0)
    j = pl.program_id(1)

    def normed(x, m):
        y = x * lax.rsqrt(jnp.mean(x * x, axis=-1, keepdims=True) + NORM_EPS) * g_ref[...]
        return (y * (1.0 + m[1:2, :]) + m[0:1, :]).astype(BF16)

    @pl.when((i == 0) & (j == 0))
    def _():
        h_even[...] = normed(x0_ref[...], mod0_ref[0])

    rows = xn_ref.shape[0] // nslices
    start = pl.multiple_of(jnp.minimum(j, nslices - 1) * rows, rows)
    sl = pl.ds(start, rows)

    def step(h_cur, h_next):
        h_next[sl, :] = normed(xn_ref[sl, :], modn_ref[0])
        o_ref[...] = jnp.dot(h_cur[...], w_ref[...], preferred_element_type=F32).astype(BF16)

    @pl.when(i % 2 == 0)
    def _():
        step(h_even, h_odd)

    @pl.when(i % 2 == 1)
    def _():
        step(h_odd, h_even)


def _inproj(x2d, mod3, g1, w_in_bf, seq, tm=1024, tn=1024):
    t, d = x2d.shape
    n = w_in_bf.shape[1]
    tm = min(tm, seq)
    tn = math.gcd(tn, n)
    kern = functools.partial(_inproj_kernel, nslices=min(INPROJ_SLICES, n // tn))
    tpb = seq // tm
    last = t // tm - 1
    nxt = lambda i: jnp.minimum(i + 1, last)
    return pl.pallas_call(
        kern,
        grid=(t // tm, n // tn),
        in_specs=[pl.BlockSpec((tm, d), lambda i, j: (0, 0), pipeline_mode=pl.Buffered(1)),
                  pl.BlockSpec((tm, d), lambda i, j: (nxt(i), 0)),
                  pl.BlockSpec((1, N_MOD, d), lambda i, j: (0, 0, 0)),
                  pl.BlockSpec((1, N_MOD, d), lambda i, j: (nxt(i) // tpb, 0, 0)),
                  pl.BlockSpec((1, d), lambda i, j: (0, 0)),
                  pl.BlockSpec((d, tn), lambda i, j: (0, j))],
        out_specs=pl.BlockSpec((tm, tn), lambda i, j: (i, j)),
        out_shape=jax.ShapeDtypeStruct((t, n), BF16),
        scratch_shapes=[pltpu.VMEM((tm, d), BF16), pltpu.VMEM((tm, d), BF16)],
        compiler_params=_cparams(("arbitrary", "arbitrary")),
        name="norm_inproj",
    )(x2d, x2d, mod3, mod3, g1.reshape(1, d), w_in_bf)


def _s5_disc_kernel(are_ref, aim_ref, ldt_ref, bre_ref, bim_ref,
                    abre_ref, abim_ref, bbre_ref, bbim_ref):
    a_re = are_ref[...]
    a_im = aim_ref[...]
    dt = jnp.exp(ldt_ref[...])
    mag = jnp.exp(dt * a_re)
    ab_re = mag * jnp.cos(dt * a_im)
    ab_im = mag * jnp.sin(dt * a_im)
    inv_abs2 = 1.0 / (a_re * a_re + a_im * a_im)
    n_re = ab_re - 1.0
    f_re = (n_re * a_re + ab_im * a_im) * inv_abs2
    f_im = (ab_im * a_re - n_re * a_im) * inv_abs2
    abre_ref[...] = ab_re
    abim_ref[...] = ab_im
    b_re = bre_ref[...]
    b_im = bim_ref[...]
    fr = f_re[:, None, :]
    fi = f_im[:, None, :]
    bbre_ref[...] = fr * b_re - fi * b_im
    bbim_ref[...] = fr * b_im + fi * b_re


def _s5_disc(a_re, a_im, log_dt, bt_re, bt_im):
    g, p = a_re.shape
    h = bt_re.shape[1]
    return pl.pallas_call(
        _s5_disc_kernel,
        out_shape=[jax.ShapeDtypeStruct((g, p), F32), jax.ShapeDtypeStruct((g, p), F32),
                   jax.ShapeDtypeStruct((g, h, p), F32), jax.ShapeDtypeStruct((g, h, p), F32)],
        name="s5_discretise",
    )(a_re, a_im, log_dt.reshape(g, 1), bt_re, bt_im)


def _s5_pack(ab_re, ab_im, bbt_re, bbt_im, c_re, c_im):
    g, h, p = bbt_re.shape
    npair = g // 2
    ppb = LANES // (2 * h)
    eye2 = jnp.eye(2, dtype=F32)
    qsel = jax.nn.one_hot(jnp.arange(npair) % ppb, ppb, dtype=F32)

    def in_mat(bbt):
        b4 = bbt.reshape(npair, 2, h, p)
        return jnp.einsum('ab,xahp->xahbp', eye2, b4).reshape(npair, 2 * h, 2 * p)

    bsmall = jnp.concatenate([in_mat(bbt_re), in_mat(bbt_im)], axis=-1)
    bz = jnp.einsum('xq,xrc->xqrc', qsel, bsmall).reshape(npair, LANES, 4 * p)

    def out_mat(c):
        c4 = jnp.swapaxes(c, 1, 2).reshape(npair, 2, p, h)
        return jnp.einsum('ab,xaph->xapbh', eye2, c4).reshape(npair, 2 * p, 2 * h)

    csmall = jnp.concatenate([out_mat(c_re), -out_mat(c_im)], axis=1)
    cz = jnp.einsum('xq,xrc->xrqc', qsel, csmall).reshape(npair, 4 * p, LANES)
    a_re = ab_re.reshape(npair // ppb, ppb, 2 * p)
    a_im = ab_im.reshape(npair // ppb, ppb, 2 * p)
    return bz.astype(BF16), cz.astype(BF16), a_re, a_im


def _s5_kernel(u_ref, bz_ref, cz_ref, are_ref, aim_ref, d_ref, o_ref, state, *bufs, ppb):
    t = pl.program_id(0)
    j = pl.program_id(1)
    bn, tt, width = u_ref.shape
    nblk = width // LANES
    rt = bn * tt

    @pl.when(t == 0)
    def _():
        for i in range(nblk):
            state[j * nblk + i] = jnp.zeros(state.shape[1:], F32)

    for i in range(nblk):
        buf = bufs[i]
        kb = j * nblk + i
        lanes = slice(i * LANES, (i + 1) * LANES)

        u = u_ref[:, :, lanes].reshape(rt, LANES)
        for q in range(ppb):
            bu = jnp.dot(u, bz_ref[i * ppb + q], preferred_element_type=F32)
            for b in range(bn):
                rows = pl.ds(b, tt, stride=bn)
                buf[2 * q, rows, :] = bu[b * tt:(b + 1) * tt, :LANES]
                buf[2 * q + 1, rows, :] = bu[b * tt:(b + 1) * tt, LANES:]

        a_re = [jnp.broadcast_to(are_ref[i, q:q + 1, :], (bn, LANES)) for q in range(ppb)]
        a_im = [jnp.broadcast_to(aim_ref[i, q:q + 1, :], (bn, LANES)) for q in range(ppb)]
        s = [state[kb, k] for k in range(2 * ppb)]
        for step in range(tt):
            rows = slice(step * bn, (step + 1) * bn)
            for q in range(ppb):
                s_re, s_im = s[2 * q], s[2 * q + 1]
                n_re = a_re[q] * s_re - a_im[q] * s_im + buf[2 * q, rows, :]
                n_im = a_re[q] * s_im + a_im[q] * s_re + buf[2 * q + 1, rows, :]
                buf[2 * q, rows, :] = n_re
                buf[2 * q + 1, rows, :] = n_im
                s[2 * q], s[2 * q + 1] = n_re, n_im
        for k in range(2 * ppb):
            state[kb, k] = s[k]

        acc = None
        for q in range(ppb):
            lhs = jnp.concatenate([buf[2 * q], buf[2 * q + 1]], axis=1).astype(BF16)
            part = jnp.dot(lhs, cz_ref[i * ppb + q], preferred_element_type=F32)
            acc = part if acc is None else acc + part
        buf[2 * ppb] = acc
        d_skip = d_ref[:, lanes]
        for b in range(bn):
            o_ref[b, :, lanes] = (buf[2 * ppb, pl.ds(b, tt, stride=bn), :]
                                  + d_skip * u_ref[b, :, lanes].astype(F32))


def _s5_scan(u3, bz, cz, a_re, a_im, d_skip, tt=S5_TT, nblk=S5_BLOCKS_PER_STEP):
    bn, seq, width = u3.shape[0], u3.shape[1], d_skip.shape[0]
    nkb = width // LANES
    ppb = bz.shape[0] // nkb
    nblk = math.gcd(nblk, nkb)
    bw = nblk * LANES
    kern = functools.partial(_s5_kernel, ppb=ppb)
    return pl.pallas_call(
        kern,
        grid=(seq // tt, nkb // nblk),
        in_specs=[pl.BlockSpec((bn, tt, bw), lambda t, k: (0, t, k)),
                  pl.BlockSpec((nblk * ppb,) + bz.shape[1:], lambda t, k: (k, 0, 0)),
                  pl.BlockSpec((nblk * ppb,) + cz.shape[1:], lambda t, k: (k, 0, 0)),
                  pl.BlockSpec((nblk,) + a_re.shape[1:], lambda t, k: (k, 0, 0)),
                  pl.BlockSpec((nblk,) + a_im.shape[1:], lambda t, k: (k, 0, 0)),
                  pl.BlockSpec((1, bw), lambda t, k: (0, k))],
        out_specs=pl.BlockSpec((bn, tt, bw), lambda t, k: (0, t, k)),
        out_shape=jax.ShapeDtypeStruct((bn, seq, width), F32),
        scratch_shapes=[pltpu.VMEM((nkb, 2 * ppb, bn, LANES), F32)]
        + [pltpu.VMEM((2 * ppb + 1, bn * tt, LANES), F32) for _ in range(nblk)],
        compiler_params=_cparams(("arbitrary", "arbitrary")),
        name="s5_scan",
    )(u3, bz, cz, a_re, a_im, d_skip.reshape(1, width))


def _ret_kernel(q_ref, k_ref, v_ref, g_ref, cq_ref, sq_ref, ck_ref, sk_ref, lg_ref, beta_ref, o_ref,
                q_scr, k_scr, kzt_scr, y_scr, *, chunk):
    seq, dh = q_ref.shape[1], q_ref.shape[2]
    lg = lg_ref[0][:, :1]
    ri = lax.broadcasted_iota(I32, (chunk, chunk), 0)
    ci = lax.broadcasted_iota(I32, (chunk, chunk), 1)
    rel = (ri - ci).astype(F32)
    decay = jnp.where(rel >= 0, jnp.exp(lg * jnp.maximum(rel, 0.0)), 0.0)
    idx = lax.broadcasted_iota(I32, (chunk, 1), 0).astype(F32)
    xi = jnp.exp(lg * (idx + 1.0))
    g_chunk = jnp.exp(lg * float(chunk))
    beta = beta_ref[...]
    half = dh // 2
    pr = lax.broadcasted_iota(I32, (dh, dh), 0)
    pc = lax.broadcasted_iota(I32, (dh, dh), 1)
    swap = jnp.where(((pr + half) & (dh - 1)) == pc, 1.0, 0.0).astype(BF16)

    def rope(x_ref, cos_ref, sin_ref, sl):
        xb = x_ref[0, sl, :]
        rolled = jnp.dot(xb, swap, preferred_element_type=F32).astype(BF16)
        return xb * cos_ref[sl, :] + rolled * sin_ref[sl, :]

    zeta = jnp.exp(lg * (chunk - 1.0 - idx))
    for n in range(seq // chunk):
        sl = slice(n * chunk, (n + 1) * chunk)
        q_scr[sl, :] = rope(q_ref, cq_ref, sq_ref, sl)
        kb = rope(k_ref, ck_ref, sk_ref, sl)
        k_scr[sl, :] = kb
        kzt_scr[:, sl] = (kb.astype(F32) * zeta).T.astype(BF16)

    decay = decay.astype(BF16)
    r = jnp.zeros((dh, dh), F32)
    for n in range(seq // chunk):
        sl = slice(n * chunk, (n + 1) * chunk)
        qb = q_scr[sl, :]
        v = v_ref[0, sl, :]
        s = lax.dot_general(qb, k_scr[sl, :], (((1,), (1,)), ((), ())),
                            preferred_element_type=F32).astype(BF16) * decay
        y = jnp.dot(s, v, preferred_element_type=F32)
        y_scr[sl, :] = y + jnp.dot(qb, r.astype(BF16), preferred_element_type=F32) * xi
        r = r * g_chunk + jnp.dot(kzt_scr[:, sl], v, preferred_element_type=F32)

    y = y_scr[...]
    mu = jnp.mean(y, axis=-1, keepdims=True)
    yc = y - mu
    var = jnp.mean(yc * yc, axis=-1, keepdims=True)
    yn = yc * lax.rsqrt(var + GN_EPS) * beta
    o_ref[0] = (_silu(g_ref[0].astype(F32)) * yn).astype(BF16)


def _retention(proj3, col0, cosf, sinf, log_g, beta_ret, chunk=RET_CHUNK):
    bn, seq, _ = proj3.shape
    dh = RET_HEAD_DIM
    heads = beta_ret.shape[0] // dh
    blk = lambda off: pl.BlockSpec((1, seq, dh), lambda b, h, off=off: (b, 0, col0 + off + h))
    k_scale = dh ** -0.5
    kern = functools.partial(_ret_kernel, chunk=chunk)
    return pl.pallas_call(
        kern,
        grid=(bn, heads),
        in_specs=[blk(0), blk(heads), blk(2 * heads), blk(3 * heads),
                  pl.BlockSpec((seq, dh), lambda b, h: (0, 0)),
                  pl.BlockSpec((seq, dh), lambda b, h: (0, 0)),
                  pl.BlockSpec((seq, dh), lambda b, h: (0, 0)),
                  pl.BlockSpec((seq, dh), lambda b, h: (0, 0)),
                  pl.BlockSpec((1, 1, LANES), lambda b, h: (h, 0, 0)),
                  pl.BlockSpec((1, dh), lambda b, h: (0, h))],
        out_specs=pl.BlockSpec((1, seq, dh), lambda b, h: (b, 0, h)),
        out_shape=jax.ShapeDtypeStruct((bn, seq, heads * dh), BF16),
        scratch_shapes=[pltpu.VMEM((seq, dh), BF16), pltpu.VMEM((seq, dh), BF16),
                        pltpu.VMEM((dh, seq), BF16), pltpu.VMEM((seq, dh), F32)],
        compiler_params=_cparams(("arbitrary", "arbitrary")),
        name="retention",
    )(proj3, proj3, proj3, proj3, cosf.astype(BF16), sinf.astype(BF16),
      (cosf * k_scale).astype(BF16), (sinf * k_scale).astype(BF16),
      log_g, beta_ret.reshape(1, heads * dh))


def _gelu_tanh(x):
    return 0.5 * x * (1.0 + jnp.tanh(math.sqrt(2.0 / math.pi) * (x + 0.044715 * (x * x * x))))


def _route_rows(lg, base):
    tm = lg.shape[0]
    lane = lax.broadcasted_iota(I32, (tm, LANES), 1).astype(F32)
    neg = jnp.float32(-jnp.inf)

    def first_max(vals):
        m = jnp.max(vals, axis=-1, keepdims=True)
        idx = jnp.min(jnp.where(vals == m, lane, float(LANES)), axis=-1, keepdims=True)
        return m, idx

    gl = jnp.where(lane < N_GROUPS, lg, neg)
    gmax, gsel = first_max(gl)
    g_w = 1.0 / jnp.sum(jnp.exp(gl - gmax), axis=-1, keepdims=True)
    lo = ROUTE_LANE0 + gsel * EXPERTS_PER_GROUP
    el = jnp.where((lane >= lo) & (lane < lo + EXPERTS_PER_GROUP), lg, neg)
    m1, i1 = first_max(el)
    m2, i2 = first_max(jnp.where(lane == i1, neg, el))
    e21 = jnp.exp(m2 - m1)
    w1 = g_w / (1.0 + e21)
    w2 = g_w * e21 / (1.0 + e21)

    sel1 = lane == i1
    sel2 = lane == i2
    onehot = jnp.where(sel1 | sel2, 1.0, 0.0).astype(BF16)
    ri = lax.broadcasted_iota(I32, (tm, tm), 0)
    ci = lax.broadcasted_iota(I32, (tm, tm), 1)
    tri = jnp.where(ci < ri, 1.0, 0.0).astype(BF16)
    excl = jnp.dot(tri, onehot, preferred_element_type=F32) + base
    r1 = jnp.sum(jnp.where(sel1, excl, 0.0), axis=-1, keepdims=True)
    r2 = jnp.sum(jnp.where(sel2, excl, 0.0), axis=-1, keepdims=True)
    total = base + jnp.sum(jnp.where(sel1 | sel2, 1.0, 0.0), axis=0, keepdims=True)
    meta = (jnp.where(lane == 0, i1, 0.0) + jnp.where(lane == 1, i2, 0.0)
            + jnp.where(lane == 2, r1, 0.0) + jnp.where(lane == 3, r2, 0.0))
    w = jnp.where(lane == 0, w1, 0.0) + jnp.where(lane == 1, w2, 0.0)
    return meta, w, total


def _mix_kernel(ys_ref, yr_ref, x_ref, mod_ref, wglu_ref, bssm_ref, wout_ref, g2_ref,
                wr_ref, br_ref, x1_ref, h2_ref, meta_ref, w_ref, cnt_ref, carry):
    @pl.when(pl.program_id(0) == 0)
    def _():
        carry[...] = jnp.zeros(carry.shape, F32)

    tm, sw = ys_ref.shape
    m = mod_ref[0]
    counts = carry[0:1, :]
    sub = tm // MIX_SUBTILES
    rows = [slice(h * sub, (h + 1) * sub) for h in range(MIX_SUBTILES)]
    z = [_gelu_tanh(ys_ref[r, :]) for r in rows]
    gate = [jax.nn.sigmoid(jnp.dot(zz.astype(BF16), wglu_ref[...], preferred_element_type=F32))
            for zz in z]
    o = [zz * gg for zz, gg in zip(z, gate)]
    y_ssm = [oo * lax.rsqrt(jnp.mean(oo * oo, axis=-1, keepdims=True) + NORM_EPS) * bssm_ref[...]
             for oo in o]
    mixed = [jnp.dot(ys.astype(BF16), wout_ref[:sw, :], preferred_element_type=F32)
             + jnp.dot(yr_ref[r, :], wout_ref[sw:, :], preferred_element_type=F32)
             for ys, r in zip(y_ssm, rows)]
    x1 = [x_ref[r, :] + m[2:3, :] * mm for r, mm in zip(rows, mixed)]
    for r, xx in zip(rows, x1):
        x1_ref[r, :] = xx
    y = [xx * lax.rsqrt(jnp.mean(xx * xx, axis=-1, keepdims=True) + NORM_EPS) * g2_ref[...]
         for xx in x1]
    h2 = [yy * (1.0 + m[4:5, :]) + m[3:4, :] for yy in y]
    half = h2[0].shape[1] // 2
    for r, hh in zip(rows, h2):
        h2_ref[r, :] = _pack_bf16_pairs(hh[:, :half], hh[:, half:])
    logits = [jnp.dot(hh.astype(BF16), wr_ref[...], preferred_element_type=F32) + br_ref[...]
              for hh in h2]
    for r, lg in zip(rows, logits):
        meta, w, counts = _route_rows(lg, counts)
        meta_ref[r, :] = meta
        w_ref[r, :] = w
    carry[...] = jnp.broadcast_to(counts, carry.shape)
    cnt_ref[...] = jnp.broadcast_to(counts, cnt_ref.shape)


def _mix(ypre, yret, x2d, mod3, wglu_bf, beta_ssm, wout_bf, g2, wr_bf, br, seq, tm=512):
    t, d = x2d.shape
    sw = wglu_bf.shape[0]
    tpb = seq // tm
    const = lambda shape: pl.BlockSpec(shape, lambda i: (0,) * len(shape),
                                       pipeline_mode=pl.Buffered(1))
    return pl.pallas_call(
        _mix_kernel,
        grid=(t // tm,),
        in_specs=[pl.BlockSpec((tm, sw), lambda i: (i, 0)),
                  pl.BlockSpec((tm, d - sw), lambda i: (i, 0)),
                  pl.BlockSpec((tm, d), lambda i: (i, 0)),
                  pl.BlockSpec((1, N_MOD, d), lambda i: (i // tpb, 0, 0)),
                  const((sw, sw)), const((1, sw)), const((d, d)), const((1, d)),
                  const((d, LANES)), const((1, LANES))],
        out_specs=[pl.BlockSpec((tm, d), lambda i: (i, 0)),
                   pl.BlockSpec((tm, d // 2), lambda i: (i, 0)),
                   pl.BlockSpec((tm, LANES), lambda i: (i, 0)),
                   pl.BlockSpec((tm, LANES), lambda i: (i, 0)),
                   pl.BlockSpec((SUBLANES, LANES), lambda i: (0, 0))],
        out_shape=[jax.ShapeDtypeStruct((t, d), F32),
                   jax.ShapeDtypeStruct((t, d // 2), jnp.uint32),
                   jax.ShapeDtypeStruct((t, LANES), F32),
                   jax.ShapeDtypeStruct((t, LANES), F32),
                   jax.ShapeDtypeStruct((SUBLANES, LANES), F32)],
        scratch_shapes=[pltpu.VMEM((SUBLANES, LANES), F32)],
        compiler_params=_cparams(("arbitrary",)),
        name="mix_outproj_router",
    )(ypre, yret, x2d, mod3, wglu_bf, beta_ssm.reshape(1, sw), wout_bf, g2.reshape(1, d),
      wr_bf, br)


def _lane_cumsum(x):
    lane = lax.broadcasted_iota(I32, x.shape, 1)
    sh = 1
    while sh < LANES:
        x = x + jnp.where(lane >= sh, pltpu.roll(x, sh, axis=1), 0)
        sh *= 2
    return x


def _dest_kernel(meta_ref, cnt_ref, dest_ref, be_ref, *, nb):
    tm = meta_ref.shape[0]
    shift = MOE_BLOCK.bit_length() - 1
    cnt = cnt_ref[...].astype(I32)
    padded = ((cnt + (MOE_BLOCK - 1)) >> shift) << shift
    pend_i = _lane_cumsum(padded)
    pend = pend_i.astype(F32)[0:1, :]
    poff = (pend_i - padded).astype(F32)[0:1, :]
    meta = meta_ref[...]
    lane = lax.broadcasted_iota(I32, (tm, LANES), 1).astype(F32)
    pick = lambda col: jnp.sum(jnp.where(lane == col, meta, 0.0), axis=-1, keepdims=True)
    i1, i2, r1, r2 = pick(0), pick(1), pick(2), pick(3)
    d1 = jnp.sum(jnp.where(lane == i1, poff, 0.0), axis=-1, keepdims=True) + r1
    d2 = jnp.sum(jnp.where(lane == i2, poff, 0.0), axis=-1, keepdims=True) + r2
    dest_ref[...] = (jnp.where(lane == 0, d1, 0.0) + jnp.where(lane == 1, d2, 0.0)).astype(I32)

    nrow = be_ref.shape[0]
    blk = lax.broadcasted_iota(I32, (nrow, LANES), 0).astype(F32)
    lane_b = lax.broadcasted_iota(I32, (nrow, LANES), 1).astype(F32)
    is_e = (lane_b >= ROUTE_LANE0) & (lane_b < ROUTE_LANE0 + N_EXPERTS)
    below = jnp.sum(jnp.where(is_e & (pend <= blk * MOE_BLOCK), 1.0, 0.0), axis=-1, keepdims=True)
    block_e = jnp.minimum(below, N_EXPERTS - 1.0)
    used = jnp.sum(jnp.where(lane_b == ROUTE_LANE0 + N_EXPERTS - 1, pend, 0.0),
                   axis=-1, keepdims=True) * (1.0 / MOE_BLOCK)
    be = jnp.where(blk[:, :1] == nb, used, block_e)
    be_ref[...] = jnp.broadcast_to(be, (nrow, LANES)).astype(I32)


def _dest(meta, cnt, nb, tm=512):
    t = meta.shape[0]
    nrow = -(-(nb + 1) // SUBLANES) * SUBLANES
    kern = functools.partial(_dest_kernel, nb=nb)
    return pl.pallas_call(
        kern,
        grid=(t // tm,),
        in_specs=[pl.BlockSpec((tm, LANES), lambda i: (i, 0)),
                  pl.BlockSpec((SUBLANES, LANES), lambda i: (0, 0))],
        out_specs=[pl.BlockSpec((tm, LANES), lambda i: (i, 0)),
                   pl.BlockSpec((nrow, LANES), lambda i: (0, 0))],
        out_shape=[jax.ShapeDtypeStruct((t, LANES), I32),
                   jax.ShapeDtypeStruct((nrow, LANES), I32)],
        compiler_params=_cparams(("arbitrary",)),
        name="route_dest",
    )(meta, cnt)


def _invert_kernel(dest_ref, fill_hbm, row_ref, sem):
    cp = pltpu.make_async_copy(fill_hbm, row_ref, sem)
    cp.start()
    cp.wait()

    def put(a, c):
        row_ref[dest_ref[a]] = a
        return c

    lax.fori_loop(0, dest_ref.shape[0], put, 0, unroll=8)


def _invert(dest_flat, n_slot, t):
    spare = MOE_SPARE_SETS * MOE_BLOCK
    fill = 2 * t + (jnp.arange(n_slot, dtype=I32) & (spare - 1))
    return pl.pallas_call(
        _invert_kernel,
        in_specs=[pl.BlockSpec(memory_space=pltpu.SMEM), pl.BlockSpec(memory_space=pl.ANY)],
        out_specs=pl.BlockSpec(memory_space=pltpu.SMEM),
        out_shape=jax.ShapeDtypeStruct((n_slot,), I32),
        scratch_shapes=[pltpu.SemaphoreType.DMA(())],
        name="route_invert",
    )(dest_flat, fill)


def _expert_weights(b, used, be_ref, w_hbms, wst, wsem, run_ref, load):
    def fetch(e, p):
        for i, w in enumerate(w_hbms):
            pltpu.make_async_copy(w.at[e], wst.at[p, i], wsem.at[p]).start(priority=1)

    @pl.when((b == 0) & (used > 0))
    def _():
        run_ref[0] = 0
        fetch(be_ref[0], 0)

    bc = jnp.minimum(b, be_ref.shape[0] - 2)
    new_expert = (b == 0) | (be_ref[bc] != be_ref[jnp.maximum(bc - 1, 0)])

    @pl.when(new_expert & (b < used))
    def _():
        p = run_ref[0]
        for i, w in enumerate(w_hbms):
            pltpu.make_async_copy(w.at[0], wst.at[p, i], wsem.at[p]).wait()
        load(p)
        e = be_ref[b]
        nxt = lax.while_loop(lambda i: (i < used) & (be_ref[i] == e), lambda i: i + 1, b + 1)

        @pl.when(nxt < used)
        def _():
            fetch(be_ref[nxt], 1 - p)

        run_ref[0] = 1 - p


def _moe_up_kernel(row_ref, be_ref, h2_hbm, wg_hbm, wu_hbm, o_ref, wst, wbuf, xbuf, xb_scr, sem,
                   wsem, run_ref, *, nb):
    b = pl.program_id(0)
    used = be_ref[nb]
    blk = MOE_BLOCK
    fe = wg_hbm.shape[2]
    t = h2_hbm.shape[0]

    def gather_rows(block, rows):
        slot = block % MOE_GATHER_DEPTH
        base = block * blk
        for r in rows:
            row = row_ref[base + r]
            tok = jnp.where(row >= t, row - t, row)
            tok = jnp.where(tok >= t, 0, tok)
            pltpu.make_async_copy(h2_hbm.at[pl.ds(tok, 1)], xbuf.at[slot, pl.ds(r, 1)],
                                  sem.at[slot]).start()

    ahead = MOE_GATHER_DEPTH - 1

    @pl.when(b == 0)
    def _():
        for first in range(ahead):
            @pl.when(first < used)
            def _():
                gather_rows(first, range(blk))

    def load(p):
        wbuf[:, :fe] = wst[p, 0].astype(BF16)
        wbuf[:, fe:] = wst[p, 1].astype(BF16)

    _expert_weights(b, used, be_ref, (wg_hbm, wu_hbm), wst, wsem, run_ref, load)

    nchunk = fe // MOE_NCHUNK_COLS
    rows_per = blk // nchunk

    def step(prefetch):
        slot = b % MOE_GATHER_DEPTH
        pltpu.make_async_copy(xbuf.at[slot], xbuf.at[slot], sem.at[slot]).wait()
        lo, hi = _unpack_bf16_pairs(xbuf[slot])
        half = lo.shape[1]
        xb_scr[:, :half] = lo.astype(BF16)
        xb_scr[:, half:] = hi.astype(BF16)
        for c in range(nchunk):
            if prefetch:
                gather_rows(b + ahead, range(c * rows_per, (c + 1) * rows_per))
            cols = slice(c * MOE_NCHUNK_COLS, (c + 1) * MOE_NCHUNK_COLS)
            ucols = slice(fe + c * MOE_NCHUNK_COLS, fe + (c + 1) * MOE_NCHUNK_COLS)
            g = jnp.dot(xb_scr[...], wbuf[:, cols], preferred_element_type=F32)
            u = jnp.dot(xb_scr[...], wbuf[:, ucols], preferred_element_type=F32)
            o_ref[:, cols] = (_silu(g) * u).astype(BF16)

    @pl.when(b + ahead < used)
    def _():
        step(True)

    @pl.when((b < used) & (b + ahead >= used))
    def _():
        step(False)

    @pl.when(b >= used)
    def _():
        o_ref[...] = jnp.zeros(o_ref.shape, BF16)


def _moe_up(slot_row, block_e, h2, w_gate, w_up, nb):
    d, fe = w_gate.shape[1], w_gate.shape[2]
    kern = functools.partial(_moe_up_kernel, nb=nb)
    grid_spec = pltpu.PrefetchScalarGridSpec(
        num_scalar_prefetch=2,
        grid=(nb,),
        in_specs=[pl.BlockSpec(memory_space=pl.ANY),
                  pl.BlockSpec(memory_space=pl.ANY),
                  pl.BlockSpec(memory_space=pl.ANY)],
        out_specs=pl.BlockSpec((MOE_BLOCK, fe), lambda b, s, e: (b, 0)),
        scratch_shapes=[pltpu.VMEM((2, 2, d, fe), F32),
                        pltpu.VMEM((d, 2 * fe), BF16),
                        pltpu.VMEM((MOE_GATHER_DEPTH, MOE_BLOCK, d // 2), jnp.uint32),
                        pltpu.VMEM((MOE_BLOCK, d), BF16),
                        pltpu.SemaphoreType.DMA((MOE_GATHER_DEPTH,)),
                        pltpu.SemaphoreType.DMA((2,)),
                        pltpu.SMEM((1,), I32)],
    )
    return pl.pallas_call(
        kern,
        grid_spec=grid_spec,
        out_shape=jax.ShapeDtypeStruct((nb * MOE_BLOCK, fe), BF16),
        compiler_params=_cparams(("arbitrary",)),
        name="moe_up",
    )(slot_row, block_e, h2, w_gate, w_up)


def _moe_down_kernel(row_ref, be_ref, hm_ref, wd_hbm, ys_hbm, wst, wbuf, obuf, sem, wsem, run_ref,
                     *, nb, t):
    b = pl.program_id(0)
    used = be_ref[nb]
    blk = MOE_BLOCK

    def wait_slot(slot):
        pltpu.make_async_copy(obuf.at[slot], obuf.at[slot], sem.at[slot]).wait()

    depth = obuf.shape[0]

    @pl.when(b == 0)
    def _():
        obuf[0] = jnp.zeros(obuf.shape[1:], jnp.uint32)
        for k in range(MOE_SPARE_SETS):
            spare = ys_hbm.at[pl.ds(2 * t + k * blk, blk)]
            pltpu.make_async_copy(obuf.at[0], spare, sem.at[depth]).start()
        for k in range(MOE_SPARE_SETS):
            spare = ys_hbm.at[pl.ds(2 * t + k * blk, blk)]
            pltpu.make_async_copy(obuf.at[0], spare, sem.at[depth]).wait()

    @pl.when((b >= depth) & (b - depth < used))
    def _():
        wait_slot(b % depth)

    def load(p):
        wbuf[...] = wst[p, 0].astype(BF16)

    _expert_weights(b, used, be_ref, (wd_hbm,), wst, wsem, run_ref, load)

    half = wbuf.shape[1] // 2
    nchunk = half // MOE_NCHUNK_COLS
    rows_per = blk // nchunk

    def step(scatter, matmul):
        slot = (b - 1) % depth
        base = (b - 1) * blk
        for c in range(nchunk):
            if scatter:
                for r in range(c * rows_per, (c + 1) * rows_per):
                    pltpu.make_async_copy(obuf.at[slot, pl.ds(r, 1)],
                                          ys_hbm.at[pl.ds(row_ref[base + r], 1)],
                                          sem.at[slot]).start(priority=r % 2)
            if matmul:
                cols = slice(c * MOE_NCHUNK_COLS, (c + 1) * MOE_NCHUNK_COLS)
                hcols = slice(half + c * MOE_NCHUNK_COLS, half + (c + 1) * MOE_NCHUNK_COLS)
                lo = jnp.dot(hm_ref[...], wbuf[:, cols], preferred_element_type=F32)
                hi = jnp.dot(hm_ref[...], wbuf[:, hcols], preferred_element_type=F32)
                obuf[b % depth, :, cols] = _pack_bf16_pairs(lo, hi)

    @pl.when((b >= 1) & (b < used))
    def _():
        step(True, True)

    @pl.when((b == 0) & (b < used))
    def _():
        step(False, True)

    @pl.when((b >= 1) & (b == used))
    def _():
        step(True, False)

    @pl.when(b == nb)
    def _():
        for k in range(1, depth):
            @pl.when(nb - k < used)
            def _():
                wait_slot((nb - k) % depth)


def _moe_down(slot_row, block_e, hmid, w_down, nb, t):
    fe, d = w_down.shape[1], w_down.shape[2]
    kern = functools.partial(_moe_down_kernel, nb=nb, t=t)
    grid_spec = pltpu.PrefetchScalarGridSpec(
        num_scalar_prefetch=2,
        grid=(nb + 1,),
        in_specs=[pl.BlockSpec((MOE_BLOCK, fe), lambda b, s, e: (jnp.minimum(b, nb - 1), 0)),
                  pl.BlockSpec(memory_space=pl.ANY)],
        out_specs=pl.BlockSpec(memory_space=pl.ANY),
        scratch_shapes=[pltpu.VMEM((2, 1, fe, d), F32),
                        pltpu.VMEM((fe, d), BF16),
                        pltpu.VMEM((MOE_SCATTER_DEPTH, MOE_BLOCK, d // 2), jnp.uint32),
                        pltpu.SemaphoreType.DMA((MOE_SCATTER_DEPTH + 1,)),
                        pltpu.SemaphoreType.DMA((2,)),
                        pltpu.SMEM((1,), I32)],
    )
    return pl.pallas_call(
        kern,
        grid_spec=grid_spec,
        out_shape=jax.ShapeDtypeStruct((2 * t + MOE_SPARE_SETS * MOE_BLOCK, d // 2), jnp.uint32),
        compiler_params=_cparams(("arbitrary",)),
        name="moe_down",
    )(slot_row, block_e, hmid, w_down)


def _final_kernel(x1_ref, y0_ref, y1_ref, w_ref, mod_ref, gf_ref, o_ref):
    w = w_ref[...]
    lo0, hi0 = _unpack_bf16_pairs(y0_ref[...])
    lo1, hi1 = _unpack_bf16_pairs(y1_ref[...])
    y = jnp.concatenate([w[:, 0:1] * lo0 + w[:, 1:2] * lo1,
                         w[:, 0:1] * hi0 + w[:, 1:2] * hi1], axis=1)
    x2 = x1_ref[...] + mod_ref[0][5:6, :] * y
    o_ref[...] = x2 * lax.rsqrt(jnp.mean(x2 * x2, axis=-1, keepdims=True) + NORM_EPS) * gf_ref[...]


def _final(x1, ys, w, mod3, g_final, seq, tm=512):
    t, d = x1.shape
    tpb = seq // tm
    return pl.pallas_call(
        _final_kernel,
        grid=(t // tm,),
        in_specs=[pl.BlockSpec((tm, d), lambda i: (i, 0)),
                  pl.BlockSpec((tm, d // 2), lambda i: (i, 0)),
                  pl.BlockSpec((tm, d // 2), lambda i: (i + t // tm, 0)),
                  pl.BlockSpec((tm, LANES), lambda i: (i, 0)),
                  pl.BlockSpec((1, N_MOD, d), lambda i: (i // tpb, 0, 0)),
                  pl.BlockSpec((1, d), lambda i: (0, 0))],
        out_specs=pl.BlockSpec((tm, d), lambda i: (i, 0)),
        out_shape=jax.ShapeDtypeStruct((t, d), F32),
        compiler_params=_cparams(("arbitrary",)),
        name="combine_final_norm",
    )(x1, ys, ys, w, mod3, g_final.reshape(1, d))


def _rope_tables(seq, dh):
    half = dh // 2
    inv_freq = 1.0 / (ROPE_BASE ** (jnp.arange(half, dtype=F32) * 2.0 / dh))
    ang = jnp.arange(seq, dtype=F32)[:, None] * inv_freq[None, :]
    cos, sin = jnp.cos(ang), jnp.sin(ang)
    return jnp.concatenate([cos, cos], axis=1), jnp.concatenate([-sin, sin], axis=1)


def _layer(x, mod, g_norm1, w_in, ssm_a_re, ssm_a_im, ssm_b_re, ssm_b_im, ssm_c_re, ssm_c_im,
           ssm_d, ssm_log_dt, w_glu, beta_ssm, beta_ret, w_out, g_norm2, w_rg, b_rg, w_re, b_re,
           w_gate, w_up, w_down):
    bn, seq, d = x.shape
    t = bn * seq
    x2d = x.reshape(t, d)
    mod3 = mod.reshape(bn, N_MOD, d)

    sw = w_glu.shape[0]
    proj = _inproj(x2d, mod3, g_norm1, w_in.astype(BF16), seq)
    proj3 = proj.reshape(bn, seq, proj.shape[1])

    ab_re, ab_im, bbt_re, bbt_im = _s5_disc(ssm_a_re, ssm_a_im, ssm_log_dt,
                                            jnp.swapaxes(ssm_b_re, 1, 2), jnp.swapaxes(ssm_b_im, 1, 2))
    bz, cz, a_re, a_im = _s5_pack(ab_re, ab_im, bbt_re, bbt_im, ssm_c_re, ssm_c_im)
    ypre = _s5_scan(proj3, bz, cz, a_re, a_im, ssm_d.reshape(-1)).reshape(t, sw)

    heads = (d - sw) // RET_HEAD_DIM
    cosf, sinf = _rope_tables(seq, RET_HEAD_DIM)
    gamma = 1.0 - jnp.exp2(-5.0 - jnp.arange(heads, dtype=F32))
    log_g = jnp.broadcast_to(jnp.log(gamma)[:, None, None], (heads, 1, LANES))
    yret = _retention(proj3, sw // RET_HEAD_DIM, cosf, sinf, log_g, beta_ret)

    n_route = N_GROUPS + N_EXPERTS
    wr = jnp.concatenate([w_rg, w_re, jnp.zeros((d, LANES - n_route), F32)], axis=1)
    br = jnp.concatenate([b_rg, b_re, jnp.zeros((LANES - n_route,), F32)]).reshape(1, LANES)
    x1, h2, meta, w_tok, cnt = _mix(ypre, yret.reshape(t, d - sw), x2d, mod3, w_glu.astype(BF16),
                                    beta_ssm, w_out.astype(BF16), g_norm2, wr.astype(BF16), br, seq)

    nb = -(-(t * 2) // MOE_BLOCK) + N_EXPERTS
    dest, be = _dest(meta, cnt, nb)
    slot_row = _invert(dest[:, :2].T.reshape(-1), nb * MOE_BLOCK, t)
    block_e = be[:nb + 1, 0]

    hmid = _moe_up(slot_row, block_e, h2, w_gate, w_up, nb)
    ys = _moe_down(slot_row, block_e, hmid, w_down, nb, t)
    return x1, ys, w_tok, mod3


def kernel(x, c, w_ada, b_ada, g_norm1, w_in, ssm_a_re, ssm_a_im, ssm_b_re, ssm_b_im, ssm_c_re,
           ssm_c_im, ssm_d, ssm_log_dt, w_glu, beta_ssm, beta_ret, w_out, g_norm2, w_router_group,
           b_router_group, w_router_expert, b_router_expert, w_gate, w_up, w_down, g_final):
    depth = w_ada.shape[0]
    bn, seq, d = x.shape
    assert depth == 1, "the final norm is fused into the single layer's last kernel"
    l = 0
    mod = _ada(c, w_ada[l], b_ada[l])
    x1, ys, w_tok, mod3 = _layer(
        x, mod, g_norm1[l], w_in[l], ssm_a_re[l], ssm_a_im[l], ssm_b_re[l], ssm_b_im[l],
        ssm_c_re[l], ssm_c_im[l], ssm_d[l], ssm_log_dt[l], w_glu[l], beta_ssm[l], beta_ret[l],
        w_out[l], g_norm2[l], w_router_group[l], b_router_group[l], w_router_expert[l],
        b_router_expert[l], w_gate[l], w_up[l], w_down[l])
    out = _final(x1, ys, w_tok, mod3, g_final, seq)
    return out.reshape(bn, seq, d)
```

```python
import functools
import math

import jax
import jax.numpy as jnp
from jax import lax
from jax.experimental import pallas as pl
from jax.experimental.pallas import tpu as pltpu

F32 = jnp.float32
BF16 = jnp.bfloat16
I32 = jnp.int32

SSM_GROUP = 16
SSM_STATE = 64
RET_HEAD_DIM = 128
ROPE_BASE = 10000.0
N_GROUPS = 4
EXPERTS_PER_GROUP = 8
N_EXPERTS = N_GROUPS * EXPERTS_PER_GROUP
N_MOD = 6
NORM_EPS = 1e-6
GN_EPS = 1e-5

LANES = 128
SUBLANES = 8
VMEM_LIMIT = 56 * 1024 * 1024

ROUTE_LANE0 = N_GROUPS
MOE_BLOCK = 256
MOE_NCHUNK_COLS = 256
MOE_GATHER_DEPTH = 4
MOE_SCATTER_DEPTH = 4
MOE_SPARE_SETS = 4
RET_CHUNK = 256
INPROJ_SLICES = 4
MIX_SUBTILES = 2
S5_TT = 128
S5_BLOCKS_PER_STEP = 4


def _cparams(sem):
    return pltpu.CompilerParams(dimension_semantics=sem, vmem_limit_bytes=VMEM_LIMIT)


def _silu(x):
    return x * jax.nn.sigmoid(x)


def _pack_bf16_pairs(lo, hi):
    lo_w = pltpu.bitcast(lo.astype(BF16).astype(F32), jnp.uint32)
    hi_w = pltpu.bitcast(hi.astype(BF16).astype(F32), jnp.uint32)
    return (lo_w >> 16) | (hi_w & jnp.uint32(0xFFFF0000))


def _unpack_bf16_pairs(w):
    return (pltpu.bitcast(w << 16, F32), pltpu.bitcast(w & jnp.uint32(0xFFFF0000), F32))


def _ada_kernel(c_ref, w_ref, b_ref, o_ref):
    s = _silu(c_ref[...])
    o_ref[...] = jnp.dot(s.astype(BF16), w_ref[...].astype(BF16),
                         preferred_element_type=F32) + b_ref[...]


def _ada(c, w, b, tn=1024):
    bn, d = c.shape
    n = w.shape[1]
    return pl.pallas_call(
        _ada_kernel,
        grid=(n // tn,),
        in_specs=[pl.BlockSpec((bn, d), lambda j: (0, 0)),
                  pl.BlockSpec((d, tn), lambda j: (0, j)),
                  pl.BlockSpec((1, tn), lambda j: (0, j))],
        out_specs=pl.BlockSpec((bn, tn), lambda j: (0, j)),
        out_shape=jax.ShapeDtypeStruct((bn, n), F32),
        compiler_params=_cparams(("arbitrary",)),
        name="ada_mod",
    )(c, w, b.reshape(1, n))


def _inproj_kernel(x0_ref, xn_ref, mod0_ref, modn_ref, g_ref, w_ref, o_ref, h_even, h_odd, *,
                   nslices):
    i = pl.program_id(0)
    j = pl.program_id(1)

    def normed(x, m):
        y = x * lax.rsqrt(jnp.mean(x * x, axis=-1, keepdims=True) + NORM_EPS) * g_ref[...]
        return (y * (1.0 + m[1:2, :]) + m[0:1, :]).astype(BF16)

    @pl.when((i == 0) & (j == 0))
    def _():
        h_even[...] = normed(x0_ref[...], mod0_ref[0])

    rows = xn_ref.shape[0] // nslices
    start = pl.multiple_of(jnp.minimum(j, nslices - 1) * rows, rows)
    sl = pl.ds(start, rows)

    def step(h_cur, h_next):
        h_next[sl, :] = normed(xn_ref[sl, :], modn_ref[0])
        o_ref[...] = jnp.dot(h_cur[...], w_ref[...], preferred_element_type=F32).astype(BF16)

    @pl.when(i % 2 == 0)
    def _():
        step(h_even, h_odd)

    @pl.when(i % 2 == 1)
    def _():
        step(h_odd, h_even)


def _inproj(x2d, mod3, g1, w_in_bf, seq, tm=1024, tn=1024):
    t, d = x2d.shape
    n = w_in_bf.shape[1]
    tm = min(tm, seq)
    tn = math.gcd(tn, n)
    kern = functools.partial(_inproj_kernel, nslices=min(INPROJ_SLICES, n // tn))
    tpb = seq // tm
    last = t // tm - 1
    nxt = lambda i: jnp.minimum(i + 1, last)
    return pl.pallas_call(
        kern,
        grid=(t // tm, n // tn),
        in_specs=[pl.BlockSpec((tm, d), lambda i, j: (0, 0), pipeline_mode=pl.Buffered(1)),
                  pl.BlockSpec((tm, d), lambda i, j: (nxt(i), 0)),
                  pl.BlockSpec((1, N_MOD, d), lambda i, j: (0, 0, 0)),
                  pl.BlockSpec((1, N_MOD, d), lambda i, j: (nxt(i) // tpb, 0, 0)),
                  pl.BlockSpec((1, d), lambda i, j: (0, 0)),
                  pl.BlockSpec((d, tn), lambda i, j: (0, j))],
        out_specs=pl.BlockSpec((tm, tn), lambda i, j: (i, j)),
        out_shape=jax.ShapeDtypeStruct((t, n), BF16),
        scratch_shapes=[pltpu.VMEM((tm, d), BF16), pltpu.VMEM((tm, d), BF16)],
        compiler_params=_cparams(("arbitrary", "arbitrary")),
        name="norm_inproj",
    )(x2d, x2d, mod3, mod3, g1.reshape(1, d), w_in_bf)


def _s5_disc_kernel(are_ref, aim_ref, ldt_ref, bre_ref, bim_ref,
                    abre_ref, abim_ref, bbre_ref, bbim_ref):
    a_re = are_ref[...]
    a_im = aim_ref[...]
    dt = jnp.exp(ldt_ref[...])
    mag = jnp.exp(dt * a_re)
    ab_re = mag * jnp.cos(dt * a_im)
    ab_im = mag * jnp.sin(dt * a_im)
    inv_abs2 = 1.0 / (a_re * a_re + a_im * a_im)
    n_re = ab_re - 1.0
    f_re = (n_re * a_re + ab_im * a_im) * inv_abs2
    f_im = (ab_im * a_re - n_re * a_im) * inv_abs2
    abre_ref[...] = ab_re
    abim_ref[...] = ab_im
    b_re = bre_ref[...]
    b_im = bim_ref[...]
    fr = f_re[:, None, :]
    fi = f_im[:, None, :]
    bbre_ref[...] = fr * b_re - fi * b_im
    bbim_ref[...] = fr * b_im + fi * b_re


def _s5_disc(a_re, a_im, log_dt, bt_re, bt_im):
    g, p = a_re.shape
    h = bt_re.shape[1]
    return pl.pallas_call(
        _s5_disc_kernel,
        out_shape=[jax.ShapeDtypeStruct((g, p), F32), jax.ShapeDtypeStruct((g, p), F32),
                   jax.ShapeDtypeStruct((g, h, p), F32), jax.ShapeDtypeStruct((g, h, p), F32)],
        name="s5_discretise",
    )(a_re, a_im, log_dt.reshape(g, 1), bt_re, bt_im)


def _s5_pack(ab_re, ab_im, bbt_re, bbt_im, c_re, c_im):
    g, h, p = bbt_re.shape
    npair = g // 2
    ppb = LANES // (2 * h)
    eye2 = jnp.eye(2, dtype=F32)
    qsel = jax.nn.one_hot(jnp.arange(npair) % ppb, ppb, dtype=F32)

    def in_mat(bbt):
        b4 = bbt.reshape(npair, 2, h, p)
        return jnp.einsum('ab,xahp->xahbp', eye2, b4).reshape(npair, 2 * h, 2 * p)

    bsmall = jnp.concatenate([in_mat(bbt_re), in_mat(bbt_im)], axis=-1)
    bz = jnp.einsum('xq,xrc->xqrc', qsel, bsmall).reshape(npair, LANES, 4 * p)

    def out_mat(c):
        c4 = jnp.swapaxes(c, 1, 2).reshape(npair, 2, p, h)
        return jnp.einsum('ab,xaph->xapbh', eye2, c4).reshape(npair, 2 * p, 2 * h)

    csmall = jnp.concatenate([out_mat(c_re), -out_mat(c_im)], axis=1)
    cz = jnp.einsum('xq,xrc->xrqc', qsel, csmall).reshape(npair, 4 * p, LANES)
    a_re = ab_re.reshape(npair // ppb, ppb, 2 * p)
    a_im = ab_im.reshape(npair // ppb, ppb, 2 * p)
    return bz.astype(BF16), cz.astype(BF16), a_re, a_im


def _s5_kernel(u_ref, bz_ref, cz_ref, are_ref, aim_ref, d_ref, o_ref, state, *bufs, ppb):
    t = pl.program_id(0)
    j = pl.program_id(1)
    bn, tt, width = u_ref.shape
    nblk = width // LANES
    rt = bn * tt

    @pl.when(t == 0)
    def _():
        for i in range(nblk):
            state[j * nblk + i] = jnp.zeros(state.shape[1:], F32)

    for i in range(nblk):
        buf = bufs[i]
        kb = j * nblk + i
        lanes = slice(i * LANES, (i + 1) * LANES)

        u = u_ref[:, :, lanes].reshape(rt, LANES)
        for q in range(ppb):
            bu = jnp.dot(u, bz_ref[i * ppb + q], preferred_element_type=F32)
            for b in range(bn):
                rows = pl.ds(b, tt, stride=bn)
                buf[2 * q, rows, :] = bu[b * tt:(b + 1) * tt, :LANES]
                buf[2 * q + 1, rows, :] = bu[b * tt:(b + 1) * tt, LANES:]

        a_re = [jnp.broadcast_to(are_ref[i, q:q + 1, :], (bn, LANES)) for q in range(ppb)]
        a_im = [jnp.broadcast_to(aim_ref[i, q:q + 1, :], (bn, LANES)) for q in range(ppb)]
        s = [state[kb, k] for k in range(2 * ppb)]
        for step in range(tt):
            rows = slice(step * bn, (step + 1) * bn)
            for q in range(ppb):
                s_re, s_im = s[2 * q], s[2 * q + 1]
                n_re = a_re[q] * s_re - a_im[q] * s_im + buf[2 * q, rows, :]
                n_im = a_re[q] * s_im + a_im[q] * s_re + buf[2 * q + 1, rows, :]
                buf[2 * q, rows, :] = n_re
                buf[2 * q + 1, rows, :] = n_im
                s[2 * q], s[2 * q + 1] = n_re, n_im
        for k in range(2 * ppb):
            state[kb, k] = s[k]

        acc = None
        for q in range(ppb):
            lhs = jnp.concatenate([buf[2 * q], buf[2 * q + 1]], axis=1).astype(BF16)
            part = jnp.dot(lhs, cz_ref[i * ppb + q], preferred_element_type=F32)
            acc = part if acc is None else acc + part
        buf[2 * ppb] = acc
        d_skip = d_ref[:, lanes]
        for b in range(bn):
            o_ref[b, :, lanes] = (buf[2 * ppb, pl.ds(b, tt, stride=bn), :]
                                  + d_skip * u_ref[b, :, lanes].astype(F32))


def _s5_scan(u3, bz, cz, a_re, a_im, d_skip, tt=S5_TT, nblk=S5_BLOCKS_PER_STEP):
    bn, seq, width = u3.shape[0], u3.shape[1], d_skip.shape[0]
    nkb = width // LANES
    ppb = bz.shape[0] // nkb
    nblk = math.gcd(nblk, nkb)
    bw = nblk * LANES
    kern = functools.partial(_s5_kernel, ppb=ppb)
    return pl.pallas_call(
        kern,
        grid=(seq // tt, nkb // nblk),
        in_specs=[pl.BlockSpec((bn, tt, bw), lambda t, k: (0, t, k)),
                  pl.BlockSpec((nblk * ppb,) + bz.shape[1:], lambda t, k: (k, 0, 0)),
                  pl.BlockSpec((nblk * ppb,) + cz.shape[1:], lambda t, k: (k, 0, 0)),
                  pl.BlockSpec((nblk,) + a_re.shape[1:], lambda t, k: (k, 0, 0)),
                  pl.BlockSpec((nblk,) + a_im.shape[1:], lambda t, k: (k, 0, 0)),
                  pl.BlockSpec((1, bw), lambda t, k: (0, k))],
        out_specs=pl.BlockSpec((bn, tt, bw), lambda t, k: (0, t, k)),
        out_shape=jax.ShapeDtypeStruct((bn, seq, width), F32),
        scratch_shapes=[pltpu.VMEM((nkb, 2 * ppb, bn, LANES), F32)]
        + [pltpu.VMEM((2 * ppb + 1, bn * tt, LANES), F32) for _ in range(nblk)],
        compiler_params=_cparams(("arbitrary", "arbitrary")),
        name="s5_scan",
    )(u3, bz, cz, a_re, a_im, d_skip.reshape(1, width))


def _ret_kernel(q_ref, k_ref, v_ref, g_ref, cq_ref, sq_ref, ck_ref, sk_ref, lg_ref, beta_ref, o_ref,
                q_scr, k_scr, kzt_scr, y_scr, *, chunk):
    seq, dh = q_ref.shape[1], q_ref.shape[2]
    lg = lg_ref[0][:, :1]
    ri = lax.broadcasted_iota(I32, (chunk, chunk), 0)
    ci = lax.broadcasted_iota(I32, (chunk, chunk), 1)
    rel = (ri - ci).astype(F32)
    decay = jnp.where(rel >= 0, jnp.exp(lg * jnp.maximum(rel, 0.0)), 0.0)
    idx = lax.broadcasted_iota(I32, (chunk, 1), 0).astype(F32)
    xi = jnp.exp(lg * (idx + 1.0))
    g_chunk = jnp.exp(lg * float(chunk))
    beta = beta_ref[...]
    half = dh // 2
    pr = lax.broadcasted_iota(I32, (dh, dh), 0)
    pc = lax.broadcasted_iota(I32, (dh, dh), 1)
    swap = jnp.where(((pr + half) & (dh - 1)) == pc, 1.0, 0.0).astype(BF16)

    def rope(x_ref, cos_ref, sin_ref, sl):
        xb = x_ref[0, sl, :]
        rolled = jnp.dot(xb, swap, preferred_element_type=F32).astype(BF16)
        return xb * cos_ref[sl, :] + rolled * sin_ref[sl, :]

    zeta = jnp.exp(lg * (chunk - 1.0 - idx))
    for n in range(seq // chunk):
        sl = slice(n * chunk, (n + 1) * chunk)
        q_scr[sl, :] = rope(q_ref, cq_ref, sq_ref, sl)
        kb = rope(k_ref, ck_ref, sk_ref, sl)
        k_scr[sl, :] = kb
        kzt_scr[:, sl] = (kb.astype(F32) * zeta).T.astype(BF16)

    decay = decay.astype(BF16)
    r = jnp.zeros((dh, dh), F32)
    for n in range(seq // chunk):
        sl = slice(n * chunk, (n + 1) * chunk)
        qb = q_scr[sl, :]
        v = v_ref[0, sl, :]
        s = lax.dot_general(qb, k_scr[sl, :], (((1,), (1,)), ((), ())),
                            preferred_element_type=F32).astype(BF16) * decay
        y = jnp.dot(s, v, preferred_element_type=F32)
        y_scr[sl, :] = y + jnp.dot(qb, r.astype(BF16), preferred_element_type=F32) * xi
        r = r * g_chunk + jnp.dot(kzt_scr[:, sl], v, preferred_element_type=F32)

    y = y_scr[...]
    mu = jnp.mean(y, axis=-1, keepdims=True)
    yc = y - mu
    var = jnp.mean(yc * yc, axis=-1, keepdims=True)
    yn = yc * lax.rsqrt(var + GN_EPS) * beta
    o_ref[0] = (_silu(g_ref[0].astype(F32)) * yn).astype(BF16)


def _retention(proj3, col0, cosf, sinf, log_g, beta_ret, chunk=RET_CHUNK):
    bn, seq, _ = proj3.shape
    dh = RET_HEAD_DIM
    heads = beta_ret.shape[0] // dh
    blk = lambda off: pl.BlockSpec((1, seq, dh), lambda b, h, off=off: (b, 0, col0 + off + h))
    k_scale = dh ** -0.5
    kern = functools.partial(_ret_kernel, chunk=chunk)
    return pl.pallas_call(
        kern,
        grid=(bn, heads),
        in_specs=[blk(0), blk(heads), blk(2 * heads), blk(3 * heads),
                  pl.BlockSpec((seq, dh), lambda b, h: (0, 0)),
                  pl.BlockSpec((seq, dh), lambda b, h: (0, 0)),
                  pl.BlockSpec((seq, dh), lambda b, h: (0, 0)),
                  pl.BlockSpec((seq, dh), lambda b, h: (0, 0)),
                  pl.BlockSpec((1, 1, LANES), lambda b, h: (h, 0, 0)),
                  pl.BlockSpec((1, dh), lambda b, h: (0, h))],
        out_specs=pl.BlockSpec((1, seq, dh), lambda b, h: (b, 0, h)),
        out_shape=jax.ShapeDtypeStruct((bn, seq, heads * dh), BF16),
        scratch_shapes=[pltpu.VMEM((seq, dh), BF16), pltpu.VMEM((seq, dh), BF16),
                        pltpu.VMEM((dh, seq), BF16), pltpu.VMEM((seq, dh), F32)],
        compiler_params=_cparams(("arbitrary", "arbitrary")),
        name="retention",
    )(proj3, proj3, proj3, proj3, cosf.astype(BF16), sinf.astype(BF16),
      (cosf * k_scale).astype(BF16), (sinf * k_scale).astype(BF16),
      log_g, beta_ret.reshape(1, heads * dh))


def _gelu_tanh(x):
    return 0.5 * x * (1.0 + jnp.tanh(math.sqrt(2.0 / math.pi) * (x + 0.044715 * (x * x * x))))


def _route_rows(lg, base):
    tm = lg.shape[0]
    lane = lax.broadcasted_iota(I32, (tm, LANES), 1).astype(F32)
    neg = jnp.float32(-jnp.inf)

    def first_max(vals):
        m = jnp.max(vals, axis=-1, keepdims=True)
        idx = jnp.min(jnp.where(vals == m, lane, float(LANES)), axis=-1, keepdims=True)
        return m, idx

    gl = jnp.where(lane < N_GROUPS, lg, neg)
    gmax, gsel = first_max(gl)
    g_w = 1.0 / jnp.sum(jnp.exp(gl - gmax), axis=-1, keepdims=True)
    lo = ROUTE_LANE0 + gsel * EXPERTS_PER_GROUP
    el = jnp.where((lane >= lo) & (lane < lo + EXPERTS_PER_GROUP), lg, neg)
    m1, i1 = first_max(el)
    m2, i2 = first_max(jnp.where(lane == i1, neg, el))
    e21 = jnp.exp(m2 - m1)
    w1 = g_w / (1.0 + e21)
    w2 = g_w * e21 / (1.0 + e21)

    sel1 = lane == i1
    sel2 = lane == i2
    onehot = jnp.where(sel1 | sel2, 1.0, 0.0).astype(BF16)
    ri = lax.broadcasted_iota(I32, (tm, tm), 0)
    ci = lax.broadcasted_iota(I32, (tm, tm), 1)
    tri = jnp.where(ci < ri, 1.0, 0.0).astype(BF16)
    excl = jnp.dot(tri, onehot, preferred_element_type=F32) + base
    r1 = jnp.sum(jnp.where(sel1, excl, 0.0), axis=-1, keepdims=True)
    r2 = jnp.sum(jnp.where(sel2, excl, 0.0), axis=-1, keepdims=True)
    total = base + jnp.sum(jnp.where(sel1 | sel2, 1.0, 0.0), axis=0, keepdims=True)
    meta = (jnp.where(lane == 0, i1, 0.0) + jnp.where(lane == 1, i2, 0.0)
            + jnp.where(lane == 2, r1, 0.0) + jnp.where(lane == 3, r2, 0.0))
    w = jnp.where(lane == 0, w1, 0.0) + jnp.where(lane == 1, w2, 0.0)
    return meta, w, total


def _mix_kernel(ys_ref, yr_ref, x_ref, mod_ref, wglu_ref, bssm_ref, wout_ref, g2_ref,
                wr_ref, br_ref, x1_ref, h2_ref, meta_ref, w_ref, cnt_ref, carry):
    @pl.when(pl.program_id(0) == 0)
    def _():
        carry[...] = jnp.zeros(carry.shape, F32)

    tm, sw = ys_ref.shape
    m = mod_ref[0]
    counts = carry[0:1, :]
    sub = tm // MIX_SUBTILES
    rows = [slice(h * sub, (h + 1) * sub) for h in range(MIX_SUBTILES)]
    z = [_gelu_tanh(ys_ref[r, :]) for r in rows]
    gate = [jax.nn.sigmoid(jnp.dot(zz.astype(BF16), wglu_ref[...], preferred_element_type=F32))
            for zz in z]
    o = [zz * gg for zz, gg in zip(z, gate)]
    y_ssm = [oo * lax.rsqrt(jnp.mean(oo * oo, axis=-1, keepdims=True) + NORM_EPS) * bssm_ref[...]
             for oo in o]
    mixed = [jnp.dot(ys.astype(BF16), wout_ref[:sw, :], preferred_element_type=F32)
             + jnp.dot(yr_ref[r, :], wout_ref[sw:, :], preferred_element_type=F32)
             for ys, r in zip(y_ssm, rows)]
    x1 = [x_ref[r, :] + m[2:3, :] * mm for r, mm in zip(rows, mixed)]
    for r, xx in zip(rows, x1):
        x1_ref[r, :] = xx
    y = [xx * lax.rsqrt(jnp.mean(xx * xx, axis=-1, keepdims=True) + NORM_EPS) * g2_ref[...]
         for xx in x1]
    h2 = [yy * (1.0 + m[4:5, :]) + m[3:4, :] for yy in y]
    half = h2[0].shape[1] // 2
    for r, hh in zip(rows, h2):
        h2_ref[r, :] = _pack_bf16_pairs(hh[:, :half], hh[:, half:])
    logits = [jnp.dot(hh.astype(BF16), wr_ref[...], preferred_element_type=F32) + br_ref[...]
              for hh in h2]
    for r, lg in zip(rows, logits):
        meta, w, counts = _route_rows(lg, counts)
        meta_ref[r, :] = meta
        w_ref[r, :] = w
    carry[...] = jnp.broadcast_to(counts, carry.shape)
    cnt_ref[...] = jnp.broadcast_to(counts, cnt_ref.shape)


def _mix(ypre, yret, x2d, mod3, wglu_bf, beta_ssm, wout_bf, g2, wr_bf, br, seq, tm=512):
    t, d = x2d.shape
    sw = wglu_bf.shape[0]
    tpb = seq // tm
    const = lambda shape: pl.BlockSpec(shape, lambda i: (0,) * len(shape),
                                       pipeline_mode=pl.Buffered(1))
    return pl.pallas_call(
        _mix_kernel,
        grid=(t // tm,),
        in_specs=[pl.BlockSpec((tm, sw), lambda i: (i, 0)),
                  pl.BlockSpec((tm, d - sw), lambda i: (i, 0)),
                  pl.BlockSpec((tm, d), lambda i: (i, 0)),
                  pl.BlockSpec((1, N_MOD, d), lambda i: (i // tpb, 0, 0)),
                  const((sw, sw)), const((1, sw)), const((d, d)), const((1, d)),
                  const((d, LANES)), const((1, LANES))],
        out_specs=[pl.BlockSpec((tm, d), lambda i: (i, 0)),
                   pl.BlockSpec((tm, d // 2), lambda i: (i, 0)),
                   pl.BlockSpec((tm, LANES), lambda i: (i, 0)),
                   pl.BlockSpec((tm, LANES), lambda i: (i, 0)),
                   pl.BlockSpec((SUBLANES, LANES), lambda i: (0, 0))],
        out_shape=[jax.ShapeDtypeStruct((t, d), F32),
                   jax.ShapeDtypeStruct((t, d // 2), jnp.uint32),
                   jax.ShapeDtypeStruct((t, LANES), F32),
                   jax.ShapeDtypeStruct((t, LANES), F32),
                   jax.ShapeDtypeStruct((SUBLANES, LANES), F32)],
        scratch_shapes=[pltpu.VMEM((SUBLANES, LANES), F32)],
        compiler_params=_cparams(("arbitrary",)),
        name="mix_outproj_router",
    )(ypre, yret, x2d, mod3, wglu_bf, beta_ssm.reshape(1, sw), wout_bf, g2.reshape(1, d),
      wr_bf, br)


def _lane_cumsum(x):
    lane = lax.broadcasted_iota(I32, x.shape, 1)
    sh = 1
    while sh < LANES:
        x = x + jnp.where(lane >= sh, pltpu.roll(x, sh, axis=1), 0)
        sh *= 2
    return x


def _dest_kernel(meta_ref, cnt_ref, dest_ref, be_ref, *, nb):
    tm = meta_ref.shape[0]
    shift = MOE_BLOCK.bit_length() - 1
    cnt = cnt_ref[...].astype(I32)
    padded = ((cnt + (MOE_BLOCK - 1)) >> shift) << shift
    pend_i = _lane_cumsum(padded)
    pend = pend_i.astype(F32)[0:1, :]
    poff = (pend_i - padded).astype(F32)[0:1, :]
    meta = meta_ref[...]
    lane = lax.broadcasted_iota(I32, (tm, LANES), 1).astype(F32)
    pick = lambda col: jnp.sum(jnp.where(lane == col, meta, 0.0), axis=-1, keepdims=True)
    i1, i2, r1, r2 = pick(0), pick(1), pick(2), pick(3)
    d1 = jnp.sum(jnp.where(lane == i1, poff, 0.0), axis=-1, keepdims=True) + r1
    d2 = jnp.sum(jnp.where(lane == i2, poff, 0.0), axis=-1, keepdims=True) + r2
    dest_ref[...] = (jnp.where(lane == 0, d1, 0.0) + jnp.where(lane == 1, d2, 0.0)).astype(I32)

    nrow = be_ref.shape[0]
    blk = lax.broadcasted_iota(I32, (nrow, LANES), 0).astype(F32)
    lane_b = lax.broadcasted_iota(I32, (nrow, LANES), 1).astype(F32)
    is_e = (lane_b >= ROUTE_LANE0) & (lane_b < ROUTE_LANE0 + N_EXPERTS)
    below = jnp.sum(jnp.where(is_e & (pend <= blk * MOE_BLOCK), 1.0, 0.0), axis=-1, keepdims=True)
    block_e = jnp.minimum(below, N_EXPERTS - 1.0)
    used = jnp.sum(jnp.where(lane_b == ROUTE_LANE0 + N_EXPERTS - 1, pend, 0.0),
                   axis=-1, keepdims=True) * (1.0 / MOE_BLOCK)
    be = jnp.where(blk[:, :1] == nb, used, block_e)
    be_ref[...] = jnp.broadcast_to(be, (nrow, LANES)).astype(I32)


def _dest(meta, cnt, nb, tm=4096):
    t = meta.shape[0]
    tm = math.gcd(tm, t)
    nrow = -(-(nb + 1) // SUBLANES) * SUBLANES
    kern = functools.partial(_dest_kernel, nb=nb)
    return pl.pallas_call(
        kern,
        grid=(t // tm,),
        in_specs=[pl.BlockSpec((tm, LANES), lambda i: (i, 0)),
                  pl.BlockSpec((SUBLANES, LANES), lambda i: (0, 0))],
        out_specs=[pl.BlockSpec((tm, LANES), lambda i: (i, 0)),
                   pl.BlockSpec((nrow, LANES), lambda i: (0, 0))],
        out_shape=[jax.ShapeDtypeStruct((t, LANES), I32),
                   jax.ShapeDtypeStruct((nrow, LANES), I32)],
        compiler_params=_cparams(("arbitrary",)),
        name="route_dest",
    )(meta, cnt)


def _invert_kernel(dest_ref, fill_hbm, row_ref, sem):
    cp = pltpu.make_async_copy(fill_hbm, row_ref, sem)
    cp.start()
    cp.wait()

    half = dest_ref.shape[0] // 2

    def put(tok, c):
        row_ref[dest_ref[tok]] = tok
        row_ref[dest_ref[tok + half]] = tok + half
        return c

    lax.fori_loop(0, half, put, 0, unroll=16)


def _invert(dest_flat, n_slot, t):
    spare = MOE_SPARE_SETS * MOE_BLOCK
    fill = 2 * t + (jnp.arange(n_slot, dtype=I32) & (spare - 1))
    return pl.pallas_call(
        _invert_kernel,
        in_specs=[pl.BlockSpec(memory_space=pltpu.SMEM), pl.BlockSpec(memory_space=pl.ANY)],
        out_specs=pl.BlockSpec(memory_space=pltpu.SMEM),
        out_shape=jax.ShapeDtypeStruct((n_slot,), I32),
        scratch_shapes=[pltpu.SemaphoreType.DMA(())],
        name="route_invert",
    )(dest_flat, fill)


def _expert_weights(b, used, be_ref, w_hbms, wst, wsem, run_ref, load):
    def fetch(e, p):
        for i, w in enumerate(w_hbms):
            pltpu.make_async_copy(w.at[e], wst.at[p, i], wsem.at[p]).start(priority=1)

    @pl.when((b == 0) & (used > 0))
    def _():
        run_ref[0] = 0
        fetch(be_ref[0], 0)

    bc = jnp.minimum(b, be_ref.shape[0] - 2)
    new_expert = (b == 0) | (be_ref[bc] != be_ref[jnp.maximum(bc - 1, 0)])

    @pl.when(new_expert & (b < used))
    def _():
        p = run_ref[0]
        for i, w in enumerate(w_hbms):
            pltpu.make_async_copy(w.at[0], wst.at[p, i], wsem.at[p]).wait()
        load(p)
        e = be_ref[b]
        nxt = lax.while_loop(lambda i: (i < used) & (be_ref[i] == e), lambda i: i + 1, b + 1)

        @pl.when(nxt < used)
        def _():
            fetch(be_ref[nxt], 1 - p)

        run_ref[0] = 1 - p


def _moe_up_kernel(row_ref, be_ref, h2_hbm, wg_hbm, wu_hbm, o_ref, wst, wbuf, xbuf, xb_scr, sem,
                   wsem, run_ref, *, nb):
    b = pl.program_id(0)
    used = be_ref[nb]
    blk = MOE_BLOCK
    fe = wg_hbm.shape[2]
    t = h2_hbm.shape[0]

    def gather_rows(block, rows):
        slot = block % MOE_GATHER_DEPTH
        base = block * blk
        for r in rows:
            row = row_ref[base + r]
            tok = jnp.where(row >= t, row - t, row)
            tok = jnp.where(tok >= t, 0, tok)
            pltpu.make_async_copy(h2_hbm.at[pl.ds(tok, 1)], xbuf.at[slot, pl.ds(r, 1)],
                                  sem.at[slot]).start()

    ahead = MOE_GATHER_DEPTH - 1

    @pl.when(b == 0)
    def _():
        for first in range(ahead):
            @pl.when(first < used)
            def _():
                gather_rows(first, range(blk))

    def load(p):
        wbuf[:, :fe] = wst[p, 0].astype(BF16)
        wbuf[:, fe:] = wst[p, 1].astype(BF16)

    _expert_weights(b, used, be_ref, (wg_hbm, wu_hbm), wst, wsem, run_ref, load)

    nchunk = fe // MOE_NCHUNK_COLS
    rows_per = blk // nchunk

    def step(prefetch):
        slot = b % MOE_GATHER_DEPTH
        pltpu.make_async_copy(xbuf.at[slot], xbuf.at[slot], sem.at[slot]).wait()
        lo, hi = _unpack_bf16_pairs(xbuf[slot])
        half = lo.shape[1]
        xb_scr[:, :half] = lo.astype(BF16)
        xb_scr[:, half:] = hi.astype(BF16)
        for c in range(nchunk):
            if prefetch:
                gather_rows(b + ahead, range(c * rows_per, (c + 1) * rows_per))
            cols = slice(c * MOE_NCHUNK_COLS, (c + 1) * MOE_NCHUNK_COLS)
            ucols = slice(fe + c * MOE_NCHUNK_COLS, fe + (c + 1) * MOE_NCHUNK_COLS)
            g = jnp.dot(xb_scr[...], wbuf[:, cols], preferred_element_type=F32)
            u = jnp.dot(xb_scr[...], wbuf[:, ucols], preferred_element_type=F32)
            o_ref[:, cols] = (_silu(g) * u).astype(BF16)

    @pl.when(b + ahead < used)
    def _():
        step(True)

    @pl.when((b < used) & (b + ahead >= used))
    def _():
        step(False)

    @pl.when(b >= used)
    def _():
        o_ref[...] = jnp.zeros(o_ref.shape, BF16)


def _moe_up(slot_row, block_e, h2, w_gate, w_up, nb):
    d, fe = w_gate.shape[1], w_gate.shape[2]
    kern = functools.partial(_moe_up_kernel, nb=nb)
    grid_spec = pltpu.PrefetchScalarGridSpec(
        num_scalar_prefetch=2,
        grid=(nb,),
        in_specs=[pl.BlockSpec(memory_space=pl.ANY),
                  pl.BlockSpec(memory_space=pl.ANY),
                  pl.BlockSpec(memory_space=pl.ANY)],
        out_specs=pl.BlockSpec((MOE_BLOCK, fe), lambda b, s, e: (b, 0)),
        scratch_shapes=[pltpu.VMEM((2, 2, d, fe), F32),
                        pltpu.VMEM((d, 2 * fe), BF16),
                        pltpu.VMEM((MOE_GATHER_DEPTH, MOE_BLOCK, d // 2), jnp.uint32),
                        pltpu.VMEM((MOE_BLOCK, d), BF16),
                        pltpu.SemaphoreType.DMA((MOE_GATHER_DEPTH,)),
                        pltpu.SemaphoreType.DMA((2,)),
                        pltpu.SMEM((1,), I32)],
    )
    return pl.pallas_call(
        kern,
        grid_spec=grid_spec,
        out_shape=jax.ShapeDtypeStruct((nb * MOE_BLOCK, fe), BF16),
        compiler_params=_cparams(("arbitrary",)),
        name="moe_up",
    )(slot_row, block_e, h2, w_gate, w_up)


def _moe_down_kernel(row_ref, be_ref, hm_ref, wd_hbm, ys_hbm, wst, wbuf, obuf, sem, wsem, run_ref,
                     *, nb, t):
    b = pl.program_id(0)
    used = be_ref[nb]
    blk = MOE_BLOCK

    def wait_slot(slot):
        pltpu.make_async_copy(obuf.at[slot], obuf.at[slot], sem.at[slot]).wait()

    depth = obuf.shape[0]

    @pl.when(b == 0)
    def _():
        obuf[0] = jnp.zeros(obuf.shape[1:], jnp.uint32)
        for k in range(MOE_SPARE_SETS):
            spare = ys_hbm.at[pl.ds(2 * t + k * blk, blk)]
            pltpu.make_async_copy(obuf.at[0], spare, sem.at[depth]).start()
        for k in range(MOE_SPARE_SETS):
            spare = ys_hbm.at[pl.ds(2 * t + k * blk, blk)]
            pltpu.make_async_copy(obuf.at[0], spare, sem.at[depth]).wait()

    @pl.when((b >= depth) & (b - depth < used))
    def _():
        wait_slot(b % depth)

    def load(p):
        wbuf[...] = wst[p, 0].astype(BF16)

    _expert_weights(b, used, be_ref, (wd_hbm,), wst, wsem, run_ref, load)

    half = wbuf.shape[1] // 2
    nchunk = half // MOE_NCHUNK_COLS
    rows_per = blk // nchunk

    def step(scatter, matmul):
        slot = (b - 1) % depth
        base = (b - 1) * blk
        for c in range(nchunk):
            if scatter:
                for r in range(c * rows_per, (c + 1) * rows_per):
                    pltpu.make_async_copy(obuf.at[slot, pl.ds(r, 1)],
                                          ys_hbm.at[pl.ds(row_ref[base + r], 1)],
                                          sem.at[slot]).start(priority=r % 2)
            if matmul:
                cols = slice(c * MOE_NCHUNK_COLS, (c + 1) * MOE_NCHUNK_COLS)
                hcols = slice(half + c * MOE_NCHUNK_COLS, half + (c + 1) * MOE_NCHUNK_COLS)
                lo = jnp.dot(hm_ref[...], wbuf[:, cols], preferred_element_type=F32)
                hi = jnp.dot(hm_ref[...], wbuf[:, hcols], preferred_element_type=F32)
                obuf[b % depth, :, cols] = _pack_bf16_pairs(lo, hi)

    @pl.when((b >= 1) & (b < used))
    def _():
        step(True, True)

    @pl.when((b == 0) & (b < used))
    def _():
        step(False, True)

    @pl.when((b >= 1) & (b == used))
    def _():
        step(True, False)

    @pl.when(b == nb)
    def _():
        for k in range(1, depth):
            @pl.when(nb - k < used)
            def _():
                wait_slot((nb - k) % depth)


def _moe_down(slot_row, block_e, hmid, w_down, nb, t):
    fe, d = w_down.shape[1], w_down.shape[2]
    kern = functools.partial(_moe_down_kernel, nb=nb, t=t)
    grid_spec = pltpu.PrefetchScalarGridSpec(
        num_scalar_prefetch=2,
        grid=(nb + 1,),
        in_specs=[pl.BlockSpec((MOE_BLOCK, fe), lambda b, s, e: (jnp.minimum(b, nb - 1), 0)),
                  pl.BlockSpec(memory_space=pl.ANY)],
        out_specs=pl.BlockSpec(memory_space=pl.ANY),
        scratch_shapes=[pltpu.VMEM((2, 1, fe, d), F32),
                        pltpu.VMEM((fe, d), BF16),
                        pltpu.VMEM((MOE_SCATTER_DEPTH, MOE_BLOCK, d // 2), jnp.uint32),
                        pltpu.SemaphoreType.DMA((MOE_SCATTER_DEPTH + 1,)),
                        pltpu.SemaphoreType.DMA((2,)),
                        pltpu.SMEM((1,), I32)],
    )
    return pl.pallas_call(
        kern,
        grid_spec=grid_spec,
        out_shape=jax.ShapeDtypeStruct((2 * t + MOE_SPARE_SETS * MOE_BLOCK, d // 2), jnp.uint32),
        compiler_params=_cparams(("arbitrary",)),
        name="moe_down",
    )(slot_row, block_e, hmid, w_down)


def _final_kernel(x1_ref, y0_ref, y1_ref, w_ref, mod_ref, gf_ref, o_ref):
    w = w_ref[...]
    lo0, hi0 = _unpack_bf16_pairs(y0_ref[...])
    lo1, hi1 = _unpack_bf16_pairs(y1_ref[...])
    y = jnp.concatenate([w[:, 0:1] * lo0 + w[:, 1:2] * lo1,
                         w[:, 0:1] * hi0 + w[:, 1:2] * hi1], axis=1)
    x2 = x1_ref[...] + mod_ref[0][5:6, :] * y
    o_ref[...] = x2 * lax.rsqrt(jnp.mean(x2 * x2, axis=-1, keepdims=True) + NORM_EPS) * gf_ref[...]


def _final(x1, ys, w, mod3, g_final, seq, tm=512):
    t, d = x1.shape
    tpb = seq // tm
    return pl.pallas_call(
        _final_kernel,
        grid=(t // tm,),
        in_specs=[pl.BlockSpec((tm, d), lambda i: (i, 0)),
                  pl.BlockSpec((tm, d // 2), lambda i: (i, 0)),
                  pl.BlockSpec((tm, d // 2), lambda i: (i + t // tm, 0)),
                  pl.BlockSpec((tm, LANES), lambda i: (i, 0)),
                  pl.BlockSpec((1, N_MOD, d), lambda i: (i // tpb, 0, 0)),
                  pl.BlockSpec((1, d), lambda i: (0, 0))],
        out_specs=pl.BlockSpec((tm, d), lambda i: (i, 0)),
        out_shape=jax.ShapeDtypeStruct((t, d), F32),
        compiler_params=_cparams(("arbitrary",)),
        name="combine_final_norm",
    )(x1, ys, ys, w, mod3, g_final.reshape(1, d))


def _rope_tables(seq, dh):
    half = dh // 2
    inv_freq = 1.0 / (ROPE_BASE ** (jnp.arange(half, dtype=F32) * 2.0 / dh))
    ang = jnp.arange(seq, dtype=F32)[:, None] * inv_freq[None, :]
    cos, sin = jnp.cos(ang), jnp.sin(ang)
    return jnp.concatenate([cos, cos], axis=1), jnp.concatenate([-sin, sin], axis=1)


def _layer(x, mod, g_norm1, w_in, ssm_a_re, ssm_a_im, ssm_b_re, ssm_b_im, ssm_c_re, ssm_c_im,
           ssm_d, ssm_log_dt, w_glu, beta_ssm, beta_ret, w_out, g_norm2, w_rg, b_rg, w_re, b_re,
           w_gate, w_up, w_down):
    bn, seq, d = x.shape
    t = bn * seq
    x2d = x.reshape(t, d)
    mod3 = mod.reshape(bn, N_MOD, d)

    sw = w_glu.shape[0]
    proj = _inproj(x2d, mod3, g_norm1, w_in.astype(BF16), seq)
    proj3 = proj.reshape(bn, seq, proj.shape[1])

    ab_re, ab_im, bbt_re, bbt_im = _s5_disc(ssm_a_re, ssm_a_im, ssm_log_dt,
                                            jnp.swapaxes(ssm_b_re, 1, 2), jnp.swapaxes(ssm_b_im, 1, 2))
    bz, cz, a_re, a_im = _s5_pack(ab_re, ab_im, bbt_re, bbt_im, ssm_c_re, ssm_c_im)
    ypre = _s5_scan(proj3, bz, cz, a_re, a_im, ssm_d.reshape(-1)).reshape(t, sw)

    heads = (d - sw) // RET_HEAD_DIM
    cosf, sinf = _rope_tables(seq, RET_HEAD_DIM)
    gamma = 1.0 - jnp.exp2(-5.0 - jnp.arange(heads, dtype=F32))
    log_g = jnp.broadcast_to(jnp.log(gamma)[:, None, None], (heads, 1, LANES))
    yret = _retention(proj3, sw // RET_HEAD_DIM, cosf, sinf, log_g, beta_ret)

    n_route = N_GROUPS + N_EXPERTS
    wr = jnp.concatenate([w_rg, w_re, jnp.zeros((d, LANES - n_route), F32)], axis=1)
    br = jnp.concatenate([b_rg, b_re, jnp.zeros((LANES - n_route,), F32)]).reshape(1, LANES)
    x1, h2, meta, w_tok, cnt = _mix(ypre, yret.reshape(t, d - sw), x2d, mod3, w_glu.astype(BF16),
                                    beta_ssm, w_out.astype(BF16), g_norm2, wr.astype(BF16), br, seq)

    nb = -(-(t * 2) // MOE_BLOCK) + N_EXPERTS
    dest, be = _dest(meta, cnt, nb)
    slot_row = _invert(dest[:, :2].T.reshape(-1), nb * MOE_BLOCK, t)
    block_e = be[:nb + 1, 0]

    hmid = _moe_up(slot_row, block_e, h2, w_gate, w_up, nb)
    ys = _moe_down(slot_row, block_e, hmid, w_down, nb, t)
    return x1, ys, w_tok, mod3


def kernel(x, c, w_ada, b_ada, g_norm1, w_in, ssm_a_re, ssm_a_im, ssm_b_re, ssm_b_im, ssm_c_re,
           ssm_c_im, ssm_d, ssm_log_dt, w_glu, beta_ssm, beta_ret, w_out, g_norm2, w_router_group,
           b_router_group, w_router_expert, b_router_expert, w_gate, w_up, w_down, g_final):
    depth = w_ada.shape[0]
    bn, seq, d = x.shape
    assert depth == 1, "the final norm is fused into the single layer's last kernel"
    l = 0
    mod = _ada(c, w_ada[l], b_ada[l])
    x1, ys, w_tok, mod3 = _layer(
        x, mod, g_norm1[l], w_in[l], ssm_a_re[l], ssm_a_im[l], ssm_b_re[l], ssm_b_im[l],
        ssm_c_re[l], ssm_c_im[l], ssm_d[l], ssm_log_dt[l], w_glu[l], beta_ssm[l], beta_ret[l],
        w_out[l], g_norm2[l], w_router_group[l], b_router_group[l], w_router_expert[l],
        b_router_expert[l], w_gate[l], w_up[l], w_down[l])
    out = _final(x1, ys, w_tok, mod3, g_final, seq)
    return out.reshape(bn, seq, d)
```

```python
import functools
import math

import jax
import jax.numpy as jnp
from jax import lax
from jax.experimental import pallas as pl
from jax.experimental.pallas import tpu as pltpu

F32 = jnp.float32
BF16 = jnp.bfloat16
I32 = jnp.int32

SSM_GROUP = 16
SSM_STATE = 64
RET_HEAD_DIM = 128
ROPE_BASE = 10000.0
N_GROUPS = 4
EXPERTS_PER_GROUP = 8
N_EXPERTS = N_GROUPS * EXPERTS_PER_GROUP
N_MOD = 6
NORM_EPS = 1e-6
GN_EPS = 1e-5

LANES = 128
SUBLANES = 8
VMEM_LIMIT = 56 * 1024 * 1024

ROUTE_LANE0 = N_GROUPS
MOE_BLOCK = 256
MOE_NCHUNK_COLS = 256
MOE_GATHER_DEPTH = 4
MOE_SCATTER_DEPTH = 4
MOE_SPARE_SETS = 4
RET_CHUNK = 256
INPROJ_SLICES = 4
MIX_SUBTILES = 2
S5_TT = 64
S5_BLOCKS_PER_STEP = 8


def _cparams(sem):
    return pltpu.CompilerParams(dimension_semantics=sem, vmem_limit_bytes=VMEM_LIMIT)


def _silu(x):
    return x * jax.nn.sigmoid(x)


def _pack_bf16_pairs(lo, hi):
    lo_w = pltpu.bitcast(lo.astype(BF16).astype(F32), jnp.uint32)
    hi_w = pltpu.bitcast(hi.astype(BF16).astype(F32), jnp.uint32)
    return (lo_w >> 16) | (hi_w & jnp.uint32(0xFFFF0000))


def _unpack_bf16_pairs(w):
    return (pltpu.bitcast(w << 16, F32), pltpu.bitcast(w & jnp.uint32(0xFFFF0000), F32))


def _ada_kernel(c_ref, w_ref, b_ref, o_ref):
    s = _silu(c_ref[...])
    o_ref[...] = jnp.dot(s.astype(BF16), w_ref[...].astype(BF16),
                         preferred_element_type=F32) + b_ref[...]


def _ada(c, w, b, tn=1024):
    bn, d = c.shape
    n = w.shape[1]
    return pl.pallas_call(
        _ada_kernel,
        grid=(n // tn,),
        in_specs=[pl.BlockSpec((bn, d), lambda j: (0, 0)),
                  pl.BlockSpec((d, tn), lambda j: (0, j)),
                  pl.BlockSpec((1, tn), lambda j: (0, j))],
        out_specs=pl.BlockSpec((bn, tn), lambda j: (0, j)),
        out_shape=jax.ShapeDtypeStruct((bn, n), F32),
        compiler_params=_cparams(("arbitrary",)),
        name="ada_mod",
    )(c, w, b.reshape(1, n))


def _inproj_kernel(x0_ref, xn_ref, mod0_ref, modn_ref, g_ref, w_ref, o_ref, h_even, h_odd, *,
                   nslices):
    i = pl.program_id(0)
    j = pl.program_id(1)

    def normed(x, m):
        y = x * lax.rsqrt(jnp.mean(x * x, axis=-1, keepdims=True) + NORM_EPS) * g_ref[...]
        return (y * (1.0 + m[1:2, :]) + m[0:1, :]).astype(BF16)

    @pl.when((i == 0) & (j == 0))
    def _():
        h_even[...] = normed(x0_ref[...], mod0_ref[0])

    rows = xn_ref.shape[0] // nslices
    start = pl.multiple_of(jnp.minimum(j, nslices - 1) * rows, rows)
    sl = pl.ds(start, rows)

    def step(h_cur, h_next):
        h_next[sl, :] = normed(xn_ref[sl, :], modn_ref[0])
        o_ref[...] = jnp.dot(h_cur[...], w_ref[...], preferred_element_type=F32).astype(BF16)

    @pl.when(i % 2 == 0)
    def _():
        step(h_even, h_odd)

    @pl.when(i % 2 == 1)
    def _():
        step(h_odd, h_even)


def _inproj(x2d, mod3, g1, w_in_bf, seq, tm=1024, tn=1024):
    t, d = x2d.shape
    n = w_in_bf.shape[1]
    tm = min(tm, seq)
    tn = math.gcd(tn, n)
    kern = functools.partial(_inproj_kernel, nslices=min(INPROJ_SLICES, n // tn))
    tpb = seq // tm
    last = t // tm - 1
    nxt = lambda i: jnp.minimum(i + 1, last)
    return pl.pallas_call(
        kern,
        grid=(t // tm, n // tn),
        in_specs=[pl.BlockSpec((tm, d), lambda i, j: (0, 0), pipeline_mode=pl.Buffered(1)),
                  pl.BlockSpec((tm, d), lambda i, j: (nxt(i), 0)),
                  pl.BlockSpec((1, N_MOD, d), lambda i, j: (0, 0, 0)),
                  pl.BlockSpec((1, N_MOD, d), lambda i, j: (nxt(i) // tpb, 0, 0)),
                  pl.BlockSpec((1, d), lambda i, j: (0, 0)),
                  pl.BlockSpec((d, tn), lambda i, j: (0, j))],
        out_specs=pl.BlockSpec((tm, tn), lambda i, j: (i, j)),
        out_shape=jax.ShapeDtypeStruct((t, n), BF16),
        scratch_shapes=[pltpu.VMEM((tm, d), BF16), pltpu.VMEM((tm, d), BF16)],
        compiler_params=_cparams(("arbitrary", "arbitrary")),
        name="norm_inproj",
    )(x2d, x2d, mod3, mod3, g1.reshape(1, d), w_in_bf)


def _s5_disc_kernel(are_ref, aim_ref, ldt_ref, bre_ref, bim_ref,
                    abre_ref, abim_ref, bbre_ref, bbim_ref):
    a_re = are_ref[...]
    a_im = aim_ref[...]
    dt = jnp.exp(ldt_ref[...])
    mag = jnp.exp(dt * a_re)
    ab_re = mag * jnp.cos(dt * a_im)
    ab_im = mag * jnp.sin(dt * a_im)
    inv_abs2 = 1.0 / (a_re * a_re + a_im * a_im)
    n_re = ab_re - 1.0
    f_re = (n_re * a_re + ab_im * a_im) * inv_abs2
    f_im = (ab_im * a_re - n_re * a_im) * inv_abs2
    abre_ref[...] = ab_re
    abim_ref[...] = ab_im
    b_re = bre_ref[...]
    b_im = bim_ref[...]
    fr = f_re[:, None, :]
    fi = f_im[:, None, :]
    bbre_ref[...] = fr * b_re - fi * b_im
    bbim_ref[...] = fr * b_im + fi * b_re


def _s5_disc(a_re, a_im, log_dt, bt_re, bt_im):
    g, p = a_re.shape
    h = bt_re.shape[1]
    return pl.pallas_call(
        _s5_disc_kernel,
        out_shape=[jax.ShapeDtypeStruct((g, p), F32), jax.ShapeDtypeStruct((g, p), F32),
                   jax.ShapeDtypeStruct((g, h, p), F32), jax.ShapeDtypeStruct((g, h, p), F32)],
        name="s5_discretise",
    )(a_re, a_im, log_dt.reshape(g, 1), bt_re, bt_im)


def _s5_pack(ab_re, ab_im, bbt_re, bbt_im, c_re, c_im):
    g, h, p = bbt_re.shape
    npair = g // 2
    ppb = LANES // (2 * h)
    eye2 = jnp.eye(2, dtype=F32)
    qsel = jax.nn.one_hot(jnp.arange(npair) % ppb, ppb, dtype=F32)

    def in_mat(bbt):
        b4 = bbt.reshape(npair, 2, h, p)
        return jnp.einsum('ab,xahp->xahbp', eye2, b4).reshape(npair, 2 * h, 2 * p)

    bsmall = jnp.concatenate([in_mat(bbt_re), in_mat(bbt_im)], axis=-1)
    bz = jnp.einsum('xq,xrc->xqrc', qsel, bsmall).reshape(npair, LANES, 4 * p)

    def out_mat(c):
        c4 = jnp.swapaxes(c, 1, 2).reshape(npair, 2, p, h)
        return jnp.einsum('ab,xaph->xapbh', eye2, c4).reshape(npair, 2 * p, 2 * h)

    csmall = jnp.concatenate([out_mat(c_re), -out_mat(c_im)], axis=1)
    cz = jnp.einsum('xq,xrc->xrqc', qsel, csmall).reshape(npair, 4 * p, LANES)
    a_re = ab_re.reshape(npair // ppb, ppb, 2 * p)
    a_im = ab_im.reshape(npair // ppb, ppb, 2 * p)
    return bz.astype(BF16), cz.astype(BF16), a_re, a_im


def _s5_kernel(u_ref, bz_ref, cz_ref, are_ref, aim_ref, d_ref, o_ref, state, *bufs, ppb):
    t = pl.program_id(0)
    j = pl.program_id(1)
    bn, tt, width = u_ref.shape
    nblk = width // LANES
    rt = bn * tt

    @pl.when(t == 0)
    def _():
        for i in range(nblk):
            state[j * nblk + i] = jnp.zeros(state.shape[1:], F32)

    for i in range(nblk):
        buf = bufs[i]
        kb = j * nblk + i
        lanes = slice(i * LANES, (i + 1) * LANES)

        u = u_ref[:, :, lanes].reshape(rt, LANES)
        for q in range(ppb):
            bu = jnp.dot(u, bz_ref[i * ppb + q], preferred_element_type=F32)
            for b in range(bn):
                rows = pl.ds(b, tt, stride=bn)
                buf[2 * q, rows, :] = bu[b * tt:(b + 1) * tt, :LANES]
                buf[2 * q + 1, rows, :] = bu[b * tt:(b + 1) * tt, LANES:]

        a_re = [jnp.broadcast_to(are_ref[i, q:q + 1, :], (bn, LANES)) for q in range(ppb)]
        a_im = [jnp.broadcast_to(aim_ref[i, q:q + 1, :], (bn, LANES)) for q in range(ppb)]
        s = [state[kb, k] for k in range(2 * ppb)]
        for step in range(tt):
            rows = slice(step * bn, (step + 1) * bn)
            for q in range(ppb):
                s_re, s_im = s[2 * q], s[2 * q + 1]
                n_re = a_re[q] * s_re - a_im[q] * s_im + buf[2 * q, rows, :]
                n_im = a_re[q] * s_im + a_im[q] * s_re + buf[2 * q + 1, rows, :]
                buf[2 * q, rows, :] = n_re
                buf[2 * q + 1, rows, :] = n_im
                s[2 * q], s[2 * q + 1] = n_re, n_im
        for k in range(2 * ppb):
            state[kb, k] = s[k]

        acc = None
        for q in range(ppb):
            lhs = jnp.concatenate([buf[2 * q], buf[2 * q + 1]], axis=1).astype(BF16)
            part = jnp.dot(lhs, cz_ref[i * ppb + q], preferred_element_type=F32)
            acc = part if acc is None else acc + part
        buf[2 * ppb] = acc
        d_skip = d_ref[:, lanes]
        for b in range(bn):
            o_ref[b, :, lanes] = (buf[2 * ppb, pl.ds(b, tt, stride=bn), :]
                                  + d_skip * u_ref[b, :, lanes].astype(F32))


def _s5_scan(u3, bz, cz, a_re, a_im, d_skip, tt=S5_TT, nblk=S5_BLOCKS_PER_STEP):
    bn, seq, width = u3.shape[0], u3.shape[1], d_skip.shape[0]
    nkb = width // LANES
    ppb = bz.shape[0] // nkb
    nblk = math.gcd(nblk, nkb)
    bw = nblk * LANES
    kern = functools.partial(_s5_kernel, ppb=ppb)
    return pl.pallas_call(
        kern,
        grid=(seq // tt, nkb // nblk),
        in_specs=[pl.BlockSpec((bn, tt, bw), lambda t, k: (0, t, k)),
                  pl.BlockSpec((nblk * ppb,) + bz.shape[1:], lambda t, k: (k, 0, 0)),
                  pl.BlockSpec((nblk * ppb,) + cz.shape[1:], lambda t, k: (k, 0, 0)),
                  pl.BlockSpec((nblk,) + a_re.shape[1:], lambda t, k: (k, 0, 0)),
                  pl.BlockSpec((nblk,) + a_im.shape[1:], lambda t, k: (k, 0, 0)),
                  pl.BlockSpec((1, bw), lambda t, k: (0, k))],
        out_specs=pl.BlockSpec((bn, tt, bw), lambda t, k: (0, t, k)),
        out_shape=jax.ShapeDtypeStruct((bn, seq, width), F32),
        scratch_shapes=[pltpu.VMEM((nkb, 2 * ppb, bn, LANES), F32)]
        + [pltpu.VMEM((2 * ppb + 1, bn * tt, LANES), F32) for _ in range(nblk)],
        compiler_params=_cparams(("arbitrary", "arbitrary")),
        name="s5_scan",
    )(u3, bz, cz, a_re, a_im, d_skip.reshape(1, width))


def _ret_kernel(q_ref, k_ref, v_ref, g_ref, cq_ref, sq_ref, ck_ref, sk_ref, lg_ref, beta_ref, o_ref,
                q_scr, k_scr, kzt_scr, y_scr, *, chunk):
    seq, dh = q_ref.shape[1], q_ref.shape[2]
    lg = lg_ref[0][:, :1]
    ri = lax.broadcasted_iota(I32, (chunk, chunk), 0)
    ci = lax.broadcasted_iota(I32, (chunk, chunk), 1)
    rel = (ri - ci).astype(F32)
    decay = jnp.where(rel >= 0, jnp.exp(lg * jnp.maximum(rel, 0.0)), 0.0)
    idx = lax.broadcasted_iota(I32, (chunk, 1), 0).astype(F32)
    xi = jnp.exp(lg * (idx + 1.0))
    g_chunk = jnp.exp(lg * float(chunk))
    beta = beta_ref[...]
    half = dh // 2
    pr = lax.broadcasted_iota(I32, (dh, dh), 0)
    pc = lax.broadcasted_iota(I32, (dh, dh), 1)
    swap = jnp.where(((pr + half) & (dh - 1)) == pc, 1.0, 0.0).astype(BF16)

    def rope(x_ref, cos_ref, sin_ref, sl):
        xb = x_ref[0, sl, :]
        rolled = jnp.dot(xb, swap, preferred_element_type=F32).astype(BF16)
        return xb * cos_ref[sl, :] + rolled * sin_ref[sl, :]

    zeta = jnp.exp(lg * (chunk - 1.0 - idx))
    for n in range(seq // chunk):
        sl = slice(n * chunk, (n + 1) * chunk)
        q_scr[sl, :] = rope(q_ref, cq_ref, sq_ref, sl)
        kb = rope(k_ref, ck_ref, sk_ref, sl)
        k_scr[sl, :] = kb
        kzt_scr[:, sl] = (kb.astype(F32) * zeta).T.astype(BF16)

    decay = decay.astype(BF16)
    r = jnp.zeros((dh, dh), F32)
    for n in range(seq // chunk):
        sl = slice(n * chunk, (n + 1) * chunk)
        qb = q_scr[sl, :]
        v = v_ref[0, sl, :]
        s = lax.dot_general(qb, k_scr[sl, :], (((1,), (1,)), ((), ())),
                            preferred_element_type=F32).astype(BF16) * decay
        y = jnp.dot(s, v, preferred_element_type=F32)
        y_scr[sl, :] = y + jnp.dot(qb, r.astype(BF16), preferred_element_type=F32) * xi
        r = r * g_chunk + jnp.dot(kzt_scr[:, sl], v, preferred_element_type=F32)

    y = y_scr[...]
    mu = jnp.mean(y, axis=-1, keepdims=True)
    yc = y - mu
    var = jnp.mean(yc * yc, axis=-1, keepdims=True)
    yn = yc * lax.rsqrt(var + GN_EPS) * beta
    o_ref[0] = (_silu(g_ref[0].astype(F32)) * yn).astype(BF16)


def _retention(proj3, col0, cosf, sinf, log_g, beta_ret, chunk=RET_CHUNK):
    bn, seq, _ = proj3.shape
    dh = RET_HEAD_DIM
    heads = beta_ret.shape[0] // dh
    blk = lambda off: pl.BlockSpec((1, seq, dh), lambda b, h, off=off: (b, 0, col0 + off + h))
    k_scale = dh ** -0.5
    kern = functools.partial(_ret_kernel, chunk=chunk)
    return pl.pallas_call(
        kern,
        grid=(bn, heads),
        in_specs=[blk(0), blk(heads), blk(2 * heads), blk(3 * heads),
                  pl.BlockSpec((seq, dh), lambda b, h: (0, 0)),
                  pl.BlockSpec((seq, dh), lambda b, h: (0, 0)),
                  pl.BlockSpec((seq, dh), lambda b, h: (0, 0)),
                  pl.BlockSpec((seq, dh), lambda b, h: (0, 0)),
                  pl.BlockSpec((1, 1, LANES), lambda b, h: (h, 0, 0)),
                  pl.BlockSpec((1, dh), lambda b, h: (0, h))],
        out_specs=pl.BlockSpec((1, seq, dh), lambda b, h: (b, 0, h)),
        out_shape=jax.ShapeDtypeStruct((bn, seq, heads * dh), BF16),
        scratch_shapes=[pltpu.VMEM((seq, dh), BF16), pltpu.VMEM((seq, dh), BF16),
                        pltpu.VMEM((dh, seq), BF16), pltpu.VMEM((seq, dh), F32)],
        compiler_params=_cparams(("arbitrary", "arbitrary")),
        name="retention",
    )(proj3, proj3, proj3, proj3, cosf.astype(BF16), sinf.astype(BF16),
      (cosf * k_scale).astype(BF16), (sinf * k_scale).astype(BF16),
      log_g, beta_ret.reshape(1, heads * dh))


def _gelu_tanh(x):
    return 0.5 * x * (1.0 + jnp.tanh(math.sqrt(2.0 / math.pi) * (x + 0.044715 * (x * x * x))))


def _route_rows(lg, base):
    tm = lg.shape[0]
    lane = lax.broadcasted_iota(I32, (tm, LANES), 1).astype(F32)
    neg = jnp.float32(-jnp.inf)

    def first_max(vals):
        m = jnp.max(vals, axis=-1, keepdims=True)
        idx = jnp.min(jnp.where(vals == m, lane, float(LANES)), axis=-1, keepdims=True)
        return m, idx

    gl = jnp.where(lane < N_GROUPS, lg, neg)
    gmax, gsel = first_max(gl)
    g_w = 1.0 / jnp.sum(jnp.exp(gl - gmax), axis=-1, keepdims=True)
    lo = ROUTE_LANE0 + gsel * EXPERTS_PER_GROUP
    el = jnp.where((lane >= lo) & (lane < lo + EXPERTS_PER_GROUP), lg, neg)
    m1, i1 = first_max(el)
    m2, i2 = first_max(jnp.where(lane == i1, neg, el))
    e21 = jnp.exp(m2 - m1)
    w1 = g_w / (1.0 + e21)
    w2 = g_w * e21 / (1.0 + e21)

    sel1 = lane == i1
    sel2 = lane == i2
    onehot = jnp.where(sel1 | sel2, 1.0, 0.0).astype(BF16)
    ri = lax.broadcasted_iota(I32, (tm, tm), 0)
    ci = lax.broadcasted_iota(I32, (tm, tm), 1)
    tri = jnp.where(ci < ri, 1.0, 0.0).astype(BF16)
    excl = jnp.dot(tri, onehot, preferred_element_type=F32) + base
    r1 = jnp.sum(jnp.where(sel1, excl, 0.0), axis=-1, keepdims=True)
    r2 = jnp.sum(jnp.where(sel2, excl, 0.0), axis=-1, keepdims=True)
    total = base + jnp.sum(jnp.where(sel1 | sel2, 1.0, 0.0), axis=0, keepdims=True)
    meta = (jnp.where(lane == 0, i1, 0.0) + jnp.where(lane == 1, i2, 0.0)
            + jnp.where(lane == 2, r1, 0.0) + jnp.where(lane == 3, r2, 0.0))
    w = jnp.where(lane == 0, w1, 0.0) + jnp.where(lane == 1, w2, 0.0)
    return meta, w, total


def _mix_kernel(ys_ref, yr_ref, x_ref, mod_ref, wglu_ref, bssm_ref, wout_ref, g2_ref,
                wr_ref, br_ref, x1_ref, h2_ref, meta_ref, w_ref, cnt_ref, carry):
    @pl.when(pl.program_id(0) == 0)
    def _():
        carry[...] = jnp.zeros(carry.shape, F32)

    tm, sw = ys_ref.shape
    m = mod_ref[0]
    counts = carry[0:1, :]
    sub = tm // MIX_SUBTILES
    rows = [slice(h * sub, (h + 1) * sub) for h in range(MIX_SUBTILES)]
    z = [_gelu_tanh(ys_ref[r, :]) for r in rows]
    gate = [jax.nn.sigmoid(jnp.dot(zz.astype(BF16), wglu_ref[...], preferred_element_type=F32))
            for zz in z]
    o = [zz * gg for zz, gg in zip(z, gate)]
    y_ssm = [oo * lax.rsqrt(jnp.mean(oo * oo, axis=-1, keepdims=True) + NORM_EPS) * bssm_ref[...]
             for oo in o]
    mixed = [jnp.dot(ys.astype(BF16), wout_ref[:sw, :], preferred_element_type=F32)
             + jnp.dot(yr_ref[r, :], wout_ref[sw:, :], preferred_element_type=F32)
             for ys, r in zip(y_ssm, rows)]
    x1 = [x_ref[r, :] + m[2:3, :] * mm for r, mm in zip(rows, mixed)]
    for r, xx in zip(rows, x1):
        x1_ref[r, :] = xx
    y = [xx * lax.rsqrt(jnp.mean(xx * xx, axis=-1, keepdims=True) + NORM_EPS) * g2_ref[...]
         for xx in x1]
    h2 = [yy * (1.0 + m[4:5, :]) + m[3:4, :] for yy in y]
    half = h2[0].shape[1] // 2
    for r, hh in zip(rows, h2):
        h2_ref[r, :] = _pack_bf16_pairs(hh[:, :half], hh[:, half:])
    logits = [jnp.dot(hh.astype(BF16), wr_ref[...], preferred_element_type=F32) + br_ref[...]
              for hh in h2]
    for r, lg in zip(rows, logits):
        meta, w, counts = _route_rows(lg, counts)
        meta_ref[r, :] = meta
        w_ref[r, :] = w
    carry[...] = jnp.broadcast_to(counts, carry.shape)
    cnt_ref[...] = jnp.broadcast_to(counts, cnt_ref.shape)


def _mix(ypre, yret, x2d, mod3, wglu_bf, beta_ssm, wout_bf, g2, wr_bf, br, seq, tm=512):
    t, d = x2d.shape
    sw = wglu_bf.shape[0]
    tpb = seq // tm
    const = lambda shape: pl.BlockSpec(shape, lambda i: (0,) * len(shape),
                                       pipeline_mode=pl.Buffered(1))
    return pl.pallas_call(
        _mix_kernel,
        grid=(t // tm,),
        in_specs=[pl.BlockSpec((tm, sw), lambda i: (i, 0)),
                  pl.BlockSpec((tm, d - sw), lambda i: (i, 0)),
                  pl.BlockSpec((tm, d), lambda i: (i, 0)),
                  pl.BlockSpec((1, N_MOD, d), lambda i: (i // tpb, 0, 0)),
                  const((sw, sw)), const((1, sw)), const((d, d)), const((1, d)),
                  const((d, LANES)), const((1, LANES))],
        out_specs=[pl.BlockSpec((tm, d), lambda i: (i, 0)),
                   pl.BlockSpec((tm, d // 2), lambda i: (i, 0)),
                   pl.BlockSpec((tm, LANES), lambda i: (i, 0)),
                   pl.BlockSpec((tm, LANES), lambda i: (i, 0)),
                   pl.BlockSpec((SUBLANES, LANES), lambda i: (0, 0))],
        out_shape=[jax.ShapeDtypeStruct((t, d), F32),
                   jax.ShapeDtypeStruct((t, d // 2), jnp.uint32),
                   jax.ShapeDtypeStruct((t, LANES), F32),
                   jax.ShapeDtypeStruct((t, LANES), F32),
                   jax.ShapeDtypeStruct((SUBLANES, LANES), F32)],
        scratch_shapes=[pltpu.VMEM((SUBLANES, LANES), F32)],
        compiler_params=_cparams(("arbitrary",)),
        name="mix_outproj_router",
    )(ypre, yret, x2d, mod3, wglu_bf, beta_ssm.reshape(1, sw), wout_bf, g2.reshape(1, d),
      wr_bf, br)


def _lane_cumsum(x):
    lane = lax.broadcasted_iota(I32, x.shape, 1)
    sh = 1
    while sh < LANES:
        x = x + jnp.where(lane >= sh, pltpu.roll(x, sh, axis=1), 0)
        sh *= 2
    return x


def _dest_kernel(meta_ref, cnt_ref, dest_ref, be_ref, *, nb):
    tm = meta_ref.shape[0]
    shift = MOE_BLOCK.bit_length() - 1
    cnt = cnt_ref[...].astype(I32)
    padded = ((cnt + (MOE_BLOCK - 1)) >> shift) << shift
    pend_i = _lane_cumsum(padded)
    pend = pend_i.astype(F32)[0:1, :]
    poff = (pend_i - padded).astype(F32)[0:1, :]
    meta = meta_ref[...]
    lane = lax.broadcasted_iota(I32, (tm, LANES), 1).astype(F32)
    pick = lambda col: jnp.sum(jnp.where(lane == col, meta, 0.0), axis=-1, keepdims=True)
    i1, i2, r1, r2 = pick(0), pick(1), pick(2), pick(3)
    d1 = jnp.sum(jnp.where(lane == i1, poff, 0.0), axis=-1, keepdims=True) + r1
    d2 = jnp.sum(jnp.where(lane == i2, poff, 0.0), axis=-1, keepdims=True) + r2
    dest_ref[...] = (jnp.where(lane == 0, d1, 0.0) + jnp.where(lane == 1, d2, 0.0)).astype(I32)

    nrow = be_ref.shape[0]
    blk = lax.broadcasted_iota(I32, (nrow, LANES), 0).astype(F32)
    lane_b = lax.broadcasted_iota(I32, (nrow, LANES), 1).astype(F32)
    is_e = (lane_b >= ROUTE_LANE0) & (lane_b < ROUTE_LANE0 + N_EXPERTS)
    below = jnp.sum(jnp.where(is_e & (pend <= blk * MOE_BLOCK), 1.0, 0.0), axis=-1, keepdims=True)
    block_e = jnp.minimum(below, N_EXPERTS - 1.0)
    used = jnp.sum(jnp.where(lane_b == ROUTE_LANE0 + N_EXPERTS - 1, pend, 0.0),
                   axis=-1, keepdims=True) * (1.0 / MOE_BLOCK)
    be = jnp.where(blk[:, :1] == nb, used, block_e)
    be_ref[...] = jnp.broadcast_to(be, (nrow, LANES)).astype(I32)


def _dest(meta, cnt, nb, tm=4096):
    t = meta.shape[0]
    tm = math.gcd(tm, t)
    nrow = -(-(nb + 1) // SUBLANES) * SUBLANES
    kern = functools.partial(_dest_kernel, nb=nb)
    return pl.pallas_call(
        kern,
        grid=(t // tm,),
        in_specs=[pl.BlockSpec((tm, LANES), lambda i: (i, 0)),
                  pl.BlockSpec((SUBLANES, LANES), lambda i: (0, 0))],
        out_specs=[pl.BlockSpec((tm, LANES), lambda i: (i, 0)),
                   pl.BlockSpec((nrow, LANES), lambda i: (0, 0))],
        out_shape=[jax.ShapeDtypeStruct((t, LANES), I32),
                   jax.ShapeDtypeStruct((nrow, LANES), I32)],
        compiler_params=_cparams(("arbitrary",)),
        name="route_dest",
    )(meta, cnt)


def _invert_kernel(dest_ref, fill_hbm, row_ref, sem):
    cp = pltpu.make_async_copy(fill_hbm, row_ref, sem)
    cp.start()
    cp.wait()

    half = dest_ref.shape[0] // 2

    def put(tok, c):
        row_ref[dest_ref[tok]] = tok
        row_ref[dest_ref[tok + half]] = tok + half
        return c

    lax.fori_loop(0, half, put, 0, unroll=16)


def _invert(dest_flat, n_slot, t):
    spare = MOE_SPARE_SETS * MOE_BLOCK
    fill = 2 * t + (jnp.arange(n_slot, dtype=I32) & (spare - 1))
    return pl.pallas_call(
        _invert_kernel,
        in_specs=[pl.BlockSpec(memory_space=pltpu.SMEM), pl.BlockSpec(memory_space=pl.ANY)],
        out_specs=pl.BlockSpec(memory_space=pltpu.SMEM),
        out_shape=jax.ShapeDtypeStruct((n_slot,), I32),
        scratch_shapes=[pltpu.SemaphoreType.DMA(())],
        name="route_invert",
    )(dest_flat, fill)


def _expert_weights(b, used, be_ref, w_hbms, wst, wsem, run_ref, load):
    def fetch(e, p):
        for i, w in enumerate(w_hbms):
            pltpu.make_async_copy(w.at[e], wst.at[p, i], wsem.at[p]).start(priority=1)

    @pl.when((b == 0) & (used > 0))
    def _():
        run_ref[0] = 0
        fetch(be_ref[0], 0)

    bc = jnp.minimum(b, be_ref.shape[0] - 2)
    new_expert = (b == 0) | (be_ref[bc] != be_ref[jnp.maximum(bc - 1, 0)])

    @pl.when(new_expert & (b < used))
    def _():
        p = run_ref[0]
        for i, w in enumerate(w_hbms):
            pltpu.make_async_copy(w.at[0], wst.at[p, i], wsem.at[p]).wait()
        load(p)
        e = be_ref[b]
        nxt = lax.while_loop(lambda i: (i < used) & (be_ref[i] == e), lambda i: i + 1, b + 1)

        @pl.when(nxt < used)
        def _():
            fetch(be_ref[nxt], 1 - p)

        run_ref[0] = 1 - p


def _moe_up_kernel(row_ref, be_ref, h2_hbm, wg_hbm, wu_hbm, o_ref, wst, wbuf, xbuf, xb_scr, sem,
                   wsem, run_ref, *, nb):
    b = pl.program_id(0)
    used = be_ref[nb]
    blk = MOE_BLOCK
    fe = wg_hbm.shape[2]
    t = h2_hbm.shape[0]

    def gather_rows(block, rows):
        slot = block % MOE_GATHER_DEPTH
        base = block * blk
        for r in rows:
            row = row_ref[base + r]
            tok = jnp.where(row >= t, row - t, row)
            tok = jnp.where(tok >= t, 0, tok)
            pltpu.make_async_copy(h2_hbm.at[pl.ds(tok, 1)], xbuf.at[slot, pl.ds(r, 1)],
                                  sem.at[slot]).start()

    ahead = MOE_GATHER_DEPTH - 1

    @pl.when(b == 0)
    def _():
        for first in range(ahead):
            @pl.when(first < used)
            def _():
                gather_rows(first, range(blk))

    def load(p):
        wbuf[:, :fe] = wst[p, 0].astype(BF16)
        wbuf[:, fe:] = wst[p, 1].astype(BF16)

    _expert_weights(b, used, be_ref, (wg_hbm, wu_hbm), wst, wsem, run_ref, load)

    nchunk = fe // MOE_NCHUNK_COLS
    rows_per = blk // nchunk

    def step(prefetch):
        slot = b % MOE_GATHER_DEPTH
        pltpu.make_async_copy(xbuf.at[slot], xbuf.at[slot], sem.at[slot]).wait()
        lo, hi = _unpack_bf16_pairs(xbuf[slot])
        half = lo.shape[1]
        xb_scr[:, :half] = lo.astype(BF16)
        xb_scr[:, half:] = hi.astype(BF16)
        for c in range(nchunk):
            if prefetch:
                gather_rows(b + ahead, range(c * rows_per, (c + 1) * rows_per))
            cols = slice(c * MOE_NCHUNK_COLS, (c + 1) * MOE_NCHUNK_COLS)
            ucols = slice(fe + c * MOE_NCHUNK_COLS, fe + (c + 1) * MOE_NCHUNK_COLS)
            g = jnp.dot(xb_scr[...], wbuf[:, cols], preferred_element_type=F32)
            u = jnp.dot(xb_scr[...], wbuf[:, ucols], preferred_element_type=F32)
            o_ref[:, cols] = (_silu(g) * u).astype(BF16)

    @pl.when(b + ahead < used)
    def _():
        step(True)

    @pl.when((b < used) & (b + ahead >= used))
    def _():
        step(False)

    @pl.when(b >= used)
    def _():
        o_ref[...] = jnp.zeros(o_ref.shape, BF16)


def _moe_up(slot_row, block_e, h2, w_gate, w_up, nb):
    d, fe = w_gate.shape[1], w_gate.shape[2]
    kern = functools.partial(_moe_up_kernel, nb=nb)
    grid_spec = pltpu.PrefetchScalarGridSpec(
        num_scalar_prefetch=2,
        grid=(nb,),
        in_specs=[pl.BlockSpec(memory_space=pl.ANY),
                  pl.BlockSpec(memory_space=pl.ANY),
                  pl.BlockSpec(memory_space=pl.ANY)],
        out_specs=pl.BlockSpec((MOE_BLOCK, fe), lambda b, s, e: (b, 0)),
        scratch_shapes=[pltpu.VMEM((2, 2, d, fe), F32),
                        pltpu.VMEM((d, 2 * fe), BF16),
                        pltpu.VMEM((MOE_GATHER_DEPTH, MOE_BLOCK, d // 2), jnp.uint32),
                        pltpu.VMEM((MOE_BLOCK, d), BF16),
                        pltpu.SemaphoreType.DMA((MOE_GATHER_DEPTH,)),
                        pltpu.SemaphoreType.DMA((2,)),
                        pltpu.SMEM((1,), I32)],
    )
    return pl.pallas_call(
        kern,
        grid_spec=grid_spec,
        out_shape=jax.ShapeDtypeStruct((nb * MOE_BLOCK, fe), BF16),
        compiler_params=_cparams(("arbitrary",)),
        name="moe_up",
    )(slot_row, block_e, h2, w_gate, w_up)


def _moe_down_kernel(row_ref, be_ref, hm_ref, wd_hbm, ys_hbm, wst, wbuf, obuf, sem, wsem, run_ref,
                     *, nb, t):
    b = pl.program_id(0)
    used = be_ref[nb]
    blk = MOE_BLOCK

    def wait_slot(slot):
        pltpu.make_async_copy(obuf.at[slot], obuf.at[slot], sem.at[slot]).wait()

    depth = obuf.shape[0]

    @pl.when(b == 0)
    def _():
        obuf[0] = jnp.zeros(obuf.shape[1:], jnp.uint32)
        for k in range(MOE_SPARE_SETS):
            spare = ys_hbm.at[pl.ds(2 * t + k * blk, blk)]
            pltpu.make_async_copy(obuf.at[0], spare, sem.at[depth]).start()
        for k in range(MOE_SPARE_SETS):
            spare = ys_hbm.at[pl.ds(2 * t + k * blk, blk)]
            pltpu.make_async_copy(obuf.at[0], spare, sem.at[depth]).wait()

    @pl.when((b >= depth) & (b - depth < used))
    def _():
        wait_slot(b % depth)

    def load(p):
        wbuf[...] = wst[p, 0].astype(BF16)

    _expert_weights(b, used, be_ref, (wd_hbm,), wst, wsem, run_ref, load)

    half = wbuf.shape[1] // 2
    nchunk = half // MOE_NCHUNK_COLS
    rows_per = blk // nchunk

    def step(scatter, matmul):
        slot = (b - 1) % depth
        base = (b - 1) * blk
        for c in range(nchunk):
            if scatter:
                for r in range(c * rows_per, (c + 1) * rows_per):
                    pltpu.make_async_copy(obuf.at[slot, pl.ds(r, 1)],
                                          ys_hbm.at[pl.ds(row_ref[base + r], 1)],
                                          sem.at[slot]).start(priority=r % 2)
            if matmul:
                cols = slice(c * MOE_NCHUNK_COLS, (c + 1) * MOE_NCHUNK_COLS)
                hcols = slice(half + c * MOE_NCHUNK_COLS, half + (c + 1) * MOE_NCHUNK_COLS)
                lo = jnp.dot(hm_ref[...], wbuf[:, cols], preferred_element_type=F32)
                hi = jnp.dot(hm_ref[...], wbuf[:, hcols], preferred_element_type=F32)
                obuf[b % depth, :, cols] = _pack_bf16_pairs(lo, hi)

    @pl.when((b >= 1) & (b < used))
    def _():
        step(True, True)

    @pl.when((b == 0) & (b < used))
    def _():
        step(False, True)

    @pl.when((b >= 1) & (b == used))
    def _():
        step(True, False)

    @pl.when(b == nb)
    def _():
        for k in range(1, depth):
            @pl.when(nb - k < used)
            def _():
                wait_slot((nb - k) % depth)


def _moe_down(slot_row, block_e, hmid, w_down, nb, t):
    fe, d = w_down.shape[1], w_down.shape[2]
    kern = functools.partial(_moe_down_kernel, nb=nb, t=t)
    grid_spec = pltpu.PrefetchScalarGridSpec(
        num_scalar_prefetch=2,
        grid=(nb + 1,),
        in_specs=[pl.BlockSpec((MOE_BLOCK, fe), lambda b, s, e: (jnp.minimum(b, nb - 1), 0)),
                  pl.BlockSpec(memory_space=pl.ANY)],
        out_specs=pl.BlockSpec(memory_space=pl.ANY),
        scratch_shapes=[pltpu.VMEM((2, 1, fe, d), F32),
                        pltpu.VMEM((fe, d), BF16),
                        pltpu.VMEM((MOE_SCATTER_DEPTH, MOE_BLOCK, d // 2), jnp.uint32),
                        pltpu.SemaphoreType.DMA((MOE_SCATTER_DEPTH + 1,)),
                        pltpu.SemaphoreType.DMA((2,)),
                        pltpu.SMEM((1,), I32)],
    )
    return pl.pallas_call(
        kern,
        grid_spec=grid_spec,
        out_shape=jax.ShapeDtypeStruct((2 * t + MOE_SPARE_SETS * MOE_BLOCK, d // 2), jnp.uint32),
        compiler_params=_cparams(("arbitrary",)),
        name="moe_down",
    )(slot_row, block_e, hmid, w_down)


def _final_kernel(x1_ref, y0_ref, y1_ref, w_ref, mod_ref, gf_ref, o_ref):
    w = w_ref[...]
    lo0, hi0 = _unpack_bf16_pairs(y0_ref[...])
    lo1, hi1 = _unpack_bf16_pairs(y1_ref[...])
    y = jnp.concatenate([w[:, 0:1] * lo0 + w[:, 1:2] * lo1,
                         w[:, 0:1] * hi0 + w[:, 1:2] * hi1], axis=1)
    x2 = x1_ref[...] + mod_ref[0][5:6, :] * y
    o_ref[...] = x2 * lax.rsqrt(jnp.mean(x2 * x2, axis=-1, keepdims=True) + NORM_EPS) * gf_ref[...]


def _final(x1, ys, w, mod3, g_final, seq, tm=1024):
    t, d = x1.shape
    tm = min(tm, seq)
    tpb = seq // tm
    return pl.pallas_call(
        _final_kernel,
        grid=(t // tm,),
        in_specs=[pl.BlockSpec((tm, d), lambda i: (i, 0)),
                  pl.BlockSpec((tm, d // 2), lambda i: (i, 0)),
                  pl.BlockSpec((tm, d // 2), lambda i: (i + t // tm, 0)),
                  pl.BlockSpec((tm, LANES), lambda i: (i, 0)),
                  pl.BlockSpec((1, N_MOD, d), lambda i: (i // tpb, 0, 0)),
                  pl.BlockSpec((1, d), lambda i: (0, 0))],
        out_specs=pl.BlockSpec((tm, d), lambda i: (i, 0)),
        out_shape=jax.ShapeDtypeStruct((t, d), F32),
        compiler_params=_cparams(("arbitrary",)),
        name="combine_final_norm",
    )(x1, ys, ys, w, mod3, g_final.reshape(1, d))


def _rope_tables(seq, dh):
    half = dh // 2
    inv_freq = 1.0 / (ROPE_BASE ** (jnp.arange(half, dtype=F32) * 2.0 / dh))
    ang = jnp.arange(seq, dtype=F32)[:, None] * inv_freq[None, :]
    cos, sin = jnp.cos(ang), jnp.sin(ang)
    return jnp.concatenate([cos, cos], axis=1), jnp.concatenate([-sin, sin], axis=1)


def _layer(x, mod, g_norm1, w_in, ssm_a_re, ssm_a_im, ssm_b_re, ssm_b_im, ssm_c_re, ssm_c_im,
           ssm_d, ssm_log_dt, w_glu, beta_ssm, beta_ret, w_out, g_norm2, w_rg, b_rg, w_re, b_re,
           w_gate, w_up, w_down):
    bn, seq, d = x.shape
    t = bn * seq
    x2d = x.reshape(t, d)
    mod3 = mod.reshape(bn, N_MOD, d)

    sw = w_glu.shape[0]
    proj = _inproj(x2d, mod3, g_norm1, w_in.astype(BF16), seq)
    proj3 = proj.reshape(bn, seq, proj.shape[1])

    ab_re, ab_im, bbt_re, bbt_im = _s5_disc(ssm_a_re, ssm_a_im, ssm_log_dt,
                                            jnp.swapaxes(ssm_b_re, 1, 2), jnp.swapaxes(ssm_b_im, 1, 2))
    bz, cz, a_re, a_im = _s5_pack(ab_re, ab_im, bbt_re, bbt_im, ssm_c_re, ssm_c_im)
    ypre = _s5_scan(proj3, bz, cz, a_re, a_im, ssm_d.reshape(-1)).reshape(t, sw)

    heads = (d - sw) // RET_HEAD_DIM
    cosf, sinf = _rope_tables(seq, RET_HEAD_DIM)
    gamma = 1.0 - jnp.exp2(-5.0 - jnp.arange(heads, dtype=F32))
    log_g = jnp.broadcast_to(jnp.log(gamma)[:, None, None], (heads, 1, LANES))
    yret = _retention(proj3, sw // RET_HEAD_DIM, cosf, sinf, log_g, beta_ret)

    n_route = N_GROUPS + N_EXPERTS
    wr = jnp.concatenate([w_rg, w_re, jnp.zeros((d, LANES - n_route), F32)], axis=1)
    br = jnp.concatenate([b_rg, b_re, jnp.zeros((LANES - n_route,), F32)]).reshape(1, LANES)
    x1, h2, meta, w_tok, cnt = _mix(ypre, yret.reshape(t, d - sw), x2d, mod3, w_glu.astype(BF16),
                                    beta_ssm, w_out.astype(BF16), g_norm2, wr.astype(BF16), br, seq)

    nb = -(-(t * 2) // MOE_BLOCK) + N_EXPERTS
    dest, be = _dest(meta, cnt, nb)
    slot_row = _invert(dest[:, :2].T.reshape(-1), nb * MOE_BLOCK, t)
    block_e = be[:nb + 1, 0]

    hmid = _moe_up(slot_row, block_e, h2, w_gate, w_up, nb)
    ys = _moe_down(slot_row, block_e, hmid, w_down, nb, t)
    return x1, ys, w_tok, mod3


def kernel(x, c, w_ada, b_ada, g_norm1, w_in, ssm_a_re, ssm_a_im, ssm_b_re, ssm_b_im, ssm_c_re,
           ssm_c_im, ssm_d, ssm_log_dt, w_glu, beta_ssm, beta_ret, w_out, g_norm2, w_router_group,
           b_router_group, w_router_expert, b_router_expert, w_gate, w_up, w_down, g_final):
    depth = w_ada.shape[0]
    bn, seq, d = x.shape
    assert depth == 1, "the final norm is fused into the single layer's last kernel"
    l = 0
    mod = _ada(c, w_ada[l], b_ada[l])
    x1, ys, w_tok, mod3 = _layer(
        x, mod, g_norm1[l], w_in[l], ssm_a_re[l], ssm_a_im[l], ssm_b_re[l], ssm_b_im[l],
        ssm_c_re[l], ssm_c_im[l], ssm_d[l], ssm_log_dt[l], w_glu[l], beta_ssm[l], beta_ret[l],
        w_out[l], g_norm2[l], w_router_group[l], b_router_group[l], w_router_expert[l],
        b_router_expert[l], w_gate[l], w_up[l], w_down[l])
    out = _final(x1, ys, w_tok, mod3, g_final, seq)
    return out.reshape(bn, seq, d)
```
